```python
import jax, jax.numpy as jnp
from jax import lax
import numpy as np

D_MODEL = 1024
BATCH = 16
SEQ = 256
DEPTH = 4
DEC_BATCH = 2
DEC_SEQ = 2048
PAST_LEN = 256

GRID_W = 64
N_MIXERS = 3
CONV_WIDTH = 3
HEAD_DIM = 128
N_HEADS = D_MODEL // HEAD_DIM
N_KV_HEADS = 2
ROPE_THETA = 10000.0
Q_BLOCK = 128
GLA_HEADS = 4
GLA_DK = (D_MODEL // 2) // GLA_HEADS
GLA_DV = D_MODEL // GLA_HEADS
GLA_GATE_RANK = 16
GLA_TAU = 16.0
GLA_CHUNK = 64
D_FF = -(-8 * D_MODEL // (3 * 256)) * 256
DEEPNORM_ALPHA = (2.0 * DEPTH) ** 0.25
DEEPNORM_BETA = (8.0 * DEPTH) ** -0.25
LN_EPS = 1e-5
RMS_EPS = 1e-6
N_CONV_LAYERS = len(range(0, DEPTH, N_MIXERS))
N_ATTN_LAYERS = len(range(1, DEPTH, N_MIXERS))
N_GLA_LAYERS = len(range(2, DEPTH, N_MIXERS))

kernel_name = "hybrid_diffusion_conv_gqa_gla_step"


def layer_norm(x, g, b):
    xf = x.astype(jnp.float32)
    mu = jnp.mean(xf, axis=-1, keepdims=True)
    var = jnp.mean(jnp.square(xf - mu), axis=-1, keepdims=True)
    return ((xf - mu) * lax.rsqrt(var + LN_EPS) * g + b).astype(x.dtype)


def rms_norm(x, g):
    xf = x.astype(jnp.float32)
    return (xf * lax.rsqrt(jnp.mean(xf * xf, axis=-1, keepdims=True) + RMS_EPS) * g).astype(x.dtype)


def modulation(cond, w, b):
    m = (jax.nn.silu(cond) @ w + b)[:, None, :]
    return jnp.split(m, 6, axis=-1)


def short_conv(h, w_in, w_conv, w_out):
    bg, cg, u = jnp.split(h @ w_in, 3, axis=-1)
    u = cg * u
    up = jnp.pad(u, ((0, 0), (1, 1), (0, 0)))
    y = up[:, :-2] * w_conv[0] + up[:, 1:-1] * w_conv[1] + up[:, 2:] * w_conv[2]
    return (bg * y) @ w_out


def axial_rope_tables(seq):
    rows = seq // GRID_W
    row = jnp.repeat(jnp.arange(rows), GRID_W)
    col = jnp.tile(jnp.arange(GRID_W), rows)
    pos = jnp.stack([row, col], axis=-1).astype(jnp.float32)
    n_freq = HEAD_DIM // 4
    freqs = ROPE_THETA ** (-jnp.arange(n_freq, dtype=jnp.float32) / n_freq)
    ang = pos[:, :, None] * freqs
    return jnp.cos(ang), jnp.sin(ang)


def apply_axial_rope(x, cos, sin):
    B, S, Hh, _ = x.shape
    xr = x.astype(jnp.float32).reshape(B, S, Hh, 2, 2, HEAD_DIM // 4)
    x1, x2 = xr[..., 0, :], xr[..., 1, :]
    cb, sb = cos[None, :, None], sin[None, :, None]
    out = jnp.stack([x1 * cb - x2 * sb, x2 * cb + x1 * sb], axis=-2)
    return out.reshape(B, S, Hh, HEAD_DIM).astype(x.dtype)


def attn_qkv(h, w_qkv, q_g, k_g):
    B, S, _ = h.shape
    q, k, v = jnp.split(h @ w_qkv, [N_HEADS * HEAD_DIM, (N_HEADS + N_KV_HEADS) * HEAD_DIM], axis=-1)
    q = rms_norm(q.reshape(B, S, N_HEADS, HEAD_DIM), q_g)
    k = rms_norm(k.reshape(B, S, N_KV_HEADS, HEAD_DIM), k_g)
    v = v.reshape(B, S, N_KV_HEADS, HEAD_DIM)
    return q, k, v


def blocked_attention(q, k, v):
    B, S, H, hd = q.shape
    KV = k.shape[2]
    G = H // KV
    nb = S // Q_BLOCK
    qb = q.reshape(B, nb, Q_BLOCK, KV, G, hd).transpose(1, 0, 2, 3, 4, 5)
    scale = HEAD_DIM ** -0.5

    def one_block(qblk):
        s = jnp.einsum('bqkgd,btkd->bkgqt', qblk, k).astype(jnp.float32) * scale
        p = jax.nn.softmax(s, axis=-1).astype(v.dtype)
        return jnp.einsum('bkgqt,btkd->bqkgd', p, v)

    o = lax.map(one_block, qb)
    return o.transpose(1, 0, 2, 3, 4, 5).reshape(B, S, H * hd)


def gla_log_gate(h, w1, w2, b):
    B, S, _ = h.shape
    z = ((h @ w1) @ w2 + b).astype(jnp.float32)
    return (jax.nn.log_sigmoid(z) / GLA_TAU).reshape(B, S, GLA_HEADS, GLA_DK)


def gla_chunk_scan(q, k, v, logg, s0):
    B, S, H, _ = q.shape
    L = GLA_CHUNK
    n = S // L

    def to_chunks(a):
        return a.astype(jnp.float32).reshape(B, n, L, H, a.shape[-1]).transpose(1, 0, 3, 2, 4)

    mask = jnp.tril(jnp.ones((L, L), dtype=bool))[:, :, None]

    def step(state, inp):
        qc, kc, vc, gc = inp
        bcum = jnp.cumsum(gc, axis=2)
        o_inter = jnp.einsum('bhld,bhde->bhle', qc * jnp.exp(bcum), state)
        diff = bcum[:, :, :, None, :] - bcum[:, :, None, :, :]
        decay = jnp.where(mask, jnp.exp(jnp.where(mask, diff, 0.0)), 0.0)
        a = jnp.einsum('bhtd,bhsd,bhtsd->bhts', qc, kc, decay)
        o_intra = jnp.einsum('bhts,bhse->bhte', a, vc)
        b_last = bcum[:, :, -1]
        new_state = jnp.exp(b_last)[..., None] * state + jnp.einsum(
            'bhsd,bhse->bhde', kc * jnp.exp(b_last[:, :, None] - bcum), vc)
        return new_state, o_inter + o_intra

    s_fin, o = lax.scan(step, s0.astype(jnp.float32),
                        (to_chunks(q), to_chunks(k), to_chunks(v), to_chunks(logg)))
    o = o.transpose(1, 0, 3, 2, 4).reshape(B, S, H, v.shape[-1])
    return o.astype(v.dtype), s_fin


def gla_mixer(h, s0f, s0b, w_in, w_g1, w_g2, b_g, norm_g, w_o):
    B, S, _ = h.shape
    hk = GLA_HEADS * GLA_DK
    hv = GLA_HEADS * GLA_DV
    q, k, v, og = jnp.split(h @ w_in, [hk, 2 * hk, 2 * hk + hv], axis=-1)
    q = q.reshape(B, S, GLA_HEADS, GLA_DK) * (GLA_DK ** -0.5)
    k = k.reshape(B, S, GLA_HEADS, GLA_DK)
    v = v.reshape(B, S, GLA_HEADS, GLA_DV)
    lg_f = gla_log_gate(h, w_g1[0], w_g2[0], b_g[0])
    lg_b = gla_log_gate(h, w_g1[1], w_g2[1], b_g[1])
    o_f, sf = gla_chunk_scan(q, k, v, lg_f, s0f)
    o_b, sb = gla_chunk_scan(jnp.flip(q, 1), jnp.flip(k, 1), jnp.flip(v, 1), jnp.flip(lg_b, 1), s0b)
    o = o_f + jnp.flip(o_b, 1)
    o = rms_norm(o, norm_g) * jax.nn.silu(og).reshape(B, S, GLA_HEADS, GLA_DV)
    return o.reshape(B, S, hv) @ w_o, sf, sb


def swiglu(h, w_in, w_out):
    g, u = jnp.split(h @ w_in, 2, axis=-1)
    return (jax.nn.silu(g) * u) @ w_out


def setup_inputs(seed: int = 0) -> dict:
    key = jax.random.key(seed)
    ks = jax.random.split(key, 32)
    f32 = jnp.float32
    D = D_MODEL

    def nrm(k, shape, scale=1.0):
        return jax.random.normal(k, shape, f32) * scale

    qkv_out = (N_HEADS + 2 * N_KV_HEADS) * HEAD_DIM
    gla_in = 2 * GLA_HEADS * GLA_DK + 2 * GLA_HEADS * GLA_DV
    return {
        "x_prompt": nrm(ks[0], (BATCH, SEQ, D)),
        "x_sample": nrm(ks[1], (DEC_BATCH, DEC_SEQ, D)),
        "c": nrm(ks[2], (DEC_BATCH, D)),
        "cache_k": nrm(ks[3], (DEC_BATCH, N_ATTN_LAYERS, PAST_LEN, N_KV_HEADS, HEAD_DIM)),
        "cache_v": nrm(ks[4], (DEC_BATCH, N_ATTN_LAYERS, PAST_LEN, N_KV_HEADS, HEAD_DIM)),
        "state_gla_fwd": nrm(ks[5], (DEC_BATCH, N_GLA_LAYERS, GLA_HEADS, GLA_DK, GLA_DV), 0.5),
        "state_gla_bwd": nrm(ks[6], (DEC_BATCH, N_GLA_LAYERS, GLA_HEADS, GLA_DK, GLA_DV), 0.5),
        "c_ctx": nrm(ks[7], (D,)),
        "w_ada": nrm(ks[8], (DEPTH, D, 6 * D), 0.5 * D ** -0.5),
        "b_ada": nrm(ks[9], (DEPTH, 6 * D), 0.02),
        "ln_g": 1.0 + nrm(ks[10], (DEPTH, 2, D), 0.02),
        "ln_b": nrm(ks[11], (DEPTH, 2, D), 0.02),
        "conv_w_in": nrm(ks[12], (N_CONV_LAYERS, D, 3 * D), D ** -0.5),
        "conv_w": nrm(ks[13], (N_CONV_LAYERS, CONV_WIDTH, D), CONV_WIDTH ** -0.5),
        "conv_w_out": nrm(ks[14], (N_CONV_LAYERS, D, D), DEEPNORM_BETA * D ** -0.5),
        "attn_w_qkv": nrm(ks[15], (N_ATTN_LAYERS, D, qkv_out), D ** -0.5),
        "attn_q_norm": 1.0 + nrm(ks[16], (N_ATTN_LAYERS, HEAD_DIM), 0.02),
        "attn_k_norm": 1.0 + nrm(ks[17], (N_ATTN_LAYERS, HEAD_DIM), 0.02),
        "attn_w_o": nrm(ks[18], (N_ATTN_LAYERS, N_HEADS * HEAD_DIM, D), DEEPNORM_BETA * (N_HEADS * HEAD_DIM) ** -0.5),
        "gla_w_in": nrm(ks[19], (N_GLA_LAYERS, D, gla_in), D ** -0.5),
        "gla_w_gate1": nrm(ks[20], (N_GLA_LAYERS, 2, D, GLA_GATE_RANK), D ** -0.5),
        "gla_w_gate2": nrm(ks[21], (N_GLA_LAYERS, 2, GLA_GATE_RANK, GLA_HEADS * GLA_DK), GLA_GATE_RANK ** -0.5),
        "gla_b_gate": nrm(ks[22], (N_GLA_LAYERS, 2, GLA_HEADS * GLA_DK), 0.02),
        "gla_norm": 1.0 + nrm(ks[23], (N_GLA_LAYERS, GLA_DV), 0.02),
        "gla_w_o": nrm(ks[24], (N_GLA_LAYERS, GLA_HEADS * GLA_DV, D), DEEPNORM_BETA * (GLA_HEADS * GLA_DV) ** -0.5),
        "ffn_w_in": nrm(ks[25], (DEPTH, D, 2 * D_FF), D ** -0.5),
        "ffn_w_out": nrm(ks[26], (DEPTH, D_FF, D), DEEPNORM_BETA * D_FF ** -0.5),
    }


def reference(x_prompt, x_sample, c, cache_k, cache_v, state_gla_fwd, state_gla_bwd, c_ctx,
              w_ada, b_ada, ln_g, ln_b, conv_w_in, conv_w, conv_w_out,
              attn_w_qkv, attn_q_norm, attn_k_norm, attn_w_o,
              gla_w_in, gla_w_gate1, gla_w_gate2, gla_b_gate, gla_norm, gla_w_o,
              ffn_w_in, ffn_w_out):
    xp = x_prompt
    xs = x_sample
    cos, sin = axial_rope_tables(xs.shape[1])
    cond_ctx = c_ctx[None, :]
    new_k, new_v, new_sf, new_sb = [], [], [], []
    for i in range(DEPTH):
        kind = i % N_MIXERS
        j = i // N_MIXERS
        sh_p, sc_p, ga_p, sh2_p, sc2_p, ga2_p = modulation(cond_ctx, w_ada[i], b_ada[i])
        sh_s, sc_s, ga_s, sh2_s, sc2_s, ga2_s = modulation(c, w_ada[i], b_ada[i])
        hp = xp * (1.0 + sc_p) + sh_p
        hs = xs * (1.0 + sc_s) + sh_s
        if kind == 0:
            mix_p = short_conv(hp, conv_w_in[j], conv_w[j], conv_w_out[j])
            mix_s = short_conv(hs, conv_w_in[j], conv_w[j], conv_w_out[j])
        elif kind == 1:
            qp, kp, vp = attn_qkv(hp, attn_w_qkv[j], attn_q_norm[j], attn_k_norm[j])
            mix_p = blocked_attention(qp, kp, vp) @ attn_w_o[j]
            new_k.append(kp)
            new_v.append(vp)
            qs, ks_, vs = attn_qkv(hs, attn_w_qkv[j], attn_q_norm[j], attn_k_norm[j])
            qs = apply_axial_rope(qs, cos, sin)
            ks_ = apply_axial_rope(ks_, cos, sin)
            k_all = jnp.concatenate([cache_k[:, j].astype(ks_.dtype), ks_], axis=1)
            v_all = jnp.concatenate([cache_v[:, j].astype(vs.dtype), vs], axis=1)
            mix_s = blocked_attention(qs, k_all, v_all) @ attn_w_o[j]
        else:
            zeros = jnp.zeros((xp.shape[0], GLA_HEADS, GLA_DK, GLA_DV), jnp.float32)
            mix_p, sf, sb = gla_mixer(hp, zeros, zeros, gla_w_in[j], gla_w_gate1[j], gla_w_gate2[j],
                                      gla_b_gate[j], gla_norm[j], gla_w_o[j])
            new_sf.append(sf)
            new_sb.append(sb)
            mix_s, _, _ = gla_mixer(hs, state_gla_fwd[:, j], state_gla_bwd[:, j], gla_w_in[j],
                                    gla_w_gate1[j], gla_w_gate2[j], gla_b_gate[j], gla_norm[j], gla_w_o[j])
        xp = layer_norm(DEEPNORM_ALPHA * xp + ga_p * mix_p, ln_g[i, 0], ln_b[i, 0])
        xs = layer_norm(DEEPNORM_ALPHA * xs + ga_s * mix_s, ln_g[i, 0], ln_b[i, 0])
        hp = xp * (1.0 + sc2_p) + sh2_p
        hs = xs * (1.0 + sc2_s) + sh2_s
        xp = layer_norm(DEEPNORM_ALPHA * xp + ga2_p * swiglu(hp, ffn_w_in[i], ffn_w_out[i]), ln_g[i, 1], ln_b[i, 1])
        xs = layer_norm(DEEPNORM_ALPHA * xs + ga2_s * swiglu(hs, ffn_w_in[i], ffn_w_out[i]), ln_g[i, 1], ln_b[i, 1])
    new_cache_k = jnp.stack(new_k, axis=1)
    new_cache_v = jnp.stack(new_v, axis=1)
    new_state_fwd = jnp.stack(new_sf, axis=1)
    new_state_bwd = jnp.stack(new_sb, axis=1)
    return (xp, xs, new_cache_k, new_cache_v, new_state_fwd, new_state_bwd)
```

```python
import functools

import jax
import jax.numpy as jnp
from jax import lax
from jax.experimental import pallas as pl
from jax.experimental.pallas import tpu as pltpu

F32 = jnp.float32
BF16 = jnp.bfloat16

N_MIXERS = 3
CONV_WIDTH = 3
HEAD_DIM = 128
N_KV_HEADS = 2
GRID_W = 64
ROPE_THETA = 10000.0
GLA_HEADS = 4
GLA_TAU = 16.0
GLA_CHUNK = 64
LN_EPS = 1e-5
RMS_EPS = 1e-6

LANES = 128
BF16_SUBLANES = 16
VMEM_LIMIT = 56 * 1024 * 1024

MOD_ROWS = 16
MOD_NT = 1536
TM = 512
HALO = BF16_SUBLANES
GLA_SUB = 16
NEG_BIG = -1e30


def _dot(a, b):
    return jnp.dot(a, b, preferred_element_type=F32)


def _dot_nt(a, b):
    return lax.dot_general(a, b, (((1,), (1,)), ((), ())), preferred_element_type=F32)


def _dot_tn(a, b):
    return lax.dot_general(a, b, (((0,), (0,)), ((), ())), preferred_element_type=F32)


def _layer_norm(y, g, b):
    mu = jnp.mean(y, axis=-1, keepdims=True)
    yc = y - mu
    var = jnp.mean(yc * yc, axis=-1, keepdims=True)
    return yc * lax.rsqrt(var + LN_EPS) * g + b


def _params(semantics):
    return pltpu.CompilerParams(dimension_semantics=semantics, vmem_limit_bytes=VMEM_LIMIT)


def _resident(block_shape, index_map):
    return pl.BlockSpec(block_shape, index_map, pipeline_mode=pl.Buffered(1))


class _Layout:
    def __init__(self, batch, seq, dec_batch, dec_seq, d_model, depth):
        self.batch, self.seq, self.dec_batch, self.dec_seq = batch, seq, dec_batch, dec_seq
        self.d, self.depth = d_model, depth
        self.n_prompt = batch * seq
        self.n_tok = self.n_prompt + dec_batch * dec_seq
        assert self.n_prompt % TM == 0 and dec_seq % TM == 0 and TM % seq == 0
        assert seq & (seq - 1) == 0 and dec_seq & (dec_seq - 1) == 0
        self.alpha = (2.0 * depth) ** 0.25

    def mod_index(self, i, rows):
        r0 = i * rows
        return jnp.where(r0 < self.n_prompt, 0, 1 + (r0 - self.n_prompt) // self.dec_seq)


def _mod_kernel(cond_ref, w_ref, b_ref, o_ref):
    c = cond_ref[...]
    a = (c * jax.nn.sigmoid(c)).astype(BF16)
    o_ref[...] = _dot(a, w_ref[...].astype(BF16)) + b_ref[...]


def _modulation(cond, w_ada, b_ada):
    depth, d, n_out = w_ada.shape
    return pl.pallas_call(
        _mod_kernel,
        out_shape=jax.ShapeDtypeStruct((depth, MOD_ROWS, n_out), F32),
        grid=(depth, n_out // MOD_NT),
        in_specs=[
            pl.BlockSpec((MOD_ROWS, d), lambda l, j: (0, 0)),
            pl.BlockSpec((None, d, MOD_NT), lambda l, j: (l, 0, j)),
            pl.BlockSpec((None, 1, MOD_NT), lambda l, j: (l, 0, j)),
        ],
        out_specs=pl.BlockSpec((None, MOD_ROWS, MOD_NT), lambda l, j: (l, 0, j)),
        compiler_params=_params(("arbitrary", "arbitrary")),
        name="modulation",
    )(cond, w_ada, b_ada.reshape(depth, 1, n_out))


def _conv_kernel(lay, layer, xc_ref, xp_ref, xn_ref, mod_ref, lng_ref, lnb_ref, win_ref, cw_ref,
                 wout_ref, o_ref, h_scr, uu_scr):
    d = lay.d
    i = pl.program_id(0)
    sh, sc, ga = mod_ref[0:1, :], mod_ref[1:2, :], mod_ref[2:3, :]
    one_sc = 1.0 + sc
    x = xc_ref[...]
    h_scr[0:HALO, :] = (xp_ref[...] * one_sc + sh).astype(BF16)
    h_scr[HALO:HALO + TM, :] = (x * one_sc + sh).astype(BF16)
    h_scr[HALO + TM:HALO + TM + HALO, :] = (xn_ref[...] * one_sc + sh).astype(BF16)
    bg = _dot(h_scr[HALO:HALO + TM, :], win_ref[:, 0:d])
    cgu = _dot(h_scr[...], win_ref[:, d:3 * d])
    uu_scr[...] = cgu[:, 0:d] * cgu[:, d:2 * d]
    row = i * TM + lax.broadcasted_iota(jnp.int32, (TM, 1), 0)
    seq_len = jnp.where(row < lay.n_prompt, lay.seq, lay.dec_seq)
    pos = jnp.bitwise_and(row, seq_len - 1)
    u_prev = jnp.where(pos != 0, uu_scr[pl.ds(HALO - 1, TM), :], 0.0)
    u_next = jnp.where(pos != seq_len - 1, uu_scr[pl.ds(HALO + 1, TM), :], 0.0)
    y = u_prev * cw_ref[0:1, :] + uu_scr[pl.ds(HALO, TM), :] * cw_ref[1:2, :] + u_next * cw_ref[2:3, :]
    mix = _dot((bg * y).astype(BF16), wout_ref[...])
    o_ref[...] = _layer_norm(lay.alpha * x + ga * mix,
                             lng_ref[2 * layer:2 * layer + 1, :], lnb_ref[2 * layer:2 * layer + 1, :])


def _conv_mixer(lay, layer, j, x, mod, lng, lnb, w_in, cw, w_out):
    d = lay.d
    n_halo_blocks = lay.n_tok // HALO
    per = TM // HALO
    return pl.pallas_call(
        functools.partial(_conv_kernel, lay, layer),
        out_shape=jax.ShapeDtypeStruct((lay.n_tok, d), F32),
        grid=(lay.n_tok // TM,),
        in_specs=[
            pl.BlockSpec((TM, d), lambda i: (i, 0)),
            pl.BlockSpec((HALO, d), lambda i: (jnp.maximum(i * per - 1, 0), 0)),
            pl.BlockSpec((HALO, d), lambda i: (jnp.minimum((i + 1) * per, n_halo_blocks - 1), 0)),
            pl.BlockSpec((None, None, 6, d), lambda i: (layer, lay.mod_index(i, TM), 0, 0)),
            _resident(lng.shape, lambda i: (0, 0)),
            _resident(lnb.shape, lambda i: (0, 0)),
            _resident((None, d, 3 * d), lambda i: (j, 0, 0)),
            _resident((None, CONV_WIDTH, d), lambda i: (j, 0, 0)),
            _resident((None, d, d), lambda i: (j, 0, 0)),
        ],
        out_specs=pl.BlockSpec((TM, d), lambda i: (i, 0)),
        scratch_shapes=[pltpu.VMEM((TM + 2 * HALO, d), BF16), pltpu.VMEM((TM + 2 * HALO, d), F32)],
        compiler_params=_params(("arbitrary",)),
        name=f"conv_mixer_{layer}",
    )(x, x, x, mod, lng, lnb, w_in, cw, w_out)


def _ffn_kernel(lay, layer, x_ref, mod_ref, lng_ref, lnb_ref, win_ref, wout_ref, o_ref):
    d_ff = wout_ref.shape[0]
    sh, sc, ga = mod_ref[3:4, :], mod_ref[4:5, :], mod_ref[5:6, :]
    x = x_ref[...]
    h = (x * (1.0 + sc) + sh).astype(BF16)
    g = _dot(h, win_ref[:, 0:d_ff])
    u = _dot(h, win_ref[:, d_ff:2 * d_ff])
    a = (g * jax.nn.sigmoid(g) * u).astype(BF16)
    y = _dot(a, wout_ref[...])
    o_ref[...] = _layer_norm(lay.alpha * x + ga * y,
                             lng_ref[2 * layer + 1:2 * layer + 2, :], lnb_ref[2 * layer + 1:2 * layer + 2, :])


def _ffn(lay, layer, x, mod, lng, lnb, w_in, w_out):
    d = lay.d
    d_ff = w_out.shape[1]
    return pl.pallas_call(
        functools.partial(_ffn_kernel, lay, layer),
        out_shape=jax.ShapeDtypeStruct((lay.n_tok, d), F32),
        grid=(lay.n_tok // TM,),
        in_specs=[
            pl.BlockSpec((TM, d), lambda i: (i, 0)),
            pl.BlockSpec((None, None, 6, d), lambda i: (layer, lay.mod_index(i, TM), 0, 0)),
            _resident(lng.shape, lambda i: (0, 0)),
            _resident(lnb.shape, lambda i: (0, 0)),
            _resident((None, d, 2 * d_ff), lambda i: (layer, 0, 0)),
            _resident((None, d_ff, d), lambda i: (layer, 0, 0)),
        ],
        out_specs=pl.BlockSpec((TM, d), lambda i: (i, 0)),
        compiler_params=_params(("arbitrary",)),
        name=f"ffn_{layer}",
    )(x, mod, lng, lnb, w_in, w_out)


def _qkv_kernel(lay, n_heads, x_ref, mod_ref, w_ref, qg_ref, kg_ref, cos_ref, sin_ref,
                q_ref, k_ref, v_ref, kf_ref, vf_ref):
    sh, sc = mod_ref[0:1, :], mod_ref[1:2, :]
    h = (x_ref[...] * (1.0 + sc) + sh).astype(BF16)
    qkv = _dot(h, w_ref[...])
    cos, sin = cos_ref[...], sin_ref[...]
    lane = lax.broadcasted_iota(jnp.int32, (1, HEAD_DIM), 1)
    first_half = jnp.bitwise_and(lane, HEAD_DIM // 4) == 0

    def norm_rope(t, g):
        t = t * lax.rsqrt(jnp.mean(t * t, axis=-1, keepdims=True) + RMS_EPS) * g
        partner = jnp.where(first_half,
                            pltpu.roll(t, HEAD_DIM - HEAD_DIM // 4, axis=1),
                            pltpu.roll(t, HEAD_DIM // 4, axis=1))
        return t * cos + partner * sin

    q_gain = qg_ref[...] * (HEAD_DIM ** -0.5)
    for hq in range(n_heads):
        sl = slice(hq * HEAD_DIM, (hq + 1) * HEAD_DIM)
        q_ref[:, sl] = norm_rope(qkv[:, sl], q_gain).astype(BF16)
    k0 = n_heads * HEAD_DIM
    v0 = k0 + N_KV_HEADS * HEAD_DIM
    for hk in range(N_KV_HEADS):
        sl = slice(hk * HEAD_DIM, (hk + 1) * HEAD_DIM)
        kh = norm_rope(qkv[:, k0 + hk * HEAD_DIM:k0 + (hk + 1) * HEAD_DIM], kg_ref[...])
        kf_ref[:, sl] = kh
        k_ref[:, sl] = kh.astype(BF16)
    v = qkv[:, v0:v0 + N_KV_HEADS * HEAD_DIM]
    vf_ref[...] = v
    v_ref[...] = v.astype(BF16)


def _attn_qkv(lay, layer, j, x, mod, w_qkv, q_gain, k_gain, cos_t, sin_t):
    d = lay.d
    n_qkv = w_qkv.shape[2]
    kv = N_KV_HEADS * HEAD_DIM
    n_heads = (n_qkv - 2 * kv) // HEAD_DIM
    row = lambda i: (i, 0)
    return pl.pallas_call(
        functools.partial(_qkv_kernel, lay, n_heads),
        out_shape=(jax.ShapeDtypeStruct((lay.n_tok, n_heads * HEAD_DIM), BF16),
                   jax.ShapeDtypeStruct((lay.n_tok, kv), BF16),
                   jax.ShapeDtypeStruct((lay.n_tok, kv), BF16),
                   jax.ShapeDtypeStruct((lay.n_tok, kv), F32),
                   jax.ShapeDtypeStruct((lay.n_tok, kv), F32)),
        grid=(lay.n_tok // TM,),
        in_specs=[
            pl.BlockSpec((TM, d), row),
            pl.BlockSpec((None, None, 6, d), lambda i: (layer, lay.mod_index(i, TM), 0, 0)),
            _resident((None, d, n_qkv), lambda i: (j, 0, 0)),
            _resident((None, 1, HEAD_DIM), lambda i: (j, 0, 0)),
            _resident((None, 1, HEAD_DIM), lambda i: (j, 0, 0)),
            pl.BlockSpec((TM, HEAD_DIM), row),
            pl.BlockSpec((TM, HEAD_DIM), row),
        ],
        out_specs=(pl.BlockSpec((TM, n_heads * HEAD_DIM), row), pl.BlockSpec((TM, kv), row),
                   pl.BlockSpec((TM, kv), row), pl.BlockSpec((TM, kv), row), pl.BlockSpec((TM, kv), row)),
        compiler_params=_params(("arbitrary",)),
        name=f"attn_qkv_{layer}",
    )(x, mod, w_qkv, q_gain, k_gain, cos_t, sin_t)


def _attn_kernel(lay, layer, n_heads, has_cache, *refs):
    if has_cache:
        (q_ref, k_ref, v_ref, kc_ref, vc_ref, x_ref, mod_ref, lng_ref, lnb_ref, wo_ref, o_ref, att_scr) = refs
    else:
        (q_ref, k_ref, v_ref, x_ref, mod_ref, lng_ref, lnb_ref, wo_ref, o_ref, att_scr) = refs
    group = n_heads // N_KV_HEADS
    for hk in range(N_KV_HEADS):
        ksl = slice(hk * HEAD_DIM, (hk + 1) * HEAD_DIM)
        for g in range(group):
            hq = hk * group + g
            qsl = slice(hq * HEAD_DIM, (hq + 1) * HEAD_DIM)
            qh = q_ref[:, qsl]
            s = _dot_nt(qh, k_ref[:, ksl])
            m = jnp.max(s, axis=-1, keepdims=True)
            if has_cache:
                s_c = _dot_nt(qh, kc_ref[:, ksl])
                m = jnp.maximum(m, jnp.max(s_c, axis=-1, keepdims=True))
            p = jnp.exp(s - m)
            l = jnp.sum(p, axis=-1, keepdims=True)
            o = _dot(p.astype(BF16), v_ref[:, ksl])
            if has_cache:
                p_c = jnp.exp(s_c - m)
                l = l + jnp.sum(p_c, axis=-1, keepdims=True)
                o = o + _dot(p_c.astype(BF16), vc_ref[:, ksl])
            att_scr[:, qsl] = (o / l).astype(BF16)
    ga = mod_ref[2:3, :]
    mix = _dot(att_scr[...], wo_ref[...])
    o_ref[...] = _layer_norm(lay.alpha * x_ref[...] + ga * mix,
                             lng_ref[2 * layer:2 * layer + 1, :], lnb_ref[2 * layer:2 * layer + 1, :])


def _attention(lay, layer, j, q, k, v, cache, x, mod, lng, lnb, w_o, *, row0, n_seq, seq, tq):
    d = lay.d
    n_heads = q.shape[1] // HEAD_DIM
    kv = N_KV_HEADS * HEAD_DIM
    has_cache = cache is not None
    q_blocks = seq // tq
    tile = lambda b, i: (row0 // tq + b * q_blocks + i, 0)
    seq_blk = lambda b, i: (row0 // seq + b, 0)
    in_specs = [pl.BlockSpec((tq, n_heads * HEAD_DIM), tile),
                pl.BlockSpec((seq, kv), seq_blk), pl.BlockSpec((seq, kv), seq_blk)]
    args = [q, k, v]
    if has_cache:
        kc, vc = cache
        past = kc.shape[0] // n_seq
        in_specs += [pl.BlockSpec((past, kv), lambda b, i: (b, 0))] * 2
        args += [kc, vc]
    x_index = len(args)
    in_specs += [
        pl.BlockSpec((tq, d), tile),
        pl.BlockSpec((None, None, 6, d), lambda b, i: (layer, lay.mod_index(row0 // tq + b * q_blocks + i, tq), 0, 0)),
        _resident(lng.shape, lambda b, i: (0, 0)),
        _resident(lnb.shape, lambda b, i: (0, 0)),
        _resident((None, n_heads * HEAD_DIM, d), lambda b, i: (j, 0, 0)),
    ]
    args += [x, mod, lng, lnb, w_o]

    return pl.pallas_call(
        functools.partial(_attn_kernel, lay, layer, n_heads, has_cache),
        out_shape=jax.ShapeDtypeStruct((lay.n_tok, d), F32),
        grid=(n_seq, q_blocks),
        in_specs=in_specs,
        out_specs=pl.BlockSpec((tq, d), tile),
        scratch_shapes=[pltpu.VMEM((tq, n_heads * HEAD_DIM), BF16)],
        input_output_aliases={x_index: 0},
        compiler_params=_params(("arbitrary", "arbitrary")),
        name=f"attention_{layer}_{'latent' if has_cache else 'context'}",
    )(*args)


def _split3(x):
    hi = x.astype(BF16)
    r = x - hi.astype(F32)
    mid = r.astype(BF16)
    lo = (r - mid.astype(F32)).astype(BF16)
    return hi, mid, lo


def _gla_proj_kernel(lay, hk, hv, x_ref, mod_ref, win_ref, w1_ref, w2_ref, bg_ref,
                     q_ref, k_ref, v_ref, og_ref, bf_ref, bb_ref):
    sh, sc = mod_ref[0:1, :], mod_ref[1:2, :]
    h = (x_ref[...] * (1.0 + sc) + sh).astype(BF16)
    proj = _dot(h, win_ref[...])
    dk = hk // GLA_HEADS
    q_ref[...] = proj[:, 0:hk] * (dk ** -0.5)
    k_ref[...] = proj[:, hk:2 * hk]
    v_ref[...] = proj[:, 2 * hk:2 * hk + hv].astype(BF16)
    og_ref[...] = proj[:, 2 * hk + hv:2 * hk + 2 * hv]
    z = _dot(_dot(h, w1_ref[...]).astype(BF16), w2_ref[...]) + bg_ref[...]
    log_gate = (jnp.minimum(z, 0.0) - jnp.log1p(jnp.exp(-jnp.abs(z)))) * (1.0 / GLA_TAU)
    r = lax.broadcasted_iota(jnp.int32, (GLA_CHUNK, GLA_CHUNK), 0)
    c = lax.broadcasted_iota(jnp.int32, (GLA_CHUNK, GLA_CHUNK), 1)
    lower = jnp.where(c <= r, 1.0, 0.0).astype(BF16)
    upper = jnp.where(c >= r, 1.0, 0.0).astype(BF16)
    for ch in range(TM // GLA_CHUNK):
        rows = slice(ch * GLA_CHUNK, (ch + 1) * GLA_CHUNK)
        f_hi, f_mid, f_lo = _split3(log_gate[rows, 0:hk])
        bf_ref[rows, :] = _dot(lower, f_hi) + _dot(lower, f_mid) + _dot(lower, f_lo)
        b_hi, b_mid, b_lo = _split3(log_gate[rows, hk:2 * hk])
        bb_ref[rows, :] = _dot(upper, b_hi) + _dot(upper, b_mid) + _dot(upper, b_lo)


def _gla_proj(lay, layer, j, x, mod, w_in, w1, w2, b_gate):
    d = lay.d
    n_in = w_in.shape[2]
    hk = w2.shape[2] // 2
    hv = (n_in - 2 * hk) // 2
    rank2 = w1.shape[2]
    row = lambda i: (i, 0)
    return pl.pallas_call(
        functools.partial(_gla_proj_kernel, lay, hk, hv),
        out_shape=(jax.ShapeDtypeStruct((lay.n_tok, hk), F32),
                   jax.ShapeDtypeStruct((lay.n_tok, hk), F32),
                   jax.ShapeDtypeStruct((lay.n_tok, hv), BF16),
                   jax.ShapeDtypeStruct((lay.n_tok, hv), F32),
                   jax.ShapeDtypeStruct((lay.n_tok, hk), F32),
                   jax.ShapeDtypeStruct((lay.n_tok, hk), F32)),
        grid=(lay.n_tok // TM,),
        in_specs=[
            pl.BlockSpec((TM, d), row),
            pl.BlockSpec((None, None, 6, d), lambda i: (layer, lay.mod_index(i, TM), 0, 0)),
            _resident((None, d, n_in), lambda i: (j, 0, 0)),
            _resident((None, d, rank2), lambda i: (j, 0, 0)),
            _resident((None, rank2, 2 * hk), lambda i: (j, 0, 0)),
            _resident((None, 1, 2 * hk), lambda i: (j, 0, 0)),
        ],
        out_specs=(pl.BlockSpec((TM, hk), row), pl.BlockSpec((TM, hk), row), pl.BlockSpec((TM, hv), row),
                   pl.BlockSpec((TM, hv), row), pl.BlockSpec((TM, hk), row), pl.BlockSpec((TM, hk), row)),
        compiler_params=_params(("arbitrary",)),
        name=f"gla_proj_{layer}",
    )(x, mod, w_in, w1, w2, b_gate)


def _gla_chunk(q, k, v, b, st, forward):
    n_sub = GLA_CHUNK // GLA_SUB
    o_rows = []
    qi = (q * jnp.exp(b)).astype(BF16)
    o_inter = _dot_nt(qi, st.astype(BF16))
    t_idx = lax.broadcasted_iota(jnp.int32, (GLA_SUB, 1), 0)
    for blk in range(n_sub):
        rows = slice(blk * GLA_SUB, (blk + 1) * GLA_SUB)
        qb, kb, bb_, vb = q[rows], k[rows], b[rows], v[rows].astype(F32)
        o_blk = o_inter[rows]
        if forward and blk > 0:
            others = slice(0, blk * GLA_SUB)
            rho = b[blk * GLA_SUB - 1:blk * GLA_SUB]
        elif (not forward) and blk < n_sub - 1:
            others = slice((blk + 1) * GLA_SUB, GLA_CHUNK)
            rho = b[(blk + 1) * GLA_SUB:(blk + 1) * GLA_SUB + 1]
        else:
            others = None
        if others is not None:
            q_s = (qb * jnp.exp(bb_ - rho)).astype(BF16)
            k_s = (k[others] * jnp.exp(rho - b[others])).astype(BF16)
            a = _dot_nt(q_s, k_s)
            o_blk = o_blk + _dot(a.astype(BF16), v[others])
        for s in range(GLA_SUB):
            keep = (t_idx >= s) if forward else (t_idx <= s)
            decay = jnp.exp(jnp.where(keep, bb_ - bb_[s:s + 1], NEG_BIG))
            a_col = jnp.sum(qb * kb[s:s + 1] * decay, axis=-1, keepdims=True)
            o_blk = o_blk + a_col * vb[s:s + 1]
        o_rows.append(o_blk)
    b_exit = b[GLA_CHUNK - 1:GLA_CHUNK] if forward else b[0:1]
    k_hat = (k * jnp.exp(b_exit - b)).astype(BF16)
    st_new = st * jnp.exp(b_exit) + _dot_tn(v, k_hat)
    return jnp.concatenate(o_rows, axis=0), st_new


def _gla_scan_kernel(n_chunks, has_init, *refs):
    if has_init:
        (q_ref, k_ref, v_ref, bf_ref, bb_ref, s0f_ref, s0b_ref, o_ref, stf, stb) = refs
        stf[...] = s0f_ref[...].T
        stb[...] = s0b_ref[...].T
    else:
        (q_ref, k_ref, v_ref, bf_ref, bb_ref, o_ref, sf_ref, sb_ref, stf, stb) = refs
        stf[...] = jnp.zeros_like(stf)
        stb[...] = jnp.zeros_like(stb)

    def step(i, accumulate):
        for forward, b_ref, st in ((True, bf_ref, stf), (False, bb_ref, stb)):
            c = i if forward else n_chunks - 1 - i
            rows = pl.ds(pl.multiple_of(c * GLA_CHUNK, GLA_CHUNK), GLA_CHUNK)
            o, st_new = _gla_chunk(q_ref[rows, :], k_ref[rows, :], v_ref[rows, :], b_ref[rows, :],
                                   st[...], forward)
            st[...] = st_new
            if accumulate:
                o_ref[rows, :] += o
            else:
                o_ref[rows, :] = o

    half = n_chunks // 2

    def first(i, carry):
        step(i, False)
        return carry

    def second(i, carry):
        step(i, True)
        return carry

    lax.fori_loop(0, half, first, 0)
    lax.fori_loop(half, n_chunks, second, 0)
    if not has_init:
        sf_ref[...] = stf[...].T
        sb_ref[...] = stb[...].T


def _gla_scan(lay, j, q, k, v, bf, bb, init, *, row0, n_seq, seq, o_alias):
    hk, hv = q.shape[1], v.shape[1]
    dk, dv = hk // GLA_HEADS, hv // GLA_HEADS
    n_chunks = seq // GLA_CHUNK
    assert n_chunks % 2 == 0
    has_init = init is not None
    blk = lambda b, h: (row0 // seq + b, h)
    in_specs = [pl.BlockSpec((seq, dk), blk), pl.BlockSpec((seq, dk), blk), pl.BlockSpec((seq, dv), blk),
                pl.BlockSpec((seq, dk), blk), pl.BlockSpec((seq, dk), blk)]
    args = [q, k, v, bf, bb]
    o_shape = jax.ShapeDtypeStruct(o_alias.shape, F32) if o_alias is not None else \
        jax.ShapeDtypeStruct((q.shape[0], hv), F32)
    o_spec = pl.BlockSpec((seq, dv), blk)
    if has_init:
        s0f, s0b = init
        st_spec = pl.BlockSpec((None, None, dk, dv), lambda b, h: (b, j * GLA_HEADS + h, 0, 0))
        in_specs += [st_spec, st_spec]
        args += [s0f, s0b]
        out_shape, out_specs = o_shape, o_spec
    else:
        st_shape = jax.ShapeDtypeStruct((n_seq, GLA_HEADS, dk, dv), F32)
        st_spec = pl.BlockSpec((None, None, dk, dv), lambda b, h: (b, h, 0, 0))
        out_shape, out_specs = (o_shape, st_shape, st_shape), (o_spec, st_spec, st_spec)
    aliases = {}
    n_in = len(args)
    if o_alias is not None:
        in_specs.append(pl.BlockSpec(memory_space=pl.ANY))
        args.append(o_alias)
        aliases = {n_in: 0}

    def body(*refs):
        refs = refs[:n_in] + refs[len(args):]
        _gla_scan_kernel(n_chunks, has_init, *refs)

    return pl.pallas_call(
        body,
        out_shape=out_shape,
        grid=(n_seq, GLA_HEADS),
        in_specs=in_specs,
        out_specs=out_specs,
        scratch_shapes=[pltpu.VMEM((dv, dk), F32), pltpu.VMEM((dv, dk), F32)],
        input_output_aliases=aliases,
        compiler_params=_params(("arbitrary", "arbitrary")),
        name=f"gla_scan_{'latent' if has_init else 'context'}",
    )(*args)


def _gla_out_kernel(lay, layer, o_ref, og_ref, x_ref, mod_ref, lng_ref, lnb_ref, ng_ref, wo_ref,
                    out_ref, z_scr):
    dv = ng_ref.shape[1]
    for h in range(GLA_HEADS):
        sl = slice(h * dv, (h + 1) * dv)
        o = o_ref[:, sl]
        og = og_ref[:, sl]
        o = o * lax.rsqrt(jnp.mean(o * o, axis=-1, keepdims=True) + RMS_EPS) * ng_ref[...]
        z_scr[:, sl] = (o * (og * jax.nn.sigmoid(og))).astype(BF16)
    ga = mod_ref[2:3, :]
    mix = _dot(z_scr[...], wo_ref[...])
    out_ref[...] = _layer_norm(lay.alpha * x_ref[...] + ga * mix,
                               lng_ref[2 * layer:2 * layer + 1, :], lnb_ref[2 * layer:2 * layer + 1, :])


def _gla_out(lay, layer, j, o, og, x, mod, lng, lnb, norm_g, w_o):
    d = lay.d
    hv = o.shape[1]
    dv = hv // GLA_HEADS
    row = lambda i: (i, 0)
    return pl.pallas_call(
        functools.partial(_gla_out_kernel, lay, layer),
        out_shape=jax.ShapeDtypeStruct((lay.n_tok, d), F32),
        grid=(lay.n_tok // TM,),
        in_specs=[
            pl.BlockSpec((TM, hv), row), pl.BlockSpec((TM, hv), row), pl.BlockSpec((TM, d), row),
            pl.BlockSpec((None, None, 6, d), lambda i: (layer, lay.mod_index(i, TM), 0, 0)),
            _resident(lng.shape, lambda i: (0, 0)),
            _resident(lnb.shape, lambda i: (0, 0)),
            _resident((None, 1, dv), lambda i: (j, 0, 0)),
            _resident((None, hv, d), lambda i: (j, 0, 0)),
        ],
        out_specs=pl.BlockSpec((TM, d), row),
        scratch_shapes=[pltpu.VMEM((TM, hv), BF16)],
        compiler_params=_params(("arbitrary",)),
        name=f"gla_out_{layer}",
    )(o, og, x, mod, lng, lnb, norm_g, w_o)


def _rope_tables(lay):
    n_freq = HEAD_DIM // 4
    pos = jnp.arange(lay.dec_seq)
    freqs = ROPE_THETA ** (-jnp.arange(n_freq, dtype=F32) / n_freq)
    ang_r = (pos // GRID_W).astype(F32)[:, None] * freqs
    ang_c = (pos % GRID_W).astype(F32)[:, None] * freqs
    cos = jnp.concatenate([jnp.cos(ang_r)] * 2 + [jnp.cos(ang_c)] * 2, axis=-1)
    sin = jnp.concatenate([-jnp.sin(ang_r), jnp.sin(ang_r), -jnp.sin(ang_c), jnp.sin(ang_c)], axis=-1)
    cos = jnp.concatenate([jnp.ones((lay.n_prompt, HEAD_DIM), F32)] + [cos] * lay.dec_batch, axis=0)
    sin = jnp.concatenate([jnp.zeros((lay.n_prompt, HEAD_DIM), F32)] + [sin] * lay.dec_batch, axis=0)
    return cos, sin


def kernel(x_prompt, x_sample, c, cache_k, cache_v, state_gla_fwd, state_gla_bwd, c_ctx, w_ada, b_ada, ln_g, ln_b, conv_w_in, conv_w, conv_w_out, attn_w_qkv, attn_q_norm, attn_k_norm, attn_w_o, gla_w_in, gla_w_gate1, gla_w_gate2, gla_b_gate, gla_norm, gla_w_o, ffn_w_in, ffn_w_out):
    batch, seq, d = x_prompt.shape
    dec_batch, dec_seq, _ = x_sample.shape
    depth = w_ada.shape[0]
    lay = _Layout(batch, seq, dec_batch, dec_seq, d, depth)
    kv = N_KV_HEADS * HEAD_DIM
    past = cache_k.shape[2]

    x = jnp.concatenate([x_prompt.reshape(lay.n_prompt, d), x_sample.reshape(dec_batch * dec_seq, d)], axis=0)
    cond = jnp.concatenate([c_ctx[None, :], c, jnp.zeros((MOD_ROWS - 1 - dec_batch, d), F32)], axis=0)
    mod = _modulation(cond, w_ada, b_ada).reshape(depth, MOD_ROWS, 6, d)
    lng = ln_g.reshape(depth * 2, d)
    lnb = ln_b.reshape(depth * 2, d)

    conv_w_in_b, conv_w_out_b = conv_w_in.astype(BF16), conv_w_out.astype(BF16)
    attn_w_qkv_b, attn_w_o_b = attn_w_qkv.astype(BF16), attn_w_o.astype(BF16)
    gla_w_in_b, gla_w_o_b = gla_w_in.astype(BF16), gla_w_o.astype(BF16)
    ffn_w_in_b, ffn_w_out_b = ffn_w_in.astype(BF16), ffn_w_out.astype(BF16)
    rank = gla_w_gate1.shape[3]
    hk = gla_w_gate2.shape[3]
    g1 = jnp.concatenate([gla_w_gate1[:, 0], gla_w_gate1[:, 1]], axis=-1).astype(BF16)
    zeros = jnp.zeros_like(gla_w_gate2[:, 0])
    g2 = jnp.concatenate([jnp.concatenate([gla_w_gate2[:, 0], zeros], axis=-1),
                          jnp.concatenate([zeros, gla_w_gate2[:, 1]], axis=-1)], axis=1).astype(BF16)
    gb = jnp.concatenate([gla_b_gate[:, 0], gla_b_gate[:, 1]], axis=-1)[:, None, :]
    cos_t, sin_t = _rope_tables(lay)

    new_k, new_v, new_sf, new_sb = [], [], [], []
    for i in range(depth):
        kind, j = i % N_MIXERS, i // N_MIXERS
        if kind == 0:
            x = _conv_mixer(lay, i, j, x, mod, lng, lnb, conv_w_in_b, conv_w, conv_w_out_b)
        elif kind == 1:
            q, k, v, kf, vf = _attn_qkv(lay, i, j, x, mod, attn_w_qkv_b,
                                        attn_q_norm[:, None, :], attn_k_norm[:, None, :], cos_t, sin_t)
            new_k.append(kf[:lay.n_prompt].reshape(batch, seq, N_KV_HEADS, HEAD_DIM))
            new_v.append(vf[:lay.n_prompt].reshape(batch, seq, N_KV_HEADS, HEAD_DIM))
            kc = cache_k[:, j].reshape(dec_batch * past, kv).astype(BF16)
            vc = cache_v[:, j].reshape(dec_batch * past, kv).astype(BF16)
            x = _attention(lay, i, j, q, k, v, None, x, mod, lng, lnb, attn_w_o_b,
                           row0=0, n_seq=batch, seq=seq, tq=seq)
            x = _attention(lay, i, j, q, k, v, (kc, vc), x, mod, lng, lnb, attn_w_o_b,
                           row0=lay.n_prompt, n_seq=dec_batch, seq=dec_seq, tq=TM)
        else:
            q, k, v, og, bf, bb = _gla_proj(lay, i, j, x, mod, gla_w_in_b, g1, g2, gb)
            s0f = state_gla_fwd.reshape(dec_batch, -1, *state_gla_fwd.shape[3:])
            s0b = state_gla_bwd.reshape(dec_batch, -1, *state_gla_bwd.shape[3:])
            o, sf, sb = _gla_scan(lay, j, q, k, v, bf, bb, None, row0=0, n_seq=batch, seq=seq, o_alias=None)
            o = _gla_scan(lay, j, q, k, v, bf, bb, (s0f, s0b), row0=lay.n_prompt, n_seq=dec_batch,
                          seq=dec_seq, o_alias=o)
            new_sf.append(sf)
            new_sb.append(sb)
            x = _gla_out(lay, i, j, o, og, x, mod, lng, lnb, gla_norm[:, None, :], gla_w_o_b)
        x = _ffn(lay, i, x, mod, lng, lnb, ffn_w_in_b, ffn_w_out_b)

    y_prompt = x[:lay.n_prompt].reshape(batch, seq, d)
    y_sample = x[lay.n_prompt:].reshape(dec_batch, dec_seq, d)
    return (y_prompt, y_sample, jnp.stack(new_k, axis=1), jnp.stack(new_v, axis=1),
            jnp.stack(new_sf, axis=1), jnp.stack(new_sb, axis=1))
```

```python
import functools

import jax
import jax.numpy as jnp
from jax import lax
from jax.experimental import pallas as pl
from jax.experimental.pallas import tpu as pltpu

F32 = jnp.float32
BF16 = jnp.bfloat16

N_MIXERS = 3
CONV_WIDTH = 3
HEAD_DIM = 128
N_KV_HEADS = 2
GRID_W = 64
ROPE_THETA = 10000.0
GLA_HEADS = 4
GLA_TAU = 16.0
GLA_CHUNK = 64
LN_EPS = 1e-5
RMS_EPS = 1e-6

LANES = 128
BF16_SUBLANES = 16
VMEM_LIMIT = 56 * 1024 * 1024

MOD_ROWS = 16
MOD_NT = 1536
TM = 512
HALO = BF16_SUBLANES
GLA_SUB = 16
GLA_SAFE_RANGE = 80.0
GLA_HEADS_PER_STEP_CONTEXT = 4
GLA_HEADS_PER_STEP_LATENT = 2
NEG_BIG = -1e30


def _dot(a, b):
    return jnp.dot(a, b, preferred_element_type=F32)


def _dot_nt(a, b):
    return lax.dot_general(a, b, (((1,), (1,)), ((), ())), preferred_element_type=F32)


def _dot_tn(a, b):
    return lax.dot_general(a, b, (((0,), (0,)), ((), ())), preferred_element_type=F32)


def _layer_norm(y, g, b):
    mu = jnp.mean(y, axis=-1, keepdims=True)
    yc = y - mu
    var = jnp.mean(yc * yc, axis=-1, keepdims=True)
    return yc * lax.rsqrt(var + LN_EPS) * g + b


def _params(semantics):
    return pltpu.CompilerParams(dimension_semantics=semantics, vmem_limit_bytes=VMEM_LIMIT)


def _resident(block_shape, index_map):
    return pl.BlockSpec(block_shape, index_map, pipeline_mode=pl.Buffered(1))


class _Layout:
    def __init__(self, batch, seq, dec_batch, dec_seq, d_model, depth):
        self.batch, self.seq, self.dec_batch, self.dec_seq = batch, seq, dec_batch, dec_seq
        self.d, self.depth = d_model, depth
        self.n_prompt = batch * seq
        self.n_tok = self.n_prompt + dec_batch * dec_seq
        assert self.n_prompt % TM == 0 and dec_seq % TM == 0 and TM % seq == 0
        assert seq & (seq - 1) == 0 and dec_seq & (dec_seq - 1) == 0
        self.alpha = (2.0 * depth) ** 0.25

    def mod_index(self, i, rows):
        r0 = i * rows
        return jnp.where(r0 < self.n_prompt, 0, 1 + (r0 - self.n_prompt) // self.dec_seq)


def _mod_kernel(cond_ref, w_ref, b_ref, o_ref):
    c = cond_ref[...]
    a = (c * jax.nn.sigmoid(c)).astype(BF16)
    o_ref[...] = _dot(a, w_ref[...].astype(BF16)) + b_ref[...]


def _modulation(cond, w_ada, b_ada):
    depth, d, n_out = w_ada.shape
    return pl.pallas_call(
        _mod_kernel,
        out_shape=jax.ShapeDtypeStruct((depth, MOD_ROWS, n_out), F32),
        grid=(depth, n_out // MOD_NT),
        in_specs=[
            pl.BlockSpec((MOD_ROWS, d), lambda l, j: (0, 0)),
            pl.BlockSpec((None, d, MOD_NT), lambda l, j: (l, 0, j)),
            pl.BlockSpec((None, 1, MOD_NT), lambda l, j: (l, 0, j)),
        ],
        out_specs=pl.BlockSpec((None, MOD_ROWS, MOD_NT), lambda l, j: (l, 0, j)),
        compiler_params=_params(("arbitrary", "arbitrary")),
        name="modulation",
    )(cond, w_ada, b_ada.reshape(depth, 1, n_out))


def _conv_kernel(lay, layer, xc_ref, xp_ref, xn_ref, mod_ref, lng_ref, lnb_ref, win_ref, cw_ref,
                 wout_ref, o_ref, h_scr, uu_scr):
    d = lay.d
    i = pl.program_id(0)
    sh, sc, ga = mod_ref[0:1, :], mod_ref[1:2, :], mod_ref[2:3, :]
    one_sc = 1.0 + sc
    x = xc_ref[...]
    h_scr[0:HALO, :] = (xp_ref[...] * one_sc + sh).astype(BF16)
    h_scr[HALO:HALO + TM, :] = (x * one_sc + sh).astype(BF16)
    h_scr[HALO + TM:HALO + TM + HALO, :] = (xn_ref[...] * one_sc + sh).astype(BF16)
    bg = _dot(h_scr[HALO:HALO + TM, :], win_ref[:, 0:d])
    cgu = _dot(h_scr[...], win_ref[:, d:3 * d])
    uu_scr[...] = cgu[:, 0:d] * cgu[:, d:2 * d]
    row = i * TM + lax.broadcasted_iota(jnp.int32, (TM, 1), 0)
    seq_len = jnp.where(row < lay.n_prompt, lay.seq, lay.dec_seq)
    pos = jnp.bitwise_and(row, seq_len - 1)
    u_prev = jnp.where(pos != 0, uu_scr[pl.ds(HALO - 1, TM), :], 0.0)
    u_next = jnp.where(pos != seq_len - 1, uu_scr[pl.ds(HALO + 1, TM), :], 0.0)
    y = u_prev * cw_ref[0:1, :] + uu_scr[pl.ds(HALO, TM), :] * cw_ref[1:2, :] + u_next * cw_ref[2:3, :]
    mix = _dot((bg * y).astype(BF16), wout_ref[...])
    o_ref[...] = _layer_norm(lay.alpha * x + ga * mix,
                             lng_ref[2 * layer:2 * layer + 1, :], lnb_ref[2 * layer:2 * layer + 1, :])


def _conv_mixer(lay, layer, j, x, mod, lng, lnb, w_in, cw, w_out):
    d = lay.d
    n_halo_blocks = lay.n_tok // HALO
    per = TM // HALO
    return pl.pallas_call(
        functools.partial(_conv_kernel, lay, layer),
        out_shape=jax.ShapeDtypeStruct((lay.n_tok, d), F32),
        grid=(lay.n_tok // TM,),
        in_specs=[
            pl.BlockSpec((TM, d), lambda i: (i, 0)),
            pl.BlockSpec((HALO, d), lambda i: (jnp.maximum(i * per - 1, 0), 0)),
            pl.BlockSpec((HALO, d), lambda i: (jnp.minimum((i + 1) * per, n_halo_blocks - 1), 0)),
            pl.BlockSpec((None, None, 6, d), lambda i: (layer, lay.mod_index(i, TM), 0, 0)),
            _resident(lng.shape, lambda i: (0, 0)),
            _resident(lnb.shape, lambda i: (0, 0)),
            _resident((None, d, 3 * d), lambda i: (j, 0, 0)),
            _resident((None, CONV_WIDTH, d), lambda i: (j, 0, 0)),
            _resident((None, d, d), lambda i: (j, 0, 0)),
        ],
        out_specs=pl.BlockSpec((TM, d), lambda i: (i, 0)),
        scratch_shapes=[pltpu.VMEM((TM + 2 * HALO, d), BF16), pltpu.VMEM((TM + 2 * HALO, d), F32)],
        compiler_params=_params(("arbitrary",)),
        name=f"conv_mixer_{layer}",
    )(x, x, x, mod, lng, lnb, w_in, cw, w_out)


def _ffn_kernel(lay, layer, x_ref, mod_ref, lng_ref, lnb_ref, win_ref, wout_ref, o_ref):
    d_ff = wout_ref.shape[0]
    sh, sc, ga = mod_ref[3:4, :], mod_ref[4:5, :], mod_ref[5:6, :]
    x = x_ref[...]
    h = (x * (1.0 + sc) + sh).astype(BF16)
    g = _dot(h, win_ref[:, 0:d_ff])
    u = _dot(h, win_ref[:, d_ff:2 * d_ff])
    a = (g * jax.nn.sigmoid(g) * u).astype(BF16)
    y = _dot(a, wout_ref[...])
    o_ref[...] = _layer_norm(lay.alpha * x + ga * y,
                             lng_ref[2 * layer + 1:2 * layer + 2, :], lnb_ref[2 * layer + 1:2 * layer + 2, :])


def _ffn(lay, layer, x, mod, lng, lnb, w_in, w_out):
    d = lay.d
    d_ff = w_out.shape[1]
    return pl.pallas_call(
        functools.partial(_ffn_kernel, lay, layer),
        out_shape=jax.ShapeDtypeStruct((lay.n_tok, d), F32),
        grid=(lay.n_tok // TM,),
        in_specs=[
            pl.BlockSpec((TM, d), lambda i: (i, 0)),
            pl.BlockSpec((None, None, 6, d), lambda i: (layer, lay.mod_index(i, TM), 0, 0)),
            _resident(lng.shape, lambda i: (0, 0)),
            _resident(lnb.shape, lambda i: (0, 0)),
            _resident((None, d, 2 * d_ff), lambda i: (layer, 0, 0)),
            _resident((None, d_ff, d), lambda i: (layer, 0, 0)),
        ],
        out_specs=pl.BlockSpec((TM, d), lambda i: (i, 0)),
        compiler_params=_params(("arbitrary",)),
        name=f"ffn_{layer}",
    )(x, mod, lng, lnb, w_in, w_out)


def _qkv_kernel(lay, n_heads, x_ref, mod_ref, w_ref, qg_ref, kg_ref, cos_ref, sin_ref,
                q_ref, k_ref, v_ref, kf_ref, vf_ref):
    sh, sc = mod_ref[0:1, :], mod_ref[1:2, :]
    h = (x_ref[...] * (1.0 + sc) + sh).astype(BF16)
    qkv = _dot(h, w_ref[...])
    cos, sin = cos_ref[...], sin_ref[...]
    lane = lax.broadcasted_iota(jnp.int32, (1, HEAD_DIM), 1)
    first_half = jnp.bitwise_and(lane, HEAD_DIM // 4) == 0

    def norm_rope(t, g):
        t = t * lax.rsqrt(jnp.mean(t * t, axis=-1, keepdims=True) + RMS_EPS) * g
        partner = jnp.where(first_half,
                            pltpu.roll(t, HEAD_DIM - HEAD_DIM // 4, axis=1),
                            pltpu.roll(t, HEAD_DIM // 4, axis=1))
        return t * cos + partner * sin

    q_gain = qg_ref[...] * (HEAD_DIM ** -0.5)
    for hq in range(n_heads):
        sl = slice(hq * HEAD_DIM, (hq + 1) * HEAD_DIM)
        q_ref[:, sl] = norm_rope(qkv[:, sl], q_gain).astype(BF16)
    k0 = n_heads * HEAD_DIM
    v0 = k0 + N_KV_HEADS * HEAD_DIM
    for hk in range(N_KV_HEADS):
        sl = slice(hk * HEAD_DIM, (hk + 1) * HEAD_DIM)
        kh = norm_rope(qkv[:, k0 + hk * HEAD_DIM:k0 + (hk + 1) * HEAD_DIM], kg_ref[...])
        kf_ref[:, sl] = kh
        k_ref[:, sl] = kh.astype(BF16)
    v = qkv[:, v0:v0 + N_KV_HEADS * HEAD_DIM]
    vf_ref[...] = v
    v_ref[...] = v.astype(BF16)


def _attn_qkv(lay, layer, j, x, mod, w_qkv, q_gain, k_gain, cos_t, sin_t):
    d = lay.d
    n_qkv = w_qkv.shape[2]
    kv = N_KV_HEADS * HEAD_DIM
    n_heads = (n_qkv - 2 * kv) // HEAD_DIM
    row = lambda i: (i, 0)

    def rope_row(i):
        r0 = i * TM
        return (jnp.where(r0 < lay.n_prompt, 0, 1 + ((r0 - lay.n_prompt) % lay.dec_seq) // TM), 0)

    return pl.pallas_call(
        functools.partial(_qkv_kernel, lay, n_heads),
        out_shape=(jax.ShapeDtypeStruct((lay.n_tok, n_heads * HEAD_DIM), BF16),
                   jax.ShapeDtypeStruct((lay.n_tok, kv), BF16),
                   jax.ShapeDtypeStruct((lay.n_tok, kv), BF16),
                   jax.ShapeDtypeStruct((lay.n_tok, kv), F32),
                   jax.ShapeDtypeStruct((lay.n_tok, kv), F32)),
        grid=(lay.n_tok // TM,),
        in_specs=[
            pl.BlockSpec((TM, d), row),
            pl.BlockSpec((None, None, 6, d), lambda i: (layer, lay.mod_index(i, TM), 0, 0)),
            _resident((None, d, n_qkv), lambda i: (j, 0, 0)),
            _resident((None, 1, HEAD_DIM), lambda i: (j, 0, 0)),
            _resident((None, 1, HEAD_DIM), lambda i: (j, 0, 0)),
            pl.BlockSpec((TM, HEAD_DIM), rope_row),
            pl.BlockSpec((TM, HEAD_DIM), rope_row),
        ],
        out_specs=(pl.BlockSpec((TM, n_heads * HEAD_DIM), row), pl.BlockSpec((TM, kv), row),
                   pl.BlockSpec((TM, kv), row), pl.BlockSpec((TM, kv), row), pl.BlockSpec((TM, kv), row)),
        compiler_params=_params(("arbitrary",)),
        name=f"attn_qkv_{layer}",
    )(x, mod, w_qkv, q_gain, k_gain, cos_t, sin_t)


def _attn_kernel(lay, layer, n_heads, has_cache, *refs):
    if has_cache:
        (q_ref, k_ref, v_ref, kc_ref, vc_ref, x_ref, mod_ref, lng_ref, lnb_ref, wo_ref, o_ref, att_scr) = refs
    else:
        (q_ref, k_ref, v_ref, x_ref, mod_ref, lng_ref, lnb_ref, wo_ref, o_ref, att_scr) = refs
    group = n_heads // N_KV_HEADS
    for hk in range(N_KV_HEADS):
        ksl = slice(hk * HEAD_DIM, (hk + 1) * HEAD_DIM)
        for g in range(group):
            hq = hk * group + g
            qsl = slice(hq * HEAD_DIM, (hq + 1) * HEAD_DIM)
            qh = q_ref[:, qsl]
            s = _dot_nt(qh, k_ref[:, ksl])
            m = jnp.max(s, axis=-1, keepdims=True)
            if has_cache:
                s_c = _dot_nt(qh, kc_ref[:, ksl])
                m = jnp.maximum(m, jnp.max(s_c, axis=-1, keepdims=True))
            p = jnp.exp(s - m)
            l = jnp.sum(p, axis=-1, keepdims=True)
            o = _dot(p.astype(BF16), v_ref[:, ksl])
            if has_cache:
                p_c = jnp.exp(s_c - m)
                l = l + jnp.sum(p_c, axis=-1, keepdims=True)
                o = o + _dot(p_c.astype(BF16), vc_ref[:, ksl])
            att_scr[:, qsl] = (o / l).astype(BF16)
    ga = mod_ref[2:3, :]
    mix = _dot(att_scr[...], wo_ref[...])
    o_ref[...] = _layer_norm(lay.alpha * x_ref[...] + ga * mix,
                             lng_ref[2 * layer:2 * layer + 1, :], lnb_ref[2 * layer:2 * layer + 1, :])


def _attention(lay, layer, j, q, k, v, cache, x, mod, lng, lnb, w_o, *, row0, n_seq, seq, tq):
    d = lay.d
    n_heads = q.shape[1] // HEAD_DIM
    kv = N_KV_HEADS * HEAD_DIM
    has_cache = cache is not None
    q_blocks = seq // tq
    tile = lambda b, i: (row0 // tq + b * q_blocks + i, 0)
    seq_blk = lambda b, i: (row0 // seq + b, 0)
    in_specs = [pl.BlockSpec((tq, n_heads * HEAD_DIM), tile),
                pl.BlockSpec((seq, kv), seq_blk), pl.BlockSpec((seq, kv), seq_blk)]
    args = [q, k, v]
    if has_cache:
        kc, vc = cache
        past = kc.shape[0] // n_seq
        in_specs += [pl.BlockSpec((past, kv), lambda b, i: (b, 0))] * 2
        args += [kc, vc]
    x_index = len(args)
    in_specs += [
        pl.BlockSpec((tq, d), tile),
        pl.BlockSpec((None, None, 6, d), lambda b, i: (layer, lay.mod_index(row0 // tq + b * q_blocks + i, tq), 0, 0)),
        _resident(lng.shape, lambda b, i: (0, 0)),
        _resident(lnb.shape, lambda b, i: (0, 0)),
        _resident((None, n_heads * HEAD_DIM, d), lambda b, i: (j, 0, 0)),
    ]
    args += [x, mod, lng, lnb, w_o]

    return pl.pallas_call(
        functools.partial(_attn_kernel, lay, layer, n_heads, has_cache),
        out_shape=jax.ShapeDtypeStruct((lay.n_tok, d), F32),
        grid=(n_seq, q_blocks),
        in_specs=in_specs,
        out_specs=pl.BlockSpec((tq, d), tile),
        scratch_shapes=[pltpu.VMEM((tq, n_heads * HEAD_DIM), BF16)],
        input_output_aliases={x_index: 0},
        compiler_params=_params(("arbitrary", "arbitrary")),
        name=f"attention_{layer}_{'latent' if has_cache else 'context'}",
    )(*args)


def _split3(x):
    hi = x.astype(BF16)
    r = x - hi.astype(F32)
    mid = r.astype(BF16)
    lo = (r - mid.astype(F32)).astype(BF16)
    return hi, mid, lo


def _gla_proj_kernel(lay, hk, hv, x_ref, mod_ref, win_ref, w1_ref, w2_ref, bg_ref,
                     q_ref, k_ref, v_ref, og_ref, bf_ref, bb_ref):
    sh, sc = mod_ref[0:1, :], mod_ref[1:2, :]
    h = (x_ref[...] * (1.0 + sc) + sh).astype(BF16)
    proj = _dot(h, win_ref[...])
    dk = hk // GLA_HEADS
    q_ref[...] = proj[:, 0:hk] * (dk ** -0.5)
    k_ref[...] = proj[:, hk:2 * hk]
    v_ref[...] = proj[:, 2 * hk:2 * hk + hv].astype(BF16)
    og_ref[...] = proj[:, 2 * hk + hv:2 * hk + 2 * hv]
    z = _dot(_dot(h, w1_ref[...]).astype(BF16), w2_ref[...]) + bg_ref[...]
    log_gate = (jnp.minimum(z, 0.0) - jnp.log1p(jnp.exp(-jnp.abs(z)))) * (1.0 / GLA_TAU)
    r = lax.broadcasted_iota(jnp.int32, (GLA_CHUNK, GLA_CHUNK), 0)
    c = lax.broadcasted_iota(jnp.int32, (GLA_CHUNK, GLA_CHUNK), 1)
    lower = jnp.where(c <= r, 1.0, 0.0).astype(BF16)
    upper = jnp.where(c >= r, 1.0, 0.0).astype(BF16)
    for ch in range(TM // GLA_CHUNK):
        rows = slice(ch * GLA_CHUNK, (ch + 1) * GLA_CHUNK)
        f_hi, f_mid, f_lo = _split3(log_gate[rows, 0:hk])
        bf_ref[rows, :] = _dot(lower, f_hi) + _dot(lower, f_mid) + _dot(lower, f_lo)
        b_hi, b_mid, b_lo = _split3(log_gate[rows, hk:2 * hk])
        bb_ref[rows, :] = _dot(upper, b_hi) + _dot(upper, b_mid) + _dot(upper, b_lo)


def _gla_proj(lay, layer, j, x, mod, w_in, w1, w2, b_gate):
    d = lay.d
    n_in = w_in.shape[2]
    hk = w2.shape[2] // 2
    hv = (n_in - 2 * hk) // 2
    rank2 = w1.shape[2]
    row = lambda i: (i, 0)
    return pl.pallas_call(
        functools.partial(_gla_proj_kernel, lay, hk, hv),
        out_shape=(jax.ShapeDtypeStruct((lay.n_tok, hk), F32),
                   jax.ShapeDtypeStruct((lay.n_tok, hk), F32),
                   jax.ShapeDtypeStruct((lay.n_tok, hv), BF16),
                   jax.ShapeDtypeStruct((lay.n_tok, hv), F32),
                   jax.ShapeDtypeStruct((lay.n_tok, hk), F32),
                   jax.ShapeDtypeStruct((lay.n_tok, hk), F32)),
        grid=(lay.n_tok // TM,),
        in_specs=[
            pl.BlockSpec((TM, d), row),
            pl.BlockSpec((None, None, 6, d), lambda i: (layer, lay.mod_index(i, TM), 0, 0)),
            _resident((None, d, n_in), lambda i: (j, 0, 0)),
            _resident((None, d, rank2), lambda i: (j, 0, 0)),
            _resident((None, rank2, 2 * hk), lambda i: (j, 0, 0)),
            _resident((None, 1, 2 * hk), lambda i: (j, 0, 0)),
        ],
        out_specs=(pl.BlockSpec((TM, hk), row), pl.BlockSpec((TM, hk), row), pl.BlockSpec((TM, hv), row),
                   pl.BlockSpec((TM, hv), row), pl.BlockSpec((TM, hk), row), pl.BlockSpec((TM, hk), row)),
        compiler_params=_params(("arbitrary",)),
        name=f"gla_proj_{layer}",
    )(x, mod, w_in, w1, w2, b_gate)


def _decay_rows_to_cols(row, n_cols):
    dk = row.shape[1]
    sq = jnp.broadcast_to(row, (dk, dk)).T
    return jnp.concatenate([sq] * (n_cols // dk), axis=1)


def _gla_chunk_fast(q, k, v, b, st, forward):
    mid = GLA_CHUNK // 2
    rho = b[mid:mid + 1]
    q_s = q * jnp.exp(b - rho)
    k_s = k * jnp.exp(rho - b)
    a = _dot_nt(q_s.astype(BF16), k_s.astype(BF16))
    r = lax.broadcasted_iota(jnp.int32, (GLA_CHUNK, GLA_CHUNK), 0)
    c = lax.broadcasted_iota(jnp.int32, (GLA_CHUNK, GLA_CHUNK), 1)
    a = jnp.where((c <= r) if forward else (c >= r), a, 0.0).astype(BF16)
    q_i = (q_s * jnp.exp(rho)).astype(BF16)
    o = _dot(a, v) + _dot(q_i, st.astype(BF16))
    b_exit = b[GLA_CHUNK - 1:GLA_CHUNK] if forward else b[0:1]
    k_hat = (k_s * jnp.exp(b_exit - rho)).astype(BF16)
    st_new = st * _decay_rows_to_cols(jnp.exp(b_exit), st.shape[1]) + _dot_tn(k_hat, v)
    return o, st_new


def _gla_chunk_safe(q, k, v, b, st, forward):
    n_sub = GLA_CHUNK // GLA_SUB
    o_rows = []
    qi = (q * jnp.exp(b)).astype(BF16)
    o_inter = _dot(qi, st.astype(BF16))
    t_idx = lax.broadcasted_iota(jnp.int32, (GLA_SUB, 1), 0)
    for blk in range(n_sub):
        rows = slice(blk * GLA_SUB, (blk + 1) * GLA_SUB)
        qb, kb, bb_, vb = q[rows], k[rows], b[rows], v[rows].astype(F32)
        o_blk = o_inter[rows]
        if forward and blk > 0:
            others = slice(0, blk * GLA_SUB)
            rho = b[blk * GLA_SUB - 1:blk * GLA_SUB]
        elif (not forward) and blk < n_sub - 1:
            others = slice((blk + 1) * GLA_SUB, GLA_CHUNK)
            rho = b[(blk + 1) * GLA_SUB:(blk + 1) * GLA_SUB + 1]
        else:
            others = None
        if others is not None:
            q_s = (qb * jnp.exp(bb_ - rho)).astype(BF16)
            k_s = (k[others] * jnp.exp(rho - b[others])).astype(BF16)
            a = _dot_nt(q_s, k_s)
            o_blk = o_blk + _dot(a.astype(BF16), v[others])
        for s in range(GLA_SUB):
            keep = (t_idx >= s) if forward else (t_idx <= s)
            decay = jnp.exp(jnp.where(keep, bb_ - bb_[s:s + 1], NEG_BIG))
            a_col = jnp.sum(qb * kb[s:s + 1] * decay, axis=-1, keepdims=True)
            o_blk = o_blk + a_col * vb[s:s + 1]
        o_rows.append(o_blk)
    b_exit = b[GLA_CHUNK - 1:GLA_CHUNK] if forward else b[0:1]
    k_hat = (k * jnp.exp(b_exit - b)).astype(BF16)
    st_new = st * _decay_rows_to_cols(jnp.exp(b_exit), st.shape[1]) + _dot_tn(k_hat, v)
    return jnp.concatenate(o_rows, axis=0), st_new


def _gla_scan_kernel(n_chunks, hps, dk, dv, has_init, *refs):
    if has_init:
        (q_ref, k_ref, v_ref, bf_ref, bb_ref, s0f_ref, s0b_ref, o_ref, st_scr) = refs
        for h in range(hps):
            st_scr[2 * h] = s0f_ref[h]
            st_scr[2 * h + 1] = s0b_ref[h]
    else:
        (q_ref, k_ref, v_ref, bf_ref, bb_ref, o_ref, sf_ref, sb_ref, st_scr) = refs
        st_scr[...] = jnp.zeros_like(st_scr)

    def step(chunk_fn, i, accumulate):
        for h in range(hps):
            kcols = slice(h * dk, (h + 1) * dk)
            vcols = slice(h * dv, (h + 1) * dv)
            for forward, b_ref in ((True, bf_ref), (False, bb_ref)):
                c = i if forward else n_chunks - 1 - i
                rows = pl.ds(pl.multiple_of(c * GLA_CHUNK, GLA_CHUNK), GLA_CHUNK)
                slot = 2 * h + (0 if forward else 1)
                o, st_new = chunk_fn(q_ref[rows, kcols], k_ref[rows, kcols], v_ref[rows, vcols],
                                     b_ref[rows, kcols], st_scr[slot], forward)
                st_scr[slot] = st_new
                if accumulate:
                    o_ref[rows, vcols] += o
                else:
                    o_ref[rows, vcols] = o

    def run(chunk_fn):
        def first(i, carry):
            step(chunk_fn, i, False)
            return carry

        def second(i, carry):
            step(chunk_fn, i, True)
            return carry

        lax.fori_loop(0, n_chunks // 2, first, 0)
        lax.fori_loop(n_chunks // 2, n_chunks, second, 0)

    span = jnp.zeros((1, hps * dk), F32)
    for c in range(n_chunks):
        top = slice(c * GLA_CHUNK, c * GLA_CHUNK + 1)
        bottom = slice((c + 1) * GLA_CHUNK - 1, (c + 1) * GLA_CHUNK)
        span = jnp.maximum(span, jnp.maximum(bf_ref[top, :] - bf_ref[bottom, :],
                                             bb_ref[bottom, :] - bb_ref[top, :]))
    bounded = jnp.max(span) < GLA_SAFE_RANGE

    @pl.when(bounded)
    def _():
        run(_gla_chunk_fast)

    @pl.when(jnp.logical_not(bounded))
    def _():
        run(_gla_chunk_safe)

    if not has_init:
        for h in range(hps):
            sf_ref[h] = st_scr[2 * h]
            sb_ref[h] = st_scr[2 * h + 1]


def _gla_scan(lay, j, q, k, v, bf, bb, init, *, row0, n_seq, seq, hps, o_alias):
    hk, hv = q.shape[1], v.shape[1]
    dk, dv = hk // GLA_HEADS, hv // GLA_HEADS
    n_chunks = seq // GLA_CHUNK
    groups = GLA_HEADS // hps
    assert n_chunks % 2 == 0 and GLA_HEADS % hps == 0
    has_init = init is not None
    blk = lambda b, g: (row0 // seq + b, g)
    kspec = pl.BlockSpec((seq, hps * dk), blk)
    vspec = pl.BlockSpec((seq, hps * dv), blk)
    in_specs = [kspec, kspec, vspec, kspec, kspec]
    args = [q, k, v, bf, bb]
    o_shape = jax.ShapeDtypeStruct(o_alias.shape, F32) if o_alias is not None else \
        jax.ShapeDtypeStruct((q.shape[0], hv), F32)
    if has_init:
        s0f, s0b = init
        st_spec = pl.BlockSpec((None, hps, dk, dv), lambda b, g: (b, j * groups + g, 0, 0))
        in_specs += [st_spec, st_spec]
        args += [s0f, s0b]
        out_shape, out_specs = o_shape, vspec
    else:
        st_shape = jax.ShapeDtypeStruct((n_seq, GLA_HEADS, dk, dv), F32)
        st_spec = pl.BlockSpec((None, hps, dk, dv), lambda b, g: (b, g, 0, 0))
        out_shape, out_specs = (o_shape, st_shape, st_shape), (vspec, st_spec, st_spec)
    aliases = {}
    n_in = len(args)
    if o_alias is not None:
        in_specs.append(pl.BlockSpec(memory_space=pl.ANY))
        args.append(o_alias)
        aliases = {n_in: 0}

    def body(*refs):
        refs = refs[:n_in] + refs[len(args):]
        _gla_scan_kernel(n_chunks, hps, dk, dv, has_init, *refs)

    return pl.pallas_call(
        body,
        out_shape=out_shape,
        grid=(n_seq, groups),
        in_specs=in_specs,
        out_specs=out_specs,
        scratch_shapes=[pltpu.VMEM((2 * hps, dk, dv), F32)],
        input_output_aliases=aliases,
        compiler_params=_params(("arbitrary", "arbitrary")),
        name=f"gla_scan_{'latent' if has_init else 'context'}",
    )(*args)


def _gla_out_kernel(lay, layer, o_ref, og_ref, x_ref, mod_ref, lng_ref, lnb_ref, ng_ref, wo_ref,
                    out_ref, z_scr):
    dv = ng_ref.shape[1]
    for h in range(GLA_HEADS):
        sl = slice(h * dv, (h + 1) * dv)
        o = o_ref[:, sl]
        og = og_ref[:, sl]
        o = o * lax.rsqrt(jnp.mean(o * o, axis=-1, keepdims=True) + RMS_EPS) * ng_ref[...]
        z_scr[:, sl] = (o * (og * jax.nn.sigmoid(og))).astype(BF16)
    ga = mod_ref[2:3, :]
    mix = _dot(z_scr[...], wo_ref[...])
    out_ref[...] = _layer_norm(lay.alpha * x_ref[...] + ga * mix,
                               lng_ref[2 * layer:2 * layer + 1, :], lnb_ref[2 * layer:2 * layer + 1, :])


def _gla_out(lay, layer, j, o, og, x, mod, lng, lnb, norm_g, w_o):
    d = lay.d
    hv = o.shape[1]
    dv = hv // GLA_HEADS
    row = lambda i: (i, 0)
    return pl.pallas_call(
        functools.partial(_gla_out_kernel, lay, layer),
        out_shape=jax.ShapeDtypeStruct((lay.n_tok, d), F32),
        grid=(lay.n_tok // TM,),
        in_specs=[
            pl.BlockSpec((TM, hv), row), pl.BlockSpec((TM, hv), row), pl.BlockSpec((TM, d), row),
            pl.BlockSpec((None, None, 6, d), lambda i: (layer, lay.mod_index(i, TM), 0, 0)),
            _resident(lng.shape, lambda i: (0, 0)),
            _resident(lnb.shape, lambda i: (0, 0)),
            _resident((None, 1, dv), lambda i: (j, 0, 0)),
            _resident((None, hv, d), lambda i: (j, 0, 0)),
        ],
        out_specs=pl.BlockSpec((TM, d), row),
        scratch_shapes=[pltpu.VMEM((TM, hv), BF16)],
        compiler_params=_params(("arbitrary",)),
        name=f"gla_out_{layer}",
    )(o, og, x, mod, lng, lnb, norm_g, w_o)


def _rope_tables(lay):
    n_freq = HEAD_DIM // 4
    pos = jnp.arange(lay.dec_seq)
    freqs = ROPE_THETA ** (-jnp.arange(n_freq, dtype=F32) / n_freq)
    ang_r = (pos // GRID_W).astype(F32)[:, None] * freqs
    ang_c = (pos % GRID_W).astype(F32)[:, None] * freqs
    cos = jnp.concatenate([jnp.cos(ang_r)] * 2 + [jnp.cos(ang_c)] * 2, axis=-1)
    sin = jnp.concatenate([-jnp.sin(ang_r), jnp.sin(ang_r), -jnp.sin(ang_c), jnp.sin(ang_c)], axis=-1)
    cos = jnp.concatenate([jnp.ones((TM, HEAD_DIM), F32), cos], axis=0)
    sin = jnp.concatenate([jnp.zeros((TM, HEAD_DIM), F32), sin], axis=0)
    return cos, sin


def kernel(x_prompt, x_sample, c, cache_k, cache_v, state_gla_fwd, state_gla_bwd, c_ctx, w_ada, b_ada, ln_g, ln_b, conv_w_in, conv_w, conv_w_out, attn_w_qkv, attn_q_norm, attn_k_norm, attn_w_o, gla_w_in, gla_w_gate1, gla_w_gate2, gla_b_gate, gla_norm, gla_w_o, ffn_w_in, ffn_w_out):
    batch, seq, d = x_prompt.shape
    dec_batch, dec_seq, _ = x_sample.shape
    depth = w_ada.shape[0]
    lay = _Layout(batch, seq, dec_batch, dec_seq, d, depth)
    kv = N_KV_HEADS * HEAD_DIM
    past = cache_k.shape[2]

    x = jnp.concatenate([x_prompt.reshape(lay.n_prompt, d), x_sample.reshape(dec_batch * dec_seq, d)], axis=0)
    cond = jnp.concatenate([c_ctx[None, :], c, jnp.zeros((MOD_ROWS - 1 - dec_batch, d), F32)], axis=0)
    mod = _modulation(cond, w_ada, b_ada).reshape(depth, MOD_ROWS, 6, d)
    lng = ln_g.reshape(depth * 2, d)
    lnb = ln_b.reshape(depth * 2, d)

    conv_w_in_b, conv_w_out_b = conv_w_in.astype(BF16), conv_w_out.astype(BF16)
    attn_w_qkv_b, attn_w_o_b = attn_w_qkv.astype(BF16), attn_w_o.astype(BF16)
    gla_w_in_b, gla_w_o_b = gla_w_in.astype(BF16), gla_w_o.astype(BF16)
    ffn_w_in_b, ffn_w_out_b = ffn_w_in.astype(BF16), ffn_w_out.astype(BF16)
    rank = gla_w_gate1.shape[3]
    hk = gla_w_gate2.shape[3]
    g1 = jnp.concatenate([gla_w_gate1[:, 0], gla_w_gate1[:, 1]], axis=-1).astype(BF16)
    zeros = jnp.zeros_like(gla_w_gate2[:, 0])
    g2 = jnp.concatenate([jnp.concatenate([gla_w_gate2[:, 0], zeros], axis=-1),
                          jnp.concatenate([zeros, gla_w_gate2[:, 1]], axis=-1)], axis=1).astype(BF16)
    gb = jnp.concatenate([gla_b_gate[:, 0], gla_b_gate[:, 1]], axis=-1)[:, None, :]
    cos_t, sin_t = _rope_tables(lay)

    new_k, new_v, new_sf, new_sb = [], [], [], []
    for i in range(depth):
        kind, j = i % N_MIXERS, i // N_MIXERS
        if kind == 0:
            x = _conv_mixer(lay, i, j, x, mod, lng, lnb, conv_w_in_b, conv_w, conv_w_out_b)
        elif kind == 1:
            q, k, v, kf, vf = _attn_qkv(lay, i, j, x, mod, attn_w_qkv_b,
                                        attn_q_norm[:, None, :], attn_k_norm[:, None, :], cos_t, sin_t)
            new_k.append(kf[:lay.n_prompt].reshape(batch, seq, N_KV_HEADS, HEAD_DIM))
            new_v.append(vf[:lay.n_prompt].reshape(batch, seq, N_KV_HEADS, HEAD_DIM))
            kc = cache_k[:, j].reshape(dec_batch * past, kv).astype(BF16)
            vc = cache_v[:, j].reshape(dec_batch * past, kv).astype(BF16)
            x = _attention(lay, i, j, q, k, v, None, x, mod, lng, lnb, attn_w_o_b,
                           row0=0, n_seq=batch, seq=seq, tq=seq)
            x = _attention(lay, i, j, q, k, v, (kc, vc), x, mod, lng, lnb, attn_w_o_b,
                           row0=lay.n_prompt, n_seq=dec_batch, seq=dec_seq, tq=TM)
        else:
            q, k, v, og, bf, bb = _gla_proj(lay, i, j, x, mod, gla_w_in_b, g1, g2, gb)
            s0f = state_gla_fwd.reshape(dec_batch, -1, *state_gla_fwd.shape[3:])
            s0b = state_gla_bwd.reshape(dec_batch, -1, *state_gla_bwd.shape[3:])
            o, sf, sb = _gla_scan(lay, j, q, k, v, bf, bb, None, row0=0, n_seq=batch, seq=seq,
                                  hps=GLA_HEADS_PER_STEP_CONTEXT, o_alias=None)
            o = _gla_scan(lay, j, q, k, v, bf, bb, (s0f, s0b), row0=lay.n_prompt, n_seq=dec_batch,
                          seq=dec_seq, hps=GLA_HEADS_PER_STEP_LATENT, o_alias=o)
            new_sf.append(sf)
            new_sb.append(sb)
            x = _gla_out(lay, i, j, o, og, x, mod, lng, lnb, gla_norm[:, None, :], gla_w_o_b)
        x = _ffn(lay, i, x, mod, lng, lnb, ffn_w_in_b, ffn_w_out_b)

    y_prompt = x[:lay.n_prompt].reshape(batch, seq, d)
    y_sample = x[lay.n_prompt:].reshape(dec_batch, dec_seq, d)
    return (y_prompt, y_sample, jnp.stack(new_k, axis=1), jnp.stack(new_v, axis=1),
            jnp.stack(new_sf, axis=1), jnp.stack(new_sb, axis=1))
```

```python
import functools

import jax
import jax.numpy as jnp
from jax import lax
from jax.experimental import pallas as pl
from jax.experimental.pallas import tpu as pltpu

F32 = jnp.float32
BF16 = jnp.bfloat16

N_MIXERS = 3
CONV_WIDTH = 3
HEAD_DIM = 128
N_KV_HEADS = 2
GRID_W = 64
ROPE_THETA = 10000.0
GLA_HEADS = 4
GLA_TAU = 16.0
GLA_CHUNK = 64
LN_EPS = 1e-5
RMS_EPS = 1e-6

LANES = 128
BF16_SUBLANES = 16
VMEM_LIMIT = 56 * 1024 * 1024

MOD_ROWS = 16
MOD_NT = 1536
TM = 512
HALO = BF16_SUBLANES
GLA_SUB = 16
GLA_SAFE_RANGE = 80.0
GLA_HEADS_PER_STEP_CONTEXT = 4
GLA_HEADS_PER_STEP_LATENT = 2
NEG_BIG = -1e30


def _dot(a, b):
    return jnp.dot(a, b, preferred_element_type=F32)


def _dot_nt(a, b):
    return lax.dot_general(a, b, (((1,), (1,)), ((), ())), preferred_element_type=F32)


def _dot_tn(a, b):
    return lax.dot_general(a, b, (((0,), (0,)), ((), ())), preferred_element_type=F32)


def _layer_norm(y, g, b):
    mu = jnp.mean(y, axis=-1, keepdims=True)
    yc = y - mu
    var = jnp.mean(yc * yc, axis=-1, keepdims=True)
    return yc * lax.rsqrt(var + LN_EPS) * g + b


def _params(semantics):
    return pltpu.CompilerParams(dimension_semantics=semantics, vmem_limit_bytes=VMEM_LIMIT)


def _resident(block_shape, index_map):
    return pl.BlockSpec(block_shape, index_map, pipeline_mode=pl.Buffered(1))


class _Layout:
    def __init__(self, batch, seq, dec_batch, dec_seq, d_model, depth):
        self.batch, self.seq, self.dec_batch, self.dec_seq = batch, seq, dec_batch, dec_seq
        self.d, self.depth = d_model, depth
        self.n_prompt = batch * seq
        self.n_tok = self.n_prompt + dec_batch * dec_seq
        assert self.n_prompt % TM == 0 and dec_seq % TM == 0 and TM % seq == 0
        assert seq & (seq - 1) == 0 and dec_seq & (dec_seq - 1) == 0
        self.alpha = (2.0 * depth) ** 0.25

    def mod_index(self, i, rows):
        r0 = i * rows
        return jnp.where(r0 < self.n_prompt, 0, 1 + (r0 - self.n_prompt) // self.dec_seq)


def _mod_kernel(cond_ref, w_ref, b_ref, o_ref):
    c = cond_ref[...]
    a = (c * jax.nn.sigmoid(c)).astype(BF16)
    o_ref[...] = _dot(a, w_ref[...].astype(BF16)) + b_ref[...]


def _modulation(cond, w_ada, b_ada):
    depth, d, n_out = w_ada.shape
    return pl.pallas_call(
        _mod_kernel,
        out_shape=jax.ShapeDtypeStruct((depth, MOD_ROWS, n_out), F32),
        grid=(depth, n_out // MOD_NT),
        in_specs=[
            pl.BlockSpec((MOD_ROWS, d), lambda l, j: (0, 0)),
            pl.BlockSpec((None, d, MOD_NT), lambda l, j: (l, 0, j)),
            pl.BlockSpec((None, 1, MOD_NT), lambda l, j: (l, 0, j)),
        ],
        out_specs=pl.BlockSpec((None, MOD_ROWS, MOD_NT), lambda l, j: (l, 0, j)),
        compiler_params=_params(("arbitrary", "arbitrary")),
        name="modulation",
    )(cond, w_ada, b_ada.reshape(depth, 1, n_out))


def _conv_kernel(lay, layer, split, *refs):
    d = lay.d
    i = pl.program_id(0)
    if split:
        (xca, xpa, xna, xcb, xpb, xnb, mod_ref, lng_ref, lnb_ref, win_ref, cw_ref, wout_ref,
         o_ref, h_scr, uu_scr) = refs
        is_context = i * TM < lay.n_prompt
        x = jnp.where(is_context, xca[...], xcb[...])
        x_prev = jnp.where(is_context, xpa[...], xpb[...])
        x_next = jnp.where(is_context, xna[...], xnb[...])
    else:
        (xc_ref, xp_ref, xn_ref, mod_ref, lng_ref, lnb_ref, win_ref, cw_ref, wout_ref,
         o_ref, h_scr, uu_scr) = refs
        x, x_prev, x_next = xc_ref[...], xp_ref[...], xn_ref[...]
    sh, sc, ga = mod_ref[0:1, :], mod_ref[1:2, :], mod_ref[2:3, :]
    one_sc = 1.0 + sc
    h_scr[0:HALO, :] = (x_prev * one_sc + sh).astype(BF16)
    h_scr[HALO:HALO + TM, :] = (x * one_sc + sh).astype(BF16)
    h_scr[HALO + TM:HALO + TM + HALO, :] = (x_next * one_sc + sh).astype(BF16)
    bg = _dot(h_scr[HALO:HALO + TM, :], win_ref[:, 0:d])
    cgu = _dot(h_scr[...], win_ref[:, d:3 * d])
    uu_scr[...] = cgu[:, 0:d] * cgu[:, d:2 * d]
    row = i * TM + lax.broadcasted_iota(jnp.int32, (TM, 1), 0)
    seq_len = jnp.where(row < lay.n_prompt, lay.seq, lay.dec_seq)
    pos = jnp.bitwise_and(row, seq_len - 1)
    u_prev = jnp.where(pos != 0, uu_scr[pl.ds(HALO - 1, TM), :], 0.0)
    u_next = jnp.where(pos != seq_len - 1, uu_scr[pl.ds(HALO + 1, TM), :], 0.0)
    y = u_prev * cw_ref[0:1, :] + uu_scr[pl.ds(HALO, TM), :] * cw_ref[1:2, :] + u_next * cw_ref[2:3, :]
    mix = _dot((bg * y).astype(BF16), wout_ref[...])
    o_ref[...] = _layer_norm(lay.alpha * x + ga * mix,
                             lng_ref[2 * layer:2 * layer + 1, :], lnb_ref[2 * layer:2 * layer + 1, :])


def _conv_mixer(lay, layer, j, x_parts, mod, lng, lnb, w_in, cw, w_out):
    d = lay.d
    per = TM // HALO
    x_specs, x_args = [], []
    tile0 = 0
    for part in x_parts:
        n_tiles = part.shape[0] // TM

        def center(i, t0=tile0, n=n_tiles):
            return (jnp.clip(i - t0, 0, n - 1), 0)

        def prev_halo(i, t0=tile0, n=n_tiles):
            return (jnp.clip((i - t0) * per - 1, 0, n * per - 1), 0)

        def next_halo(i, t0=tile0, n=n_tiles):
            return (jnp.clip((i - t0 + 1) * per, 0, n * per - 1), 0)

        x_specs += [pl.BlockSpec((TM, d), center), pl.BlockSpec((HALO, d), prev_halo),
                    pl.BlockSpec((HALO, d), next_halo)]
        x_args += [part, part, part]
        tile0 += n_tiles
    return pl.pallas_call(
        functools.partial(_conv_kernel, lay, layer, len(x_parts) == 2),
        out_shape=jax.ShapeDtypeStruct((lay.n_tok, d), F32),
        grid=(lay.n_tok // TM,),
        in_specs=x_specs + [
            pl.BlockSpec((None, None, 6, d), lambda i: (layer, lay.mod_index(i, TM), 0, 0)),
            _resident(lng.shape, lambda i: (0, 0)),
            _resident(lnb.shape, lambda i: (0, 0)),
            _resident((None, d, 3 * d), lambda i: (j, 0, 0)),
            _resident((None, CONV_WIDTH, d), lambda i: (j, 0, 0)),
            _resident((None, d, d), lambda i: (j, 0, 0)),
        ],
        out_specs=pl.BlockSpec((TM, d), lambda i: (i, 0)),
        scratch_shapes=[pltpu.VMEM((TM + 2 * HALO, d), BF16), pltpu.VMEM((TM + 2 * HALO, d), F32)],
        compiler_params=_params(("arbitrary",)),
        name=f"conv_mixer_{layer}",
    )(*x_args, mod, lng, lnb, w_in, cw, w_out)


def _ffn_kernel(lay, layer, x_ref, mod_ref, lng_ref, lnb_ref, win_ref, wout_ref, o_ref):
    d_ff = wout_ref.shape[0]
    sh, sc, ga = mod_ref[3:4, :], mod_ref[4:5, :], mod_ref[5:6, :]
    x = x_ref[...]
    h = (x * (1.0 + sc) + sh).astype(BF16)
    g = _dot(h, win_ref[:, 0:d_ff])
    u = _dot(h, win_ref[:, d_ff:2 * d_ff])
    a = (g * jax.nn.sigmoid(g) * u).astype(BF16)
    y = _dot(a, wout_ref[...])
    o_ref[...] = _layer_norm(lay.alpha * x + ga * y,
                             lng_ref[2 * layer + 1:2 * layer + 2, :], lnb_ref[2 * layer + 1:2 * layer + 2, :])


def _ffn(lay, layer, x, mod, lng, lnb, w_in, w_out):
    d = lay.d
    d_ff = w_out.shape[1]
    return pl.pallas_call(
        functools.partial(_ffn_kernel, lay, layer),
        out_shape=jax.ShapeDtypeStruct((lay.n_tok, d), F32),
        grid=(lay.n_tok // TM,),
        in_specs=[
            pl.BlockSpec((TM, d), lambda i: (i, 0)),
            pl.BlockSpec((None, None, 6, d), lambda i: (layer, lay.mod_index(i, TM), 0, 0)),
            _resident(lng.shape, lambda i: (0, 0)),
            _resident(lnb.shape, lambda i: (0, 0)),
            _resident((None, d, 2 * d_ff), lambda i: (layer, 0, 0)),
            _resident((None, d_ff, d), lambda i: (layer, 0, 0)),
        ],
        out_specs=pl.BlockSpec((TM, d), lambda i: (i, 0)),
        compiler_params=_params(("arbitrary",)),
        name=f"ffn_{layer}",
    )(x, mod, lng, lnb, w_in, w_out)


def _qkv_kernel(lay, n_heads, x_ref, mod_ref, w_ref, qg_ref, kg_ref, cos_ref, sin_ref,
                q_ref, k_ref, v_ref, kf_ref, vf_ref):
    sh, sc = mod_ref[0:1, :], mod_ref[1:2, :]
    h = (x_ref[...] * (1.0 + sc) + sh).astype(BF16)
    qkv = _dot(h, w_ref[...])
    cos, sin = cos_ref[...], sin_ref[...]
    lane = lax.broadcasted_iota(jnp.int32, (1, HEAD_DIM), 1)
    first_half = jnp.bitwise_and(lane, HEAD_DIM // 4) == 0

    def norm_rope(t, g):
        t = t * lax.rsqrt(jnp.mean(t * t, axis=-1, keepdims=True) + RMS_EPS) * g
        partner = jnp.where(first_half,
                            pltpu.roll(t, HEAD_DIM - HEAD_DIM // 4, axis=1),
                            pltpu.roll(t, HEAD_DIM // 4, axis=1))
        return t * cos + partner * sin

    q_gain = qg_ref[...] * (HEAD_DIM ** -0.5)
    for hq in range(n_heads):
        sl = slice(hq * HEAD_DIM, (hq + 1) * HEAD_DIM)
        q_ref[:, sl] = norm_rope(qkv[:, sl], q_gain).astype(BF16)
    k0 = n_heads * HEAD_DIM
    v0 = k0 + N_KV_HEADS * HEAD_DIM
    for hk in range(N_KV_HEADS):
        sl = slice(hk * HEAD_DIM, (hk + 1) * HEAD_DIM)
        kh = norm_rope(qkv[:, k0 + hk * HEAD_DIM:k0 + (hk + 1) * HEAD_DIM], kg_ref[...])
        kf_ref[:, sl] = kh
        k_ref[:, sl] = kh.astype(BF16)
    v = qkv[:, v0:v0 + N_KV_HEADS * HEAD_DIM]
    vf_ref[...] = v
    v_ref[...] = v.astype(BF16)


def _attn_qkv(lay, layer, j, x, mod, w_qkv, q_gain, k_gain, cos_t, sin_t):
    d = lay.d
    n_qkv = w_qkv.shape[2]
    kv = N_KV_HEADS * HEAD_DIM
    n_heads = (n_qkv - 2 * kv) // HEAD_DIM
    row = lambda i: (i, 0)

    def rope_row(i):
        r0 = i * TM
        return (jnp.where(r0 < lay.n_prompt, 0, 1 + ((r0 - lay.n_prompt) % lay.dec_seq) // TM), 0)

    return pl.pallas_call(
        functools.partial(_qkv_kernel, lay, n_heads),
        out_shape=(jax.ShapeDtypeStruct((lay.n_tok, n_heads * HEAD_DIM), BF16),
                   jax.ShapeDtypeStruct((lay.n_tok, kv), BF16),
                   jax.ShapeDtypeStruct((lay.n_tok, kv), BF16),
                   jax.ShapeDtypeStruct((lay.n_tok, kv), F32),
                   jax.ShapeDtypeStruct((lay.n_tok, kv), F32)),
        grid=(lay.n_tok // TM,),
        in_specs=[
            pl.BlockSpec((TM, d), row),
            pl.BlockSpec((None, None, 6, d), lambda i: (layer, lay.mod_index(i, TM), 0, 0)),
            _resident((None, d, n_qkv), lambda i: (j, 0, 0)),
            _resident((None, 1, HEAD_DIM), lambda i: (j, 0, 0)),
            _resident((None, 1, HEAD_DIM), lambda i: (j, 0, 0)),
            pl.BlockSpec((TM, HEAD_DIM), rope_row),
            pl.BlockSpec((TM, HEAD_DIM), rope_row),
        ],
        out_specs=(pl.BlockSpec((TM, n_heads * HEAD_DIM), row), pl.BlockSpec((TM, kv), row),
                   pl.BlockSpec((TM, kv), row), pl.BlockSpec((TM, kv), row), pl.BlockSpec((TM, kv), row)),
        compiler_params=_params(("arbitrary",)),
        name=f"attn_qkv_{layer}",
    )(x, mod, w_qkv, q_gain, k_gain, cos_t, sin_t)


def _attn_kernel(lay, layer, n_heads, has_cache, *refs):
    if has_cache:
        (q_ref, k_ref, v_ref, kc_ref, vc_ref, x_ref, mod_ref, lng_ref, lnb_ref, wo_ref, o_ref, att_scr) = refs
    else:
        (q_ref, k_ref, v_ref, x_ref, mod_ref, lng_ref, lnb_ref, wo_ref, o_ref, att_scr) = refs
    group = n_heads // N_KV_HEADS
    for hk in range(N_KV_HEADS):
        ksl = slice(hk * HEAD_DIM, (hk + 1) * HEAD_DIM)
        for g in range(group):
            hq = hk * group + g
            qsl = slice(hq * HEAD_DIM, (hq + 1) * HEAD_DIM)
            qh = q_ref[:, qsl]
            s = _dot_nt(qh, k_ref[:, ksl])
            m = jnp.max(s, axis=-1, keepdims=True)
            if has_cache:
                s_c = _dot_nt(qh, kc_ref[:, ksl])
                m = jnp.maximum(m, jnp.max(s_c, axis=-1, keepdims=True))
            p = jnp.exp(s - m)
            l = jnp.sum(p, axis=-1, keepdims=True)
            o = _dot(p.astype(BF16), v_ref[:, ksl])
            if has_cache:
                p_c = jnp.exp(s_c - m)
                l = l + jnp.sum(p_c, axis=-1, keepdims=True)
                o = o + _dot(p_c.astype(BF16), vc_ref[:, ksl])
            att_scr[:, qsl] = (o / l).astype(BF16)
    ga = mod_ref[2:3, :]
    mix = _dot(att_scr[...], wo_ref[...])
    o_ref[...] = _layer_norm(lay.alpha * x_ref[...] + ga * mix,
                             lng_ref[2 * layer:2 * layer + 1, :], lnb_ref[2 * layer:2 * layer + 1, :])


def _attention(lay, layer, j, q, k, v, cache, x, mod, lng, lnb, w_o, *, row0, n_seq, seq, tq):
    d = lay.d
    n_heads = q.shape[1] // HEAD_DIM
    kv = N_KV_HEADS * HEAD_DIM
    has_cache = cache is not None
    q_blocks = seq // tq
    tile = lambda b, i: (row0 // tq + b * q_blocks + i, 0)
    seq_blk = lambda b, i: (row0 // seq + b, 0)
    in_specs = [pl.BlockSpec((tq, n_heads * HEAD_DIM), tile),
                pl.BlockSpec((seq, kv), seq_blk), pl.BlockSpec((seq, kv), seq_blk)]
    args = [q, k, v]
    if has_cache:
        kc, vc = cache
        past = kc.shape[0] // n_seq
        in_specs += [pl.BlockSpec((past, kv), lambda b, i: (b, 0))] * 2
        args += [kc, vc]
    x_index = len(args)
    in_specs += [
        pl.BlockSpec((tq, d), tile),
        pl.BlockSpec((None, None, 6, d), lambda b, i: (layer, lay.mod_index(row0 // tq + b * q_blocks + i, tq), 0, 0)),
        _resident(lng.shape, lambda b, i: (0, 0)),
        _resident(lnb.shape, lambda b, i: (0, 0)),
        _resident((None, n_heads * HEAD_DIM, d), lambda b, i: (j, 0, 0)),
    ]
    args += [x, mod, lng, lnb, w_o]

    return pl.pallas_call(
        functools.partial(_attn_kernel, lay, layer, n_heads, has_cache),
        out_shape=jax.ShapeDtypeStruct((lay.n_tok, d), F32),
        grid=(n_seq, q_blocks),
        in_specs=in_specs,
        out_specs=pl.BlockSpec((tq, d), tile),
        scratch_shapes=[pltpu.VMEM((tq, n_heads * HEAD_DIM), BF16)],
        input_output_aliases={x_index: 0},
        compiler_params=_params(("arbitrary", "arbitrary")),
        name=f"attention_{layer}_{'latent' if has_cache else 'context'}",
    )(*args)


def _split3(x):
    hi = x.astype(BF16)
    r = x - hi.astype(F32)
    mid = r.astype(BF16)
    lo = (r - mid.astype(F32)).astype(BF16)
    return hi, mid, lo


def _gla_proj_kernel(lay, hk, hv, x_ref, mod_ref, win_ref, w1_ref, w2_ref, bg_ref,
                     q_ref, k_ref, v_ref, og_ref, bf_ref, bb_ref):
    sh, sc = mod_ref[0:1, :], mod_ref[1:2, :]
    h = (x_ref[...] * (1.0 + sc) + sh).astype(BF16)
    proj = _dot(h, win_ref[...])
    dk = hk // GLA_HEADS
    q_ref[...] = proj[:, 0:hk] * (dk ** -0.5)
    k_ref[...] = proj[:, hk:2 * hk]
    v_ref[...] = proj[:, 2 * hk:2 * hk + hv].astype(BF16)
    og_ref[...] = proj[:, 2 * hk + hv:2 * hk + 2 * hv]
    z = _dot(_dot(h, w1_ref[...]).astype(BF16), w2_ref[...]) + bg_ref[...]
    log_gate = (jnp.minimum(z, 0.0) - jnp.log1p(jnp.exp(-jnp.abs(z)))) * (1.0 / GLA_TAU)
    r = lax.broadcasted_iota(jnp.int32, (GLA_CHUNK, GLA_CHUNK), 0)
    c = lax.broadcasted_iota(jnp.int32, (GLA_CHUNK, GLA_CHUNK), 1)
    lower = jnp.where(c <= r, 1.0, 0.0).astype(BF16)
    upper = jnp.where(c >= r, 1.0, 0.0).astype(BF16)
    for ch in range(TM // GLA_CHUNK):
        rows = slice(ch * GLA_CHUNK, (ch + 1) * GLA_CHUNK)
        f_hi, f_mid, f_lo = _split3(log_gate[rows, 0:hk])
        bf_ref[rows, :] = _dot(lower, f_hi) + _dot(lower, f_mid) + _dot(lower, f_lo)
        b_hi, b_mid, b_lo = _split3(log_gate[rows, hk:2 * hk])
        bb_ref[rows, :] = _dot(upper, b_hi) + _dot(upper, b_mid) + _dot(upper, b_lo)


def _gla_proj(lay, layer, j, x, mod, w_in, w1, w2, b_gate):
    d = lay.d
    n_in = w_in.shape[2]
    hk = w2.shape[2] // 2
    hv = (n_in - 2 * hk) // 2
    rank2 = w1.shape[2]
    row = lambda i: (i, 0)
    return pl.pallas_call(
        functools.partial(_gla_proj_kernel, lay, hk, hv),
        out_shape=(jax.ShapeDtypeStruct((lay.n_tok, hk), F32),
                   jax.ShapeDtypeStruct((lay.n_tok, hk), F32),
                   jax.ShapeDtypeStruct((lay.n_tok, hv), BF16),
                   jax.ShapeDtypeStruct((lay.n_tok, hv), F32),
                   jax.ShapeDtypeStruct((lay.n_tok, hk), F32),
                   jax.ShapeDtypeStruct((lay.n_tok, hk), F32)),
        grid=(lay.n_tok // TM,),
        in_specs=[
            pl.BlockSpec((TM, d), row),
            pl.BlockSpec((None, None, 6, d), lambda i: (layer, lay.mod_index(i, TM), 0, 0)),
            _resident((None, d, n_in), lambda i: (j, 0, 0)),
            _resident((None, d, rank2), lambda i: (j, 0, 0)),
            _resident((None, rank2, 2 * hk), lambda i: (j, 0, 0)),
            _resident((None, 1, 2 * hk), lambda i: (j, 0, 0)),
        ],
        out_specs=(pl.BlockSpec((TM, hk), row), pl.BlockSpec((TM, hk), row), pl.BlockSpec((TM, hv), row),
                   pl.BlockSpec((TM, hv), row), pl.BlockSpec((TM, hk), row), pl.BlockSpec((TM, hk), row)),
        compiler_params=_params(("arbitrary",)),
        name=f"gla_proj_{layer}",
    )(x, mod, w_in, w1, w2, b_gate)


def _decay_rows_to_cols(row, n_cols):
    dk = row.shape[1]
    sq = jnp.broadcast_to(row, (dk, dk)).T
    return jnp.concatenate([sq] * (n_cols // dk), axis=1)


def _gla_chunk_fast(q, k, v, b, st, forward):
    mid = GLA_CHUNK // 2
    rho = b[mid:mid + 1]
    q_s = q * jnp.exp(b - rho)
    k_s = k * jnp.exp(rho - b)
    a = _dot_nt(q_s.astype(BF16), k_s.astype(BF16))
    r = lax.broadcasted_iota(jnp.int32, (GLA_CHUNK, GLA_CHUNK), 0)
    c = lax.broadcasted_iota(jnp.int32, (GLA_CHUNK, GLA_CHUNK), 1)
    a = jnp.where((c <= r) if forward else (c >= r), a, 0.0).astype(BF16)
    q_i = (q_s * jnp.exp(rho)).astype(BF16)
    o = _dot(a, v) + _dot(q_i, st.astype(BF16))
    b_exit = b[GLA_CHUNK - 1:GLA_CHUNK] if forward else b[0:1]
    k_hat = (k_s * jnp.exp(b_exit - rho)).astype(BF16)
    st_new = st * _decay_rows_to_cols(jnp.exp(b_exit), st.shape[1]) + _dot_tn(k_hat, v)
    return o, st_new


def _gla_chunk_safe(q, k, v, b, st, forward):
    n_sub = GLA_CHUNK // GLA_SUB
    o_rows = []
    qi = (q * jnp.exp(b)).astype(BF16)
    o_inter = _dot(qi, st.astype(BF16))
    t_idx = lax.broadcasted_iota(jnp.int32, (GLA_SUB, 1), 0)
    for blk in range(n_sub):
        rows = slice(blk * GLA_SUB, (blk + 1) * GLA_SUB)
        qb, kb, bb_, vb = q[rows], k[rows], b[rows], v[rows].astype(F32)
        o_blk = o_inter[rows]
        if forward and blk > 0:
            others = slice(0, blk * GLA_SUB)
            rho = b[blk * GLA_SUB - 1:blk * GLA_SUB]
        elif (not forward) and blk < n_sub - 1:
            others = slice((blk + 1) * GLA_SUB, GLA_CHUNK)
            rho = b[(blk + 1) * GLA_SUB:(blk + 1) * GLA_SUB + 1]
        else:
            others = None
        if others is not None:
            q_s = (qb * jnp.exp(bb_ - rho)).astype(BF16)
            k_s = (k[others] * jnp.exp(rho - b[others])).astype(BF16)
            a = _dot_nt(q_s, k_s)
            o_blk = o_blk + _dot(a.astype(BF16), v[others])
        for s in range(GLA_SUB):
            keep = (t_idx >= s) if forward else (t_idx <= s)
            decay = jnp.exp(jnp.where(keep, bb_ - bb_[s:s + 1], NEG_BIG))
            a_col = jnp.sum(qb * kb[s:s + 1] * decay, axis=-1, keepdims=True)
            o_blk = o_blk + a_col * vb[s:s + 1]
        o_rows.append(o_blk)
    b_exit = b[GLA_CHUNK - 1:GLA_CHUNK] if forward else b[0:1]
    k_hat = (k * jnp.exp(b_exit - b)).astype(BF16)
    st_new = st * _decay_rows_to_cols(jnp.exp(b_exit), st.shape[1]) + _dot_tn(k_hat, v)
    return jnp.concatenate(o_rows, axis=0), st_new


def _gla_scan_kernel(n_chunks, hps, dk, dv, has_init, *refs):
    if has_init:
        (q_ref, k_ref, v_ref, bf_ref, bb_ref, s0f_ref, s0b_ref, o_ref, st_scr) = refs
        for h in range(hps):
            st_scr[2 * h] = s0f_ref[h]
            st_scr[2 * h + 1] = s0b_ref[h]
    else:
        (q_ref, k_ref, v_ref, bf_ref, bb_ref, o_ref, sf_ref, sb_ref, st_scr) = refs
        st_scr[...] = jnp.zeros_like(st_scr)

    def step(chunk_fn, i, accumulate):
        pending = []
        for h in range(hps):
            kcols = slice(h * dk, (h + 1) * dk)
            vcols = slice(h * dv, (h + 1) * dv)
            for forward, b_ref in ((True, bf_ref), (False, bb_ref)):
                c = i if forward else n_chunks - 1 - i
                rows = pl.ds(pl.multiple_of(c * GLA_CHUNK, GLA_CHUNK), GLA_CHUNK)
                slot = 2 * h + (0 if forward else 1)
                o, st_new = chunk_fn(q_ref[rows, kcols], k_ref[rows, kcols], v_ref[rows, vcols],
                                     b_ref[rows, kcols], st_scr[slot], forward)
                if accumulate:
                    o = o + o_ref[rows, vcols]
                pending.append((rows, vcols, slot, o, st_new))
        for rows, vcols, slot, o, st_new in pending:
            st_scr[slot] = st_new
            o_ref[rows, vcols] = o

    def run(chunk_fn):
        def first(i, carry):
            step(chunk_fn, i, False)
            return carry

        def second(i, carry):
            step(chunk_fn, i, True)
            return carry

        lax.fori_loop(0, n_chunks // 2, first, 0)
        lax.fori_loop(n_chunks // 2, n_chunks, second, 0)

    span = jnp.zeros((1, hps * dk), F32)
    for c in range(n_chunks):
        top = slice(c * GLA_CHUNK, c * GLA_CHUNK + 1)
        bottom = slice((c + 1) * GLA_CHUNK - 1, (c + 1) * GLA_CHUNK)
        span = jnp.maximum(span, jnp.maximum(bf_ref[top, :] - bf_ref[bottom, :],
                                             bb_ref[bottom, :] - bb_ref[top, :]))
    bounded = jnp.max(span) < GLA_SAFE_RANGE

    @pl.when(bounded)
    def _():
        run(_gla_chunk_fast)

    @pl.when(jnp.logical_not(bounded))
    def _():
        run(_gla_chunk_safe)

    if not has_init:
        for h in range(hps):
            sf_ref[h] = st_scr[2 * h]
            sb_ref[h] = st_scr[2 * h + 1]


def _gla_scan(lay, j, q, k, v, bf, bb, init, *, row0, n_seq, seq, hps):
    hk, hv = q.shape[1], v.shape[1]
    dk, dv = hk // GLA_HEADS, hv // GLA_HEADS
    n_chunks = seq // GLA_CHUNK
    groups = GLA_HEADS // hps
    assert n_chunks % 2 == 0 and GLA_HEADS % hps == 0
    has_init = init is not None
    blk = lambda b, g: (row0 // seq + b, g)
    kspec = pl.BlockSpec((seq, hps * dk), blk)
    vspec = pl.BlockSpec((seq, hps * dv), blk)
    in_specs = [kspec, kspec, vspec, kspec, kspec]
    args = [q, k, v, bf, bb]
    o_shape = jax.ShapeDtypeStruct((n_seq * seq, hv), F32)
    o_spec = pl.BlockSpec((seq, hps * dv), lambda b, g: (b, g))
    if has_init:
        s0f, s0b = init
        st_spec = pl.BlockSpec((None, hps, dk, dv), lambda b, g: (b, j * groups + g, 0, 0))
        in_specs += [st_spec, st_spec]
        args += [s0f, s0b]
        out_shape, out_specs = o_shape, o_spec
    else:
        st_shape = jax.ShapeDtypeStruct((n_seq, GLA_HEADS, dk, dv), F32)
        st_spec = pl.BlockSpec((None, hps, dk, dv), lambda b, g: (b, g, 0, 0))
        out_shape, out_specs = (o_shape, st_shape, st_shape), (o_spec, st_spec, st_spec)

    return pl.pallas_call(
        functools.partial(_gla_scan_kernel, n_chunks, hps, dk, dv, has_init),
        out_shape=out_shape,
        grid=(n_seq, groups),
        in_specs=in_specs,
        out_specs=out_specs,
        scratch_shapes=[pltpu.VMEM((2 * hps, dk, dv), F32)],
        compiler_params=_params(("arbitrary", "arbitrary")),
        name=f"gla_scan_{'latent' if has_init else 'context'}",
    )(*args)


def _gla_out_kernel(lay, layer, oc_ref, ol_ref, og_ref, x_ref, mod_ref, lng_ref, lnb_ref, ng_ref, wo_ref,
                    out_ref, z_scr):
    dv = ng_ref.shape[1]
    is_context = pl.program_id(0) * TM < lay.n_prompt
    for h in range(GLA_HEADS):
        sl = slice(h * dv, (h + 1) * dv)
        o = jnp.where(is_context, oc_ref[:, sl], ol_ref[:, sl])
        og = og_ref[:, sl]
        o = o * lax.rsqrt(jnp.mean(o * o, axis=-1, keepdims=True) + RMS_EPS) * ng_ref[...]
        z_scr[:, sl] = (o * (og * jax.nn.sigmoid(og))).astype(BF16)
    ga = mod_ref[2:3, :]
    mix = _dot(z_scr[...], wo_ref[...])
    out_ref[...] = _layer_norm(lay.alpha * x_ref[...] + ga * mix,
                               lng_ref[2 * layer:2 * layer + 1, :], lnb_ref[2 * layer:2 * layer + 1, :])


def _gla_out(lay, layer, j, o_context, o_latent, og, x, mod, lng, lnb, norm_g, w_o):
    d = lay.d
    hv = og.shape[1]
    dv = hv // GLA_HEADS
    row = lambda i: (i, 0)
    n_ctx = lay.n_prompt // TM
    n_lat = (lay.n_tok - lay.n_prompt) // TM
    return pl.pallas_call(
        functools.partial(_gla_out_kernel, lay, layer),
        out_shape=jax.ShapeDtypeStruct((lay.n_tok, d), F32),
        grid=(lay.n_tok // TM,),
        in_specs=[
            pl.BlockSpec((TM, hv), lambda i: (jnp.minimum(i, n_ctx - 1), 0)),
            pl.BlockSpec((TM, hv), lambda i: (jnp.clip(i - n_ctx, 0, n_lat - 1), 0)),
            pl.BlockSpec((TM, hv), row), pl.BlockSpec((TM, d), row),
            pl.BlockSpec((None, None, 6, d), lambda i: (layer, lay.mod_index(i, TM), 0, 0)),
            _resident(lng.shape, lambda i: (0, 0)),
            _resident(lnb.shape, lambda i: (0, 0)),
            _resident((None, 1, dv), lambda i: (j, 0, 0)),
            _resident((None, hv, d), lambda i: (j, 0, 0)),
        ],
        out_specs=pl.BlockSpec((TM, d), row),
        scratch_shapes=[pltpu.VMEM((TM, hv), BF16)],
        compiler_params=_params(("arbitrary",)),
        name=f"gla_out_{layer}",
    )(o_context, o_latent, og, x, mod, lng, lnb, norm_g, w_o)


def _rope_tables(lay):
    n_freq = HEAD_DIM // 4
    pos = jnp.arange(lay.dec_seq)
    freqs = ROPE_THETA ** (-jnp.arange(n_freq, dtype=F32) / n_freq)
    ang_r = (pos // GRID_W).astype(F32)[:, None] * freqs
    ang_c = (pos % GRID_W).astype(F32)[:, None] * freqs
    cos = jnp.concatenate([jnp.cos(ang_r)] * 2 + [jnp.cos(ang_c)] * 2, axis=-1)
    sin = jnp.concatenate([-jnp.sin(ang_r), jnp.sin(ang_r), -jnp.sin(ang_c), jnp.sin(ang_c)], axis=-1)
    cos = jnp.concatenate([jnp.ones((TM, HEAD_DIM), F32), cos], axis=0)
    sin = jnp.concatenate([jnp.zeros((TM, HEAD_DIM), F32), sin], axis=0)
    return cos, sin


def kernel(x_prompt, x_sample, c, cache_k, cache_v, state_gla_fwd, state_gla_bwd, c_ctx, w_ada, b_ada, ln_g, ln_b, conv_w_in, conv_w, conv_w_out, attn_w_qkv, attn_q_norm, attn_k_norm, attn_w_o, gla_w_in, gla_w_gate1, gla_w_gate2, gla_b_gate, gla_norm, gla_w_o, ffn_w_in, ffn_w_out):
    batch, seq, d = x_prompt.shape
    dec_batch, dec_seq, _ = x_sample.shape
    depth = w_ada.shape[0]
    lay = _Layout(batch, seq, dec_batch, dec_seq, d, depth)
    kv = N_KV_HEADS * HEAD_DIM
    past = cache_k.shape[2]

    x = (x_prompt.reshape(lay.n_prompt, d), x_sample.reshape(dec_batch * dec_seq, d))
    cond = jnp.concatenate([c_ctx[None, :], c, jnp.zeros((MOD_ROWS - 1 - dec_batch, d), F32)], axis=0)
    mod = _modulation(cond, w_ada, b_ada).reshape(depth, MOD_ROWS, 6, d)
    lng = ln_g.reshape(depth * 2, d)
    lnb = ln_b.reshape(depth * 2, d)

    conv_w_in_b, conv_w_out_b = conv_w_in.astype(BF16), conv_w_out.astype(BF16)
    attn_w_qkv_b, attn_w_o_b = attn_w_qkv.astype(BF16), attn_w_o.astype(BF16)
    gla_w_in_b, gla_w_o_b = gla_w_in.astype(BF16), gla_w_o.astype(BF16)
    ffn_w_in_b, ffn_w_out_b = ffn_w_in.astype(BF16), ffn_w_out.astype(BF16)
    rank = gla_w_gate1.shape[3]
    hk = gla_w_gate2.shape[3]
    g1 = jnp.concatenate([gla_w_gate1[:, 0], gla_w_gate1[:, 1]], axis=-1).astype(BF16)
    zeros = jnp.zeros_like(gla_w_gate2[:, 0])
    g2 = jnp.concatenate([jnp.concatenate([gla_w_gate2[:, 0], zeros], axis=-1),
                          jnp.concatenate([zeros, gla_w_gate2[:, 1]], axis=-1)], axis=1).astype(BF16)
    gb = jnp.concatenate([gla_b_gate[:, 0], gla_b_gate[:, 1]], axis=-1)[:, None, :]
    cos_t, sin_t = _rope_tables(lay)

    new_k, new_v, new_sf, new_sb = [], [], [], []
    for i in range(depth):
        kind, j = i % N_MIXERS, i // N_MIXERS
        if kind == 0:
            x = _conv_mixer(lay, i, j, x if i == 0 else (x,), mod, lng, lnb, conv_w_in_b, conv_w, conv_w_out_b)
        elif kind == 1:
            q, k, v, kf, vf = _attn_qkv(lay, i, j, x, mod, attn_w_qkv_b,
                                        attn_q_norm[:, None, :], attn_k_norm[:, None, :], cos_t, sin_t)
            new_k.append(kf[:lay.n_prompt].reshape(batch, seq, N_KV_HEADS, HEAD_DIM))
            new_v.append(vf[:lay.n_prompt].reshape(batch, seq, N_KV_HEADS, HEAD_DIM))
            kc = cache_k[:, j].reshape(dec_batch * past, kv).astype(BF16)
            vc = cache_v[:, j].reshape(dec_batch * past, kv).astype(BF16)
            x = _attention(lay, i, j, q, k, v, None, x, mod, lng, lnb, attn_w_o_b,
                           row0=0, n_seq=batch, seq=seq, tq=seq)
            x = _attention(lay, i, j, q, k, v, (kc, vc), x, mod, lng, lnb, attn_w_o_b,
                           row0=lay.n_prompt, n_seq=dec_batch, seq=dec_seq, tq=TM)
        else:
            q, k, v, og, bf, bb = _gla_proj(lay, i, j, x, mod, gla_w_in_b, g1, g2, gb)
            s0f = state_gla_fwd.reshape(dec_batch, -1, *state_gla_fwd.shape[3:])
            s0b = state_gla_bwd.reshape(dec_batch, -1, *state_gla_bwd.shape[3:])
            o_ctx, sf, sb = _gla_scan(lay, j, q, k, v, bf, bb, None, row0=0, n_seq=batch, seq=seq,
                                      hps=GLA_HEADS_PER_STEP_CONTEXT)
            o_lat = _gla_scan(lay, j, q, k, v, bf, bb, (s0f, s0b), row0=lay.n_prompt, n_seq=dec_batch,
                              seq=dec_seq, hps=GLA_HEADS_PER_STEP_LATENT)
            new_sf.append(sf)
            new_sb.append(sb)
            x = _gla_out(lay, i, j, o_ctx, o_lat, og, x, mod, lng, lnb, gla_norm[:, None, :], gla_w_o_b)
        x = _ffn(lay, i, x, mod, lng, lnb, ffn_w_in_b, ffn_w_out_b)

    y_prompt = x[:lay.n_prompt].reshape(batch, seq, d)
    y_sample = x[lay.n_prompt:].reshape(dec_batch, dec_seq, d)
    return (y_prompt, y_sample, jnp.stack(new_k, axis=1), jnp.stack(new_v, axis=1),
            jnp.stack(new_sf, axis=1), jnp.stack(new_sb, axis=1))
```

```python
import functools

import jax
import jax.numpy as jnp
import numpy as np
from jax import lax
from jax.experimental import pallas as pl
from jax.experimental.pallas import tpu as pltpu

F32 = jnp.float32
BF16 = jnp.bfloat16

N_MIXERS = 3
CONV_WIDTH = 3
HEAD_DIM = 128
N_KV_HEADS = 2
GRID_W = 64
ROPE_THETA = 10000.0
GLA_HEADS = 4
GLA_TAU = 16.0
GLA_CHUNK = 64
LN_EPS = 1e-5
RMS_EPS = 1e-6

LANES = 128
BF16_SUBLANES = 16
VMEM_LIMIT = 56 * 1024 * 1024

MOD_ROWS = 16
MOD_NT = 1536
TM = 512
HALO = BF16_SUBLANES
WEIGHT_CAST_ROWS = 128
FFN_STAGE_CHUNKS = 8
GLA_SUB = 16
GLA_SAFE_RANGE = 80.0
GLA_HEADS_PER_STEP_CONTEXT = 4
GLA_HEADS_PER_STEP_LATENT = 2
NEG_BIG = -1e30


def _dot(a, b):
    return jnp.dot(a, b, preferred_element_type=F32)


def _dot_nt(a, b):
    return lax.dot_general(a, b, (((1,), (1,)), ((), ())), preferred_element_type=F32)


def _dot_tn(a, b):
    return lax.dot_general(a, b, (((0,), (0,)), ((), ())), preferred_element_type=F32)


def _layer_norm(y, g, b):
    mu = jnp.mean(y, axis=-1, keepdims=True)
    yc = y - mu
    var = jnp.mean(yc * yc, axis=-1, keepdims=True)
    return yc * lax.rsqrt(var + LN_EPS) * g + b


def _params(semantics):
    return pltpu.CompilerParams(dimension_semantics=semantics, vmem_limit_bytes=VMEM_LIMIT)


def _resident(block_shape, index_map):
    return pl.BlockSpec(block_shape, index_map, pipeline_mode=pl.Buffered(1))


def _round_weight_once(first_step, w_ref, w_scr):
    n_rows = w_ref.shape[0]
    rows = min(n_rows, WEIGHT_CAST_ROWS)
    assert n_rows % rows == 0

    @pl.when(first_step)
    def _():
        def body(c, carry):
            sl = pl.ds(pl.multiple_of(c * rows, rows), rows)
            w_scr[sl, :] = w_ref[sl, :].astype(BF16)
            return carry

        lax.fori_loop(0, n_rows // rows, body, 0)


class _Layout:
    def __init__(self, batch, seq, dec_batch, dec_seq, d_model, depth):
        self.batch, self.seq, self.dec_batch, self.dec_seq = batch, seq, dec_batch, dec_seq
        self.d, self.depth = d_model, depth
        self.n_prompt = batch * seq
        self.n_tok = self.n_prompt + dec_batch * dec_seq
        assert self.n_prompt % TM == 0 and dec_seq % TM == 0 and TM % seq == 0
        assert seq & (seq - 1) == 0 and dec_seq & (dec_seq - 1) == 0
        self.alpha = (2.0 * depth) ** 0.25

    def mod_index(self, i, rows):
        r0 = i * rows
        return jnp.where(r0 < self.n_prompt, 0, 1 + (r0 - self.n_prompt) // self.dec_seq)


def _mod_kernel(cond_ref, w_ref, b_ref, o_ref):
    c = cond_ref[...]
    a = (c * jax.nn.sigmoid(c)).astype(BF16)
    o_ref[...] = _dot(a, w_ref[...].astype(BF16)) + b_ref[...]


def _modulation(cond, w_ada, b_ada):
    depth, d, n_out = w_ada.shape
    return pl.pallas_call(
        _mod_kernel,
        out_shape=jax.ShapeDtypeStruct((depth, MOD_ROWS, n_out), F32),
        grid=(depth, n_out // MOD_NT),
        in_specs=[
            pl.BlockSpec((MOD_ROWS, d), lambda l, j: (0, 0)),
            pl.BlockSpec((None, d, MOD_NT), lambda l, j: (l, 0, j)),
            pl.BlockSpec((None, 1, MOD_NT), lambda l, j: (l, 0, j)),
        ],
        out_specs=pl.BlockSpec((None, MOD_ROWS, MOD_NT), lambda l, j: (l, 0, j)),
        compiler_params=_params(("arbitrary", "arbitrary")),
        name="modulation",
    )(cond, w_ada, b_ada.reshape(depth, 1, n_out))


def _conv_kernel(lay, layer, split, *refs):
    d = lay.d
    i = pl.program_id(0)
    if split:
        (xca, xpa, xna, xcb, xpb, xnb, mod_ref, lng_ref, lnb_ref, win_ref, cw_ref, wout_ref,
         o_ref, h_scr, uu_scr, win_scr, wout_scr) = refs
        is_context = i * TM < lay.n_prompt
        x = jnp.where(is_context, xca[...], xcb[...])
        x_prev = jnp.where(is_context, xpa[...], xpb[...])
        x_next = jnp.where(is_context, xna[...], xnb[...])
    else:
        (xc_ref, xp_ref, xn_ref, mod_ref, lng_ref, lnb_ref, win_ref, cw_ref, wout_ref,
         o_ref, h_scr, uu_scr, win_scr, wout_scr) = refs
        x, x_prev, x_next = xc_ref[...], xp_ref[...], xn_ref[...]
    _round_weight_once(i == 0, win_ref, win_scr)
    _round_weight_once(i == 0, wout_ref, wout_scr)
    sh, sc, ga = mod_ref[0:1, :], mod_ref[1:2, :], mod_ref[2:3, :]
    one_sc = 1.0 + sc
    h_scr[0:HALO, :] = (x_prev * one_sc + sh).astype(BF16)
    h_scr[HALO:HALO + TM, :] = (x * one_sc + sh).astype(BF16)
    h_scr[HALO + TM:HALO + TM + HALO, :] = (x_next * one_sc + sh).astype(BF16)
    bg = _dot(h_scr[HALO:HALO + TM, :], win_scr[:, 0:d])
    cgu = _dot(h_scr[...], win_scr[:, d:3 * d])
    uu_scr[...] = cgu[:, 0:d] * cgu[:, d:2 * d]
    row = i * TM + lax.broadcasted_iota(jnp.int32, (TM, 1), 0)
    seq_len = jnp.where(row < lay.n_prompt, lay.seq, lay.dec_seq)
    pos = jnp.bitwise_and(row, seq_len - 1)
    u_prev = jnp.where(pos != 0, uu_scr[pl.ds(HALO - 1, TM), :], 0.0)
    u_next = jnp.where(pos != seq_len - 1, uu_scr[pl.ds(HALO + 1, TM), :], 0.0)
    y = u_prev * cw_ref[0:1, :] + uu_scr[pl.ds(HALO, TM), :] * cw_ref[1:2, :] + u_next * cw_ref[2:3, :]
    mix = _dot((bg * y).astype(BF16), wout_scr[...])
    o_ref[...] = _layer_norm(lay.alpha * x + ga * mix,
                             lng_ref[2 * layer:2 * layer + 1, :], lnb_ref[2 * layer:2 * layer + 1, :])


def _conv_mixer(lay, layer, j, x_parts, mod, lng, lnb, w_in, cw, w_out):
    d = lay.d
    per = TM // HALO
    x_specs, x_args = [], []
    tile0 = 0
    for part in x_parts:
        n_tiles = part.shape[0] // TM

        def center(i, t0=tile0, n=n_tiles):
            return (jnp.clip(i - t0, 0, n - 1), 0)

        def prev_halo(i, t0=tile0, n=n_tiles):
            return (jnp.clip((i - t0) * per - 1, 0, n * per - 1), 0)

        def next_halo(i, t0=tile0, n=n_tiles):
            return (jnp.clip((i - t0 + 1) * per, 0, n * per - 1), 0)

        x_specs += [pl.BlockSpec((TM, d), center), pl.BlockSpec((HALO, d), prev_halo),
                    pl.BlockSpec((HALO, d), next_halo)]
        x_args += [part, part, part]
        tile0 += n_tiles
    return pl.pallas_call(
        functools.partial(_conv_kernel, lay, layer, len(x_parts) == 2),
        out_shape=jax.ShapeDtypeStruct((lay.n_tok, d), F32),
        grid=(lay.n_tok // TM,),
        in_specs=x_specs + [
            pl.BlockSpec((None, None, 6, d), lambda i: (layer, lay.mod_index(i, TM), 0, 0)),
            _resident(lng.shape, lambda i: (0, 0)),
            _resident(lnb.shape, lambda i: (0, 0)),
            _resident((None, d, 3 * d), lambda i: (j, 0, 0)),
            _resident((None, CONV_WIDTH, d), lambda i: (j, 0, 0)),
            _resident((None, d, d), lambda i: (j, 0, 0)),
        ],
        out_specs=pl.BlockSpec((TM, d), lambda i: (i, 0)),
        scratch_shapes=[pltpu.VMEM((TM + 2 * HALO, d), BF16), pltpu.VMEM((TM + 2 * HALO, d), F32),
                        pltpu.VMEM((d, 3 * d), BF16), pltpu.VMEM((d, d), BF16)],
        compiler_params=_params(("arbitrary",)),
        name=f"conv_mixer_{layer}",
    )(*x_args, mod, lng, lnb, w_in, cw, w_out)


def _stage_weight_bf16(w_hbm, w_vmem, n_chunks):
    n_rows, n_cols = w_vmem.shape
    rows = n_rows // n_chunks
    assert rows * n_chunks == n_rows and rows % BF16_SUBLANES == 0

    def run(stage, sem):
        def copy(c):
            return pltpu.make_async_copy(w_hbm.at[pl.ds(c * rows, rows), :], stage.at[c % 2], sem.at[c % 2])

        copy(0).start()
        for c in range(n_chunks):
            if c + 1 < n_chunks:
                copy(c + 1).start()
            copy(c).wait()
            w_vmem[c * rows:(c + 1) * rows, :] = stage[c % 2].astype(BF16)

    pl.run_scoped(run, pltpu.VMEM((2, rows, n_cols), F32), pltpu.SemaphoreType.DMA((2,)))


def _ffn_kernel(lay, layer, x_ref, mod_ref, lng_ref, lnb_ref, win_hbm, wout_hbm, o_ref, win_scr, wout_scr):
    d_ff = wout_scr.shape[0]

    @pl.when(pl.program_id(0) == 0)
    def _():
        _stage_weight_bf16(win_hbm.at[layer], win_scr, FFN_STAGE_CHUNKS)
        _stage_weight_bf16(wout_hbm.at[layer], wout_scr, FFN_STAGE_CHUNKS)

    sh, sc, ga = mod_ref[3:4, :], mod_ref[4:5, :], mod_ref[5:6, :]
    x = x_ref[...]
    h = (x * (1.0 + sc) + sh).astype(BF16)
    g = _dot(h, win_scr[:, 0:d_ff])
    u = _dot(h, win_scr[:, d_ff:2 * d_ff])
    a = (g * jax.nn.sigmoid(g) * u).astype(BF16)
    y = _dot(a, wout_scr[...])
    o_ref[...] = _layer_norm(lay.alpha * x + ga * y,
                             lng_ref[2 * layer + 1:2 * layer + 2, :], lnb_ref[2 * layer + 1:2 * layer + 2, :])


def _ffn(lay, layer, x, mod, lng, lnb, w_in, w_out, *, row0, n_rows):
    d = lay.d
    d_ff = w_out.shape[1]
    t0 = row0 // TM
    return pl.pallas_call(
        functools.partial(_ffn_kernel, lay, layer),
        out_shape=jax.ShapeDtypeStruct((n_rows, d), F32),
        grid=(n_rows // TM,),
        in_specs=[
            pl.BlockSpec((TM, d), lambda i: (t0 + i, 0)),
            pl.BlockSpec((None, None, 6, d), lambda i: (layer, lay.mod_index(t0 + i, TM), 0, 0)),
            _resident(lng.shape, lambda i: (0, 0)),
            _resident(lnb.shape, lambda i: (0, 0)),
            pl.BlockSpec(memory_space=pl.ANY),
            pl.BlockSpec(memory_space=pl.ANY),
        ],
        out_specs=pl.BlockSpec((TM, d), lambda i: (i, 0)),
        scratch_shapes=[pltpu.VMEM((d, 2 * d_ff), BF16), pltpu.VMEM((d_ff, d), BF16)],
        compiler_params=_params(("arbitrary",)),
        name=f"ffn_{layer}_{row0}",
    )(x, mod, lng, lnb, w_in, w_out)


def _qkv_kernel(lay, n_heads, x_ref, mod_ref, w_ref, qg_ref, kg_ref, cos_ref, sin_ref,
                q_ref, k_ref, v_ref, kf_ref, vf_ref, w_scr):
    _round_weight_once(pl.program_id(0) == 0, w_ref, w_scr)
    sh, sc = mod_ref[0:1, :], mod_ref[1:2, :]
    h = (x_ref[...] * (1.0 + sc) + sh).astype(BF16)
    qkv = _dot(h, w_scr[...])
    cos, sin = cos_ref[...], sin_ref[...]
    lane = lax.broadcasted_iota(jnp.int32, (1, HEAD_DIM), 1)
    first_half = jnp.bitwise_and(lane, HEAD_DIM // 4) == 0

    def norm_rope(t, g):
        t = t * lax.rsqrt(jnp.mean(t * t, axis=-1, keepdims=True) + RMS_EPS) * g
        partner = jnp.where(first_half,
                            pltpu.roll(t, HEAD_DIM - HEAD_DIM // 4, axis=1),
                            pltpu.roll(t, HEAD_DIM // 4, axis=1))
        return t * cos + partner * sin

    q_gain = qg_ref[...] * (HEAD_DIM ** -0.5)
    for hq in range(n_heads):
        sl = slice(hq * HEAD_DIM, (hq + 1) * HEAD_DIM)
        q_ref[:, sl] = norm_rope(qkv[:, sl], q_gain).astype(BF16)
    k0 = n_heads * HEAD_DIM
    v0 = k0 + N_KV_HEADS * HEAD_DIM
    for hk in range(N_KV_HEADS):
        sl = slice(hk * HEAD_DIM, (hk + 1) * HEAD_DIM)
        kh = norm_rope(qkv[:, k0 + hk * HEAD_DIM:k0 + (hk + 1) * HEAD_DIM], kg_ref[...])
        kf_ref[:, sl] = kh
        k_ref[:, sl] = kh.astype(BF16)
    v = qkv[:, v0:v0 + N_KV_HEADS * HEAD_DIM]
    vf_ref[...] = v
    v_ref[...] = v.astype(BF16)


def _attn_qkv(lay, layer, j, x, mod, w_qkv, q_gain, k_gain, cos_t, sin_t):
    d = lay.d
    n_qkv = w_qkv.shape[2]
    kv = N_KV_HEADS * HEAD_DIM
    n_heads = (n_qkv - 2 * kv) // HEAD_DIM
    row = lambda i: (i, 0)

    def rope_row(i):
        r0 = i * TM
        return (jnp.where(r0 < lay.n_prompt, 0, 1 + ((r0 - lay.n_prompt) % lay.dec_seq) // TM), 0)

    return pl.pallas_call(
        functools.partial(_qkv_kernel, lay, n_heads),
        out_shape=(jax.ShapeDtypeStruct((lay.n_tok, n_heads * HEAD_DIM), BF16),
                   jax.ShapeDtypeStruct((lay.n_tok, kv), BF16),
                   jax.ShapeDtypeStruct((lay.n_tok, kv), BF16),
                   jax.ShapeDtypeStruct((lay.n_tok, kv), F32),
                   jax.ShapeDtypeStruct((lay.n_tok, kv), F32)),
        grid=(lay.n_tok // TM,),
        in_specs=[
            pl.BlockSpec((TM, d), row),
            pl.BlockSpec((None, None, 6, d), lambda i: (layer, lay.mod_index(i, TM), 0, 0)),
            _resident((None, d, n_qkv), lambda i: (j, 0, 0)),
            _resident((None, 1, HEAD_DIM), lambda i: (j, 0, 0)),
            _resident((None, 1, HEAD_DIM), lambda i: (j, 0, 0)),
            pl.BlockSpec((TM, HEAD_DIM), rope_row),
            pl.BlockSpec((TM, HEAD_DIM), rope_row),
        ],
        out_specs=(pl.BlockSpec((TM, n_heads * HEAD_DIM), row), pl.BlockSpec((TM, kv), row),
                   pl.BlockSpec((TM, kv), row), pl.BlockSpec((TM, kv), row), pl.BlockSpec((TM, kv), row)),
        scratch_shapes=[pltpu.VMEM((d, n_qkv), BF16)],
        compiler_params=_params(("arbitrary",)),
        name=f"attn_qkv_{layer}",
    )(x, mod, w_qkv, q_gain, k_gain, cos_t, sin_t)


def _attn_kernel(lay, layer, n_heads, has_cache, *refs):
    if has_cache:
        (q_ref, k_ref, v_ref, kc_ref, vc_ref, x_ref, mod_ref, lng_ref, lnb_ref, wo_ref, o_ref, att_scr, wo_scr) = refs
    else:
        (q_ref, k_ref, v_ref, x_ref, mod_ref, lng_ref, lnb_ref, wo_ref, o_ref, att_scr, wo_scr) = refs
    _round_weight_once((pl.program_id(0) == 0) & (pl.program_id(1) == 0), wo_ref, wo_scr)
    group = n_heads // N_KV_HEADS
    for hk in range(N_KV_HEADS):
        ksl = slice(hk * HEAD_DIM, (hk + 1) * HEAD_DIM)
        for g in range(group):
            hq = hk * group + g
            qsl = slice(hq * HEAD_DIM, (hq + 1) * HEAD_DIM)
            qh = q_ref[:, qsl]
            s = _dot_nt(qh, k_ref[:, ksl])
            m = jnp.max(s, axis=-1, keepdims=True)
            if has_cache:
                s_c = _dot_nt(qh, kc_ref[:, ksl])
                m = jnp.maximum(m, jnp.max(s_c, axis=-1, keepdims=True))
            p = jnp.exp(s - m)
            l = jnp.sum(p, axis=-1, keepdims=True)
            o = _dot(p.astype(BF16), v_ref[:, ksl])
            if has_cache:
                p_c = jnp.exp(s_c - m)
                l = l + jnp.sum(p_c, axis=-1, keepdims=True)
                o = o + _dot(p_c.astype(BF16), vc_ref[:, ksl])
            att_scr[:, qsl] = (o / l).astype(BF16)
    ga = mod_ref[2:3, :]
    mix = _dot(att_scr[...], wo_scr[...])
    o_ref[...] = _layer_norm(lay.alpha * x_ref[...] + ga * mix,
                             lng_ref[2 * layer:2 * layer + 1, :], lnb_ref[2 * layer:2 * layer + 1, :])


def _attention(lay, layer, j, q, k, v, cache, x, mod, lng, lnb, w_o, *, row0, n_seq, seq, tq):
    d = lay.d
    n_heads = q.shape[1] // HEAD_DIM
    kv = N_KV_HEADS * HEAD_DIM
    has_cache = cache is not None
    q_blocks = seq // tq
    tile = lambda b, i: (row0 // tq + b * q_blocks + i, 0)
    seq_blk = lambda b, i: (row0 // seq + b, 0)
    in_specs = [pl.BlockSpec((tq, n_heads * HEAD_DIM), tile),
                pl.BlockSpec((seq, kv), seq_blk), pl.BlockSpec((seq, kv), seq_blk)]
    args = [q, k, v]
    if has_cache:
        kc, vc = cache
        past = kc.shape[0] // n_seq
        in_specs += [pl.BlockSpec((past, kv), lambda b, i: (b, 0))] * 2
        args += [kc, vc]
    x_index = len(args)
    in_specs += [
        pl.BlockSpec((tq, d), tile),
        pl.BlockSpec((None, None, 6, d), lambda b, i: (layer, lay.mod_index(row0 // tq + b * q_blocks + i, tq), 0, 0)),
        _resident(lng.shape, lambda b, i: (0, 0)),
        _resident(lnb.shape, lambda b, i: (0, 0)),
        _resident((None, n_heads * HEAD_DIM, d), lambda b, i: (j, 0, 0)),
    ]
    args += [x, mod, lng, lnb, w_o]

    return pl.pallas_call(
        functools.partial(_attn_kernel, lay, layer, n_heads, has_cache),
        out_shape=jax.ShapeDtypeStruct((lay.n_tok, d), F32),
        grid=(n_seq, q_blocks),
        in_specs=in_specs,
        out_specs=pl.BlockSpec((tq, d), tile),
        scratch_shapes=[pltpu.VMEM((tq, n_heads * HEAD_DIM), BF16), pltpu.VMEM((n_heads * HEAD_DIM, d), BF16)],
        input_output_aliases={x_index: 0},
        compiler_params=_params(("arbitrary", "arbitrary")),
        name=f"attention_{layer}_{'latent' if has_cache else 'context'}",
    )(*args)


def _split3(x):
    hi = x.astype(BF16)
    r = x - hi.astype(F32)
    mid = r.astype(BF16)
    lo = (r - mid.astype(F32)).astype(BF16)
    return hi, mid, lo


def _gla_proj_kernel(lay, hk, hv, x_ref, mod_ref, win_ref, w1_ref, w2_ref, bg_ref,
                     q_ref, k_ref, v_ref, og_ref, bf_ref, bb_ref, win_scr):
    _round_weight_once(pl.program_id(0) == 0, win_ref, win_scr)
    sh, sc = mod_ref[0:1, :], mod_ref[1:2, :]
    h = (x_ref[...] * (1.0 + sc) + sh).astype(BF16)
    proj = _dot(h, win_scr[...])
    dk = hk // GLA_HEADS
    q_ref[...] = proj[:, 0:hk] * (dk ** -0.5)
    k_ref[...] = proj[:, hk:2 * hk]
    v_ref[...] = proj[:, 2 * hk:2 * hk + hv].astype(BF16)
    og_ref[...] = proj[:, 2 * hk + hv:2 * hk + 2 * hv]
    z = _dot(_dot(h, w1_ref[...]).astype(BF16), w2_ref[...]) + bg_ref[...]
    log_gate = (jnp.minimum(z, 0.0) - jnp.log1p(jnp.exp(-jnp.abs(z)))) * (1.0 / GLA_TAU)
    r = lax.broadcasted_iota(jnp.int32, (GLA_CHUNK, GLA_CHUNK), 0)
    c = lax.broadcasted_iota(jnp.int32, (GLA_CHUNK, GLA_CHUNK), 1)
    lower = jnp.where(c <= r, 1.0, 0.0).astype(BF16)
    upper = jnp.where(c >= r, 1.0, 0.0).astype(BF16)
    for ch in range(TM // GLA_CHUNK):
        rows = slice(ch * GLA_CHUNK, (ch + 1) * GLA_CHUNK)
        f_hi, f_mid, f_lo = _split3(log_gate[rows, 0:hk])
        bf_ref[rows, :] = _dot(lower, f_hi) + _dot(lower, f_mid) + _dot(lower, f_lo)
        b_hi, b_mid, b_lo = _split3(log_gate[rows, hk:2 * hk])
        bb_ref[rows, :] = _dot(upper, b_hi) + _dot(upper, b_mid) + _dot(upper, b_lo)


def _gla_proj(lay, layer, j, x, mod, w_in, w1, w2, b_gate):
    d = lay.d
    n_in = w_in.shape[2]
    hk = w2.shape[2] // 2
    hv = (n_in - 2 * hk) // 2
    rank2 = w1.shape[2]
    row = lambda i: (i, 0)
    return pl.pallas_call(
        functools.partial(_gla_proj_kernel, lay, hk, hv),
        out_shape=(jax.ShapeDtypeStruct((lay.n_tok, hk), F32),
                   jax.ShapeDtypeStruct((lay.n_tok, hk), F32),
                   jax.ShapeDtypeStruct((lay.n_tok, hv), BF16),
                   jax.ShapeDtypeStruct((lay.n_tok, hv), F32),
                   jax.ShapeDtypeStruct((lay.n_tok, hk), F32),
                   jax.ShapeDtypeStruct((lay.n_tok, hk), F32)),
        grid=(lay.n_tok // TM,),
        in_specs=[
            pl.BlockSpec((TM, d), row),
            pl.BlockSpec((None, None, 6, d), lambda i: (layer, lay.mod_index(i, TM), 0, 0)),
            _resident((None, d, n_in), lambda i: (j, 0, 0)),
            _resident((None, d, rank2), lambda i: (j, 0, 0)),
            _resident((None, rank2, 2 * hk), lambda i: (j, 0, 0)),
            _resident((None, 1, 2 * hk), lambda i: (j, 0, 0)),
        ],
        out_specs=(pl.BlockSpec((TM, hk), row), pl.BlockSpec((TM, hk), row), pl.BlockSpec((TM, hv), row),
                   pl.BlockSpec((TM, hv), row), pl.BlockSpec((TM, hk), row), pl.BlockSpec((TM, hk), row)),
        scratch_shapes=[pltpu.VMEM((d, n_in), BF16)],
        compiler_params=_params(("arbitrary",)),
        name=f"gla_proj_{layer}",
    )(x, mod, w_in, w1, w2, b_gate)


def _decay_rows_to_cols(row, n_cols):
    dk = row.shape[1]
    sq = jnp.broadcast_to(row, (dk, dk)).T
    return jnp.concatenate([sq] * (n_cols // dk), axis=1)


def _gla_chunk_fast(q, k, v, b, st, forward):
    mid = GLA_CHUNK // 2
    rho = b[mid:mid + 1]
    q_s = q * jnp.exp(b - rho)
    k_s = k * jnp.exp(rho - b)
    a = _dot_nt(q_s.astype(BF16), k_s.astype(BF16))
    r = lax.broadcasted_iota(jnp.int32, (GLA_CHUNK, GLA_CHUNK), 0)
    c = lax.broadcasted_iota(jnp.int32, (GLA_CHUNK, GLA_CHUNK), 1)
    a = jnp.where((c <= r) if forward else (c >= r), a, 0.0).astype(BF16)
    q_i = (q_s * jnp.exp(rho)).astype(BF16)
    o = _dot(a, v) + _dot(q_i, st.astype(BF16))
    b_exit = b[GLA_CHUNK - 1:GLA_CHUNK] if forward else b[0:1]
    k_hat = (k_s * jnp.exp(b_exit - rho)).astype(BF16)
    st_new = st * _decay_rows_to_cols(jnp.exp(b_exit), st.shape[1]) + _dot_tn(k_hat, v)
    return o, st_new


def _gla_chunk_safe(q, k, v, b, st, forward):
    n_sub = GLA_CHUNK // GLA_SUB
    o_rows = []
    qi = (q * jnp.exp(b)).astype(BF16)
    o_inter = _dot(qi, st.astype(BF16))
    t_idx = lax.broadcasted_iota(jnp.int32, (GLA_SUB, 1), 0)
    for blk in range(n_sub):
        rows = slice(blk * GLA_SUB, (blk + 1) * GLA_SUB)
        qb, kb, bb_, vb = q[rows], k[rows], b[rows], v[rows].astype(F32)
        o_blk = o_inter[rows]
        if forward and blk > 0:
            others = slice(0, blk * GLA_SUB)
            rho = b[blk * GLA_SUB - 1:blk * GLA_SUB]
        elif (not forward) and blk < n_sub - 1:
            others = slice((blk + 1) * GLA_SUB, GLA_CHUNK)
            rho = b[(blk + 1) * GLA_SUB:(blk + 1) * GLA_SUB + 1]
        else:
            others = None
        if others is not None:
            q_s = (qb * jnp.exp(bb_ - rho)).astype(BF16)
            k_s = (k[others] * jnp.exp(rho - b[others])).astype(BF16)
            a = _dot_nt(q_s, k_s)
            o_blk = o_blk + _dot(a.astype(BF16), v[others])
        for s in range(GLA_SUB):
            keep = (t_idx >= s) if forward else (t_idx <= s)
            decay = jnp.exp(jnp.where(keep, bb_ - bb_[s:s + 1], NEG_BIG))
            a_col = jnp.sum(qb * kb[s:s + 1] * decay, axis=-1, keepdims=True)
            o_blk = o_blk + a_col * vb[s:s + 1]
        o_rows.append(o_blk)
    b_exit = b[GLA_CHUNK - 1:GLA_CHUNK] if forward else b[0:1]
    k_hat = (k * jnp.exp(b_exit - b)).astype(BF16)
    st_new = st * _decay_rows_to_cols(jnp.exp(b_exit), st.shape[1]) + _dot_tn(k_hat, v)
    return jnp.concatenate(o_rows, axis=0), st_new


def _gla_scan_kernel(n_chunks, hps, dk, dv, has_init, *refs):
    if has_init:
        (q_ref, k_ref, v_ref, bf_ref, bb_ref, s0f_ref, s0b_ref, o_ref, st_scr) = refs
        for h in range(hps):
            st_scr[2 * h] = s0f_ref[h]
            st_scr[2 * h + 1] = s0b_ref[h]
    else:
        (q_ref, k_ref, v_ref, bf_ref, bb_ref, o_ref, sf_ref, sb_ref, st_scr) = refs
        st_scr[...] = jnp.zeros_like(st_scr)

    def step(chunk_fn, i, accumulate):
        pending = []
        for h in range(hps):
            kcols = slice(h * dk, (h + 1) * dk)
            vcols = slice(h * dv, (h + 1) * dv)
            for forward, b_ref in ((True, bf_ref), (False, bb_ref)):
                c = i if forward else n_chunks - 1 - i
                rows = pl.ds(pl.multiple_of(c * GLA_CHUNK, GLA_CHUNK), GLA_CHUNK)
                slot = 2 * h + (0 if forward else 1)
                o, st_new = chunk_fn(q_ref[rows, kcols], k_ref[rows, kcols], v_ref[rows, vcols],
                                     b_ref[rows, kcols], st_scr[slot], forward)
                if accumulate:
                    o = o + o_ref[rows, vcols]
                pending.append((rows, vcols, slot, o, st_new))
        for rows, vcols, slot, o, st_new in pending:
            st_scr[slot] = st_new
            o_ref[rows, vcols] = o

    def run(chunk_fn):
        def first(i, carry):
            step(chunk_fn, i, False)
            return carry

        def second(i, carry):
            step(chunk_fn, i, True)
            return carry

        lax.fori_loop(0, n_chunks // 2, first, 0)
        lax.fori_loop(n_chunks // 2, n_chunks, second, 0)

    span = jnp.zeros((1, hps * dk), F32)
    for c in range(n_chunks):
        top = slice(c * GLA_CHUNK, c * GLA_CHUNK + 1)
        bottom = slice((c + 1) * GLA_CHUNK - 1, (c + 1) * GLA_CHUNK)
        span = jnp.maximum(span, jnp.maximum(bf_ref[top, :] - bf_ref[bottom, :],
                                             bb_ref[bottom, :] - bb_ref[top, :]))
    bounded = jnp.max(span) < GLA_SAFE_RANGE

    @pl.when(bounded)
    def _():
        run(_gla_chunk_fast)

    @pl.when(jnp.logical_not(bounded))
    def _():
        run(_gla_chunk_safe)

    if not has_init:
        for h in range(hps):
            sf_ref[h] = st_scr[2 * h]
            sb_ref[h] = st_scr[2 * h + 1]


def _gla_scan(lay, j, q, k, v, bf, bb, init, *, row0, n_seq, seq, hps):
    hk, hv = q.shape[1], v.shape[1]
    dk, dv = hk // GLA_HEADS, hv // GLA_HEADS
    n_chunks = seq // GLA_CHUNK
    groups = GLA_HEADS // hps
    assert n_chunks % 2 == 0 and GLA_HEADS % hps == 0
    has_init = init is not None
    blk = lambda b, g: (row0 // seq + b, g)
    kspec = pl.BlockSpec((seq, hps * dk), blk)
    vspec = pl.BlockSpec((seq, hps * dv), blk)
    in_specs = [kspec, kspec, vspec, kspec, kspec]
    args = [q, k, v, bf, bb]
    o_shape = jax.ShapeDtypeStruct((n_seq * seq, hv), F32)
    o_spec = pl.BlockSpec((seq, hps * dv), lambda b, g: (b, g))
    if has_init:
        s0f, s0b = init
        st_spec = pl.BlockSpec((None, hps, dk, dv), lambda b, g: (b, j * groups + g, 0, 0))
        in_specs += [st_spec, st_spec]
        args += [s0f, s0b]
        out_shape, out_specs = o_shape, o_spec
    else:
        st_shape = jax.ShapeDtypeStruct((n_seq, GLA_HEADS, dk, dv), F32)
        st_spec = pl.BlockSpec((None, hps, dk, dv), lambda b, g: (b, g, 0, 0))
        out_shape, out_specs = (o_shape, st_shape, st_shape), (o_spec, st_spec, st_spec)

    return pl.pallas_call(
        functools.partial(_gla_scan_kernel, n_chunks, hps, dk, dv, has_init),
        out_shape=out_shape,
        grid=(n_seq, groups),
        in_specs=in_specs,
        out_specs=out_specs,
        scratch_shapes=[pltpu.VMEM((2 * hps, dk, dv), F32)],
        compiler_params=_params(("arbitrary", "arbitrary")),
        name=f"gla_scan_{'latent' if has_init else 'context'}",
    )(*args)


def _gla_out_kernel(lay, layer, oc_ref, ol_ref, og_ref, x_ref, mod_ref, lng_ref, lnb_ref, ng_ref, wo_ref,
                    out_ref, z_scr, wo_scr):
    _round_weight_once(pl.program_id(0) == 0, wo_ref, wo_scr)
    dv = ng_ref.shape[1]
    is_context = pl.program_id(0) * TM < lay.n_prompt
    for h in range(GLA_HEADS):
        sl = slice(h * dv, (h + 1) * dv)
        o = jnp.where(is_context, oc_ref[:, sl], ol_ref[:, sl])
        og = og_ref[:, sl]
        o = o * lax.rsqrt(jnp.mean(o * o, axis=-1, keepdims=True) + RMS_EPS) * ng_ref[...]
        z_scr[:, sl] = (o * (og * jax.nn.sigmoid(og))).astype(BF16)
    ga = mod_ref[2:3, :]
    mix = _dot(z_scr[...], wo_scr[...])
    out_ref[...] = _layer_norm(lay.alpha * x_ref[...] + ga * mix,
                               lng_ref[2 * layer:2 * layer + 1, :], lnb_ref[2 * layer:2 * layer + 1, :])


def _gla_out(lay, layer, j, o_context, o_latent, og, x, mod, lng, lnb, norm_g, w_o):
    d = lay.d
    hv = og.shape[1]
    dv = hv // GLA_HEADS
    row = lambda i: (i, 0)
    n_ctx = lay.n_prompt // TM
    n_lat = (lay.n_tok - lay.n_prompt) // TM
    return pl.pallas_call(
        functools.partial(_gla_out_kernel, lay, layer),
        out_shape=jax.ShapeDtypeStruct((lay.n_tok, d), F32),
        grid=(lay.n_tok // TM,),
        in_specs=[
            pl.BlockSpec((TM, hv), lambda i: (jnp.minimum(i, n_ctx - 1), 0)),
            pl.BlockSpec((TM, hv), lambda i: (jnp.clip(i - n_ctx, 0, n_lat - 1), 0)),
            pl.BlockSpec((TM, hv), row), pl.BlockSpec((TM, d), row),
            pl.BlockSpec((None, None, 6, d), lambda i: (layer, lay.mod_index(i, TM), 0, 0)),
            _resident(lng.shape, lambda i: (0, 0)),
            _resident(lnb.shape, lambda i: (0, 0)),
            _resident((None, 1, dv), lambda i: (j, 0, 0)),
            _resident((None, hv, d), lambda i: (j, 0, 0)),
        ],
        out_specs=pl.BlockSpec((TM, d), row),
        scratch_shapes=[pltpu.VMEM((TM, hv), BF16), pltpu.VMEM((hv, d), BF16)],
        compiler_params=_params(("arbitrary",)),
        name=f"gla_out_{layer}",
    )(o_context, o_latent, og, x, mod, lng, lnb, norm_g, w_o)


def _rope_tables(lay):
    n_freq = HEAD_DIM // 4
    pos = np.arange(lay.dec_seq)
    freqs = (np.float32(ROPE_THETA) ** (-np.arange(n_freq, dtype=np.float32) / np.float32(n_freq))).astype(np.float32)
    ang_r = (pos // GRID_W).astype(np.float32)[:, None] * freqs
    ang_c = (pos % GRID_W).astype(np.float32)[:, None] * freqs
    cos = np.concatenate([np.cos(ang_r)] * 2 + [np.cos(ang_c)] * 2, axis=-1)
    sin = np.concatenate([-np.sin(ang_r), np.sin(ang_r), -np.sin(ang_c), np.sin(ang_c)], axis=-1)
    cos = np.concatenate([np.ones((TM, HEAD_DIM), np.float32), cos], axis=0).astype(np.float32)
    sin = np.concatenate([np.zeros((TM, HEAD_DIM), np.float32), sin], axis=0).astype(np.float32)
    return jnp.asarray(cos), jnp.asarray(sin)


def kernel(x_prompt, x_sample, c, cache_k, cache_v, state_gla_fwd, state_gla_bwd, c_ctx, w_ada, b_ada, ln_g, ln_b, conv_w_in, conv_w, conv_w_out, attn_w_qkv, attn_q_norm, attn_k_norm, attn_w_o, gla_w_in, gla_w_gate1, gla_w_gate2, gla_b_gate, gla_norm, gla_w_o, ffn_w_in, ffn_w_out):
    batch, seq, d = x_prompt.shape
    dec_batch, dec_seq, _ = x_sample.shape
    depth = w_ada.shape[0]
    lay = _Layout(batch, seq, dec_batch, dec_seq, d, depth)
    kv = N_KV_HEADS * HEAD_DIM
    past = cache_k.shape[2]

    x = (x_prompt.reshape(lay.n_prompt, d), x_sample.reshape(dec_batch * dec_seq, d))
    cond = jnp.concatenate([c_ctx[None, :], c, jnp.zeros((MOD_ROWS - 1 - dec_batch, d), F32)], axis=0)
    mod = _modulation(cond, w_ada, b_ada).reshape(depth, MOD_ROWS, 6, d)
    lng = ln_g.reshape(depth * 2, d)
    lnb = ln_b.reshape(depth * 2, d)

    rank = gla_w_gate1.shape[3]
    hk = gla_w_gate2.shape[3]
    g1 = jnp.concatenate([gla_w_gate1[:, 0], gla_w_gate1[:, 1]], axis=-1).astype(BF16)
    zeros = jnp.zeros_like(gla_w_gate2[:, 0])
    g2 = jnp.concatenate([jnp.concatenate([gla_w_gate2[:, 0], zeros], axis=-1),
                          jnp.concatenate([zeros, gla_w_gate2[:, 1]], axis=-1)], axis=1).astype(BF16)
    gb = jnp.concatenate([gla_b_gate[:, 0], gla_b_gate[:, 1]], axis=-1)[:, None, :]
    cos_t, sin_t = _rope_tables(lay)

    new_k, new_v, new_sf, new_sb = [], [], [], []
    for i in range(depth):
        kind, j = i % N_MIXERS, i // N_MIXERS
        if kind == 0:
            x = _conv_mixer(lay, i, j, x if i == 0 else (x,), mod, lng, lnb, conv_w_in, conv_w, conv_w_out)
        elif kind == 1:
            q, k, v, kf, vf = _attn_qkv(lay, i, j, x, mod, attn_w_qkv,
                                        attn_q_norm[:, None, :], attn_k_norm[:, None, :], cos_t, sin_t)
            new_k.append(kf[:lay.n_prompt].reshape(batch, seq, N_KV_HEADS, HEAD_DIM))
            new_v.append(vf[:lay.n_prompt].reshape(batch, seq, N_KV_HEADS, HEAD_DIM))
            kc = cache_k[:, j].reshape(dec_batch * past, kv).astype(BF16)
            vc = cache_v[:, j].reshape(dec_batch * past, kv).astype(BF16)
            x = _attention(lay, i, j, q, k, v, None, x, mod, lng, lnb, attn_w_o,
                           row0=0, n_seq=batch, seq=seq, tq=seq)
            x = _attention(lay, i, j, q, k, v, (kc, vc), x, mod, lng, lnb, attn_w_o,
                           row0=lay.n_prompt, n_seq=dec_batch, seq=dec_seq, tq=TM)
        else:
            q, k, v, og, bf, bb = _gla_proj(lay, i, j, x, mod, gla_w_in, g1, g2, gb)
            s0f = state_gla_fwd.reshape(dec_batch, -1, *state_gla_fwd.shape[3:])
            s0b = state_gla_bwd.reshape(dec_batch, -1, *state_gla_bwd.shape[3:])
            o_ctx, sf, sb = _gla_scan(lay, j, q, k, v, bf, bb, None, row0=0, n_seq=batch, seq=seq,
                                      hps=GLA_HEADS_PER_STEP_CONTEXT)
            o_lat = _gla_scan(lay, j, q, k, v, bf, bb, (s0f, s0b), row0=lay.n_prompt, n_seq=dec_batch,
                              seq=dec_seq, hps=GLA_HEADS_PER_STEP_LATENT)
            new_sf.append(sf)
            new_sb.append(sb)
            x = _gla_out(lay, i, j, o_ctx, o_lat, og, x, mod, lng, lnb, gla_norm[:, None, :], gla_w_o)
        if i < depth - 1:
            x = _ffn(lay, i, x, mod, lng, lnb, ffn_w_in, ffn_w_out, row0=0, n_rows=lay.n_tok)

    y_prompt = _ffn(lay, depth - 1, x, mod, lng, lnb, ffn_w_in, ffn_w_out, row0=0, n_rows=lay.n_prompt)
    y_sample = _ffn(lay, depth - 1, x, mod, lng, lnb, ffn_w_in, ffn_w_out,
                    row0=lay.n_prompt, n_rows=lay.n_tok - lay.n_prompt)
    y_prompt = y_prompt.reshape(batch, seq, d)
    y_sample = y_sample.reshape(dec_batch, dec_seq, d)
    return (y_prompt, y_sample, jnp.stack(new_k, axis=1), jnp.stack(new_v, axis=1),
            jnp.stack(new_sf, axis=1), jnp.stack(new_sb, axis=1))
```

```python
import functools

import jax
import jax.numpy as jnp
import numpy as np
from jax import lax
from jax.experimental import pallas as pl
from jax.experimental.pallas import tpu as pltpu

F32 = jnp.float32
BF16 = jnp.bfloat16

N_MIXERS = 3
CONV_WIDTH = 3
HEAD_DIM = 128
N_KV_HEADS = 2
GRID_W = 64
ROPE_THETA = 10000.0
GLA_HEADS = 4
GLA_TAU = 16.0
GLA_CHUNK = 64
LN_EPS = 1e-5
RMS_EPS = 1e-6

LANES = 128
BF16_SUBLANES = 16
VMEM_LIMIT = 56 * 1024 * 1024

MOD_ROWS = 16
MOD_NT = 1536
TM = 512
SUB_TILES = 2
HALO = BF16_SUBLANES
WEIGHT_CAST_ROWS = 128
FFN_IN_CHUNK = 512
FFN_OUT_CHUNK = 256
GLA_SUB = 16
GLA_SAFE_RANGE = 80.0
GLA_HEADS_PER_STEP_CONTEXT = 4
GLA_HEADS_PER_STEP_LATENT = 2
NEG_BIG = -1e30


def _dot(a, b):
    return jnp.dot(a, b, preferred_element_type=F32)


def _dot_nt(a, b):
    return lax.dot_general(a, b, (((1,), (1,)), ((), ())), preferred_element_type=F32)


def _dot_tn(a, b):
    return lax.dot_general(a, b, (((0,), (0,)), ((), ())), preferred_element_type=F32)


def _layer_norm(y, g, b):
    mu = jnp.mean(y, axis=-1, keepdims=True)
    yc = y - mu
    var = jnp.mean(yc * yc, axis=-1, keepdims=True)
    return yc * lax.rsqrt(var + LN_EPS) * g + b


def _params(semantics):
    return pltpu.CompilerParams(dimension_semantics=semantics, vmem_limit_bytes=VMEM_LIMIT)


def _resident(block_shape, index_map):
    return pl.BlockSpec(block_shape, index_map, pipeline_mode=pl.Buffered(1))


def _round_weight_once(first_step, w_ref, w_scr):
    n_rows = w_ref.shape[0]
    rows = min(n_rows, WEIGHT_CAST_ROWS)
    assert n_rows % rows == 0

    @pl.when(first_step)
    def _():
        def body(c, carry):
            sl = pl.ds(pl.multiple_of(c * rows, rows), rows)
            w_scr[sl, :] = w_ref[sl, :].astype(BF16)
            return carry

        lax.fori_loop(0, n_rows // rows, body, 0)


class _Layout:
    def __init__(self, batch, seq, dec_batch, dec_seq, d_model, depth):
        self.batch, self.seq, self.dec_batch, self.dec_seq = batch, seq, dec_batch, dec_seq
        self.d, self.depth = d_model, depth
        self.n_prompt = batch * seq
        self.n_tok = self.n_prompt + dec_batch * dec_seq
        assert self.n_prompt % TM == 0 and dec_seq % TM == 0 and TM % seq == 0
        assert seq & (seq - 1) == 0 and dec_seq & (dec_seq - 1) == 0
        self.alpha = (2.0 * depth) ** 0.25

    def mod_index(self, i, rows):
        r0 = i * rows
        return jnp.where(r0 < self.n_prompt, 0, 1 + (r0 - self.n_prompt) // self.dec_seq)


def _mod_kernel(cond_ref, w_ref, b_ref, o_ref):
    c = cond_ref[...]
    a = (c * jax.nn.sigmoid(c)).astype(BF16)
    o_ref[...] = _dot(a, w_ref[...].astype(BF16)) + b_ref[...]


def _modulation(cond, w_ada, b_ada):
    depth, d, n_out = w_ada.shape
    return pl.pallas_call(
        _mod_kernel,
        out_shape=jax.ShapeDtypeStruct((depth, MOD_ROWS, n_out), F32),
        grid=(depth, n_out // MOD_NT),
        in_specs=[
            pl.BlockSpec((MOD_ROWS, d), lambda l, j: (0, 0)),
            pl.BlockSpec((None, d, MOD_NT), lambda l, j: (l, 0, j)),
            pl.BlockSpec((None, 1, MOD_NT), lambda l, j: (l, 0, j)),
        ],
        out_specs=pl.BlockSpec((None, MOD_ROWS, MOD_NT), lambda l, j: (l, 0, j)),
        compiler_params=_params(("arbitrary", "arbitrary")),
        name="modulation",
    )(cond, w_ada, b_ada.reshape(depth, 1, n_out))


def _conv_kernel(lay, layer, split, *refs):
    d = lay.d
    i = pl.program_id(0)
    if split:
        (xca, xpa, xna, xcb, xpb, xnb, mod_ref, lng_ref, lnb_ref, win_ref, cw_ref, wout_ref,
         o_ref, h_scr, uu_scr, win_scr, wout_scr) = refs
        is_context = i * TM < lay.n_prompt
        x = jnp.where(is_context, xca[...], xcb[...])
        x_prev = jnp.where(is_context, xpa[...], xpb[...])
        x_next = jnp.where(is_context, xna[...], xnb[...])
    else:
        (xc_ref, xp_ref, xn_ref, mod_ref, lng_ref, lnb_ref, win_ref, cw_ref, wout_ref,
         o_ref, h_scr, uu_scr, win_scr, wout_scr) = refs
        x, x_prev, x_next = xc_ref[...], xp_ref[...], xn_ref[...]
    _round_weight_once(i == 0, win_ref, win_scr)
    _round_weight_once(i == 0, wout_ref, wout_scr)
    sh, sc, ga = mod_ref[0:1, :], mod_ref[1:2, :], mod_ref[2:3, :]
    one_sc = 1.0 + sc
    h_scr[0:HALO, :] = (x_prev * one_sc + sh).astype(BF16)
    h_scr[HALO:HALO + TM, :] = (x * one_sc + sh).astype(BF16)
    h_scr[HALO + TM:HALO + TM + HALO, :] = (x_next * one_sc + sh).astype(BF16)
    cgu = _dot(h_scr[...], win_scr[:, d:3 * d])
    uu_scr[...] = cgu[:, 0:d] * cgu[:, d:2 * d]
    sub = TM // SUB_TILES
    for s in range(SUB_TILES):
        r0 = s * sub
        bg = _dot(h_scr[HALO + r0:HALO + r0 + sub, :], win_scr[:, 0:d])
        row = i * TM + r0 + lax.broadcasted_iota(jnp.int32, (sub, 1), 0)
        seq_len = jnp.where(row < lay.n_prompt, lay.seq, lay.dec_seq)
        pos = jnp.bitwise_and(row, seq_len - 1)
        u_prev = jnp.where(pos != 0, uu_scr[pl.ds(HALO + r0 - 1, sub), :], 0.0)
        u_next = jnp.where(pos != seq_len - 1, uu_scr[pl.ds(HALO + r0 + 1, sub), :], 0.0)
        y = (u_prev * cw_ref[0:1, :] + uu_scr[pl.ds(HALO + r0, sub), :] * cw_ref[1:2, :]
             + u_next * cw_ref[2:3, :])
        mix = _dot((bg * y).astype(BF16), wout_scr[...])
        o_ref[r0:r0 + sub, :] = _layer_norm(lay.alpha * x[r0:r0 + sub] + ga * mix,
                                            lng_ref[2 * layer:2 * layer + 1, :],
                                            lnb_ref[2 * layer:2 * layer + 1, :])


def _conv_mixer(lay, layer, j, x_parts, mod, lng, lnb, w_in, cw, w_out):
    d = lay.d
    per = TM // HALO
    x_specs, x_args = [], []
    tile0 = 0
    for part in x_parts:
        n_tiles = part.shape[0] // TM

        def center(i, t0=tile0, n=n_tiles):
            return (jnp.clip(i - t0, 0, n - 1), 0)

        def prev_halo(i, t0=tile0, n=n_tiles):
            return (jnp.clip((i - t0) * per - 1, 0, n * per - 1), 0)

        def next_halo(i, t0=tile0, n=n_tiles):
            return (jnp.clip((i - t0 + 1) * per, 0, n * per - 1), 0)

        x_specs += [pl.BlockSpec((TM, d), center), pl.BlockSpec((HALO, d), prev_halo),
                    pl.BlockSpec((HALO, d), next_halo)]
        x_args += [part, part, part]
        tile0 += n_tiles
    return pl.pallas_call(
        functools.partial(_conv_kernel, lay, layer, len(x_parts) == 2),
        out_shape=jax.ShapeDtypeStruct((lay.n_tok, d), F32),
        grid=(lay.n_tok // TM,),
        in_specs=x_specs + [
            pl.BlockSpec((None, None, 6, d), lambda i: (layer, lay.mod_index(i, TM), 0, 0)),
            _resident(lng.shape, lambda i: (0, 0)),
            _resident(lnb.shape, lambda i: (0, 0)),
            _resident((None, d, 3 * d), lambda i: (j, 0, 0)),
            _resident((None, CONV_WIDTH, d), lambda i: (j, 0, 0)),
            _resident((None, d, d), lambda i: (j, 0, 0)),
        ],
        out_specs=pl.BlockSpec((TM, d), lambda i: (i, 0)),
        scratch_shapes=[pltpu.VMEM((TM + 2 * HALO, d), BF16), pltpu.VMEM((TM + 2 * HALO, d), F32),
                        pltpu.VMEM((d, 3 * d), BF16), pltpu.VMEM((d, d), BF16)],
        compiler_params=_params(("arbitrary",)),
        name=f"conv_mixer_{layer}",
    )(*x_args, mod, lng, lnb, w_in, cw, w_out)


def _ffn_first_step(x, h, ga, ln_g, ln_b, alpha, win_hbm, wout_hbm, win_scr, wout_scr, o_ref):
    d, two_dff = win_scr.shape
    d_ff = two_dff // 2
    n_in = two_dff // FFN_IN_CHUNK
    n_out = d // FFN_OUT_CHUNK
    assert n_in * FFN_IN_CHUNK == two_dff and n_out * FFN_OUT_CHUNK == d

    def run(stage_in, stage_out, sem_in, sem_out, gu_scr):
        def in_copy(c):
            return pltpu.make_async_copy(win_hbm.at[:, pl.ds(c * FFN_IN_CHUNK, FFN_IN_CHUNK)],
                                         stage_in.at[c % 2], sem_in.at[c % 2])

        def out_copy(c):
            return pltpu.make_async_copy(wout_hbm.at[:, pl.ds(c * FFN_OUT_CHUNK, FFN_OUT_CHUNK)],
                                         stage_out.at[c % 2], sem_out.at[c % 2])

        copies = [in_copy(c) for c in range(n_in)] + [out_copy(c) for c in range(n_out)]
        copies[0].start()
        y_cols = []
        a = None
        for idx, cp in enumerate(copies):
            if idx + 1 < len(copies):
                copies[idx + 1].start()
            cp.wait()
            if idx < n_in:
                cols = slice(idx * FFN_IN_CHUNK, (idx + 1) * FFN_IN_CHUNK)
                win_scr[:, cols] = stage_in[idx % 2].astype(BF16)
                gu_scr[:, cols] = _dot(h, win_scr[:, cols])
            else:
                if a is None:
                    g = gu_scr[:, 0:d_ff]
                    a = (g * jax.nn.sigmoid(g) * gu_scr[:, d_ff:two_dff]).astype(BF16)
                c = idx - n_in
                cols = slice(c * FFN_OUT_CHUNK, (c + 1) * FFN_OUT_CHUNK)
                wout_scr[:, cols] = stage_out[c % 2].astype(BF16)
                y_cols.append(_dot(a, wout_scr[:, cols]))
        y = jnp.concatenate(y_cols, axis=1)
        o_ref[...] = _layer_norm(alpha * x + ga * y, ln_g, ln_b)

    pl.run_scoped(run,
                  pltpu.VMEM((2, d, FFN_IN_CHUNK), F32), pltpu.VMEM((2, d_ff, FFN_OUT_CHUNK), F32),
                  pltpu.SemaphoreType.DMA((2,)), pltpu.SemaphoreType.DMA((2,)),
                  pltpu.VMEM((x.shape[0], two_dff), F32))


def _ffn_kernel(lay, layer, x_ref, mod_ref, lng_ref, lnb_ref, win_hbm, wout_hbm, o_ref, win_scr, wout_scr):
    d_ff = wout_scr.shape[0]
    sh, sc, ga = mod_ref[3:4, :], mod_ref[4:5, :], mod_ref[5:6, :]
    ln_g = lng_ref[2 * layer + 1:2 * layer + 2, :]
    ln_b = lnb_ref[2 * layer + 1:2 * layer + 2, :]
    first = pl.program_id(0) == 0

    @pl.when(first)
    def _():
        x = x_ref[...]
        h = (x * (1.0 + sc) + sh).astype(BF16)
        _ffn_first_step(x, h, ga, ln_g, ln_b, lay.alpha, win_hbm.at[layer], wout_hbm.at[layer],
                        win_scr, wout_scr, o_ref)

    @pl.when(jnp.logical_not(first))
    def _():
        for s in range(SUB_TILES):
            rows = slice(s * (TM // SUB_TILES), (s + 1) * (TM // SUB_TILES))
            x = x_ref[rows, :]
            h = (x * (1.0 + sc) + sh).astype(BF16)
            g = _dot(h, win_scr[:, 0:d_ff])
            u = _dot(h, win_scr[:, d_ff:2 * d_ff])
            a = (g * jax.nn.sigmoid(g) * u).astype(BF16)
            y = _dot(a, wout_scr[...])
            o_ref[rows, :] = _layer_norm(lay.alpha * x + ga * y, ln_g, ln_b)


def _ffn(lay, layer, x, mod, lng, lnb, w_in, w_out, *, row0, n_rows):
    d = lay.d
    d_ff = w_out.shape[1]
    t0 = row0 // TM
    return pl.pallas_call(
        functools.partial(_ffn_kernel, lay, layer),
        out_shape=jax.ShapeDtypeStruct((n_rows, d), F32),
        grid=(n_rows // TM,),
        in_specs=[
            pl.BlockSpec((TM, d), lambda i: (t0 + i, 0)),
            pl.BlockSpec((None, None, 6, d), lambda i: (layer, lay.mod_index(t0 + i, TM), 0, 0)),
            _resident(lng.shape, lambda i: (0, 0)),
            _resident(lnb.shape, lambda i: (0, 0)),
            pl.BlockSpec(memory_space=pl.ANY),
            pl.BlockSpec(memory_space=pl.ANY),
        ],
        out_specs=pl.BlockSpec((TM, d), lambda i: (i, 0)),
        scratch_shapes=[pltpu.VMEM((d, 2 * d_ff), BF16), pltpu.VMEM((d_ff, d), BF16)],
        compiler_params=_params(("arbitrary",)),
        name=f"ffn_{layer}_{row0}",
    )(x, mod, lng, lnb, w_in, w_out)


def _qkv_kernel(lay, n_heads, x_ref, mod_ref, w_ref, qg_ref, kg_ref, cos_ref, sin_ref,
                q_ref, k_ref, v_ref, kf_ref, vf_ref, w_scr):
    _round_weight_once(pl.program_id(0) == 0, w_ref, w_scr)
    sh, sc = mod_ref[0:1, :], mod_ref[1:2, :]
    lane = lax.broadcasted_iota(jnp.int32, (1, HEAD_DIM), 1)
    first_half = jnp.bitwise_and(lane, HEAD_DIM // 4) == 0
    q_gain = qg_ref[...] * (HEAD_DIM ** -0.5)
    k0 = n_heads * HEAD_DIM
    v0 = k0 + N_KV_HEADS * HEAD_DIM
    for s in range(SUB_TILES):
        rows = slice(s * (TM // SUB_TILES), (s + 1) * (TM // SUB_TILES))
        h = (x_ref[rows, :] * (1.0 + sc) + sh).astype(BF16)
        qkv = _dot(h, w_scr[...])
        cos, sin = cos_ref[rows, :], sin_ref[rows, :]

        def norm_rope(t, g):
            t = t * lax.rsqrt(jnp.mean(t * t, axis=-1, keepdims=True) + RMS_EPS) * g
            partner = jnp.where(first_half,
                                pltpu.roll(t, HEAD_DIM - HEAD_DIM // 4, axis=1),
                                pltpu.roll(t, HEAD_DIM // 4, axis=1))
            return t * cos + partner * sin

        for hq in range(n_heads):
            sl = slice(hq * HEAD_DIM, (hq + 1) * HEAD_DIM)
            q_ref[rows, sl] = norm_rope(qkv[:, sl], q_gain).astype(BF16)
        for hk in range(N_KV_HEADS):
            sl = slice(hk * HEAD_DIM, (hk + 1) * HEAD_DIM)
            kh = norm_rope(qkv[:, k0 + hk * HEAD_DIM:k0 + (hk + 1) * HEAD_DIM], kg_ref[...])
            kf_ref[rows, sl] = kh
            k_ref[rows, sl] = kh.astype(BF16)
        v = qkv[:, v0:v0 + N_KV_HEADS * HEAD_DIM]
        vf_ref[rows, :] = v
        v_ref[rows, :] = v.astype(BF16)


def _attn_qkv(lay, layer, j, x, mod, w_qkv, q_gain, k_gain, cos_t, sin_t):
    d = lay.d
    n_qkv = w_qkv.shape[2]
    kv = N_KV_HEADS * HEAD_DIM
    n_heads = (n_qkv - 2 * kv) // HEAD_DIM
    row = lambda i: (i, 0)

    def rope_row(i):
        r0 = i * TM
        return (jnp.where(r0 < lay.n_prompt, 0, 1 + ((r0 - lay.n_prompt) % lay.dec_seq) // TM), 0)

    return pl.pallas_call(
        functools.partial(_qkv_kernel, lay, n_heads),
        out_shape=(jax.ShapeDtypeStruct((lay.n_tok, n_heads * HEAD_DIM), BF16),
                   jax.ShapeDtypeStruct((lay.n_tok, kv), BF16),
                   jax.ShapeDtypeStruct((lay.n_tok, kv), BF16),
                   jax.ShapeDtypeStruct((lay.n_tok, kv), F32),
                   jax.ShapeDtypeStruct((lay.n_tok, kv), F32)),
        grid=(lay.n_tok // TM,),
        in_specs=[
            pl.BlockSpec((TM, d), row),
            pl.BlockSpec((None, None, 6, d), lambda i: (layer, lay.mod_index(i, TM), 0, 0)),
            _resident((None, d, n_qkv), lambda i: (j, 0, 0)),
            _resident((None, 1, HEAD_DIM), lambda i: (j, 0, 0)),
            _resident((None, 1, HEAD_DIM), lambda i: (j, 0, 0)),
            pl.BlockSpec((TM, HEAD_DIM), rope_row),
            pl.BlockSpec((TM, HEAD_DIM), rope_row),
        ],
        out_specs=(pl.BlockSpec((TM, n_heads * HEAD_DIM), row), pl.BlockSpec((TM, kv), row),
                   pl.BlockSpec((TM, kv), row), pl.BlockSpec((TM, kv), row), pl.BlockSpec((TM, kv), row)),
        scratch_shapes=[pltpu.VMEM((d, n_qkv), BF16)],
        compiler_params=_params(("arbitrary",)),
        name=f"attn_qkv_{layer}",
    )(x, mod, w_qkv, q_gain, k_gain, cos_t, sin_t)


def _attn_kernel(lay, layer, n_heads, has_cache, *refs):
    if has_cache:
        (q_ref, k_ref, v_ref, kc_ref, vc_ref, x_ref, mod_ref, lng_ref, lnb_ref, wo_ref, o_ref, att_scr, wo_scr) = refs
    else:
        (q_ref, k_ref, v_ref, x_ref, mod_ref, lng_ref, lnb_ref, wo_ref, o_ref, att_scr, wo_scr) = refs
    _round_weight_once((pl.program_id(0) == 0) & (pl.program_id(1) == 0), wo_ref, wo_scr)
    group = n_heads // N_KV_HEADS
    for hk in range(N_KV_HEADS):
        ksl = slice(hk * HEAD_DIM, (hk + 1) * HEAD_DIM)
        for g in range(group):
            hq = hk * group + g
            qsl = slice(hq * HEAD_DIM, (hq + 1) * HEAD_DIM)
            qh = q_ref[:, qsl]
            s = _dot_nt(qh, k_ref[:, ksl])
            m = jnp.max(s, axis=-1, keepdims=True)
            if has_cache:
                s_c = _dot_nt(qh, kc_ref[:, ksl])
                m = jnp.maximum(m, jnp.max(s_c, axis=-1, keepdims=True))
            p = jnp.exp(s - m)
            l = jnp.sum(p, axis=-1, keepdims=True)
            o = _dot(p.astype(BF16), v_ref[:, ksl])
            if has_cache:
                p_c = jnp.exp(s_c - m)
                l = l + jnp.sum(p_c, axis=-1, keepdims=True)
                o = o + _dot(p_c.astype(BF16), vc_ref[:, ksl])
            att_scr[:, qsl] = (o / l).astype(BF16)
    ga = mod_ref[2:3, :]
    mix = _dot(att_scr[...], wo_scr[...])
    o_ref[...] = _layer_norm(lay.alpha * x_ref[...] + ga * mix,
                             lng_ref[2 * layer:2 * layer + 1, :], lnb_ref[2 * layer:2 * layer + 1, :])


def _attention(lay, layer, j, q, k, v, cache, x, mod, lng, lnb, w_o, *, row0, n_seq, seq, tq):
    d = lay.d
    n_heads = q.shape[1] // HEAD_DIM
    kv = N_KV_HEADS * HEAD_DIM
    has_cache = cache is not None
    q_blocks = seq // tq
    tile = lambda b, i: (row0 // tq + b * q_blocks + i, 0)
    seq_blk = lambda b, i: (row0 // seq + b, 0)
    in_specs = [pl.BlockSpec((tq, n_heads * HEAD_DIM), tile),
                pl.BlockSpec((seq, kv), seq_blk), pl.BlockSpec((seq, kv), seq_blk)]
    args = [q, k, v]
    if has_cache:
        kc, vc = cache
        past = kc.shape[0] // n_seq
        in_specs += [pl.BlockSpec((past, kv), lambda b, i: (b, 0))] * 2
        args += [kc, vc]
    x_index = len(args)
    in_specs += [
        pl.BlockSpec((tq, d), tile),
        pl.BlockSpec((None, None, 6, d), lambda b, i: (layer, lay.mod_index(row0 // tq + b * q_blocks + i, tq), 0, 0)),
        _resident(lng.shape, lambda b, i: (0, 0)),
        _resident(lnb.shape, lambda b, i: (0, 0)),
        _resident((None, n_heads * HEAD_DIM, d), lambda b, i: (j, 0, 0)),
    ]
    args += [x, mod, lng, lnb, w_o]

    return pl.pallas_call(
        functools.partial(_attn_kernel, lay, layer, n_heads, has_cache),
        out_shape=jax.ShapeDtypeStruct((lay.n_tok, d), F32),
        grid=(n_seq, q_blocks),
        in_specs=in_specs,
        out_specs=pl.BlockSpec((tq, d), tile),
        scratch_shapes=[pltpu.VMEM((tq, n_heads * HEAD_DIM), BF16), pltpu.VMEM((n_heads * HEAD_DIM, d), BF16)],
        input_output_aliases={x_index: 0},
        compiler_params=_params(("arbitrary", "arbitrary")),
        name=f"attention_{layer}_{'latent' if has_cache else 'context'}",
    )(*args)


def _split3(x):
    hi = x.astype(BF16)
    r = x - hi.astype(F32)
    mid = r.astype(BF16)
    lo = (r - mid.astype(F32)).astype(BF16)
    return hi, mid, lo


def _gla_proj_kernel(lay, hk, hv, x_ref, mod_ref, win_ref, w1_ref, w2_ref, bg_ref,
                     q_ref, k_ref, v_ref, og_ref, bf_ref, bb_ref, win_scr):
    _round_weight_once(pl.program_id(0) == 0, win_ref, win_scr)
    sh, sc = mod_ref[0:1, :], mod_ref[1:2, :]
    dk = hk // GLA_HEADS
    r = lax.broadcasted_iota(jnp.int32, (GLA_CHUNK, GLA_CHUNK), 0)
    c = lax.broadcasted_iota(jnp.int32, (GLA_CHUNK, GLA_CHUNK), 1)
    lower = jnp.where(c <= r, 1.0, 0.0).astype(BF16)
    upper = jnp.where(c >= r, 1.0, 0.0).astype(BF16)
    sub = TM // SUB_TILES
    for s in range(SUB_TILES):
        rows = slice(s * sub, (s + 1) * sub)
        h = (x_ref[rows, :] * (1.0 + sc) + sh).astype(BF16)
        proj = _dot(h, win_scr[...])
        q_ref[rows, :] = proj[:, 0:hk] * (dk ** -0.5)
        k_ref[rows, :] = proj[:, hk:2 * hk]
        v_ref[rows, :] = proj[:, 2 * hk:2 * hk + hv].astype(BF16)
        og_ref[rows, :] = proj[:, 2 * hk + hv:2 * hk + 2 * hv]
        z = _dot(_dot(h, w1_ref[...]).astype(BF16), w2_ref[...]) + bg_ref[...]
        log_gate = (jnp.minimum(z, 0.0) - jnp.log1p(jnp.exp(-jnp.abs(z)))) * (1.0 / GLA_TAU)
        for ch in range(sub // GLA_CHUNK):
            crow = slice(ch * GLA_CHUNK, (ch + 1) * GLA_CHUNK)
            orow = slice(s * sub + ch * GLA_CHUNK, s * sub + (ch + 1) * GLA_CHUNK)
            f_hi, f_mid, f_lo = _split3(log_gate[crow, 0:hk])
            bf_ref[orow, :] = _dot(lower, f_hi) + _dot(lower, f_mid) + _dot(lower, f_lo)
            b_hi, b_mid, b_lo = _split3(log_gate[crow, hk:2 * hk])
            bb_ref[orow, :] = _dot(upper, b_hi) + _dot(upper, b_mid) + _dot(upper, b_lo)


def _gla_proj(lay, layer, j, x, mod, w_in, w1, w2, b_gate):
    d = lay.d
    n_in = w_in.shape[2]
    hk = w2.shape[2] // 2
    hv = (n_in - 2 * hk) // 2
    rank2 = w1.shape[2]
    row = lambda i: (i, 0)
    return pl.pallas_call(
        functools.partial(_gla_proj_kernel, lay, hk, hv),
        out_shape=(jax.ShapeDtypeStruct((lay.n_tok, hk), F32),
                   jax.ShapeDtypeStruct((lay.n_tok, hk), F32),
                   jax.ShapeDtypeStruct((lay.n_tok, hv), BF16),
                   jax.ShapeDtypeStruct((lay.n_tok, hv), F32),
                   jax.ShapeDtypeStruct((lay.n_tok, hk), F32),
                   jax.ShapeDtypeStruct((lay.n_tok, hk), F32)),
        grid=(lay.n_tok // TM,),
        in_specs=[
            pl.BlockSpec((TM, d), row),
            pl.BlockSpec((None, None, 6, d), lambda i: (layer, lay.mod_index(i, TM), 0, 0)),
            _resident((None, d, n_in), lambda i: (j, 0, 0)),
            _resident((None, d, rank2), lambda i: (j, 0, 0)),
            _resident((None, rank2, 2 * hk), lambda i: (j, 0, 0)),
            _resident((None, 1, 2 * hk), lambda i: (j, 0, 0)),
        ],
        out_specs=(pl.BlockSpec((TM, hk), row), pl.BlockSpec((TM, hk), row), pl.BlockSpec((TM, hv), row),
                   pl.BlockSpec((TM, hv), row), pl.BlockSpec((TM, hk), row), pl.BlockSpec((TM, hk), row)),
        scratch_shapes=[pltpu.VMEM((d, n_in), BF16)],
        compiler_params=_params(("arbitrary",)),
        name=f"gla_proj_{layer}",
    )(x, mod, w_in, w1, w2, b_gate)


def _decay_rows_to_cols(row, n_cols):
    dk = row.shape[1]
    sq = jnp.broadcast_to(row, (dk, dk)).T
    return jnp.concatenate([sq] * (n_cols // dk), axis=1)


def _gla_chunk_fast(q, k, v, b, st, forward):
    mid = GLA_CHUNK // 2
    rho = b[mid:mid + 1]
    q_s = q * jnp.exp(b - rho)
    k_s = k * jnp.exp(rho - b)
    a = _dot_nt(q_s.astype(BF16), k_s.astype(BF16))
    r = lax.broadcasted_iota(jnp.int32, (GLA_CHUNK, GLA_CHUNK), 0)
    c = lax.broadcasted_iota(jnp.int32, (GLA_CHUNK, GLA_CHUNK), 1)
    a = jnp.where((c <= r) if forward else (c >= r), a, 0.0).astype(BF16)
    q_i = (q_s * jnp.exp(rho)).astype(BF16)
    o = _dot(a, v) + _dot(q_i, st.astype(BF16))
    b_exit = b[GLA_CHUNK - 1:GLA_CHUNK] if forward else b[0:1]
    k_hat = (k_s * jnp.exp(b_exit - rho)).astype(BF16)
    st_new = st * _decay_rows_to_cols(jnp.exp(b_exit), st.shape[1]) + _dot_tn(k_hat, v)
    return o, st_new


def _gla_chunk_safe(q, k, v, b, st, forward):
    n_sub = GLA_CHUNK // GLA_SUB
    o_rows = []
    qi = (q * jnp.exp(b)).astype(BF16)
    o_inter = _dot(qi, st.astype(BF16))
    t_idx = lax.broadcasted_iota(jnp.int32, (GLA_SUB, 1), 0)
    for blk in range(n_sub):
        rows = slice(blk * GLA_SUB, (blk + 1) * GLA_SUB)
        qb, kb, bb_, vb = q[rows], k[rows], b[rows], v[rows].astype(F32)
        o_blk = o_inter[rows]
        if forward and blk > 0:
            others = slice(0, blk * GLA_SUB)
            rho = b[blk * GLA_SUB - 1:blk * GLA_SUB]
        elif (not forward) and blk < n_sub - 1:
            others = slice((blk + 1) * GLA_SUB, GLA_CHUNK)
            rho = b[(blk + 1) * GLA_SUB:(blk + 1) * GLA_SUB + 1]
        else:
            others = None
        if others is not None:
            q_s = (qb * jnp.exp(bb_ - rho)).astype(BF16)
            k_s = (k[others] * jnp.exp(rho - b[others])).astype(BF16)
            a = _dot_nt(q_s, k_s)
            o_blk = o_blk + _dot(a.astype(BF16), v[others])
        for s in range(GLA_SUB):
            keep = (t_idx >= s) if forward else (t_idx <= s)
            decay = jnp.exp(jnp.where(keep, bb_ - bb_[s:s + 1], NEG_BIG))
            a_col = jnp.sum(qb * kb[s:s + 1] * decay, axis=-1, keepdims=True)
            o_blk = o_blk + a_col * vb[s:s + 1]
        o_rows.append(o_blk)
    b_exit = b[GLA_CHUNK - 1:GLA_CHUNK] if forward else b[0:1]
    k_hat = (k * jnp.exp(b_exit - b)).astype(BF16)
    st_new = st * _decay_rows_to_cols(jnp.exp(b_exit), st.shape[1]) + _dot_tn(k_hat, v)
    return jnp.concatenate(o_rows, axis=0), st_new


def _gla_scan_kernel(n_chunks, hps, dk, dv, has_init, *refs):
    if has_init:
        (q_ref, k_ref, v_ref, bf_ref, bb_ref, s0f_ref, s0b_ref, o_ref, st_scr) = refs
        for h in range(hps):
            st_scr[2 * h] = s0f_ref[h]
            st_scr[2 * h + 1] = s0b_ref[h]
    else:
        (q_ref, k_ref, v_ref, bf_ref, bb_ref, o_ref, sf_ref, sb_ref, st_scr) = refs
        st_scr[...] = jnp.zeros_like(st_scr)

    def step(chunk_fn, i, accumulate):
        pending = []
        for h in range(hps):
            kcols = slice(h * dk, (h + 1) * dk)
            vcols = slice(h * dv, (h + 1) * dv)
            for forward, b_ref in ((True, bf_ref), (False, bb_ref)):
                c = i if forward else n_chunks - 1 - i
                rows = pl.ds(pl.multiple_of(c * GLA_CHUNK, GLA_CHUNK), GLA_CHUNK)
                slot = 2 * h + (0 if forward else 1)
                o, st_new = chunk_fn(q_ref[rows, kcols], k_ref[rows, kcols], v_ref[rows, vcols],
                                     b_ref[rows, kcols], st_scr[slot], forward)
                if accumulate:
                    o = o + o_ref[rows, vcols]
                pending.append((rows, vcols, slot, o, st_new))
        for rows, vcols, slot, o, st_new in pending:
            st_scr[slot] = st_new
            o_ref[rows, vcols] = o

    def run(chunk_fn):
        def first(i, carry):
            step(chunk_fn, i, False)
            return carry

        def second(i, carry):
            step(chunk_fn, i, True)
            return carry

        lax.fori_loop(0, n_chunks // 2, first, 0)
        lax.fori_loop(n_chunks // 2, n_chunks, second, 0)

    span = jnp.zeros((1, hps * dk), F32)
    for c in range(n_chunks):
        top = slice(c * GLA_CHUNK, c * GLA_CHUNK + 1)
        bottom = slice((c + 1) * GLA_CHUNK - 1, (c + 1) * GLA_CHUNK)
        span = jnp.maximum(span, jnp.maximum(bf_ref[top, :] - bf_ref[bottom, :],
                                             bb_ref[bottom, :] - bb_ref[top, :]))
    bounded = jnp.max(span) < GLA_SAFE_RANGE

    @pl.when(bounded)
    def _():
        run(_gla_chunk_fast)

    @pl.when(jnp.logical_not(bounded))
    def _():
        run(_gla_chunk_safe)

    if not has_init:
        for h in range(hps):
            sf_ref[h] = st_scr[2 * h]
            sb_ref[h] = st_scr[2 * h + 1]


def _gla_scan(lay, j, q, k, v, bf, bb, init, *, row0, n_seq, seq, hps):
    hk, hv = q.shape[1], v.shape[1]
    dk, dv = hk // GLA_HEADS, hv // GLA_HEADS
    n_chunks = seq // GLA_CHUNK
    groups = GLA_HEADS // hps
    assert n_chunks % 2 == 0 and GLA_HEADS % hps == 0
    has_init = init is not None
    blk = lambda b, g: (row0 // seq + b, g)
    kspec = pl.BlockSpec((seq, hps * dk), blk)
    vspec = pl.BlockSpec((seq, hps * dv), blk)
    in_specs = [kspec, kspec, vspec, kspec, kspec]
    args = [q, k, v, bf, bb]
    o_shape = jax.ShapeDtypeStruct((n_seq * seq, hv), F32)
    o_spec = pl.BlockSpec((seq, hps * dv), lambda b, g: (b, g))
    if has_init:
        s0f, s0b = init
        st_spec = pl.BlockSpec((None, hps, dk, dv), lambda b, g: (b, j * groups + g, 0, 0))
        in_specs += [st_spec, st_spec]
        args += [s0f, s0b]
        out_shape, out_specs = o_shape, o_spec
    else:
        st_shape = jax.ShapeDtypeStruct((n_seq, GLA_HEADS, dk, dv), F32)
        st_spec = pl.BlockSpec((None, hps, dk, dv), lambda b, g: (b, g, 0, 0))
        out_shape, out_specs = (o_shape, st_shape, st_shape), (o_spec, st_spec, st_spec)

    return pl.pallas_call(
        functools.partial(_gla_scan_kernel, n_chunks, hps, dk, dv, has_init),
        out_shape=out_shape,
        grid=(n_seq, groups),
        in_specs=in_specs,
        out_specs=out_specs,
        scratch_shapes=[pltpu.VMEM((2 * hps, dk, dv), F32)],
        compiler_params=_params(("arbitrary", "arbitrary")),
        name=f"gla_scan_{'latent' if has_init else 'context'}",
    )(*args)


def _gla_out_kernel(lay, layer, oc_ref, ol_ref, og_ref, x_ref, mod_ref, lng_ref, lnb_ref, ng_ref, wo_ref,
                    out_ref, z_scr, wo_scr):
    _round_weight_once(pl.program_id(0) == 0, wo_ref, wo_scr)
    dv = ng_ref.shape[1]
    is_context = pl.program_id(0) * TM < lay.n_prompt
    for h in range(GLA_HEADS):
        sl = slice(h * dv, (h + 1) * dv)
        o = jnp.where(is_context, oc_ref[:, sl], ol_ref[:, sl])
        og = og_ref[:, sl]
        o = o * lax.rsqrt(jnp.mean(o * o, axis=-1, keepdims=True) + RMS_EPS) * ng_ref[...]
        z_scr[:, sl] = (o * (og * jax.nn.sigmoid(og))).astype(BF16)
    ga = mod_ref[2:3, :]
    mix = _dot(z_scr[...], wo_scr[...])
    out_ref[...] = _layer_norm(lay.alpha * x_ref[...] + ga * mix,
                               lng_ref[2 * layer:2 * layer + 1, :], lnb_ref[2 * layer:2 * layer + 1, :])


def _gla_out(lay, layer, j, o_context, o_latent, og, x, mod, lng, lnb, norm_g, w_o):
    d = lay.d
    hv = og.shape[1]
    dv = hv // GLA_HEADS
    row = lambda i: (i, 0)
    n_ctx = lay.n_prompt // TM
    n_lat = (lay.n_tok - lay.n_prompt) // TM
    return pl.pallas_call(
        functools.partial(_gla_out_kernel, lay, layer),
        out_shape=jax.ShapeDtypeStruct((lay.n_tok, d), F32),
        grid=(lay.n_tok // TM,),
        in_specs=[
            pl.BlockSpec((TM, hv), lambda i: (jnp.minimum(i, n_ctx - 1), 0)),
            pl.BlockSpec((TM, hv), lambda i: (jnp.clip(i - n_ctx, 0, n_lat - 1), 0)),
            pl.BlockSpec((TM, hv), row), pl.BlockSpec((TM, d), row),
            pl.BlockSpec((None, None, 6, d), lambda i: (layer, lay.mod_index(i, TM), 0, 0)),
            _resident(lng.shape, lambda i: (0, 0)),
            _resident(lnb.shape, lambda i: (0, 0)),
            _resident((None, 1, dv), lambda i: (j, 0, 0)),
            _resident((None, hv, d), lambda i: (j, 0, 0)),
        ],
        out_specs=pl.BlockSpec((TM, d), row),
        scratch_shapes=[pltpu.VMEM((TM, hv), BF16), pltpu.VMEM((hv, d), BF16)],
        compiler_params=_params(("arbitrary",)),
        name=f"gla_out_{layer}",
    )(o_context, o_latent, og, x, mod, lng, lnb, norm_g, w_o)


def _rope_tables(lay):
    n_freq = HEAD_DIM // 4
    pos = np.arange(lay.dec_seq)
    freqs = (np.float32(ROPE_THETA) ** (-np.arange(n_freq, dtype=np.float32) / np.float32(n_freq))).astype(np.float32)
    ang_r = (pos // GRID_W).astype(np.float32)[:, None] * freqs
    ang_c = (pos % GRID_W).astype(np.float32)[:, None] * freqs
    cos = np.concatenate([np.cos(ang_r)] * 2 + [np.cos(ang_c)] * 2, axis=-1)
    sin = np.concatenate([-np.sin(ang_r), np.sin(ang_r), -np.sin(ang_c), np.sin(ang_c)], axis=-1)
    cos = np.concatenate([np.ones((TM, HEAD_DIM), np.float32), cos], axis=0).astype(np.float32)
    sin = np.concatenate([np.zeros((TM, HEAD_DIM), np.float32), sin], axis=0).astype(np.float32)
    return jnp.asarray(cos), jnp.asarray(sin)


def kernel(x_prompt, x_sample, c, cache_k, cache_v, state_gla_fwd, state_gla_bwd, c_ctx, w_ada, b_ada, ln_g, ln_b, conv_w_in, conv_w, conv_w_out, attn_w_qkv, attn_q_norm, attn_k_norm, attn_w_o, gla_w_in, gla_w_gate1, gla_w_gate2, gla_b_gate, gla_norm, gla_w_o, ffn_w_in, ffn_w_out):
    batch, seq, d = x_prompt.shape
    dec_batch, dec_seq, _ = x_sample.shape
    depth = w_ada.shape[0]
    lay = _Layout(batch, seq, dec_batch, dec_seq, d, depth)
    kv = N_KV_HEADS * HEAD_DIM
    past = cache_k.shape[2]

    x = (x_prompt.reshape(lay.n_prompt, d), x_sample.reshape(dec_batch * dec_seq, d))
    cond = jnp.concatenate([c_ctx[None, :], c, jnp.zeros((MOD_ROWS - 1 - dec_batch, d), F32)], axis=0)
    mod = _modulation(cond, w_ada, b_ada).reshape(depth, MOD_ROWS, 6, d)
    lng = ln_g.reshape(depth * 2, d)
    lnb = ln_b.reshape(depth * 2, d)

    rank = gla_w_gate1.shape[3]
    hk = gla_w_gate2.shape[3]
    g1 = jnp.concatenate([gla_w_gate1[:, 0], gla_w_gate1[:, 1]], axis=-1).astype(BF16)
    zeros = jnp.zeros_like(gla_w_gate2[:, 0])
    g2 = jnp.concatenate([jnp.concatenate([gla_w_gate2[:, 0], zeros], axis=-1),
                          jnp.concatenate([zeros, gla_w_gate2[:, 1]], axis=-1)], axis=1).astype(BF16)
    gb = jnp.concatenate([gla_b_gate[:, 0], gla_b_gate[:, 1]], axis=-1)[:, None, :]
    cos_t, sin_t = _rope_tables(lay)

    new_k, new_v, new_sf, new_sb = [], [], [], []
    for i in range(depth):
        kind, j = i % N_MIXERS, i // N_MIXERS
        if kind == 0:
            x = _conv_mixer(lay, i, j, x if i == 0 else (x,), mod, lng, lnb, conv_w_in, conv_w, conv_w_out)
        elif kind == 1:
            q, k, v, kf, vf = _attn_qkv(lay, i, j, x, mod, attn_w_qkv,
                                        attn_q_norm[:, None, :], attn_k_norm[:, None, :], cos_t, sin_t)
            new_k.append(kf[:lay.n_prompt].reshape(batch, seq, N_KV_HEADS, HEAD_DIM))
            new_v.append(vf[:lay.n_prompt].reshape(batch, seq, N_KV_HEADS, HEAD_DIM))
            kc = cache_k[:, j].reshape(dec_batch * past, kv).astype(BF16)
            vc = cache_v[:, j].reshape(dec_batch * past, kv).astype(BF16)
            x = _attention(lay, i, j, q, k, v, None, x, mod, lng, lnb, attn_w_o,
                           row0=0, n_seq=batch, seq=seq, tq=seq)
            x = _attention(lay, i, j, q, k, v, (kc, vc), x, mod, lng, lnb, attn_w_o,
                           row0=lay.n_prompt, n_seq=dec_batch, seq=dec_seq, tq=TM)
        else:
            q, k, v, og, bf, bb = _gla_proj(lay, i, j, x, mod, gla_w_in, g1, g2, gb)
            s0f = state_gla_fwd.reshape(dec_batch, -1, *state_gla_fwd.shape[3:])
            s0b = state_gla_bwd.reshape(dec_batch, -1, *state_gla_bwd.shape[3:])
            o_ctx, sf, sb = _gla_scan(lay, j, q, k, v, bf, bb, None, row0=0, n_seq=batch, seq=seq,
                                      hps=GLA_HEADS_PER_STEP_CONTEXT)
            o_lat = _gla_scan(lay, j, q, k, v, bf, bb, (s0f, s0b), row0=lay.n_prompt, n_seq=dec_batch,
                              seq=dec_seq, hps=GLA_HEADS_PER_STEP_LATENT)
            new_sf.append(sf)
            new_sb.append(sb)
            x = _gla_out(lay, i, j, o_ctx, o_lat, og, x, mod, lng, lnb, gla_norm[:, None, :], gla_w_o)
        if i < depth - 1:
            x = _ffn(lay, i, x, mod, lng, lnb, ffn_w_in, ffn_w_out, row0=0, n_rows=lay.n_tok)

    y_prompt = _ffn(lay, depth - 1, x, mod, lng, lnb, ffn_w_in, ffn_w_out, row0=0, n_rows=lay.n_prompt)
    y_sample = _ffn(lay, depth - 1, x, mod, lng, lnb, ffn_w_in, ffn_w_out,
                    row0=lay.n_prompt, n_rows=lay.n_tok - lay.n_prompt)
    y_prompt = y_prompt.reshape(batch, seq, d)
    y_sample = y_sample.reshape(dec_batch, dec_seq, d)
    return (y_prompt, y_sample, jnp.stack(new_k, axis=1), jnp.stack(new_v, axis=1),
            jnp.stack(new_sf, axis=1), jnp.stack(new_sb, axis=1))
```

```python
import functools

import jax
import jax.numpy as jnp
import numpy as np
from jax import lax
from jax.experimental import pallas as pl
from jax.experimental.pallas import tpu as pltpu

F32 = jnp.float32
BF16 = jnp.bfloat16

N_MIXERS = 3
CONV_WIDTH = 3
HEAD_DIM = 128
N_KV_HEADS = 2
GRID_W = 64
ROPE_THETA = 10000.0
GLA_HEADS = 4
GLA_TAU = 16.0
GLA_CHUNK = 64
LN_EPS = 1e-5
RMS_EPS = 1e-6

LANES = 128
BF16_SUBLANES = 16
VMEM_LIMIT = 56 * 1024 * 1024

MOD_ROWS = 16
MOD_NT = 1536
TM = 512
SUB_TILES = 2
GLA_PROJ_SUB_TILES = 1
HALO = BF16_SUBLANES
WEIGHT_CAST_ROWS = 128
FFN_IN_CHUNK = 512
FFN_OUT_CHUNK = 256
GLA_SUB = 16
GLA_SAFE_RANGE = 80.0
GLA_CHAINS = 4
GLA_HEADS_PER_STEP_CONTEXT = 4
GLA_HEADS_PER_STEP_LATENT = 2
NEG_BIG = -1e30


def _dot(a, b):
    return jnp.dot(a, b, preferred_element_type=F32)


def _dot_nt(a, b):
    return lax.dot_general(a, b, (((1,), (1,)), ((), ())), preferred_element_type=F32)


def _dot_tn(a, b):
    return lax.dot_general(a, b, (((0,), (0,)), ((), ())), preferred_element_type=F32)


def _layer_norm(y, g, b):
    mu = jnp.mean(y, axis=-1, keepdims=True)
    yc = y - mu
    var = jnp.mean(yc * yc, axis=-1, keepdims=True)
    return yc * lax.rsqrt(var + LN_EPS) * g + b


def _params(semantics):
    return pltpu.CompilerParams(dimension_semantics=semantics, vmem_limit_bytes=VMEM_LIMIT)


def _resident(block_shape, index_map):
    return pl.BlockSpec(block_shape, index_map, pipeline_mode=pl.Buffered(1))


def _round_weight_once(first_step, w_ref, w_scr):
    n_rows = w_ref.shape[0]
    rows = min(n_rows, WEIGHT_CAST_ROWS)
    assert n_rows % rows == 0

    @pl.when(first_step)
    def _():
        def body(c, carry):
            sl = pl.ds(pl.multiple_of(c * rows, rows), rows)
            w_scr[sl, :] = w_ref[sl, :].astype(BF16)
            return carry

        lax.fori_loop(0, n_rows // rows, body, 0)


class _Layout:
    def __init__(self, batch, seq, dec_batch, dec_seq, d_model, depth):
        self.batch, self.seq, self.dec_batch, self.dec_seq = batch, seq, dec_batch, dec_seq
        self.d, self.depth = d_model, depth
        self.n_prompt = batch * seq
        self.n_tok = self.n_prompt + dec_batch * dec_seq
        assert self.n_prompt % TM == 0 and dec_seq % TM == 0 and TM % seq == 0
        assert seq & (seq - 1) == 0 and dec_seq & (dec_seq - 1) == 0
        self.alpha = (2.0 * depth) ** 0.25

    def mod_index(self, i, rows):
        r0 = i * rows
        return jnp.where(r0 < self.n_prompt, 0, 1 + (r0 - self.n_prompt) // self.dec_seq)


def _mod_kernel(cond_ref, w_ref, b_ref, o_ref):
    c = cond_ref[...]
    a = (c * jax.nn.sigmoid(c)).astype(BF16)
    o_ref[...] = _dot(a, w_ref[...].astype(BF16)) + b_ref[...]


def _modulation(cond, w_ada, b_ada):
    depth, d, n_out = w_ada.shape
    return pl.pallas_call(
        _mod_kernel,
        out_shape=jax.ShapeDtypeStruct((depth, MOD_ROWS, n_out), F32),
        grid=(depth, n_out // MOD_NT),
        in_specs=[
            pl.BlockSpec((MOD_ROWS, d), lambda l, j: (0, 0)),
            pl.BlockSpec((None, d, MOD_NT), lambda l, j: (l, 0, j)),
            pl.BlockSpec((None, 1, MOD_NT), lambda l, j: (l, 0, j)),
        ],
        out_specs=pl.BlockSpec((None, MOD_ROWS, MOD_NT), lambda l, j: (l, 0, j)),
        compiler_params=_params(("arbitrary", "arbitrary")),
        name="modulation",
    )(cond, w_ada, b_ada.reshape(depth, 1, n_out))


def _conv_kernel(lay, layer, split, *refs):
    d = lay.d
    i = pl.program_id(0)
    if split:
        (xca, xpa, xna, xcb, xpb, xnb, mod_ref, lng_ref, lnb_ref, win_ref, cw_ref, wout_ref,
         o_ref, h_scr, uu_scr, win_scr, wout_scr) = refs
        is_context = i * TM < lay.n_prompt
        x = jnp.where(is_context, xca[...], xcb[...])
        x_prev = jnp.where(is_context, xpa[...], xpb[...])
        x_next = jnp.where(is_context, xna[...], xnb[...])
    else:
        (xc_ref, xp_ref, xn_ref, mod_ref, lng_ref, lnb_ref, win_ref, cw_ref, wout_ref,
         o_ref, h_scr, uu_scr, win_scr, wout_scr) = refs
        x, x_prev, x_next = xc_ref[...], xp_ref[...], xn_ref[...]
    _round_weight_once(i == 0, win_ref, win_scr)
    _round_weight_once(i == 0, wout_ref, wout_scr)
    sh, sc, ga = mod_ref[0:1, :], mod_ref[1:2, :], mod_ref[2:3, :]
    one_sc = 1.0 + sc
    h_scr[0:HALO, :] = (x_prev * one_sc + sh).astype(BF16)
    h_scr[HALO:HALO + TM, :] = (x * one_sc + sh).astype(BF16)
    h_scr[HALO + TM:HALO + TM + HALO, :] = (x_next * one_sc + sh).astype(BF16)
    cgu = _dot(h_scr[...], win_scr[:, d:3 * d])
    uu_scr[...] = cgu[:, 0:d] * cgu[:, d:2 * d]
    sub = TM // SUB_TILES
    for s in range(SUB_TILES):
        r0 = s * sub
        bg = _dot(h_scr[HALO + r0:HALO + r0 + sub, :], win_scr[:, 0:d])
        row = i * TM + r0 + lax.broadcasted_iota(jnp.int32, (sub, 1), 0)
        seq_len = jnp.where(row < lay.n_prompt, lay.seq, lay.dec_seq)
        pos = jnp.bitwise_and(row, seq_len - 1)
        u_prev = jnp.where(pos != 0, uu_scr[pl.ds(HALO + r0 - 1, sub), :], 0.0)
        u_next = jnp.where(pos != seq_len - 1, uu_scr[pl.ds(HALO + r0 + 1, sub), :], 0.0)
        y = (u_prev * cw_ref[0:1, :] + uu_scr[pl.ds(HALO + r0, sub), :] * cw_ref[1:2, :]
             + u_next * cw_ref[2:3, :])
        mix = _dot((bg * y).astype(BF16), wout_scr[...])
        o_ref[r0:r0 + sub, :] = _layer_norm(lay.alpha * x[r0:r0 + sub] + ga * mix,
                                            lng_ref[2 * layer:2 * layer + 1, :],
                                            lnb_ref[2 * layer:2 * layer + 1, :])


def _conv_mixer(lay, layer, j, x_parts, mod, lng, lnb, w_in, cw, w_out):
    d = lay.d
    per = TM // HALO
    x_specs, x_args = [], []
    tile0 = 0
    for part in x_parts:
        n_tiles = part.shape[0] // TM

        def center(i, t0=tile0, n=n_tiles):
            return (jnp.clip(i - t0, 0, n - 1), 0)

        def prev_halo(i, t0=tile0, n=n_tiles):
            return (jnp.clip((i - t0) * per - 1, 0, n * per - 1), 0)

        def next_halo(i, t0=tile0, n=n_tiles):
            return (jnp.clip((i - t0 + 1) * per, 0, n * per - 1), 0)

        x_specs += [pl.BlockSpec((TM, d), center), pl.BlockSpec((HALO, d), prev_halo),
                    pl.BlockSpec((HALO, d), next_halo)]
        x_args += [part, part, part]
        tile0 += n_tiles
    return pl.pallas_call(
        functools.partial(_conv_kernel, lay, layer, len(x_parts) == 2),
        out_shape=jax.ShapeDtypeStruct((lay.n_tok, d), F32),
        grid=(lay.n_tok // TM,),
        in_specs=x_specs + [
            pl.BlockSpec((None, None, 6, d), lambda i: (layer, lay.mod_index(i, TM), 0, 0)),
            _resident(lng.shape, lambda i: (0, 0)),
            _resident(lnb.shape, lambda i: (0, 0)),
            _resident((None, d, 3 * d), lambda i: (j, 0, 0)),
            _resident((None, CONV_WIDTH, d), lambda i: (j, 0, 0)),
            _resident((None, d, d), lambda i: (j, 0, 0)),
        ],
        out_specs=pl.BlockSpec((TM, d), lambda i: (i, 0)),
        scratch_shapes=[pltpu.VMEM((TM + 2 * HALO, d), BF16), pltpu.VMEM((TM + 2 * HALO, d), F32),
                        pltpu.VMEM((d, 3 * d), BF16), pltpu.VMEM((d, d), BF16)],
        compiler_params=_params(("arbitrary",)),
        name=f"conv_mixer_{layer}",
    )(*x_args, mod, lng, lnb, w_in, cw, w_out)


def _ffn_first_step(x, h, ga, ln_g, ln_b, alpha, win_hbm, wout_hbm, win_scr, wout_scr, o_ref):
    d, two_dff = win_scr.shape
    d_ff = two_dff // 2
    n_in = two_dff // FFN_IN_CHUNK
    n_out = d // FFN_OUT_CHUNK
    assert n_in * FFN_IN_CHUNK == two_dff and n_out * FFN_OUT_CHUNK == d

    def run(stage_in, stage_out, sem_in, sem_out, gu_scr):
        def in_copy(c):
            return pltpu.make_async_copy(win_hbm.at[:, pl.ds(c * FFN_IN_CHUNK, FFN_IN_CHUNK)],
                                         stage_in.at[c % 2], sem_in.at[c % 2])

        def out_copy(c):
            return pltpu.make_async_copy(wout_hbm.at[:, pl.ds(c * FFN_OUT_CHUNK, FFN_OUT_CHUNK)],
                                         stage_out.at[c % 2], sem_out.at[c % 2])

        copies = [in_copy(c) for c in range(n_in)] + [out_copy(c) for c in range(n_out)]
        copies[0].start()
        y_cols = []
        a = None
        for idx, cp in enumerate(copies):
            if idx + 1 < len(copies):
                copies[idx + 1].start()
            cp.wait()
            if idx < n_in:
                cols = slice(idx * FFN_IN_CHUNK, (idx + 1) * FFN_IN_CHUNK)
                win_scr[:, cols] = stage_in[idx % 2].astype(BF16)
                gu_scr[:, cols] = _dot(h, win_scr[:, cols])
            else:
                if a is None:
                    g = gu_scr[:, 0:d_ff]
                    a = (g * jax.nn.sigmoid(g) * gu_scr[:, d_ff:two_dff]).astype(BF16)
                c = idx - n_in
                cols = slice(c * FFN_OUT_CHUNK, (c + 1) * FFN_OUT_CHUNK)
                wout_scr[:, cols] = stage_out[c % 2].astype(BF16)
                y_cols.append(_dot(a, wout_scr[:, cols]))
        y = jnp.concatenate(y_cols, axis=1)
        o_ref[...] = _layer_norm(alpha * x + ga * y, ln_g, ln_b)

    pl.run_scoped(run,
                  pltpu.VMEM((2, d, FFN_IN_CHUNK), F32), pltpu.VMEM((2, d_ff, FFN_OUT_CHUNK), F32),
                  pltpu.SemaphoreType.DMA((2,)), pltpu.SemaphoreType.DMA((2,)),
                  pltpu.VMEM((x.shape[0], two_dff), F32))


def _ffn_kernel(lay, layer, split_out, x_ref, mod_ref, lng_ref, lnb_ref, win_hbm, wout_hbm, *refs):
    if split_out:
        oc_ref, ol_ref, win_scr, wout_scr = refs
    else:
        oc_ref, win_scr, wout_scr = refs
    d_ff = wout_scr.shape[0]
    sh, sc, ga = mod_ref[3:4, :], mod_ref[4:5, :], mod_ref[5:6, :]
    ln_g = lng_ref[2 * layer + 1:2 * layer + 2, :]
    ln_b = lnb_ref[2 * layer + 1:2 * layer + 2, :]
    first = pl.program_id(0) == 0

    @pl.when(first)
    def _():
        x = x_ref[...]
        h = (x * (1.0 + sc) + sh).astype(BF16)
        _ffn_first_step(x, h, ga, ln_g, ln_b, lay.alpha, win_hbm.at[layer], wout_hbm.at[layer],
                        win_scr, wout_scr, oc_ref)

    @pl.when(jnp.logical_not(first))
    def _():
        results = []
        for s in range(SUB_TILES):
            rows = slice(s * (TM // SUB_TILES), (s + 1) * (TM // SUB_TILES))
            x = x_ref[rows, :]
            h = (x * (1.0 + sc) + sh).astype(BF16)
            g = _dot(h, win_scr[:, 0:d_ff])
            u = _dot(h, win_scr[:, d_ff:2 * d_ff])
            a = (g * jax.nn.sigmoid(g) * u).astype(BF16)
            y = _dot(a, wout_scr[...])
            results.append((rows, _layer_norm(lay.alpha * x + ga * y, ln_g, ln_b)))
        if split_out:
            is_context = pl.program_id(0) * TM < lay.n_prompt

            @pl.when(is_context)
            def _():
                for rows, res in results:
                    oc_ref[rows, :] = res

            @pl.when(jnp.logical_not(is_context))
            def _():
                for rows, res in results:
                    ol_ref[rows, :] = res
        else:
            for rows, res in results:
                oc_ref[rows, :] = res


def _ffn(lay, layer, x, mod, lng, lnb, w_in, w_out, *, split_out):
    d = lay.d
    d_ff = w_out.shape[1]
    row = lambda i: (i, 0)
    if split_out:
        n_ctx = lay.n_prompt // TM
        n_lat = (lay.n_tok - lay.n_prompt) // TM
        out_shape = (jax.ShapeDtypeStruct((lay.n_prompt, d), F32),
                     jax.ShapeDtypeStruct((lay.n_tok - lay.n_prompt, d), F32))
        out_specs = (pl.BlockSpec((TM, d), lambda i: (jnp.minimum(i, n_ctx - 1), 0)),
                     pl.BlockSpec((TM, d), lambda i: (jnp.clip(i - n_ctx, 0, n_lat - 1), 0)))
    else:
        out_shape = jax.ShapeDtypeStruct((lay.n_tok, d), F32)
        out_specs = pl.BlockSpec((TM, d), row)
    return pl.pallas_call(
        functools.partial(_ffn_kernel, lay, layer, split_out),
        out_shape=out_shape,
        grid=(lay.n_tok // TM,),
        in_specs=[
            pl.BlockSpec((TM, d), row),
            pl.BlockSpec((None, None, 6, d), lambda i: (layer, lay.mod_index(i, TM), 0, 0)),
            _resident(lng.shape, lambda i: (0, 0)),
            _resident(lnb.shape, lambda i: (0, 0)),
            pl.BlockSpec(memory_space=pl.ANY),
            pl.BlockSpec(memory_space=pl.ANY),
        ],
        out_specs=out_specs,
        scratch_shapes=[pltpu.VMEM((d, 2 * d_ff), BF16), pltpu.VMEM((d_ff, d), BF16)],
        compiler_params=_params(("arbitrary",)),
        name=f"ffn_{layer}",
    )(x, mod, lng, lnb, w_in, w_out)


def _qkv_kernel(lay, n_heads, x_ref, mod_ref, w_ref, qg_ref, kg_ref, cos_ref, sin_ref,
                q_ref, k_ref, v_ref, kf_ref, vf_ref, w_scr):
    _round_weight_once(pl.program_id(0) == 0, w_ref, w_scr)
    sh, sc = mod_ref[0:1, :], mod_ref[1:2, :]
    lane = lax.broadcasted_iota(jnp.int32, (1, HEAD_DIM), 1)
    first_half = jnp.bitwise_and(lane, HEAD_DIM // 4) == 0
    q_gain = qg_ref[...] * (HEAD_DIM ** -0.5)
    k0 = n_heads * HEAD_DIM
    v0 = k0 + N_KV_HEADS * HEAD_DIM
    for s in range(SUB_TILES):
        rows = slice(s * (TM // SUB_TILES), (s + 1) * (TM // SUB_TILES))
        h = (x_ref[rows, :] * (1.0 + sc) + sh).astype(BF16)
        qkv = _dot(h, w_scr[...])
        cos, sin = cos_ref[rows, :], sin_ref[rows, :]

        def norm_rope(t, g):
            t = t * lax.rsqrt(jnp.mean(t * t, axis=-1, keepdims=True) + RMS_EPS) * g
            partner = jnp.where(first_half,
                                pltpu.roll(t, HEAD_DIM - HEAD_DIM // 4, axis=1),
                                pltpu.roll(t, HEAD_DIM // 4, axis=1))
            return t * cos + partner * sin

        for hq in range(n_heads):
            sl = slice(hq * HEAD_DIM, (hq + 1) * HEAD_DIM)
            q_ref[rows, sl] = norm_rope(qkv[:, sl], q_gain).astype(BF16)
        for hk in range(N_KV_HEADS):
            sl = slice(hk * HEAD_DIM, (hk + 1) * HEAD_DIM)
            kh = norm_rope(qkv[:, k0 + hk * HEAD_DIM:k0 + (hk + 1) * HEAD_DIM], kg_ref[...])
            kf_ref[rows, sl] = kh
            k_ref[rows, sl] = kh.astype(BF16)
        v = qkv[:, v0:v0 + N_KV_HEADS * HEAD_DIM]
        vf_ref[rows, :] = v
        v_ref[rows, :] = v.astype(BF16)


def _attn_qkv(lay, layer, j, x, mod, w_qkv, q_gain, k_gain, cos_t, sin_t):
    d = lay.d
    n_qkv = w_qkv.shape[2]
    kv = N_KV_HEADS * HEAD_DIM
    n_heads = (n_qkv - 2 * kv) // HEAD_DIM
    row = lambda i: (i, 0)

    def rope_row(i):
        r0 = i * TM
        return (jnp.where(r0 < lay.n_prompt, 0, 1 + ((r0 - lay.n_prompt) % lay.dec_seq) // TM), 0)

    return pl.pallas_call(
        functools.partial(_qkv_kernel, lay, n_heads),
        out_shape=(jax.ShapeDtypeStruct((lay.n_tok, n_heads * HEAD_DIM), BF16),
                   jax.ShapeDtypeStruct((lay.n_tok, kv), BF16),
                   jax.ShapeDtypeStruct((lay.n_tok, kv), BF16),
                   jax.ShapeDtypeStruct((lay.n_tok, kv), F32),
                   jax.ShapeDtypeStruct((lay.n_tok, kv), F32)),
        grid=(lay.n_tok // TM,),
        in_specs=[
            pl.BlockSpec((TM, d), row),
            pl.BlockSpec((None, None, 6, d), lambda i: (layer, lay.mod_index(i, TM), 0, 0)),
            _resident((None, d, n_qkv), lambda i: (j, 0, 0)),
            _resident((None, 1, HEAD_DIM), lambda i: (j, 0, 0)),
            _resident((None, 1, HEAD_DIM), lambda i: (j, 0, 0)),
            pl.BlockSpec((TM, HEAD_DIM), rope_row),
            pl.BlockSpec((TM, HEAD_DIM), rope_row),
        ],
        out_specs=(pl.BlockSpec((TM, n_heads * HEAD_DIM), row), pl.BlockSpec((TM, kv), row),
                   pl.BlockSpec((TM, kv), row), pl.BlockSpec((TM, kv), row), pl.BlockSpec((TM, kv), row)),
        scratch_shapes=[pltpu.VMEM((d, n_qkv), BF16)],
        compiler_params=_params(("arbitrary",)),
        name=f"attn_qkv_{layer}",
    )(x, mod, w_qkv, q_gain, k_gain, cos_t, sin_t)


def _attn_kernel(lay, layer, n_heads, has_cache, *refs):
    if has_cache:
        (q_ref, k_ref, v_ref, kc_ref, vc_ref, x_ref, mod_ref, lng_ref, lnb_ref, wo_ref, o_ref, att_scr, wo_scr) = refs
    else:
        (q_ref, k_ref, v_ref, x_ref, mod_ref, lng_ref, lnb_ref, wo_ref, o_ref, att_scr, wo_scr) = refs
    _round_weight_once((pl.program_id(0) == 0) & (pl.program_id(1) == 0), wo_ref, wo_scr)
    group = n_heads // N_KV_HEADS
    for hk in range(N_KV_HEADS):
        ksl = slice(hk * HEAD_DIM, (hk + 1) * HEAD_DIM)
        for g in range(group):
            hq = hk * group + g
            qsl = slice(hq * HEAD_DIM, (hq + 1) * HEAD_DIM)
            qh = q_ref[:, qsl]
            s = _dot_nt(qh, k_ref[:, ksl])
            m = jnp.max(s, axis=-1, keepdims=True)
            if has_cache:
                s_c = _dot_nt(qh, kc_ref[:, ksl])
                m = jnp.maximum(m, jnp.max(s_c, axis=-1, keepdims=True))
            p = jnp.exp(s - m)
            l = jnp.sum(p, axis=-1, keepdims=True)
            o = _dot(p.astype(BF16), v_ref[:, ksl])
            if has_cache:
                p_c = jnp.exp(s_c - m)
                l = l + jnp.sum(p_c, axis=-1, keepdims=True)
                o = o + _dot(p_c.astype(BF16), vc_ref[:, ksl])
            att_scr[:, qsl] = (o / l).astype(BF16)
    ga = mod_ref[2:3, :]
    mix = _dot(att_scr[...], wo_scr[...])
    o_ref[...] = _layer_norm(lay.alpha * x_ref[...] + ga * mix,
                             lng_ref[2 * layer:2 * layer + 1, :], lnb_ref[2 * layer:2 * layer + 1, :])


def _attention(lay, layer, j, q, k, v, cache, x, mod, lng, lnb, w_o, *, row0, n_seq, seq, tq):
    d = lay.d
    n_heads = q.shape[1] // HEAD_DIM
    kv = N_KV_HEADS * HEAD_DIM
    has_cache = cache is not None
    q_blocks = seq // tq
    tile = lambda b, i: (row0 // tq + b * q_blocks + i, 0)
    seq_blk = lambda b, i: (row0 // seq + b, 0)
    in_specs = [pl.BlockSpec((tq, n_heads * HEAD_DIM), tile),
                pl.BlockSpec((seq, kv), seq_blk), pl.BlockSpec((seq, kv), seq_blk)]
    args = [q, k, v]
    if has_cache:
        kc, vc = cache
        past = kc.shape[0] // n_seq
        in_specs += [pl.BlockSpec((past, kv), lambda b, i: (b, 0))] * 2
        args += [kc, vc]
    x_index = len(args)
    in_specs += [
        pl.BlockSpec((tq, d), tile),
        pl.BlockSpec((None, None, 6, d), lambda b, i: (layer, lay.mod_index(row0 // tq + b * q_blocks + i, tq), 0, 0)),
        _resident(lng.shape, lambda b, i: (0, 0)),
        _resident(lnb.shape, lambda b, i: (0, 0)),
        _resident((None, n_heads * HEAD_DIM, d), lambda b, i: (j, 0, 0)),
    ]
    args += [x, mod, lng, lnb, w_o]

    return pl.pallas_call(
        functools.partial(_attn_kernel, lay, layer, n_heads, has_cache),
        out_shape=jax.ShapeDtypeStruct((lay.n_tok, d), F32),
        grid=(n_seq, q_blocks),
        in_specs=in_specs,
        out_specs=pl.BlockSpec((tq, d), tile),
        scratch_shapes=[pltpu.VMEM((tq, n_heads * HEAD_DIM), BF16), pltpu.VMEM((n_heads * HEAD_DIM, d), BF16)],
        input_output_aliases={x_index: 0},
        compiler_params=_params(("arbitrary", "arbitrary")),
        name=f"attention_{layer}_{'latent' if has_cache else 'context'}",
    )(*args)


def _split3(x):
    hi = x.astype(BF16)
    r = x - hi.astype(F32)
    mid = r.astype(BF16)
    lo = (r - mid.astype(F32)).astype(BF16)
    return hi, mid, lo


def _gla_proj_kernel(lay, hk, hv, x_ref, mod_ref, win_ref, w1_ref, w2_ref, bg_ref,
                     q_ref, k_ref, v_ref, og_ref, bf_ref, bb_ref, win_scr):
    _round_weight_once(pl.program_id(0) == 0, win_ref, win_scr)
    sh, sc = mod_ref[0:1, :], mod_ref[1:2, :]
    dk = hk // GLA_HEADS
    r = lax.broadcasted_iota(jnp.int32, (GLA_CHUNK, GLA_CHUNK), 0)
    c = lax.broadcasted_iota(jnp.int32, (GLA_CHUNK, GLA_CHUNK), 1)
    lower = jnp.where(c <= r, 1.0, 0.0).astype(BF16)
    upper = jnp.where(c >= r, 1.0, 0.0).astype(BF16)
    sub = TM // GLA_PROJ_SUB_TILES
    for s in range(GLA_PROJ_SUB_TILES):
        rows = slice(s * sub, (s + 1) * sub)
        h = (x_ref[rows, :] * (1.0 + sc) + sh).astype(BF16)
        proj = _dot(h, win_scr[...])
        q_ref[rows, :] = proj[:, 0:hk] * (dk ** -0.5)
        k_ref[rows, :] = proj[:, hk:2 * hk]
        v_ref[rows, :] = proj[:, 2 * hk:2 * hk + hv].astype(BF16)
        og_ref[rows, :] = proj[:, 2 * hk + hv:2 * hk + 2 * hv]
        z = _dot(_dot(h, w1_ref[...]).astype(BF16), w2_ref[...]) + bg_ref[...]
        log_gate = (jnp.minimum(z, 0.0) - jnp.log1p(jnp.exp(-jnp.abs(z)))) * (1.0 / GLA_TAU)
        for ch in range(sub // GLA_CHUNK):
            crow = slice(ch * GLA_CHUNK, (ch + 1) * GLA_CHUNK)
            orow = slice(s * sub + ch * GLA_CHUNK, s * sub + (ch + 1) * GLA_CHUNK)
            f_hi, f_mid, f_lo = _split3(log_gate[crow, 0:hk])
            bf_ref[orow, :] = _dot(lower, f_hi) + _dot(lower, f_mid) + _dot(lower, f_lo)
            b_hi, b_mid, b_lo = _split3(log_gate[crow, hk:2 * hk])
            bb_ref[orow, :] = _dot(upper, b_hi) + _dot(upper, b_mid) + _dot(upper, b_lo)


def _gla_proj(lay, layer, j, x, mod, w_in, w1, w2, b_gate):
    d = lay.d
    n_in = w_in.shape[2]
    hk = w2.shape[2] // 2
    hv = (n_in - 2 * hk) // 2
    rank2 = w1.shape[2]
    row = lambda i: (i, 0)
    return pl.pallas_call(
        functools.partial(_gla_proj_kernel, lay, hk, hv),
        out_shape=(jax.ShapeDtypeStruct((lay.n_tok, hk), F32),
                   jax.ShapeDtypeStruct((lay.n_tok, hk), F32),
                   jax.ShapeDtypeStruct((lay.n_tok, hv), BF16),
                   jax.ShapeDtypeStruct((lay.n_tok, hv), F32),
                   jax.ShapeDtypeStruct((lay.n_tok, hk), F32),
                   jax.ShapeDtypeStruct((lay.n_tok, hk), F32)),
        grid=(lay.n_tok // TM,),
        in_specs=[
            pl.BlockSpec((TM, d), row),
            pl.BlockSpec((None, None, 6, d), lambda i: (layer, lay.mod_index(i, TM), 0, 0)),
            _resident((None, d, n_in), lambda i: (j, 0, 0)),
            _resident((None, d, rank2), lambda i: (j, 0, 0)),
            _resident((None, rank2, 2 * hk), lambda i: (j, 0, 0)),
            _resident((None, 1, 2 * hk), lambda i: (j, 0, 0)),
        ],
        out_specs=(pl.BlockSpec((TM, hk), row), pl.BlockSpec((TM, hk), row), pl.BlockSpec((TM, hv), row),
                   pl.BlockSpec((TM, hv), row), pl.BlockSpec((TM, hk), row), pl.BlockSpec((TM, hk), row)),
        scratch_shapes=[pltpu.VMEM((d, n_in), BF16)],
        compiler_params=_params(("arbitrary",)),
        name=f"gla_proj_{layer}",
    )(x, mod, w_in, w1, w2, b_gate)


def _block_diag(blocks):
    n = len(blocks)
    zero = jnp.zeros_like(blocks[0])
    return jnp.concatenate(
        [jnp.concatenate([blocks[i] if j == i else zero for j in range(n)], axis=1) for i in range(n)], axis=0)


def _gla_chain_mask(dtype):
    n = GLA_CHAINS * GLA_CHUNK
    r = lax.broadcasted_iota(jnp.int32, (n, n), 0)
    c = lax.broadcasted_iota(jnp.int32, (n, n), 1)
    shift = GLA_CHUNK.bit_length() - 1
    assert 1 << shift == GLA_CHUNK
    r_chain, r_t = jnp.right_shift(r, shift), jnp.bitwise_and(r, GLA_CHUNK - 1)
    c_chain, c_t = jnp.right_shift(c, shift), jnp.bitwise_and(c, GLA_CHUNK - 1)
    direction = 1 - 2 * jnp.bitwise_and(r_chain, 1)
    ordered = jnp.where((c_t - r_t) * direction <= 0, 1.0, 0.0)
    return jnp.where(r_chain == c_chain, ordered, 0.0).astype(dtype)


def _gla_fast_step(refs, st_scr, mask_scr, g, t, accumulate, n_chunks, dk, dv):
    q_ref, k_ref, v_ref, bf_ref, bb_ref, o_ref = refs
    mid = GLA_CHUNK // 2
    slots = []
    for u in range(2):
        qs_l, ks_l, v_l, qi_l, kh_l, dec_l, dst_l, old_l = [], [], [], [], [], [], [], []
        for j in range(2):
            h = 2 * g + j
            kcols = slice(h * dk, (h + 1) * dk)
            vcols = slice(h * dv, (h + 1) * dv)
            for forward, b_ref in ((True, bf_ref), (False, bb_ref)):
                c = (2 * t + u) if forward else (n_chunks - 1 - 2 * t - u)
                rows = pl.ds(pl.multiple_of(c * GLA_CHUNK, GLA_CHUNK), GLA_CHUNK)
                q, k, b, v = q_ref[rows, kcols], k_ref[rows, kcols], b_ref[rows, kcols], v_ref[rows, vcols]
                rho = b[mid:mid + 1]
                q_s = q * jnp.exp(b - rho)
                k_s = k * jnp.exp(rho - b)
                b_exit = b[GLA_CHUNK - 1:GLA_CHUNK] if forward else b[0:1]
                qs_l.append(q_s.astype(BF16))
                ks_l.append(k_s.astype(BF16))
                v_l.append(v)
                qi_l.append((q_s * jnp.exp(rho)).astype(BF16))
                kh_l.append((k_s * jnp.exp(b_exit - rho)).astype(BF16))
                dec_l.append(jnp.exp(b_exit))
                dst_l.append((rows, vcols))
                old_l.append(o_ref[rows, vcols] if accumulate else None)
        v_all = jnp.concatenate(v_l, axis=0)
        a = _dot_nt(jnp.concatenate(qs_l, axis=0), jnp.concatenate(ks_l, axis=0)).astype(BF16) * mask_scr[...]
        slots.append((_dot(a, v_all), _block_diag(qi_l), _block_diag(kh_l), v_all,
                      jnp.concatenate(dec_l, axis=1), dst_l, old_l))
    st = st_scr[g]
    stores = []
    for o_intra, q_bd, k_bd, v_all, decay, dst_l, old_l in slots:
        o_all = o_intra + _dot_nt(q_bd, st.astype(BF16))
        st = st * decay + _dot_tn(v_all, k_bd)
        for ci, ((rows, vcols), old) in enumerate(zip(dst_l, old_l)):
            o = o_all[ci * GLA_CHUNK:(ci + 1) * GLA_CHUNK]
            stores.append((rows, vcols, o if old is None else o + old))
    st_scr[g] = st
    for rows, vcols, o in stores:
        o_ref[rows, vcols] = o


def _gla_chunk_safe(q, k, v, b, st, forward):
    n_sub = GLA_CHUNK // GLA_SUB
    o_rows = []
    qi = (q * jnp.exp(b)).astype(BF16)
    o_inter = _dot_nt(qi, st.astype(BF16))
    t_idx = lax.broadcasted_iota(jnp.int32, (GLA_SUB, 1), 0)
    for blk in range(n_sub):
        rows = slice(blk * GLA_SUB, (blk + 1) * GLA_SUB)
        qb, kb, bb_, vb = q[rows], k[rows], b[rows], v[rows].astype(F32)
        o_blk = o_inter[rows]
        if forward and blk > 0:
            others = slice(0, blk * GLA_SUB)
            rho = b[blk * GLA_SUB - 1:blk * GLA_SUB]
        elif (not forward) and blk < n_sub - 1:
            others = slice((blk + 1) * GLA_SUB, GLA_CHUNK)
            rho = b[(blk + 1) * GLA_SUB:(blk + 1) * GLA_SUB + 1]
        else:
            others = None
        if others is not None:
            q_s = (qb * jnp.exp(bb_ - rho)).astype(BF16)
            k_s = (k[others] * jnp.exp(rho - b[others])).astype(BF16)
            a = _dot_nt(q_s, k_s)
            o_blk = o_blk + _dot(a.astype(BF16), v[others])
        for s in range(GLA_SUB):
            keep = (t_idx >= s) if forward else (t_idx <= s)
            decay = jnp.exp(jnp.where(keep, bb_ - bb_[s:s + 1], NEG_BIG))
            a_col = jnp.sum(qb * kb[s:s + 1] * decay, axis=-1, keepdims=True)
            o_blk = o_blk + a_col * vb[s:s + 1]
        o_rows.append(o_blk)
    b_exit = b[GLA_CHUNK - 1:GLA_CHUNK] if forward else b[0:1]
    k_hat = (k * jnp.exp(b_exit - b)).astype(BF16)
    st_new = st * jnp.exp(b_exit) + _dot_tn(v, k_hat)
    return jnp.concatenate(o_rows, axis=0), st_new


def _gla_scan_kernel(n_chunks, hps, dk, dv, has_init, *refs):
    n_groups = hps // 2
    if has_init:
        (q_ref, k_ref, v_ref, bf_ref, bb_ref, s0f_ref, s0b_ref, o_ref, st_scr, mask_scr) = refs
    else:
        (q_ref, k_ref, v_ref, bf_ref, bb_ref, o_ref, sf_ref, sb_ref, st_scr, mask_scr) = refs
    data_refs = (q_ref, k_ref, v_ref, bf_ref, bb_ref, o_ref)

    def chain_cols(h, forward):
        ci = 2 * (h % 2) + (0 if forward else 1)
        return h // 2, slice(ci * dk, (ci + 1) * dk)

    for h in range(hps):
        for forward in (True, False):
            g, cols = chain_cols(h, forward)
            if has_init:
                st_scr[g, :, cols] = (s0f_ref if forward else s0b_ref)[h].T
            else:
                st_scr[g, :, cols] = jnp.zeros((dv, dk), F32)
    mask_scr[...] = _gla_chain_mask(BF16)

    def safe_step(i, accumulate):
        pending = []
        for h in range(hps):
            kcols = slice(h * dk, (h + 1) * dk)
            vcols = slice(h * dv, (h + 1) * dv)
            for forward, b_ref in ((True, bf_ref), (False, bb_ref)):
                c = i if forward else n_chunks - 1 - i
                rows = pl.ds(pl.multiple_of(c * GLA_CHUNK, GLA_CHUNK), GLA_CHUNK)
                g, cols = chain_cols(h, forward)
                o, st_new = _gla_chunk_safe(q_ref[rows, kcols], k_ref[rows, kcols], v_ref[rows, vcols],
                                            b_ref[rows, kcols], st_scr[g, :, cols], forward)
                if accumulate:
                    o = o + o_ref[rows, vcols]
                pending.append((rows, vcols, g, cols, o, st_new))
        for rows, vcols, g, cols, o, st_new in pending:
            st_scr[g, :, cols] = st_new
            o_ref[rows, vcols] = o

    def fast_step(t, accumulate):
        for g in range(n_groups):
            _gla_fast_step(data_refs, st_scr, mask_scr, g, t, accumulate, n_chunks, dk, dv)

    def run(step, n_steps):
        def first(i, carry):
            step(i, False)
            return carry

        def second(i, carry):
            step(i, True)
            return carry

        lax.fori_loop(0, n_steps // 2, first, 0)
        lax.fori_loop(n_steps // 2, n_steps, second, 0)

    span = jnp.zeros((1, hps * dk), F32)
    for c in range(n_chunks):
        top = slice(c * GLA_CHUNK, c * GLA_CHUNK + 1)
        bottom = slice((c + 1) * GLA_CHUNK - 1, (c + 1) * GLA_CHUNK)
        span = jnp.maximum(span, jnp.maximum(bf_ref[top, :] - bf_ref[bottom, :],
                                             bb_ref[bottom, :] - bb_ref[top, :]))
    bounded = jnp.max(span) < GLA_SAFE_RANGE

    @pl.when(bounded)
    def _():
        run(fast_step, n_chunks // 2)

    @pl.when(jnp.logical_not(bounded))
    def _():
        run(safe_step, n_chunks)

    if not has_init:
        for h in range(hps):
            for forward in (True, False):
                g, cols = chain_cols(h, forward)
                (sf_ref if forward else sb_ref)[h] = st_scr[g, :, cols].T


def _gla_scan(lay, j, q, k, v, bf, bb, init, *, row0, n_seq, seq, hps):
    hk, hv = q.shape[1], v.shape[1]
    dk, dv = hk // GLA_HEADS, hv // GLA_HEADS
    n_chunks = seq // GLA_CHUNK
    groups = GLA_HEADS // hps
    assert n_chunks % 4 == 0 and GLA_HEADS % hps == 0 and hps % 2 == 0
    has_init = init is not None
    blk = lambda b, g: (row0 // seq + b, g)
    kspec = pl.BlockSpec((seq, hps * dk), blk)
    vspec = pl.BlockSpec((seq, hps * dv), blk)
    in_specs = [kspec, kspec, vspec, kspec, kspec]
    args = [q, k, v, bf, bb]
    o_shape = jax.ShapeDtypeStruct((n_seq * seq, hv), F32)
    o_spec = pl.BlockSpec((seq, hps * dv), lambda b, g: (b, g))
    if has_init:
        s0f, s0b = init
        st_spec = pl.BlockSpec((None, hps, dk, dv), lambda b, g: (b, j * groups + g, 0, 0))
        in_specs += [st_spec, st_spec]
        args += [s0f, s0b]
        out_shape, out_specs = o_shape, o_spec
    else:
        st_shape = jax.ShapeDtypeStruct((n_seq, GLA_HEADS, dk, dv), F32)
        st_spec = pl.BlockSpec((None, hps, dk, dv), lambda b, g: (b, g, 0, 0))
        out_shape, out_specs = (o_shape, st_shape, st_shape), (o_spec, st_spec, st_spec)

    return pl.pallas_call(
        functools.partial(_gla_scan_kernel, n_chunks, hps, dk, dv, has_init),
        out_shape=out_shape,
        grid=(n_seq, groups),
        in_specs=in_specs,
        out_specs=out_specs,
        scratch_shapes=[pltpu.VMEM((hps // 2, dv, GLA_CHAINS * dk), F32),
                        pltpu.VMEM((GLA_CHAINS * GLA_CHUNK, GLA_CHAINS * GLA_CHUNK), BF16)],
        compiler_params=_params(("arbitrary", "arbitrary")),
        name=f"gla_scan_{'latent' if has_init else 'context'}",
    )(*args)


def _gla_out_kernel(lay, layer, oc_ref, ol_ref, og_ref, x_ref, mod_ref, lng_ref, lnb_ref, ng_ref, wo_ref,
                    out_ref, z_scr, wo_scr):
    _round_weight_once(pl.program_id(0) == 0, wo_ref, wo_scr)
    dv = ng_ref.shape[1]
    is_context = pl.program_id(0) * TM < lay.n_prompt
    for h in range(GLA_HEADS):
        sl = slice(h * dv, (h + 1) * dv)
        o = jnp.where(is_context, oc_ref[:, sl], ol_ref[:, sl])
        og = og_ref[:, sl]
        o = o * lax.rsqrt(jnp.mean(o * o, axis=-1, keepdims=True) + RMS_EPS) * ng_ref[...]
        z_scr[:, sl] = (o * (og * jax.nn.sigmoid(og))).astype(BF16)
    ga = mod_ref[2:3, :]
    mix = _dot(z_scr[...], wo_scr[...])
    out_ref[...] = _layer_norm(lay.alpha * x_ref[...] + ga * mix,
                               lng_ref[2 * layer:2 * layer + 1, :], lnb_ref[2 * layer:2 * layer + 1, :])


def _gla_out(lay, layer, j, o_context, o_latent, og, x, mod, lng, lnb, norm_g, w_o):
    d = lay.d
    hv = og.shape[1]
    dv = hv // GLA_HEADS
    row = lambda i: (i, 0)
    n_ctx = lay.n_prompt // TM
    n_lat = (lay.n_tok - lay.n_prompt) // TM
    return pl.pallas_call(
        functools.partial(_gla_out_kernel, lay, layer),
        out_shape=jax.ShapeDtypeStruct((lay.n_tok, d), F32),
        grid=(lay.n_tok // TM,),
        in_specs=[
            pl.BlockSpec((TM, hv), lambda i: (jnp.minimum(i, n_ctx - 1), 0)),
            pl.BlockSpec((TM, hv), lambda i: (jnp.clip(i - n_ctx, 0, n_lat - 1), 0)),
            pl.BlockSpec((TM, hv), row), pl.BlockSpec((TM, d), row),
            pl.BlockSpec((None, None, 6, d), lambda i: (layer, lay.mod_index(i, TM), 0, 0)),
            _resident(lng.shape, lambda i: (0, 0)),
            _resident(lnb.shape, lambda i: (0, 0)),
            _resident((None, 1, dv), lambda i: (j, 0, 0)),
            _resident((None, hv, d), lambda i: (j, 0, 0)),
        ],
        out_specs=pl.BlockSpec((TM, d), row),
        scratch_shapes=[pltpu.VMEM((TM, hv), BF16), pltpu.VMEM((hv, d), BF16)],
        compiler_params=_params(("arbitrary",)),
        name=f"gla_out_{layer}",
    )(o_context, o_latent, og, x, mod, lng, lnb, norm_g, w_o)


def _rope_tables(lay):
    n_freq = HEAD_DIM // 4
    pos = np.arange(lay.dec_seq)
    freqs = (np.float32(ROPE_THETA) ** (-np.arange(n_freq, dtype=np.float32) / np.float32(n_freq))).astype(np.float32)
    ang_r = (pos // GRID_W).astype(np.float32)[:, None] * freqs
    ang_c = (pos % GRID_W).astype(np.float32)[:, None] * freqs
    cos = np.concatenate([np.cos(ang_r)] * 2 + [np.cos(ang_c)] * 2, axis=-1)
    sin = np.concatenate([-np.sin(ang_r), np.sin(ang_r), -np.sin(ang_c), np.sin(ang_c)], axis=-1)
    cos = np.concatenate([np.ones((TM, HEAD_DIM), np.float32), cos], axis=0).astype(np.float32)
    sin = np.concatenate([np.zeros((TM, HEAD_DIM), np.float32), sin], axis=0).astype(np.float32)
    return jnp.asarray(cos), jnp.asarray(sin)


def kernel(x_prompt, x_sample, c, cache_k, cache_v, state_gla_fwd, state_gla_bwd, c_ctx, w_ada, b_ada, ln_g, ln_b, conv_w_in, conv_w, conv_w_out, attn_w_qkv, attn_q_norm, attn_k_norm, attn_w_o, gla_w_in, gla_w_gate1, gla_w_gate2, gla_b_gate, gla_norm, gla_w_o, ffn_w_in, ffn_w_out):
    batch, seq, d = x_prompt.shape
    dec_batch, dec_seq, _ = x_sample.shape
    depth = w_ada.shape[0]
    lay = _Layout(batch, seq, dec_batch, dec_seq, d, depth)
    kv = N_KV_HEADS * HEAD_DIM
    past = cache_k.shape[2]

    x = (x_prompt.reshape(lay.n_prompt, d), x_sample.reshape(dec_batch * dec_seq, d))
    cond = jnp.concatenate([c_ctx[None, :], c, jnp.zeros((MOD_ROWS - 1 - dec_batch, d), F32)], axis=0)
    mod = _modulation(cond, w_ada, b_ada).reshape(depth, MOD_ROWS, 6, d)
    lng = ln_g.reshape(depth * 2, d)
    lnb = ln_b.reshape(depth * 2, d)

    rank = gla_w_gate1.shape[3]
    hk = gla_w_gate2.shape[3]
    g1 = jnp.concatenate([gla_w_gate1[:, 0], gla_w_gate1[:, 1]], axis=-1).astype(BF16)
    zeros = jnp.zeros_like(gla_w_gate2[:, 0])
    g2 = jnp.concatenate([jnp.concatenate([gla_w_gate2[:, 0], zeros], axis=-1),
                          jnp.concatenate([zeros, gla_w_gate2[:, 1]], axis=-1)], axis=1).astype(BF16)
    gb = jnp.concatenate([gla_b_gate[:, 0], gla_b_gate[:, 1]], axis=-1)[:, None, :]
    cos_t, sin_t = _rope_tables(lay)

    new_k, new_v, new_sf, new_sb = [], [], [], []
    for i in range(depth):
        kind, j = i % N_MIXERS, i // N_MIXERS
        if kind == 0:
            x = _conv_mixer(lay, i, j, x if i == 0 else (x,), mod, lng, lnb, conv_w_in, conv_w, conv_w_out)
        elif kind == 1:
            q, k, v, kf, vf = _attn_qkv(lay, i, j, x, mod, attn_w_qkv,
                                        attn_q_norm[:, None, :], attn_k_norm[:, None, :], cos_t, sin_t)
            new_k.append(kf[:lay.n_prompt].reshape(batch, seq, N_KV_HEADS, HEAD_DIM))
            new_v.append(vf[:lay.n_prompt].reshape(batch, seq, N_KV_HEADS, HEAD_DIM))
            kc = cache_k[:, j].reshape(dec_batch * past, kv).astype(BF16)
            vc = cache_v[:, j].reshape(dec_batch * past, kv).astype(BF16)
            x = _attention(lay, i, j, q, k, v, None, x, mod, lng, lnb, attn_w_o,
                           row0=0, n_seq=batch, seq=seq, tq=seq)
            x = _attention(lay, i, j, q, k, v, (kc, vc), x, mod, lng, lnb, attn_w_o,
                           row0=lay.n_prompt, n_seq=dec_batch, seq=dec_seq, tq=TM)
        else:
            q, k, v, og, bf, bb = _gla_proj(lay, i, j, x, mod, gla_w_in, g1, g2, gb)
            s0f = state_gla_fwd.reshape(dec_batch, -1, *state_gla_fwd.shape[3:])
            s0b = state_gla_bwd.reshape(dec_batch, -1, *state_gla_bwd.shape[3:])
            o_ctx, sf, sb = _gla_scan(lay, j, q, k, v, bf, bb, None, row0=0, n_seq=batch, seq=seq,
                                      hps=GLA_HEADS_PER_STEP_CONTEXT)
            o_lat = _gla_scan(lay, j, q, k, v, bf, bb, (s0f, s0b), row0=lay.n_prompt, n_seq=dec_batch,
                              seq=dec_seq, hps=GLA_HEADS_PER_STEP_LATENT)
            new_sf.append(sf)
            new_sb.append(sb)
            x = _gla_out(lay, i, j, o_ctx, o_lat, og, x, mod, lng, lnb, gla_norm[:, None, :], gla_w_o)
        if i < depth - 1:
            x = _ffn(lay, i, x, mod, lng, lnb, ffn_w_in, ffn_w_out, split_out=False)

    y_prompt, y_sample = _ffn(lay, depth - 1, x, mod, lng, lnb, ffn_w_in, ffn_w_out, split_out=True)
    y_prompt = y_prompt.reshape(batch, seq, d)
    y_sample = y_sample.reshape(dec_batch, dec_seq, d)
    return (y_prompt, y_sample, jnp.stack(new_k, axis=1), jnp.stack(new_v, axis=1),
            jnp.stack(new_sf, axis=1), jnp.stack(new_sb, axis=1))
```

```python
import functools

import jax
import jax.numpy as jnp
import numpy as np
from jax import lax
from jax.experimental import pallas as pl
from jax.experimental.pallas import tpu as pltpu

F32 = jnp.float32
BF16 = jnp.bfloat16

N_MIXERS = 3
CONV_WIDTH = 3
HEAD_DIM = 128
N_KV_HEADS = 2
GRID_W = 64
ROPE_THETA = 10000.0
GLA_HEADS = 4
GLA_TAU = 16.0
GLA_CHUNK = 64
LN_EPS = 1e-5
RMS_EPS = 1e-6

LANES = 128
BF16_SUBLANES = 16
VMEM_LIMIT = 58 * 1024 * 1024

MOD_ROWS = 16
MOD_NT = 1536
TM = 512
SUB_TILES = 2
HALO = BF16_SUBLANES
WEIGHT_CAST_ROWS = 128
GLA_SUB = 16
GLA_SAFE_RANGE = 80.0
GLA_CHAINS = 4
GLA_HEADS_PER_STEP_CONTEXT = 4
GLA_HEADS_PER_STEP_LATENT = 2
NEG_BIG = -1e30


def _dot(a, b):
    return jnp.dot(a, b, preferred_element_type=F32)


def _dot_nt(a, b):
    return lax.dot_general(a, b, (((1,), (1,)), ((), ())), preferred_element_type=F32)


def _dot_tn(a, b):
    return lax.dot_general(a, b, (((0,), (0,)), ((), ())), preferred_element_type=F32)


def _layer_norm(y, g, b):
    mu = jnp.mean(y, axis=-1, keepdims=True)
    yc = y - mu
    var = jnp.mean(yc * yc, axis=-1, keepdims=True)
    return yc * lax.rsqrt(var + LN_EPS) * g + b


def _silu(x):
    return x * jax.nn.sigmoid(x)


def _resident(block_shape, index_map):
    return pl.BlockSpec(block_shape, index_map, pipeline_mode=pl.Buffered(1))


class _Layout:
    def __init__(self, batch, seq, dec_batch, dec_seq, d_model, depth):
        self.batch, self.seq, self.dec_batch, self.dec_seq = batch, seq, dec_batch, dec_seq
        self.d, self.depth = d_model, depth
        self.n_prompt = batch * seq
        self.n_tok = self.n_prompt + dec_batch * dec_seq
        self.n_steps = self.n_tok // TM
        assert self.n_prompt % TM == 0 and dec_seq % TM == 0 and TM % seq == 0
        assert seq & (seq - 1) == 0 and dec_seq & (dec_seq - 1) == 0
        self.alpha = (2.0 * depth) ** 0.25

    def mod_index(self, i, rows):
        r0 = i * rows
        return jnp.where(r0 < self.n_prompt, 0, 1 + (r0 - self.n_prompt) // self.dec_seq)

    def mod_spec(self, tile_of):
        def index(*ids):
            tile, rows = tile_of(*ids)
            return (self.mod_index(tile, rows), 0, 0)

        return pl.BlockSpec((None, 6, self.d), index)


class _Job:
    def __init__(self, args, in_specs, out_shapes, out_specs, fn):
        self.args, self.in_specs, self.out_shapes, self.out_specs, self.fn = args, in_specs, out_shapes, out_specs, fn


def _cast_job(w, j, n_steps):
    _, n_rows, n_cols = w.shape
    rows = n_rows // n_steps
    assert rows * n_steps == n_rows and rows % BF16_SUBLANES == 0

    def fn(in_refs, out_refs):
        out_refs[0][...] = in_refs[0][...].astype(BF16)

    return _Job([w], [pl.BlockSpec((None, rows, n_cols), lambda i: (j, i, 0))],
                [jax.ShapeDtypeStruct((n_rows, n_cols), BF16)], [pl.BlockSpec((rows, n_cols), lambda i: (i, 0))], fn)


def _mod_tile(in_refs, out_refs):
    cond_ref, w_ref, b_ref = in_refs
    out_refs[0][...] = _dot(_silu(cond_ref[...]).astype(BF16), w_ref[...].astype(BF16)) + b_ref[...]


def _mod_job(cond, w_ada, b_ada3, layer, n_steps):
    _, d, n_out = w_ada.shape
    cols = n_out // n_steps
    assert cols * n_steps == n_out and cols % LANES == 0
    return _Job([cond, w_ada, b_ada3],
                [_resident((MOD_ROWS, d), lambda i: (0, 0)),
                 pl.BlockSpec((None, d, cols), lambda i: (layer, 0, i)),
                 pl.BlockSpec((None, 1, cols), lambda i: (layer, 0, i))],
                [jax.ShapeDtypeStruct((MOD_ROWS, n_out), F32)], [pl.BlockSpec((MOD_ROWS, cols), lambda i: (0, i))],
                _mod_tile)


def _run(kernel_fn, *, name, grid, args, in_specs, out_shape, out_specs, scratch_shapes=(), jobs=(), aliases=None):
    n_in, n_out = len(args), len(out_shape)

    def body(*refs):
        pos = n_in
        job_in = []
        for jb in jobs:
            job_in.append(refs[pos:pos + len(jb.args)])
            pos += len(jb.args)
        main_out = refs[pos:pos + n_out]
        pos += n_out
        job_out = []
        for jb in jobs:
            job_out.append(refs[pos:pos + len(jb.out_shapes)])
            pos += len(jb.out_shapes)
        for jb, ji, jo in zip(jobs, job_in, job_out):
            jb.fn(ji, jo)
        kernel_fn(*refs[:n_in], *main_out, *refs[pos:])

    outs = pl.pallas_call(
        body,
        out_shape=tuple(out_shape) + tuple(s for jb in jobs for s in jb.out_shapes),
        grid=grid,
        in_specs=list(in_specs) + [s for jb in jobs for s in jb.in_specs],
        out_specs=tuple(out_specs) + tuple(s for jb in jobs for s in jb.out_specs),
        scratch_shapes=list(scratch_shapes),
        input_output_aliases=aliases or {},
        compiler_params=pltpu.CompilerParams(dimension_semantics=("arbitrary",) * len(grid),
                                             vmem_limit_bytes=VMEM_LIMIT),
        name=name,
    )(*args, *[a for jb in jobs for a in jb.args])
    main, rest = outs[:n_out], list(outs[n_out:])
    job_results = []
    for jb in jobs:
        job_results.append(tuple(rest[:len(jb.out_shapes)]))
        rest = rest[len(jb.out_shapes):]
    return tuple(main), job_results


def _weight_in(w):
    if isinstance(w, tuple):
        arr, j = w
        _, r, c = arr.shape
        return arr, _resident((None, r, c), lambda *_: (j, 0, 0)), pltpu.VMEM((r, c), BF16)
    r, c = w.shape
    return w, _resident((r, c), lambda *_: (0, 0)), None


def _weight_shape(w):
    return w[0].shape[1:] if isinstance(w, tuple) else w.shape


def _round_weight_once(first_step, w_ref, w_scr):
    n_rows = w_ref.shape[0]
    rows = min(n_rows, WEIGHT_CAST_ROWS)
    assert n_rows % rows == 0

    @pl.when(first_step)
    def _():
        def body(c, carry):
            sl = pl.ds(pl.multiple_of(c * rows, rows), rows)
            w_scr[sl, :] = w_ref[sl, :].astype(BF16)
            return carry

        lax.fori_loop(0, n_rows // rows, body, 0)


def _bf16_weights(first_step, w_refs, scratch):
    scratch = list(scratch)
    out = []
    for w in w_refs:
        if w.dtype == BF16:
            out.append(w)
        else:
            scr = scratch.pop(0)
            _round_weight_once(first_step, w, scr)
            out.append(scr)
    return out, scratch


def _modulation(cond, w_ada, b_ada3, layer):
    _, d, n_out = w_ada.shape
    job = _mod_job(cond, w_ada, b_ada3, layer, n_out // MOD_NT)
    _, (result,) = _run(lambda: None, name=f"modulation_{layer}", grid=(n_out // MOD_NT,), args=[], in_specs=[],
                        out_shape=[], out_specs=[], jobs=[job])
    return result[0]


def _conv_kernel(lay, layer, split, *refs):
    d = lay.d
    i = pl.program_id(0)
    if split:
        (xca, xpa, xna, xcb, xpb, xnb, mod_ref, lng_ref, lnb_ref, win_ref, cw_ref, wout_ref,
         o_ref, h_scr, uu_scr, *scratch) = refs
        is_context = i * TM < lay.n_prompt
        x = jnp.where(is_context, xca[...], xcb[...])
        x_prev = jnp.where(is_context, xpa[...], xpb[...])
        x_next = jnp.where(is_context, xna[...], xnb[...])
    else:
        (xc_ref, xp_ref, xn_ref, mod_ref, lng_ref, lnb_ref, win_ref, cw_ref, wout_ref,
         o_ref, h_scr, uu_scr, *scratch) = refs
        x, x_prev, x_next = xc_ref[...], xp_ref[...], xn_ref[...]
    (win, wout), _ = _bf16_weights(i == 0, (win_ref, wout_ref), scratch)
    sh, sc, ga = mod_ref[0:1, :], mod_ref[1:2, :], mod_ref[2:3, :]
    one_sc = 1.0 + sc
    h_scr[0:HALO, :] = (x_prev * one_sc + sh).astype(BF16)
    h_scr[HALO:HALO + TM, :] = (x * one_sc + sh).astype(BF16)
    h_scr[HALO + TM:HALO + TM + HALO, :] = (x_next * one_sc + sh).astype(BF16)
    cgu = _dot(h_scr[...], win[:, d:3 * d])
    uu_scr[...] = cgu[:, 0:d] * cgu[:, d:2 * d]
    sub = TM // SUB_TILES
    for s in range(SUB_TILES):
        r0 = s * sub
        bg = _dot(h_scr[HALO + r0:HALO + r0 + sub, :], win[:, 0:d])
        row = i * TM + r0 + lax.broadcasted_iota(jnp.int32, (sub, 1), 0)
        seq_len = jnp.where(row < lay.n_prompt, lay.seq, lay.dec_seq)
        pos = jnp.bitwise_and(row, seq_len - 1)
        u_prev = jnp.where(pos != 0, uu_scr[pl.ds(HALO + r0 - 1, sub), :], 0.0)
        u_next = jnp.where(pos != seq_len - 1, uu_scr[pl.ds(HALO + r0 + 1, sub), :], 0.0)
        y = (u_prev * cw_ref[0:1, :] + uu_scr[pl.ds(HALO + r0, sub), :] * cw_ref[1:2, :]
             + u_next * cw_ref[2:3, :])
        mix = _dot((bg * y).astype(BF16), wout[...])
        o_ref[r0:r0 + sub, :] = _layer_norm(lay.alpha * x[r0:r0 + sub] + ga * mix,
                                            lng_ref[2 * layer:2 * layer + 1, :],
                                            lnb_ref[2 * layer:2 * layer + 1, :])


def _conv_mixer(lay, layer, j, x_parts, mod, lng, lnb, w_in, cw, w_out, jobs):
    d = lay.d
    per = TM // HALO
    x_specs, x_args = [], []
    tile0 = 0
    for part in x_parts:
        n_tiles = part.shape[0] // TM

        def center(i, t0=tile0, n=n_tiles):
            return (jnp.clip(i - t0, 0, n - 1), 0)

        def prev_halo(i, t0=tile0, n=n_tiles):
            return (jnp.clip((i - t0) * per - 1, 0, n * per - 1), 0)

        def next_halo(i, t0=tile0, n=n_tiles):
            return (jnp.clip((i - t0 + 1) * per, 0, n * per - 1), 0)

        x_specs += [pl.BlockSpec((TM, d), center), pl.BlockSpec((HALO, d), prev_halo),
                    pl.BlockSpec((HALO, d), next_halo)]
        x_args += [part, part, part]
        tile0 += n_tiles
    win_arr, win_spec, win_scr = _weight_in(w_in)
    wout_arr, wout_spec, wout_scr = _weight_in(w_out)
    (x,), job_results = _run(
        functools.partial(_conv_kernel, lay, layer, len(x_parts) == 2),
        name=f"conv_mixer_{layer}", grid=(lay.n_steps,),
        args=x_args + [mod, lng, lnb, win_arr, cw, wout_arr],
        in_specs=x_specs + [
            lay.mod_spec(lambda i: (i, TM)),
            _resident(lng.shape, lambda i: (0, 0)),
            _resident(lnb.shape, lambda i: (0, 0)),
            win_spec,
            _resident((None, CONV_WIDTH, d), lambda i: (j, 0, 0)),
            wout_spec,
        ],
        out_shape=[jax.ShapeDtypeStruct((lay.n_tok, d), F32)],
        out_specs=[pl.BlockSpec((TM, d), lambda i: (i, 0))],
        scratch_shapes=[pltpu.VMEM((TM + 2 * HALO, d), BF16), pltpu.VMEM((TM + 2 * HALO, d), F32)]
        + [s for s in (win_scr, wout_scr) if s is not None],
        jobs=jobs)
    return x, job_results


def _ffn_kernel(lay, layer, split_out, gla_pre, *refs):
    x_ref, mod_ref, lng_ref, lnb_ref, win_ref, wout_ref = refs[:6]
    pos = 6
    if gla_pre:
        oc_ref, ol_ref, og_ref, ng_ref, wo_ref = refs[pos:pos + 5]
        pos += 5
    n_out = 2 if split_out else 1
    out_refs = refs[pos:pos + n_out]
    scratch = refs[pos + n_out:]
    d_ff = wout_ref.shape[0]
    is_context = pl.program_id(0) * TM < lay.n_prompt
    sh, sc, ga = mod_ref[3:4, :], mod_ref[4:5, :], mod_ref[5:6, :]
    ln_g = lng_ref[2 * layer + 1:2 * layer + 2, :]
    ln_b = lnb_ref[2 * layer + 1:2 * layer + 2, :]
    results = []
    for s in range(SUB_TILES):
        rows = slice(s * (TM // SUB_TILES), (s + 1) * (TM // SUB_TILES))
        x = x_ref[rows, :]
        if gla_pre:
            z_scr, = scratch
            dv = ng_ref.shape[1]
            for h in range(GLA_HEADS):
                sl = slice(h * dv, (h + 1) * dv)
                o = jnp.where(is_context, oc_ref[rows, sl], ol_ref[rows, sl])
                o = o * lax.rsqrt(jnp.mean(o * o, axis=-1, keepdims=True) + RMS_EPS) * ng_ref[...]
                z_scr[rows, sl] = (o * _silu(og_ref[rows, sl])).astype(BF16)
            mix = _dot(z_scr[rows, :], wo_ref[...])
            x = _layer_norm(lay.alpha * x + mod_ref[2:3, :] * mix,
                            lng_ref[2 * layer:2 * layer + 1, :], lnb_ref[2 * layer:2 * layer + 1, :])
        h = (x * (1.0 + sc) + sh).astype(BF16)
        g = _dot(h, win_ref[:, 0:d_ff])
        u = _dot(h, win_ref[:, d_ff:2 * d_ff])
        a = (_silu(g) * u).astype(BF16)
        y = _dot(a, wout_ref[...])
        results.append((rows, _layer_norm(lay.alpha * x + ga * y, ln_g, ln_b)))
    if split_out:
        @pl.when(is_context)
        def _():
            for rows, res in results:
                out_refs[0][rows, :] = res

        @pl.when(jnp.logical_not(is_context))
        def _():
            for rows, res in results:
                out_refs[1][rows, :] = res
    else:
        for rows, res in results:
            out_refs[0][rows, :] = res


def _ffn(lay, layer, x, mod, lng, lnb, w_in, w_out, *, split_out, gla_pre, jobs):
    d = lay.d
    d_ff = w_out.shape[0]
    row = lambda i: (i, 0)
    n_ctx = lay.n_prompt // TM
    n_lat = lay.n_steps - n_ctx
    ctx_row = lambda i: (jnp.minimum(i, n_ctx - 1), 0)
    lat_row = lambda i: (jnp.clip(i - n_ctx, 0, n_lat - 1), 0)
    args = [x, mod, lng, lnb, w_in, w_out]
    in_specs = [
        pl.BlockSpec((TM, d), row),
        lay.mod_spec(lambda i: (i, TM)),
        _resident(lng.shape, lambda i: (0, 0)),
        _resident(lnb.shape, lambda i: (0, 0)),
        _resident((d, 2 * d_ff), lambda i: (0, 0)),
        _resident((d_ff, d), lambda i: (0, 0)),
    ]
    scratch = []
    if gla_pre is not None:
        o_context, o_latent, og, norm_g, w_o = gla_pre
        hv = og.shape[1]
        args += [o_context, o_latent, og, norm_g, w_o]
        in_specs += [pl.BlockSpec((TM, hv), ctx_row), pl.BlockSpec((TM, hv), lat_row), pl.BlockSpec((TM, hv), row),
                     _resident(norm_g.shape, lambda i: (0, 0)), _resident((hv, d), lambda i: (0, 0))]
        scratch = [pltpu.VMEM((TM, hv), BF16)]
    if split_out:
        out_shape = [jax.ShapeDtypeStruct((lay.n_prompt, d), F32),
                     jax.ShapeDtypeStruct((lay.n_tok - lay.n_prompt, d), F32)]
        out_specs = [pl.BlockSpec((TM, d), ctx_row), pl.BlockSpec((TM, d), lat_row)]
    else:
        out_shape = [jax.ShapeDtypeStruct((lay.n_tok, d), F32)]
        out_specs = [pl.BlockSpec((TM, d), row)]
    return _run(functools.partial(_ffn_kernel, lay, layer, split_out, gla_pre is not None),
                name=f"ffn_{layer}", grid=(lay.n_steps,), args=args, in_specs=in_specs,
                out_shape=out_shape, out_specs=out_specs, scratch_shapes=scratch, jobs=jobs)


def _qkv_kernel(lay, n_heads, x_ref, mod_ref, w_ref, qg_ref, kg_ref, cos_ref, sin_ref,
                q_ref, k_ref, v_ref, kf_ref, vf_ref, *scratch):
    (w,), _ = _bf16_weights(pl.program_id(0) == 0, (w_ref,), scratch)
    sh, sc = mod_ref[0:1, :], mod_ref[1:2, :]
    lane = lax.broadcasted_iota(jnp.int32, (1, HEAD_DIM), 1)
    first_half = jnp.bitwise_and(lane, HEAD_DIM // 4) == 0
    q_gain = qg_ref[...] * (HEAD_DIM ** -0.5)
    k0 = n_heads * HEAD_DIM
    v0 = k0 + N_KV_HEADS * HEAD_DIM
    for s in range(SUB_TILES):
        rows = slice(s * (TM // SUB_TILES), (s + 1) * (TM // SUB_TILES))
        h = (x_ref[rows, :] * (1.0 + sc) + sh).astype(BF16)
        qkv = _dot(h, w[...])
        cos, sin = cos_ref[rows, :], sin_ref[rows, :]

        def norm_rope(t, g):
            t = t * lax.rsqrt(jnp.mean(t * t, axis=-1, keepdims=True) + RMS_EPS) * g
            partner = jnp.where(first_half,
                                pltpu.roll(t, HEAD_DIM - HEAD_DIM // 4, axis=1),
                                pltpu.roll(t, HEAD_DIM // 4, axis=1))
            return t * cos + partner * sin

        for hq in range(n_heads):
            sl = slice(hq * HEAD_DIM, (hq + 1) * HEAD_DIM)
            q_ref[rows, sl] = norm_rope(qkv[:, sl], q_gain).astype(BF16)
        for hk in range(N_KV_HEADS):
            sl = slice(hk * HEAD_DIM, (hk + 1) * HEAD_DIM)
            kh = norm_rope(qkv[:, k0 + hk * HEAD_DIM:k0 + (hk + 1) * HEAD_DIM], kg_ref[...])
            kf_ref[rows, sl] = kh
            k_ref[rows, sl] = kh.astype(BF16)
        v = qkv[:, v0:v0 + N_KV_HEADS * HEAD_DIM]
        vf_ref[rows, :] = v
        v_ref[rows, :] = v.astype(BF16)


def _attn_qkv(lay, layer, j, x, mod, w_qkv, q_gain, k_gain, cos_t, sin_t, jobs):
    d = lay.d
    n_qkv = _weight_shape(w_qkv)[1]
    kv = N_KV_HEADS * HEAD_DIM
    n_heads = (n_qkv - 2 * kv) // HEAD_DIM
    row = lambda i: (i, 0)

    def rope_row(i):
        r0 = i * TM
        return (jnp.where(r0 < lay.n_prompt, 0, 1 + ((r0 - lay.n_prompt) % lay.dec_seq) // TM), 0)

    w_arr, w_spec, w_scr = _weight_in(w_qkv)
    return _run(
        functools.partial(_qkv_kernel, lay, n_heads),
        name=f"attn_qkv_{layer}", grid=(lay.n_steps,),
        args=[x, mod, w_arr, q_gain, k_gain, cos_t, sin_t],
        in_specs=[
            pl.BlockSpec((TM, d), row),
            lay.mod_spec(lambda i: (i, TM)),
            w_spec,
            _resident((None, 1, HEAD_DIM), lambda i: (j, 0, 0)),
            _resident((None, 1, HEAD_DIM), lambda i: (j, 0, 0)),
            pl.BlockSpec((TM, HEAD_DIM), rope_row),
            pl.BlockSpec((TM, HEAD_DIM), rope_row),
        ],
        out_shape=[jax.ShapeDtypeStruct((lay.n_tok, n_heads * HEAD_DIM), BF16),
                   jax.ShapeDtypeStruct((lay.n_tok, kv), BF16),
                   jax.ShapeDtypeStruct((lay.n_tok, kv), BF16),
                   jax.ShapeDtypeStruct((lay.n_tok, kv), F32),
                   jax.ShapeDtypeStruct((lay.n_tok, kv), F32)],
        out_specs=[pl.BlockSpec((TM, n_heads * HEAD_DIM), row), pl.BlockSpec((TM, kv), row),
                   pl.BlockSpec((TM, kv), row), pl.BlockSpec((TM, kv), row), pl.BlockSpec((TM, kv), row)],
        scratch_shapes=[s for s in (w_scr,) if s is not None],
        jobs=jobs)


def _attn_kernel(lay, layer, n_heads, has_cache, *refs):
    if has_cache:
        (q_ref, k_ref, v_ref, kc_ref, vc_ref, x_ref, mod_ref, lng_ref, lnb_ref, wo_ref, o_ref, att_scr,
         *scratch) = refs
    else:
        (q_ref, k_ref, v_ref, x_ref, mod_ref, lng_ref, lnb_ref, wo_ref, o_ref, att_scr, *scratch) = refs
    (wo,), _ = _bf16_weights((pl.program_id(0) == 0) & (pl.program_id(1) == 0), (wo_ref,), scratch)
    group = n_heads // N_KV_HEADS
    for hk in range(N_KV_HEADS):
        ksl = slice(hk * HEAD_DIM, (hk + 1) * HEAD_DIM)
        for g in range(group):
            hq = hk * group + g
            qsl = slice(hq * HEAD_DIM, (hq + 1) * HEAD_DIM)
            qh = q_ref[:, qsl]
            s = _dot_nt(qh, k_ref[:, ksl])
            m = jnp.max(s, axis=-1, keepdims=True)
            if has_cache:
                s_c = _dot_nt(qh, kc_ref[:, ksl])
                m = jnp.maximum(m, jnp.max(s_c, axis=-1, keepdims=True))
            p = jnp.exp(s - m)
            l = jnp.sum(p, axis=-1, keepdims=True)
            o = _dot(p.astype(BF16), v_ref[:, ksl])
            if has_cache:
                p_c = jnp.exp(s_c - m)
                l = l + jnp.sum(p_c, axis=-1, keepdims=True)
                o = o + _dot(p_c.astype(BF16), vc_ref[:, ksl])
            att_scr[:, qsl] = (o / l).astype(BF16)
    ga = mod_ref[2:3, :]
    mix = _dot(att_scr[...], wo[...])
    o_ref[...] = _layer_norm(lay.alpha * x_ref[...] + ga * mix,
                             lng_ref[2 * layer:2 * layer + 1, :], lnb_ref[2 * layer:2 * layer + 1, :])


def _attention(lay, layer, q, k, v, cache, x, mod, lng, lnb, w_o, *, row0, n_seq, seq, tq):
    d = lay.d
    n_heads = q.shape[1] // HEAD_DIM
    kv = N_KV_HEADS * HEAD_DIM
    has_cache = cache is not None
    q_blocks = seq // tq
    tile = lambda b, i: (row0 // tq + b * q_blocks + i, 0)
    seq_blk = lambda b, i: (row0 // seq + b, 0)
    in_specs = [pl.BlockSpec((tq, n_heads * HEAD_DIM), tile),
                pl.BlockSpec((seq, kv), seq_blk), pl.BlockSpec((seq, kv), seq_blk)]
    args = [q, k, v]
    if has_cache:
        kc, vc = cache
        past = kc.shape[0] // n_seq
        in_specs += [pl.BlockSpec((past, kv), lambda b, i: (b, 0))] * 2
        args += [kc, vc]
    x_index = len(args)
    wo_arr, wo_spec, wo_scr = _weight_in(w_o)
    in_specs += [
        pl.BlockSpec((tq, d), tile),
        lay.mod_spec(lambda b, i: (row0 // tq + b * q_blocks + i, tq)),
        _resident(lng.shape, lambda b, i: (0, 0)),
        _resident(lnb.shape, lambda b, i: (0, 0)),
        wo_spec,
    ]
    args += [x, mod, lng, lnb, wo_arr]
    (x,), _ = _run(
        functools.partial(_attn_kernel, lay, layer, n_heads, has_cache),
        name=f"attention_{layer}_{'latent' if has_cache else 'context'}", grid=(n_seq, q_blocks),
        args=args, in_specs=in_specs,
        out_shape=[jax.ShapeDtypeStruct((lay.n_tok, d), F32)], out_specs=[pl.BlockSpec((tq, d), tile)],
        scratch_shapes=[pltpu.VMEM((tq, n_heads * HEAD_DIM), BF16)] + [s for s in (wo_scr,) if s is not None],
        aliases={x_index: 0})
    return x


def _split3(x):
    hi = x.astype(BF16)
    r = x - hi.astype(F32)
    mid = r.astype(BF16)
    lo = (r - mid.astype(F32)).astype(BF16)
    return hi, mid, lo


def _gla_proj_kernel(lay, hk, hv, x_ref, mod_ref, win_ref, w1_ref, w2_ref, bg_ref,
                     q_ref, k_ref, v_ref, og_ref, bf_ref, bb_ref, *scratch):
    (win,), _ = _bf16_weights(pl.program_id(0) == 0, (win_ref,), scratch)
    sh, sc = mod_ref[0:1, :], mod_ref[1:2, :]
    dk = hk // GLA_HEADS
    r = lax.broadcasted_iota(jnp.int32, (GLA_CHUNK, GLA_CHUNK), 0)
    c = lax.broadcasted_iota(jnp.int32, (GLA_CHUNK, GLA_CHUNK), 1)
    lower = jnp.where(c <= r, 1.0, 0.0).astype(BF16)
    upper = jnp.where(c >= r, 1.0, 0.0).astype(BF16)
    h = (x_ref[...] * (1.0 + sc) + sh).astype(BF16)
    proj = _dot(h, win[...])
    q_ref[...] = proj[:, 0:hk] * (dk ** -0.5)
    k_ref[...] = proj[:, hk:2 * hk]
    v_ref[...] = proj[:, 2 * hk:2 * hk + hv].astype(BF16)
    og_ref[...] = proj[:, 2 * hk + hv:2 * hk + 2 * hv]
    z = _dot(_dot(h, w1_ref[...]).astype(BF16), w2_ref[...]) + bg_ref[...]
    log_gate = (jnp.minimum(z, 0.0) - jnp.log1p(jnp.exp(-jnp.abs(z)))) * (1.0 / GLA_TAU)
    for ch in range(TM // GLA_CHUNK):
        rows = slice(ch * GLA_CHUNK, (ch + 1) * GLA_CHUNK)
        f_hi, f_mid, f_lo = _split3(log_gate[rows, 0:hk])
        bf_ref[rows, :] = _dot(lower, f_hi) + _dot(lower, f_mid) + _dot(lower, f_lo)
        b_hi, b_mid, b_lo = _split3(log_gate[rows, hk:2 * hk])
        bb_ref[rows, :] = _dot(upper, b_hi) + _dot(upper, b_mid) + _dot(upper, b_lo)


def _gla_proj(lay, layer, j, x, mod, w_in, w1, w2, b_gate, jobs):
    d = lay.d
    n_in = _weight_shape(w_in)[1]
    hk = w2.shape[2] // 2
    hv = (n_in - 2 * hk) // 2
    rank2 = w1.shape[2]
    row = lambda i: (i, 0)
    win_arr, win_spec, win_scr = _weight_in(w_in)
    return _run(
        functools.partial(_gla_proj_kernel, lay, hk, hv),
        name=f"gla_proj_{layer}", grid=(lay.n_steps,),
        args=[x, mod, win_arr, w1, w2, b_gate],
        in_specs=[
            pl.BlockSpec((TM, d), row),
            lay.mod_spec(lambda i: (i, TM)),
            win_spec,
            _resident((None, d, rank2), lambda i: (j, 0, 0)),
            _resident((None, rank2, 2 * hk), lambda i: (j, 0, 0)),
            _resident((None, 1, 2 * hk), lambda i: (j, 0, 0)),
        ],
        out_shape=[jax.ShapeDtypeStruct((lay.n_tok, hk), F32),
                   jax.ShapeDtypeStruct((lay.n_tok, hk), F32),
                   jax.ShapeDtypeStruct((lay.n_tok, hv), BF16),
                   jax.ShapeDtypeStruct((lay.n_tok, hv), F32),
                   jax.ShapeDtypeStruct((lay.n_tok, hk), F32),
                   jax.ShapeDtypeStruct((lay.n_tok, hk), F32)],
        out_specs=[pl.BlockSpec((TM, hk), row), pl.BlockSpec((TM, hk), row), pl.BlockSpec((TM, hv), row),
                   pl.BlockSpec((TM, hv), row), pl.BlockSpec((TM, hk), row), pl.BlockSpec((TM, hk), row)],
        scratch_shapes=[s for s in (win_scr,) if s is not None],
        jobs=jobs)


def _block_diag(blocks):
    n = len(blocks)
    zero = jnp.zeros_like(blocks[0])
    return jnp.concatenate(
        [jnp.concatenate([blocks[i] if j == i else zero for j in range(n)], axis=1) for i in range(n)], axis=0)


def _gla_chain_mask(dtype):
    n = GLA_CHAINS * GLA_CHUNK
    r = lax.broadcasted_iota(jnp.int32, (n, n), 0)
    c = lax.broadcasted_iota(jnp.int32, (n, n), 1)
    shift = GLA_CHUNK.bit_length() - 1
    assert 1 << shift == GLA_CHUNK
    r_chain, r_t = jnp.right_shift(r, shift), jnp.bitwise_and(r, GLA_CHUNK - 1)
    c_chain, c_t = jnp.right_shift(c, shift), jnp.bitwise_and(c, GLA_CHUNK - 1)
    direction = 1 - 2 * jnp.bitwise_and(r_chain, 1)
    ordered = jnp.where((c_t - r_t) * direction <= 0, 1.0, 0.0)
    return jnp.where(r_chain == c_chain, ordered, 0.0).astype(dtype)


def _gla_fast_step(refs, st_scr, mask_scr, g, t, accumulate, n_chunks, dk, dv):
    q_ref, k_ref, v_ref, bf_ref, bb_ref, o_ref = refs
    mid = GLA_CHUNK // 2
    slots = []
    for u in range(2):
        qs_l, ks_l, v_l, qi_l, kh_l, dec_l, dst_l, old_l = [], [], [], [], [], [], [], []
        for j in range(2):
            h = 2 * g + j
            kcols = slice(h * dk, (h + 1) * dk)
            vcols = slice(h * dv, (h + 1) * dv)
            for forward, b_ref in ((True, bf_ref), (False, bb_ref)):
                c = (2 * t + u) if forward else (n_chunks - 1 - 2 * t - u)
                rows = pl.ds(pl.multiple_of(c * GLA_CHUNK, GLA_CHUNK), GLA_CHUNK)
                q, k, b, v = q_ref[rows, kcols], k_ref[rows, kcols], b_ref[rows, kcols], v_ref[rows, vcols]
                rho = b[mid:mid + 1]
                q_s = q * jnp.exp(b - rho)
                k_s = k * jnp.exp(rho - b)
                b_exit = b[GLA_CHUNK - 1:GLA_CHUNK] if forward else b[0:1]
                qs_l.append(q_s.astype(BF16))
                ks_l.append(k_s.astype(BF16))
                v_l.append(v)
                qi_l.append((q_s * jnp.exp(rho)).astype(BF16))
                kh_l.append((k_s * jnp.exp(b_exit - rho)).astype(BF16))
                dec_l.append(jnp.exp(b_exit))
                dst_l.append((rows, vcols))
                old_l.append(o_ref[rows, vcols] if accumulate else None)
        v_all = jnp.concatenate(v_l, axis=0)
        a = _dot_nt(jnp.concatenate(qs_l, axis=0), jnp.concatenate(ks_l, axis=0)).astype(BF16) * mask_scr[...]
        slots.append((_dot(a, v_all), _block_diag(qi_l), _block_diag(kh_l), v_all,
                      jnp.concatenate(dec_l, axis=1), dst_l, old_l))
    st = st_scr[g]
    stores = []
    for o_intra, q_bd, k_bd, v_all, decay, dst_l, old_l in slots:
        o_all = o_intra + _dot_nt(q_bd, st.astype(BF16))
        st = st * decay + _dot_tn(v_all, k_bd)
        for ci, ((rows, vcols), old) in enumerate(zip(dst_l, old_l)):
            o = o_all[ci * GLA_CHUNK:(ci + 1) * GLA_CHUNK]
            stores.append((rows, vcols, o if old is None else o + old))
    st_scr[g] = st
    for rows, vcols, o in stores:
        o_ref[rows, vcols] = o


def _gla_chunk_safe(q, k, v, b, st, forward):
    n_sub = GLA_CHUNK // GLA_SUB
    o_rows = []
    qi = (q * jnp.exp(b)).astype(BF16)
    o_inter = _dot_nt(qi, st.astype(BF16))
    t_idx = lax.broadcasted_iota(jnp.int32, (GLA_SUB, 1), 0)
    for blk in range(n_sub):
        rows = slice(blk * GLA_SUB, (blk + 1) * GLA_SUB)
        qb, kb, bb_, vb = q[rows], k[rows], b[rows], v[rows].astype(F32)
        o_blk = o_inter[rows]
        if forward and blk > 0:
            others = slice(0, blk * GLA_SUB)
            rho = b[blk * GLA_SUB - 1:blk * GLA_SUB]
        elif (not forward) and blk < n_sub - 1:
            others = slice((blk + 1) * GLA_SUB, GLA_CHUNK)
            rho = b[(blk + 1) * GLA_SUB:(blk + 1) * GLA_SUB + 1]
        else:
            others = None
        if others is not None:
            q_s = (qb * jnp.exp(bb_ - rho)).astype(BF16)
            k_s = (k[others] * jnp.exp(rho - b[others])).astype(BF16)
            a = _dot_nt(q_s, k_s)
            o_blk = o_blk + _dot(a.astype(BF16), v[others])
        for s in range(GLA_SUB):
            keep = (t_idx >= s) if forward else (t_idx <= s)
            decay = jnp.exp(jnp.where(keep, bb_ - bb_[s:s + 1], NEG_BIG))
            a_col = jnp.sum(qb * kb[s:s + 1] * decay, axis=-1, keepdims=True)
            o_blk = o_blk + a_col * vb[s:s + 1]
        o_rows.append(o_blk)
    b_exit = b[GLA_CHUNK - 1:GLA_CHUNK] if forward else b[0:1]
    k_hat = (k * jnp.exp(b_exit - b)).astype(BF16)
    st_new = st * jnp.exp(b_exit) + _dot_tn(v, k_hat)
    return jnp.concatenate(o_rows, axis=0), st_new


def _gla_scan_kernel(n_chunks, hps, dk, dv, has_init, *refs):
    n_groups = hps // 2
    if has_init:
        (q_ref, k_ref, v_ref, bf_ref, bb_ref, s0f_ref, s0b_ref, o_ref, st_scr, mask_scr) = refs
    else:
        (q_ref, k_ref, v_ref, bf_ref, bb_ref, o_ref, sf_ref, sb_ref, st_scr, mask_scr) = refs
    data_refs = (q_ref, k_ref, v_ref, bf_ref, bb_ref, o_ref)

    def chain_cols(h, forward):
        ci = 2 * (h % 2) + (0 if forward else 1)
        return h // 2, slice(ci * dk, (ci + 1) * dk)

    for h in range(hps):
        for forward in (True, False):
            g, cols = chain_cols(h, forward)
            if has_init:
                st_scr[g, :, cols] = (s0f_ref if forward else s0b_ref)[h].T
            else:
                st_scr[g, :, cols] = jnp.zeros((dv, dk), F32)
    mask_scr[...] = _gla_chain_mask(BF16)

    def safe_step(i, accumulate):
        pending = []
        for h in range(hps):
            kcols = slice(h * dk, (h + 1) * dk)
            vcols = slice(h * dv, (h + 1) * dv)
            for forward, b_ref in ((True, bf_ref), (False, bb_ref)):
                c = i if forward else n_chunks - 1 - i
                rows = pl.ds(pl.multiple_of(c * GLA_CHUNK, GLA_CHUNK), GLA_CHUNK)
                g, cols = chain_cols(h, forward)
                o, st_new = _gla_chunk_safe(q_ref[rows, kcols], k_ref[rows, kcols], v_ref[rows, vcols],
                                            b_ref[rows, kcols], st_scr[g, :, cols], forward)
                if accumulate:
                    o = o + o_ref[rows, vcols]
                pending.append((rows, vcols, g, cols, o, st_new))
        for rows, vcols, g, cols, o, st_new in pending:
            st_scr[g, :, cols] = st_new
            o_ref[rows, vcols] = o

    def fast_step(t, accumulate):
        for g in range(n_groups):
            _gla_fast_step(data_refs, st_scr, mask_scr, g, t, accumulate, n_chunks, dk, dv)

    def run(step, n_steps):
        def first(i, carry):
            step(i, False)
            return carry

        def second(i, carry):
            step(i, True)
            return carry

        lax.fori_loop(0, n_steps // 2, first, 0)
        lax.fori_loop(n_steps // 2, n_steps, second, 0)

    span = jnp.zeros((1, hps * dk), F32)
    for c in range(n_chunks):
        top = slice(c * GLA_CHUNK, c * GLA_CHUNK + 1)
        bottom = slice((c + 1) * GLA_CHUNK - 1, (c + 1) * GLA_CHUNK)
        span = jnp.maximum(span, jnp.maximum(bf_ref[top, :] - bf_ref[bottom, :],
                                             bb_ref[bottom, :] - bb_ref[top, :]))
    bounded = jnp.max(span) < GLA_SAFE_RANGE

    @pl.when(bounded)
    def _():
        run(fast_step, n_chunks // 2)

    @pl.when(jnp.logical_not(bounded))
    def _():
        run(safe_step, n_chunks)

    if not has_init:
        for h in range(hps):
            for forward in (True, False):
                g, cols = chain_cols(h, forward)
                (sf_ref if forward else sb_ref)[h] = st_scr[g, :, cols].T


def _gla_scan(lay, j, q, k, v, bf, bb, init, *, row0, n_seq, seq, hps):
    hk, hv = q.shape[1], v.shape[1]
    dk, dv = hk // GLA_HEADS, hv // GLA_HEADS
    n_chunks = seq // GLA_CHUNK
    groups = GLA_HEADS // hps
    assert n_chunks % 4 == 0 and GLA_HEADS % hps == 0 and hps % 2 == 0
    has_init = init is not None
    blk = lambda b, g: (row0 // seq + b, g)
    kspec = pl.BlockSpec((seq, hps * dk), blk)
    vspec = pl.BlockSpec((seq, hps * dv), blk)
    in_specs = [kspec, kspec, vspec, kspec, kspec]
    args = [q, k, v, bf, bb]
    out_shape = [jax.ShapeDtypeStruct((n_seq * seq, hv), F32)]
    out_specs = [pl.BlockSpec((seq, hps * dv), lambda b, g: (b, g))]
    if has_init:
        s0f, s0b = init
        st_spec = pl.BlockSpec((None, hps, dk, dv), lambda b, g: (b, j * groups + g, 0, 0))
        in_specs += [st_spec, st_spec]
        args += [s0f, s0b]
    else:
        st_shape = jax.ShapeDtypeStruct((n_seq, GLA_HEADS, dk, dv), F32)
        st_spec = pl.BlockSpec((None, hps, dk, dv), lambda b, g: (b, g, 0, 0))
        out_shape += [st_shape, st_shape]
        out_specs += [st_spec, st_spec]
    outs, _ = _run(
        functools.partial(_gla_scan_kernel, n_chunks, hps, dk, dv, has_init),
        name=f"gla_scan_{'latent' if has_init else 'context'}", grid=(n_seq, groups),
        args=args, in_specs=in_specs, out_shape=out_shape, out_specs=out_specs,
        scratch_shapes=[pltpu.VMEM((hps // 2, dv, GLA_CHAINS * dk), F32),
                        pltpu.VMEM((GLA_CHAINS * GLA_CHUNK, GLA_CHAINS * GLA_CHUNK), BF16)])
    return outs


def _rope_tables(lay):
    n_freq = HEAD_DIM // 4
    pos = np.arange(lay.dec_seq)
    freqs = (np.float32(ROPE_THETA) ** (-np.arange(n_freq, dtype=np.float32) / np.float32(n_freq))).astype(np.float32)
    ang_r = (pos // GRID_W).astype(np.float32)[:, None] * freqs
    ang_c = (pos % GRID_W).astype(np.float32)[:, None] * freqs
    cos = np.concatenate([np.cos(ang_r)] * 2 + [np.cos(ang_c)] * 2, axis=-1)
    sin = np.concatenate([-np.sin(ang_r), np.sin(ang_r), -np.sin(ang_c), np.sin(ang_c)], axis=-1)
    cos = np.concatenate([np.ones((TM, HEAD_DIM), np.float32), cos], axis=0).astype(np.float32)
    sin = np.concatenate([np.zeros((TM, HEAD_DIM), np.float32), sin], axis=0).astype(np.float32)
    return jnp.asarray(cos), jnp.asarray(sin)


def kernel(x_prompt, x_sample, c, cache_k, cache_v, state_gla_fwd, state_gla_bwd, c_ctx, w_ada, b_ada, ln_g, ln_b, conv_w_in, conv_w, conv_w_out, attn_w_qkv, attn_q_norm, attn_k_norm, attn_w_o, gla_w_in, gla_w_gate1, gla_w_gate2, gla_b_gate, gla_norm, gla_w_o, ffn_w_in, ffn_w_out):
    batch, seq, d = x_prompt.shape
    dec_batch, dec_seq, _ = x_sample.shape
    depth = w_ada.shape[0]
    lay = _Layout(batch, seq, dec_batch, dec_seq, d, depth)
    n_steps = lay.n_steps
    kv = N_KV_HEADS * HEAD_DIM
    past = cache_k.shape[2]

    cond = jnp.concatenate([c_ctx[None, :], c, jnp.zeros((MOD_ROWS - 1 - dec_batch, d), F32)], axis=0)
    b_ada3 = b_ada.reshape(depth, 1, -1)
    lng = ln_g.reshape(depth * 2, d)
    lnb = ln_b.reshape(depth * 2, d)
    g1 = jnp.concatenate([gla_w_gate1[:, 0], gla_w_gate1[:, 1]], axis=-1).astype(BF16)
    zeros = jnp.zeros_like(gla_w_gate2[:, 0])
    g2 = jnp.concatenate([jnp.concatenate([gla_w_gate2[:, 0], zeros], axis=-1),
                          jnp.concatenate([zeros, gla_w_gate2[:, 1]], axis=-1)], axis=1).astype(BF16)
    gb = jnp.concatenate([gla_b_gate[:, 0], gla_b_gate[:, 1]], axis=-1)[:, None, :]
    cos_t, sin_t = _rope_tables(lay)

    def mixer_weights(i):
        kind, j = i % N_MIXERS, i // N_MIXERS
        return [[(conv_w_in, j), (conv_w_out, j)], [(attn_w_qkv, j), (attn_w_o, j)], [(gla_w_in, j), (gla_w_o, j)]][kind]

    x = (x_prompt.reshape(lay.n_prompt, d), x_sample.reshape(dec_batch * dec_seq, d))
    mod = _modulation(cond, w_ada, b_ada3, 0).reshape(MOD_ROWS, 6, d)
    mix_w = mixer_weights(0)
    new_k, new_v, new_sf, new_sb = [], [], [], []
    y_prompt = y_sample = None
    for i in range(depth):
        kind, j = i % N_MIXERS, i // N_MIXERS
        last = i == depth - 1
        ffn_jobs = [_cast_job(ffn_w_in, i, n_steps), _cast_job(ffn_w_out, i, n_steps)]
        gla_pre = None
        if kind == 0:
            x, jr = _conv_mixer(lay, i, j, x if i == 0 else (x,), mod, lng, lnb, mix_w[0], conv_w, mix_w[1], ffn_jobs)
        elif kind == 1:
            (q, k, v, kf, vf), jr = _attn_qkv(lay, i, j, x, mod, mix_w[0], attn_q_norm[:, None, :],
                                              attn_k_norm[:, None, :], cos_t, sin_t, ffn_jobs)
            new_k.append(kf[:lay.n_prompt].reshape(batch, seq, N_KV_HEADS, HEAD_DIM))
            new_v.append(vf[:lay.n_prompt].reshape(batch, seq, N_KV_HEADS, HEAD_DIM))
            kc = cache_k[:, j].reshape(dec_batch * past, kv).astype(BF16)
            vc = cache_v[:, j].reshape(dec_batch * past, kv).astype(BF16)
            x = _attention(lay, i, q, k, v, None, x, mod, lng, lnb, mix_w[1],
                           row0=0, n_seq=batch, seq=seq, tq=seq)
            x = _attention(lay, i, q, k, v, (kc, vc), x, mod, lng, lnb, mix_w[1],
                           row0=lay.n_prompt, n_seq=dec_batch, seq=dec_seq, tq=TM)
        else:
            (q, k, v, og, bf, bb), jr = _gla_proj(lay, i, j, x, mod, mix_w[0], g1, g2, gb, ffn_jobs)
            s0f = state_gla_fwd.reshape(dec_batch, -1, *state_gla_fwd.shape[3:])
            s0b = state_gla_bwd.reshape(dec_batch, -1, *state_gla_bwd.shape[3:])
            o_ctx, sf, sb = _gla_scan(lay, j, q, k, v, bf, bb, None, row0=0, n_seq=batch, seq=seq,
                                      hps=GLA_HEADS_PER_STEP_CONTEXT)
            o_lat, = _gla_scan(lay, j, q, k, v, bf, bb, (s0f, s0b), row0=lay.n_prompt, n_seq=dec_batch,
                               seq=dec_seq, hps=GLA_HEADS_PER_STEP_LATENT)
            new_sf.append(sf)
            new_sb.append(sb)
            w_o = mix_w[1]
            if isinstance(w_o, tuple):
                w_o = w_o[0][w_o[1]].astype(BF16)
            gla_pre = (o_ctx, o_lat, og, gla_norm[j][None, :], w_o)
        ffn_w = (jr[0][0], jr[1][0])
        next_jobs = []
        if not last:
            next_jobs = [_mod_job(cond, w_ada, b_ada3, i + 1, n_steps)]
            next_jobs += [_cast_job(arr, jj, n_steps) for arr, jj in mixer_weights(i + 1)]
        outs, jr = _ffn(lay, i, x, mod, lng, lnb, ffn_w[0], ffn_w[1], split_out=last, gla_pre=gla_pre,
                        jobs=next_jobs)
        if last:
            y_prompt, y_sample = outs
        else:
            x, = outs
            mod = jr[0][0].reshape(MOD_ROWS, 6, d)
            mix_w = [jr[1][0], jr[2][0]]

    y_prompt = y_prompt.reshape(batch, seq, d)
    y_sample = y_sample.reshape(dec_batch, dec_seq, d)
    return (y_prompt, y_sample, jnp.stack(new_k, axis=1), jnp.stack(new_v, axis=1),
            jnp.stack(new_sf, axis=1), jnp.stack(new_sb, axis=1))
```

```python
import functools

import jax
import jax.numpy as jnp
import numpy as np
from jax import lax
from jax.experimental import pallas as pl
from jax.experimental.pallas import tpu as pltpu

F32 = jnp.float32
BF16 = jnp.bfloat16

N_MIXERS = 3
CONV_WIDTH = 3
HEAD_DIM = 128
N_KV_HEADS = 2
GRID_W = 64
ROPE_THETA = 10000.0
GLA_HEADS = 4
GLA_TAU = 16.0
GLA_CHUNK = 64
LN_EPS = 1e-5
RMS_EPS = 1e-6

LANES = 128
BF16_SUBLANES = 16
VMEM_LIMIT = 58 * 1024 * 1024

MOD_ROWS = 16
MOD_NT = 1536
TM = 512
SUB_TILES = 2
FFN_SUB_ROWS = (256, 256)
HALO = BF16_SUBLANES
WEIGHT_CAST_ROWS = 128
GLA_SUB = 16
GLA_SAFE_RANGE = 80.0
GLA_CHAINS = 4
GLA_HEADS_PER_STEP_CONTEXT = 4
GLA_HEADS_PER_STEP_LATENT = 2
NEG_BIG = -1e30


def _dot(a, b):
    return jnp.dot(a, b, preferred_element_type=F32)


def _dot_nt(a, b):
    return lax.dot_general(a, b, (((1,), (1,)), ((), ())), preferred_element_type=F32)


def _dot_tn(a, b):
    return lax.dot_general(a, b, (((0,), (0,)), ((), ())), preferred_element_type=F32)


def _layer_norm(y, g, b):
    mu = jnp.mean(y, axis=-1, keepdims=True)
    yc = y - mu
    var = jnp.mean(yc * yc, axis=-1, keepdims=True)
    return yc * lax.rsqrt(var + LN_EPS) * g + b


def _silu(x):
    return x * jax.nn.sigmoid(x)


def _resident(block_shape, index_map):
    return pl.BlockSpec(block_shape, index_map, pipeline_mode=pl.Buffered(1))


class _Layout:
    def __init__(self, batch, seq, dec_batch, dec_seq, d_model, depth):
        self.batch, self.seq, self.dec_batch, self.dec_seq = batch, seq, dec_batch, dec_seq
        self.d, self.depth = d_model, depth
        self.n_prompt = batch * seq
        self.n_tok = self.n_prompt + dec_batch * dec_seq
        self.n_steps = self.n_tok // TM
        assert self.n_prompt % TM == 0 and dec_seq % TM == 0 and TM % seq == 0
        assert seq & (seq - 1) == 0 and dec_seq & (dec_seq - 1) == 0
        self.alpha = (2.0 * depth) ** 0.25

    def mod_index(self, i, rows):
        r0 = i * rows
        return jnp.where(r0 < self.n_prompt, 0, 1 + (r0 - self.n_prompt) // self.dec_seq)

    def mod_spec(self, tile_of):
        def index(*ids):
            tile, rows = tile_of(*ids)
            return (self.mod_index(tile, rows), 0, 0)

        return pl.BlockSpec((None, 6, self.d), index)


class _Job:
    def __init__(self, args, in_specs, out_shapes, out_specs, fn):
        self.args, self.in_specs, self.out_shapes, self.out_specs, self.fn = args, in_specs, out_shapes, out_specs, fn


def _cast_job(w, j, n_steps):
    _, n_rows, n_cols = w.shape
    rows = n_rows // n_steps
    assert rows * n_steps == n_rows and rows % BF16_SUBLANES == 0

    def fn(in_refs, out_refs):
        out_refs[0][...] = in_refs[0][...].astype(BF16)

    return _Job([w], [pl.BlockSpec((None, rows, n_cols), lambda i: (j, i, 0))],
                [jax.ShapeDtypeStruct((n_rows, n_cols), BF16)], [pl.BlockSpec((rows, n_cols), lambda i: (i, 0))], fn)


def _mod_tile(in_refs, out_refs):
    cond_ref, w_ref, b_ref = in_refs
    out_refs[0][...] = _dot(_silu(cond_ref[...]).astype(BF16), w_ref[...].astype(BF16)) + b_ref[...]


def _mod_job(cond, w_ada, b_ada3, layer, n_steps):
    _, d, n_out = w_ada.shape
    cols = n_out // n_steps
    assert cols * n_steps == n_out and cols % LANES == 0
    return _Job([cond, w_ada, b_ada3],
                [_resident((MOD_ROWS, d), lambda i: (0, 0)),
                 pl.BlockSpec((None, d, cols), lambda i: (layer, 0, i)),
                 pl.BlockSpec((None, 1, cols), lambda i: (layer, 0, i))],
                [jax.ShapeDtypeStruct((MOD_ROWS, n_out), F32)], [pl.BlockSpec((MOD_ROWS, cols), lambda i: (0, i))],
                _mod_tile)


def _run(kernel_fn, *, name, grid, args, in_specs, out_shape, out_specs, scratch_shapes=(), jobs=(), aliases=None):
    n_in, n_out = len(args), len(out_shape)

    def body(*refs):
        pos = n_in
        job_in = []
        for jb in jobs:
            job_in.append(refs[pos:pos + len(jb.args)])
            pos += len(jb.args)
        main_out = refs[pos:pos + n_out]
        pos += n_out
        job_out = []
        for jb in jobs:
            job_out.append(refs[pos:pos + len(jb.out_shapes)])
            pos += len(jb.out_shapes)
        for jb, ji, jo in zip(jobs, job_in, job_out):
            jb.fn(ji, jo)
        kernel_fn(*refs[:n_in], *main_out, *refs[pos:])

    outs = pl.pallas_call(
        body,
        out_shape=tuple(out_shape) + tuple(s for jb in jobs for s in jb.out_shapes),
        grid=grid,
        in_specs=list(in_specs) + [s for jb in jobs for s in jb.in_specs],
        out_specs=tuple(out_specs) + tuple(s for jb in jobs for s in jb.out_specs),
        scratch_shapes=list(scratch_shapes),
        input_output_aliases=aliases or {},
        compiler_params=pltpu.CompilerParams(dimension_semantics=("arbitrary",) * len(grid),
                                             vmem_limit_bytes=VMEM_LIMIT),
        name=name,
    )(*args, *[a for jb in jobs for a in jb.args])
    main, rest = outs[:n_out], list(outs[n_out:])
    job_results = []
    for jb in jobs:
        job_results.append(tuple(rest[:len(jb.out_shapes)]))
        rest = rest[len(jb.out_shapes):]
    return tuple(main), job_results


def _weight_in(w):
    if isinstance(w, tuple):
        arr, j = w
        _, r, c = arr.shape
        return arr, _resident((None, r, c), lambda *_: (j, 0, 0)), pltpu.VMEM((r, c), BF16)
    r, c = w.shape
    return w, _resident((r, c), lambda *_: (0, 0)), None


def _weight_shape(w):
    return w[0].shape[1:] if isinstance(w, tuple) else w.shape


def _round_weight_once(first_step, w_ref, w_scr):
    n_rows = w_ref.shape[0]
    rows = min(n_rows, WEIGHT_CAST_ROWS)
    assert n_rows % rows == 0

    @pl.when(first_step)
    def _():
        def body(c, carry):
            sl = pl.ds(pl.multiple_of(c * rows, rows), rows)
            w_scr[sl, :] = w_ref[sl, :].astype(BF16)
            return carry

        lax.fori_loop(0, n_rows // rows, body, 0)


def _bf16_weights(first_step, w_refs, scratch):
    scratch = list(scratch)
    out = []
    for w in w_refs:
        if w.dtype == BF16:
            out.append(w)
        else:
            scr = scratch.pop(0)
            _round_weight_once(first_step, w, scr)
            out.append(scr)
    return out, scratch


def _modulation(cond, w_ada, b_ada3, layer):
    _, d, n_out = w_ada.shape
    job = _mod_job(cond, w_ada, b_ada3, layer, n_out // MOD_NT)
    _, (result,) = _run(lambda: None, name=f"modulation_{layer}", grid=(n_out // MOD_NT,), args=[], in_specs=[],
                        out_shape=[], out_specs=[], jobs=[job])
    return result[0]


def _conv_kernel(lay, layer, split, *refs):
    d = lay.d
    i = pl.program_id(0)
    if split:
        (xca, xpa, xna, xcb, xpb, xnb, mod_ref, lng_ref, lnb_ref, win_ref, cw_ref, wout_ref,
         o_ref, h_scr, uu_scr, *scratch) = refs
        is_context = i * TM < lay.n_prompt
        x = jnp.where(is_context, xca[...], xcb[...])
        x_prev = jnp.where(is_context, xpa[...], xpb[...])
        x_next = jnp.where(is_context, xna[...], xnb[...])
    else:
        (xc_ref, xp_ref, xn_ref, mod_ref, lng_ref, lnb_ref, win_ref, cw_ref, wout_ref,
         o_ref, h_scr, uu_scr, *scratch) = refs
        x, x_prev, x_next = xc_ref[...], xp_ref[...], xn_ref[...]
    (win, wout), _ = _bf16_weights(i == 0, (win_ref, wout_ref), scratch)
    sh, sc, ga = mod_ref[0:1, :], mod_ref[1:2, :], mod_ref[2:3, :]
    one_sc = 1.0 + sc
    h_scr[0:HALO, :] = (x_prev * one_sc + sh).astype(BF16)
    h_scr[HALO:HALO + TM, :] = (x * one_sc + sh).astype(BF16)
    h_scr[HALO + TM:HALO + TM + HALO, :] = (x_next * one_sc + sh).astype(BF16)
    cgu = _dot(h_scr[...], win[:, d:3 * d])
    uu_scr[...] = cgu[:, 0:d] * cgu[:, d:2 * d]
    sub = TM // SUB_TILES
    for s in range(SUB_TILES):
        r0 = s * sub
        bg = _dot(h_scr[HALO + r0:HALO + r0 + sub, :], win[:, 0:d])
        row = i * TM + r0 + lax.broadcasted_iota(jnp.int32, (sub, 1), 0)
        seq_len = jnp.where(row < lay.n_prompt, lay.seq, lay.dec_seq)
        pos = jnp.bitwise_and(row, seq_len - 1)
        u_prev = jnp.where(pos != 0, uu_scr[pl.ds(HALO + r0 - 1, sub), :], 0.0)
        u_next = jnp.where(pos != seq_len - 1, uu_scr[pl.ds(HALO + r0 + 1, sub), :], 0.0)
        y = (u_prev * cw_ref[0:1, :] + uu_scr[pl.ds(HALO + r0, sub), :] * cw_ref[1:2, :]
             + u_next * cw_ref[2:3, :])
        mix = _dot((bg * y).astype(BF16), wout[...])
        o_ref[r0:r0 + sub, :] = _layer_norm(lay.alpha * x[r0:r0 + sub] + ga * mix,
                                            lng_ref[2 * layer:2 * layer + 1, :],
                                            lnb_ref[2 * layer:2 * layer + 1, :])


def _conv_mixer(lay, layer, j, x_parts, mod, lng, lnb, w_in, cw, w_out, jobs):
    d = lay.d
    per = TM // HALO
    x_specs, x_args = [], []
    tile0 = 0
    for part in x_parts:
        n_tiles = part.shape[0] // TM

        def center(i, t0=tile0, n=n_tiles):
            return (jnp.clip(i - t0, 0, n - 1), 0)

        def prev_halo(i, t0=tile0, n=n_tiles):
            return (jnp.clip((i - t0) * per - 1, 0, n * per - 1), 0)

        def next_halo(i, t0=tile0, n=n_tiles):
            return (jnp.clip((i - t0 + 1) * per, 0, n * per - 1), 0)

        x_specs += [pl.BlockSpec((TM, d), center), pl.BlockSpec((HALO, d), prev_halo),
                    pl.BlockSpec((HALO, d), next_halo)]
        x_args += [part, part, part]
        tile0 += n_tiles
    win_arr, win_spec, win_scr = _weight_in(w_in)
    wout_arr, wout_spec, wout_scr = _weight_in(w_out)
    (x,), job_results = _run(
        functools.partial(_conv_kernel, lay, layer, len(x_parts) == 2),
        name=f"conv_mixer_{layer}", grid=(lay.n_steps,),
        args=x_args + [mod, lng, lnb, win_arr, cw, wout_arr],
        in_specs=x_specs + [
            lay.mod_spec(lambda i: (i, TM)),
            _resident(lng.shape, lambda i: (0, 0)),
            _resident(lnb.shape, lambda i: (0, 0)),
            win_spec,
            _resident((None, CONV_WIDTH, d), lambda i: (j, 0, 0)),
            wout_spec,
        ],
        out_shape=[jax.ShapeDtypeStruct((lay.n_tok, d), F32)],
        out_specs=[pl.BlockSpec((TM, d), lambda i: (i, 0))],
        scratch_shapes=[pltpu.VMEM((TM + 2 * HALO, d), BF16), pltpu.VMEM((TM + 2 * HALO, d), F32)]
        + [s for s in (win_scr, wout_scr) if s is not None],
        jobs=jobs)
    return x, job_results


def _ffn_kernel(lay, layer, split_out, gla_pre, *refs):
    x_ref, mod_ref, lng_ref, lnb_ref, win_ref, wout_ref = refs[:6]
    pos = 6
    if gla_pre:
        oc_ref, ol_ref, og_ref, ng_ref, wo_ref = refs[pos:pos + 5]
        pos += 5
    n_out = 2 if split_out else 1
    out_refs = refs[pos:pos + n_out]
    scratch = refs[pos + n_out:]
    d_ff = wout_ref.shape[0]
    is_context = pl.program_id(0) * TM < lay.n_prompt
    sh, sc, ga = mod_ref[3:4, :], mod_ref[4:5, :], mod_ref[5:6, :]
    ln_g = lng_ref[2 * layer + 1:2 * layer + 2, :]
    ln_b = lnb_ref[2 * layer + 1:2 * layer + 2, :]
    results = []
    row_starts = np.cumsum((0,) + FFN_SUB_ROWS)
    assert row_starts[-1] == TM
    for r0, r1 in zip(row_starts[:-1], row_starts[1:]):
        rows = slice(int(r0), int(r1))
        x = x_ref[rows, :]
        if gla_pre:
            z_scr, = scratch
            dv = ng_ref.shape[1]
            for h in range(GLA_HEADS):
                sl = slice(h * dv, (h + 1) * dv)
                o = jnp.where(is_context, oc_ref[rows, sl], ol_ref[rows, sl])
                o = o * lax.rsqrt(jnp.mean(o * o, axis=-1, keepdims=True) + RMS_EPS) * ng_ref[...]
                z_scr[rows, sl] = (o * _silu(og_ref[rows, sl])).astype(BF16)
            mix = _dot(z_scr[rows, :], wo_ref[...])
            x = _layer_norm(lay.alpha * x + mod_ref[2:3, :] * mix,
                            lng_ref[2 * layer:2 * layer + 1, :], lnb_ref[2 * layer:2 * layer + 1, :])
        h = (x * (1.0 + sc) + sh).astype(BF16)
        g = _dot(h, win_ref[:, 0:d_ff])
        u = _dot(h, win_ref[:, d_ff:2 * d_ff])
        a = (_silu(g) * u).astype(BF16)
        y = _dot(a, wout_ref[...])
        results.append((rows, _layer_norm(lay.alpha * x + ga * y, ln_g, ln_b)))
    if split_out:
        @pl.when(is_context)
        def _():
            for rows, res in results:
                out_refs[0][rows, :] = res

        @pl.when(jnp.logical_not(is_context))
        def _():
            for rows, res in results:
                out_refs[1][rows, :] = res
    else:
        for rows, res in results:
            out_refs[0][rows, :] = res


def _ffn(lay, layer, x, mod, lng, lnb, w_in, w_out, *, split_out, gla_pre, jobs):
    d = lay.d
    d_ff = w_out.shape[0]
    row = lambda i: (i, 0)
    n_ctx = lay.n_prompt // TM
    n_lat = lay.n_steps - n_ctx
    ctx_row = lambda i: (jnp.minimum(i, n_ctx - 1), 0)
    lat_row = lambda i: (jnp.clip(i - n_ctx, 0, n_lat - 1), 0)
    args = [x, mod, lng, lnb, w_in, w_out]
    in_specs = [
        pl.BlockSpec((TM, d), row),
        lay.mod_spec(lambda i: (i, TM)),
        _resident(lng.shape, lambda i: (0, 0)),
        _resident(lnb.shape, lambda i: (0, 0)),
        _resident((d, 2 * d_ff), lambda i: (0, 0)),
        _resident((d_ff, d), lambda i: (0, 0)),
    ]
    scratch = []
    if gla_pre is not None:
        o_context, o_latent, og, norm_g, w_o = gla_pre
        hv = og.shape[1]
        args += [o_context, o_latent, og, norm_g, w_o]
        in_specs += [pl.BlockSpec((TM, hv), ctx_row), pl.BlockSpec((TM, hv), lat_row), pl.BlockSpec((TM, hv), row),
                     _resident(norm_g.shape, lambda i: (0, 0)), _resident((hv, d), lambda i: (0, 0))]
        scratch = [pltpu.VMEM((TM, hv), BF16)]
    if split_out:
        out_shape = [jax.ShapeDtypeStruct((lay.n_prompt, d), F32),
                     jax.ShapeDtypeStruct((lay.n_tok - lay.n_prompt, d), F32)]
        out_specs = [pl.BlockSpec((TM, d), ctx_row), pl.BlockSpec((TM, d), lat_row)]
    else:
        out_shape = [jax.ShapeDtypeStruct((lay.n_tok, d), F32)]
        out_specs = [pl.BlockSpec((TM, d), row)]
    return _run(functools.partial(_ffn_kernel, lay, layer, split_out, gla_pre is not None),
                name=f"ffn_{layer}", grid=(lay.n_steps,), args=args, in_specs=in_specs,
                out_shape=out_shape, out_specs=out_specs, scratch_shapes=scratch, jobs=jobs)


def _qkv_kernel(lay, n_heads, x_ref, mod_ref, w_ref, qg_ref, kg_ref, cos_ref, sin_ref,
                q_ref, k_ref, v_ref, kf_ref, vf_ref, *scratch):
    (w,), _ = _bf16_weights(pl.program_id(0) == 0, (w_ref,), scratch)
    sh, sc = mod_ref[0:1, :], mod_ref[1:2, :]
    lane = lax.broadcasted_iota(jnp.int32, (1, HEAD_DIM), 1)
    first_half = jnp.bitwise_and(lane, HEAD_DIM // 4) == 0
    q_gain = qg_ref[...] * (HEAD_DIM ** -0.5)
    k0 = n_heads * HEAD_DIM
    v0 = k0 + N_KV_HEADS * HEAD_DIM
    for s in range(SUB_TILES):
        rows = slice(s * (TM // SUB_TILES), (s + 1) * (TM // SUB_TILES))
        h = (x_ref[rows, :] * (1.0 + sc) + sh).astype(BF16)
        qkv = _dot(h, w[...])
        cos, sin = cos_ref[rows, :], sin_ref[rows, :]

        def norm_rope(t, g):
            t = t * lax.rsqrt(jnp.mean(t * t, axis=-1, keepdims=True) + RMS_EPS) * g
            partner = jnp.where(first_half,
                                pltpu.roll(t, HEAD_DIM - HEAD_DIM // 4, axis=1),
                                pltpu.roll(t, HEAD_DIM // 4, axis=1))
            return t * cos + partner * sin

        for hq in range(n_heads):
            sl = slice(hq * HEAD_DIM, (hq + 1) * HEAD_DIM)
            q_ref[rows, sl] = norm_rope(qkv[:, sl], q_gain).astype(BF16)
        for hk in range(N_KV_HEADS):
            sl = slice(hk * HEAD_DIM, (hk + 1) * HEAD_DIM)
            kh = norm_rope(qkv[:, k0 + hk * HEAD_DIM:k0 + (hk + 1) * HEAD_DIM], kg_ref[...])
            kf_ref[rows, sl] = kh
            k_ref[rows, sl] = kh.astype(BF16)
        v = qkv[:, v0:v0 + N_KV_HEADS * HEAD_DIM]
        vf_ref[rows, :] = v
        v_ref[rows, :] = v.astype(BF16)


def _attn_qkv(lay, layer, j, x, mod, w_qkv, q_gain, k_gain, cos_t, sin_t, jobs):
    d = lay.d
    n_qkv = _weight_shape(w_qkv)[1]
    kv = N_KV_HEADS * HEAD_DIM
    n_heads = (n_qkv - 2 * kv) // HEAD_DIM
    row = lambda i: (i, 0)

    def rope_row(i):
        r0 = i * TM
        return (jnp.where(r0 < lay.n_prompt, 0, 1 + ((r0 - lay.n_prompt) % lay.dec_seq) // TM), 0)

    w_arr, w_spec, w_scr = _weight_in(w_qkv)
    return _run(
        functools.partial(_qkv_kernel, lay, n_heads),
        name=f"attn_qkv_{layer}", grid=(lay.n_steps,),
        args=[x, mod, w_arr, q_gain, k_gain, cos_t, sin_t],
        in_specs=[
            pl.BlockSpec((TM, d), row),
            lay.mod_spec(lambda i: (i, TM)),
            w_spec,
            _resident((None, 1, HEAD_DIM), lambda i: (j, 0, 0)),
            _resident((None, 1, HEAD_DIM), lambda i: (j, 0, 0)),
            pl.BlockSpec((TM, HEAD_DIM), rope_row),
            pl.BlockSpec((TM, HEAD_DIM), rope_row),
        ],
        out_shape=[jax.ShapeDtypeStruct((lay.n_tok, n_heads * HEAD_DIM), BF16),
                   jax.ShapeDtypeStruct((lay.n_tok, kv), BF16),
                   jax.ShapeDtypeStruct((lay.n_tok, kv), BF16),
                   jax.ShapeDtypeStruct((lay.n_tok, kv), F32),
                   jax.ShapeDtypeStruct((lay.n_tok, kv), F32)],
        out_specs=[pl.BlockSpec((TM, n_heads * HEAD_DIM), row), pl.BlockSpec((TM, kv), row),
                   pl.BlockSpec((TM, kv), row), pl.BlockSpec((TM, kv), row), pl.BlockSpec((TM, kv), row)],
        scratch_shapes=[s for s in (w_scr,) if s is not None],
        jobs=jobs)


def _attn_kernel(lay, layer, n_heads, has_cache, seqs, *refs):
    if has_cache:
        (q_ref, k_ref, v_ref, kc_ref, vc_ref, x_ref, mod_ref, lng_ref, lnb_ref, wo_ref, o_ref, att_scr,
         *scratch) = refs
    else:
        (q_ref, k_ref, v_ref, x_ref, mod_ref, lng_ref, lnb_ref, wo_ref, o_ref, att_scr, *scratch) = refs
    (wo,), _ = _bf16_weights((pl.program_id(0) == 0) & (pl.program_id(1) == 0), (wo_ref,), scratch)
    group = n_heads // N_KV_HEADS
    q_rows = q_ref.shape[0] // seqs
    k_rows = k_ref.shape[0] // seqs
    for sq in range(seqs):
        rq = slice(sq * q_rows, (sq + 1) * q_rows)
        rk = slice(sq * k_rows, (sq + 1) * k_rows)
        for hk in range(N_KV_HEADS):
            ksl = slice(hk * HEAD_DIM, (hk + 1) * HEAD_DIM)
            for g in range(group):
                hq = hk * group + g
                qsl = slice(hq * HEAD_DIM, (hq + 1) * HEAD_DIM)
                qh = q_ref[rq, qsl]
                s = _dot_nt(qh, k_ref[rk, ksl])
                m = jnp.max(s, axis=-1, keepdims=True)
                if has_cache:
                    s_c = _dot_nt(qh, kc_ref[:, ksl])
                    m = jnp.maximum(m, jnp.max(s_c, axis=-1, keepdims=True))
                p = jnp.exp(s - m)
                l = jnp.sum(p, axis=-1, keepdims=True)
                o = _dot(p.astype(BF16), v_ref[rk, ksl])
                if has_cache:
                    p_c = jnp.exp(s_c - m)
                    l = l + jnp.sum(p_c, axis=-1, keepdims=True)
                    o = o + _dot(p_c.astype(BF16), vc_ref[:, ksl])
                att_scr[rq, qsl] = (o / l).astype(BF16)
    ga = mod_ref[2:3, :]
    mix = _dot(att_scr[...], wo[...])
    o_ref[...] = _layer_norm(lay.alpha * x_ref[...] + ga * mix,
                             lng_ref[2 * layer:2 * layer + 1, :], lnb_ref[2 * layer:2 * layer + 1, :])


def _attention(lay, layer, q, k, v, cache, x, mod, lng, lnb, w_o, *, row0, n_seq, seq, tq):
    d = lay.d
    n_heads = q.shape[1] // HEAD_DIM
    kv = N_KV_HEADS * HEAD_DIM
    has_cache = cache is not None
    seqs = max(tq // seq, 1)
    assert not (has_cache and seqs > 1) and n_seq % seqs == 0
    n_seq, seq = n_seq // seqs, seq * seqs
    q_blocks = seq // tq
    tile = lambda b, i: (row0 // tq + b * q_blocks + i, 0)
    seq_blk = lambda b, i: (row0 // seq + b, 0)
    in_specs = [pl.BlockSpec((tq, n_heads * HEAD_DIM), tile),
                pl.BlockSpec((seq, kv), seq_blk), pl.BlockSpec((seq, kv), seq_blk)]
    args = [q, k, v]
    if has_cache:
        kc, vc = cache
        past = kc.shape[0] // n_seq
        in_specs += [pl.BlockSpec((past, kv), lambda b, i: (b, 0))] * 2
        args += [kc, vc]
    x_index = len(args)
    wo_arr, wo_spec, wo_scr = _weight_in(w_o)
    in_specs += [
        pl.BlockSpec((tq, d), tile),
        lay.mod_spec(lambda b, i: (row0 // tq + b * q_blocks + i, tq)),
        _resident(lng.shape, lambda b, i: (0, 0)),
        _resident(lnb.shape, lambda b, i: (0, 0)),
        wo_spec,
    ]
    args += [x, mod, lng, lnb, wo_arr]
    (x,), _ = _run(
        functools.partial(_attn_kernel, lay, layer, n_heads, has_cache, seqs),
        name=f"attention_{layer}_{'latent' if has_cache else 'context'}", grid=(n_seq, q_blocks),
        args=args, in_specs=in_specs,
        out_shape=[jax.ShapeDtypeStruct((lay.n_tok, d), F32)], out_specs=[pl.BlockSpec((tq, d), tile)],
        scratch_shapes=[pltpu.VMEM((tq, n_heads * HEAD_DIM), BF16)] + [s for s in (wo_scr,) if s is not None],
        aliases={x_index: 0})
    return x


def _split3(x):
    hi = x.astype(BF16)
    r = x - hi.astype(F32)
    mid = r.astype(BF16)
    lo = (r - mid.astype(F32)).astype(BF16)
    return hi, mid, lo


def _gla_proj_kernel(lay, hk, hv, x_ref, mod_ref, win_ref, w1_ref, w2_ref, bg_ref,
                     q_ref, k_ref, v_ref, og_ref, bf_ref, bb_ref, *scratch):
    (win,), _ = _bf16_weights(pl.program_id(0) == 0, (win_ref,), scratch)
    sh, sc = mod_ref[0:1, :], mod_ref[1:2, :]
    dk = hk // GLA_HEADS
    r = lax.broadcasted_iota(jnp.int32, (GLA_CHUNK, GLA_CHUNK), 0)
    c = lax.broadcasted_iota(jnp.int32, (GLA_CHUNK, GLA_CHUNK), 1)
    lower = jnp.where(c <= r, 1.0, 0.0).astype(BF16)
    upper = jnp.where(c >= r, 1.0, 0.0).astype(BF16)
    h = (x_ref[...] * (1.0 + sc) + sh).astype(BF16)
    proj = _dot(h, win[...])
    q_ref[...] = proj[:, 0:hk] * (dk ** -0.5)
    k_ref[...] = proj[:, hk:2 * hk]
    v_ref[...] = proj[:, 2 * hk:2 * hk + hv].astype(BF16)
    og_ref[...] = proj[:, 2 * hk + hv:2 * hk + 2 * hv]
    z = _dot(_dot(h, w1_ref[...]).astype(BF16), w2_ref[...]) + bg_ref[...]
    log_gate = (jnp.minimum(z, 0.0) - jnp.log1p(jnp.exp(-jnp.abs(z)))) * (1.0 / GLA_TAU)
    for ch in range(TM // GLA_CHUNK):
        rows = slice(ch * GLA_CHUNK, (ch + 1) * GLA_CHUNK)
        f_hi, f_mid, f_lo = _split3(log_gate[rows, 0:hk])
        bf_ref[rows, :] = _dot(lower, f_hi) + _dot(lower, f_mid) + _dot(lower, f_lo)
        b_hi, b_mid, b_lo = _split3(log_gate[rows, hk:2 * hk])
        bb_ref[rows, :] = _dot(upper, b_hi) + _dot(upper, b_mid) + _dot(upper, b_lo)


def _gla_proj(lay, layer, j, x, mod, w_in, w1, w2, b_gate, jobs):
    d = lay.d
    n_in = _weight_shape(w_in)[1]
    hk = w2.shape[2] // 2
    hv = (n_in - 2 * hk) // 2
    rank2 = w1.shape[2]
    row = lambda i: (i, 0)
    win_arr, win_spec, win_scr = _weight_in(w_in)
    return _run(
        functools.partial(_gla_proj_kernel, lay, hk, hv),
        name=f"gla_proj_{layer}", grid=(lay.n_steps,),
        args=[x, mod, win_arr, w1, w2, b_gate],
        in_specs=[
            pl.BlockSpec((TM, d), row),
            lay.mod_spec(lambda i: (i, TM)),
            win_spec,
            _resident((None, d, rank2), lambda i: (j, 0, 0)),
            _resident((None, rank2, 2 * hk), lambda i: (j, 0, 0)),
            _resident((None, 1, 2 * hk), lambda i: (j, 0, 0)),
        ],
        out_shape=[jax.ShapeDtypeStruct((lay.n_tok, hk), F32),
                   jax.ShapeDtypeStruct((lay.n_tok, hk), F32),
                   jax.ShapeDtypeStruct((lay.n_tok, hv), BF16),
                   jax.ShapeDtypeStruct((lay.n_tok, hv), F32),
                   jax.ShapeDtypeStruct((lay.n_tok, hk), F32),
                   jax.ShapeDtypeStruct((lay.n_tok, hk), F32)],
        out_specs=[pl.BlockSpec((TM, hk), row), pl.BlockSpec((TM, hk), row), pl.BlockSpec((TM, hv), row),
                   pl.BlockSpec((TM, hv), row), pl.BlockSpec((TM, hk), row), pl.BlockSpec((TM, hk), row)],
        scratch_shapes=[s for s in (win_scr,) if s is not None],
        jobs=jobs)


def _block_diag(blocks):
    n = len(blocks)
    zero = jnp.zeros_like(blocks[0])
    return jnp.concatenate(
        [jnp.concatenate([blocks[i] if j == i else zero for j in range(n)], axis=1) for i in range(n)], axis=0)


def _gla_chain_mask(dtype):
    n = GLA_CHAINS * GLA_CHUNK
    r = lax.broadcasted_iota(jnp.int32, (n, n), 0)
    c = lax.broadcasted_iota(jnp.int32, (n, n), 1)
    shift = GLA_CHUNK.bit_length() - 1
    assert 1 << shift == GLA_CHUNK
    r_chain, r_t = jnp.right_shift(r, shift), jnp.bitwise_and(r, GLA_CHUNK - 1)
    c_chain, c_t = jnp.right_shift(c, shift), jnp.bitwise_and(c, GLA_CHUNK - 1)
    direction = 1 - 2 * jnp.bitwise_and(r_chain, 1)
    ordered = jnp.where((c_t - r_t) * direction <= 0, 1.0, 0.0)
    return jnp.where(r_chain == c_chain, ordered, 0.0).astype(dtype)


def _gla_fast_step(refs, st_scr, mask_scr, g, t, accumulate, n_chunks, dk, dv):
    q_ref, k_ref, v_ref, bf_ref, bb_ref, o_ref = refs
    mid = GLA_CHUNK // 2
    slots = []
    for u in range(2):
        qs_l, ks_l, v_l, qi_l, kh_l, dec_l, dst_l, old_l = [], [], [], [], [], [], [], []
        for j in range(2):
            h = 2 * g + j
            kcols = slice(h * dk, (h + 1) * dk)
            vcols = slice(h * dv, (h + 1) * dv)
            for forward, b_ref in ((True, bf_ref), (False, bb_ref)):
                c = (2 * t + u) if forward else (n_chunks - 1 - 2 * t - u)
                rows = pl.ds(pl.multiple_of(c * GLA_CHUNK, GLA_CHUNK), GLA_CHUNK)
                q, k, b, v = q_ref[rows, kcols], k_ref[rows, kcols], b_ref[rows, kcols], v_ref[rows, vcols]
                rho = b[mid:mid + 1]
                q_s = q * jnp.exp(b - rho)
                k_s = k * jnp.exp(rho - b)
                b_exit = b[GLA_CHUNK - 1:GLA_CHUNK] if forward else b[0:1]
                qs_l.append(q_s.astype(BF16))
                ks_l.append(k_s.astype(BF16))
                v_l.append(v)
                qi_l.append((q_s * jnp.exp(rho)).astype(BF16))
                kh_l.append((k_s * jnp.exp(b_exit - rho)).astype(BF16))
                dec_l.append(jnp.exp(b_exit))
                dst_l.append((rows, vcols))
                old_l.append(o_ref[rows, vcols] if accumulate else None)
        v_all = jnp.concatenate(v_l, axis=0)
        a = _dot_nt(jnp.concatenate(qs_l, axis=0), jnp.concatenate(ks_l, axis=0)).astype(BF16) * mask_scr[...]
        slots.append((_dot(a, v_all), _block_diag(qi_l), _block_diag(kh_l), v_all,
                      jnp.concatenate(dec_l, axis=1), dst_l, old_l))
    st = st_scr[g]
    stores = []
    for o_intra, q_bd, k_bd, v_all, decay, dst_l, old_l in slots:
        o_all = o_intra + _dot_nt(q_bd, st.astype(BF16))
        st = st * decay + _dot_tn(v_all, k_bd)
        for ci, ((rows, vcols), old) in enumerate(zip(dst_l, old_l)):
            o = o_all[ci * GLA_CHUNK:(ci + 1) * GLA_CHUNK]
            stores.append((rows, vcols, o if old is None else o + old))
    st_scr[g] = st
    for rows, vcols, o in stores:
        o_ref[rows, vcols] = o


def _gla_chunk_safe(q, k, v, b, st, forward):
    n_sub = GLA_CHUNK // GLA_SUB
    o_rows = []
    qi = (q * jnp.exp(b)).astype(BF16)
    o_inter = _dot_nt(qi, st.astype(BF16))
    t_idx = lax.broadcasted_iota(jnp.int32, (GLA_SUB, 1), 0)
    for blk in range(n_sub):
        rows = slice(blk * GLA_SUB, (blk + 1) * GLA_SUB)
        qb, kb, bb_, vb = q[rows], k[rows], b[rows], v[rows].astype(F32)
        o_blk = o_inter[rows]
        if forward and blk > 0:
            others = slice(0, blk * GLA_SUB)
            rho = b[blk * GLA_SUB - 1:blk * GLA_SUB]
        elif (not forward) and blk < n_sub - 1:
            others = slice((blk + 1) * GLA_SUB, GLA_CHUNK)
            rho = b[(blk + 1) * GLA_SUB:(blk + 1) * GLA_SUB + 1]
        else:
            others = None
        if others is not None:
            q_s = (qb * jnp.exp(bb_ - rho)).astype(BF16)
            k_s = (k[others] * jnp.exp(rho - b[others])).astype(BF16)
            a = _dot_nt(q_s, k_s)
            o_blk = o_blk + _dot(a.astype(BF16), v[others])
        for s in range(GLA_SUB):
            keep = (t_idx >= s) if forward else (t_idx <= s)
            decay = jnp.exp(jnp.where(keep, bb_ - bb_[s:s + 1], NEG_BIG))
            a_col = jnp.sum(qb * kb[s:s + 1] * decay, axis=-1, keepdims=True)
            o_blk = o_blk + a_col * vb[s:s + 1]
        o_rows.append(o_blk)
    b_exit = b[GLA_CHUNK - 1:GLA_CHUNK] if forward else b[0:1]
    k_hat = (k * jnp.exp(b_exit - b)).astype(BF16)
    st_new = st * jnp.exp(b_exit) + _dot_tn(v, k_hat)
    return jnp.concatenate(o_rows, axis=0), st_new


def _gla_scan_kernel(n_chunks, hps, dk, dv, has_init, *refs):
    n_groups = hps // 2
    if has_init:
        (q_ref, k_ref, v_ref, bf_ref, bb_ref, s0f_ref, s0b_ref, o_ref, st_scr, mask_scr) = refs
    else:
        (q_ref, k_ref, v_ref, bf_ref, bb_ref, o_ref, sf_ref, sb_ref, st_scr, mask_scr) = refs
    data_refs = (q_ref, k_ref, v_ref, bf_ref, bb_ref, o_ref)

    def chain_cols(h, forward):
        ci = 2 * (h % 2) + (0 if forward else 1)
        return h // 2, slice(ci * dk, (ci + 1) * dk)

    for h in range(hps):
        for forward in (True, False):
            g, cols = chain_cols(h, forward)
            if has_init:
                st_scr[g, :, cols] = (s0f_ref if forward else s0b_ref)[h].T
            else:
                st_scr[g, :, cols] = jnp.zeros((dv, dk), F32)
    mask_scr[...] = _gla_chain_mask(BF16)

    def safe_step(i, accumulate):
        pending = []
        for h in range(hps):
            kcols = slice(h * dk, (h + 1) * dk)
            vcols = slice(h * dv, (h + 1) * dv)
            for forward, b_ref in ((True, bf_ref), (False, bb_ref)):
                c = i if forward else n_chunks - 1 - i
                rows = pl.ds(pl.multiple_of(c * GLA_CHUNK, GLA_CHUNK), GLA_CHUNK)
                g, cols = chain_cols(h, forward)
                o, st_new = _gla_chunk_safe(q_ref[rows, kcols], k_ref[rows, kcols], v_ref[rows, vcols],
                                            b_ref[rows, kcols], st_scr[g, :, cols], forward)
                if accumulate:
                    o = o + o_ref[rows, vcols]
                pending.append((rows, vcols, g, cols, o, st_new))
        for rows, vcols, g, cols, o, st_new in pending:
            st_scr[g, :, cols] = st_new
            o_ref[rows, vcols] = o

    def fast_step(t, accumulate):
        for g in range(n_groups):
            _gla_fast_step(data_refs, st_scr, mask_scr, g, t, accumulate, n_chunks, dk, dv)

    def run(step, n_steps):
        def first(i, carry):
            step(i, False)
            return carry

        def second(i, carry):
            step(i, True)
            return carry

        lax.fori_loop(0, n_steps // 2, first, 0)
        lax.fori_loop(n_steps // 2, n_steps, second, 0)

    span = jnp.zeros((1, hps * dk), F32)
    for c in range(n_chunks):
        top = slice(c * GLA_CHUNK, c * GLA_CHUNK + 1)
        bottom = slice((c + 1) * GLA_CHUNK - 1, (c + 1) * GLA_CHUNK)
        span = jnp.maximum(span, jnp.maximum(bf_ref[top, :] - bf_ref[bottom, :],
                                             bb_ref[bottom, :] - bb_ref[top, :]))
    bounded = jnp.max(span) < GLA_SAFE_RANGE

    @pl.when(bounded)
    def _():
        run(fast_step, n_chunks // 2)

    @pl.when(jnp.logical_not(bounded))
    def _():
        run(safe_step, n_chunks)

    if not has_init:
        for h in range(hps):
            for forward in (True, False):
                g, cols = chain_cols(h, forward)
                (sf_ref if forward else sb_ref)[h] = st_scr[g, :, cols].T


def _gla_scan(lay, j, q, k, v, bf, bb, init, *, row0, n_seq, seq, hps):
    hk, hv = q.shape[1], v.shape[1]
    dk, dv = hk // GLA_HEADS, hv // GLA_HEADS
    n_chunks = seq // GLA_CHUNK
    groups = GLA_HEADS // hps
    assert n_chunks % 4 == 0 and GLA_HEADS % hps == 0 and hps % 2 == 0
    has_init = init is not None
    blk = lambda b, g: (row0 // seq + b, g)
    kspec = pl.BlockSpec((seq, hps * dk), blk)
    vspec = pl.BlockSpec((seq, hps * dv), blk)
    in_specs = [kspec, kspec, vspec, kspec, kspec]
    args = [q, k, v, bf, bb]
    out_shape = [jax.ShapeDtypeStruct((n_seq * seq, hv), F32)]
    out_specs = [pl.BlockSpec((seq, hps * dv), lambda b, g: (b, g))]
    if has_init:
        s0f, s0b = init
        st_spec = pl.BlockSpec((None, hps, dk, dv), lambda b, g: (b, j * groups + g, 0, 0))
        in_specs += [st_spec, st_spec]
        args += [s0f, s0b]
    else:
        st_shape = jax.ShapeDtypeStruct((n_seq, GLA_HEADS, dk, dv), F32)
        st_spec = pl.BlockSpec((None, hps, dk, dv), lambda b, g: (b, g, 0, 0))
        out_shape += [st_shape, st_shape]
        out_specs += [st_spec, st_spec]
    outs, _ = _run(
        functools.partial(_gla_scan_kernel, n_chunks, hps, dk, dv, has_init),
        name=f"gla_scan_{'latent' if has_init else 'context'}", grid=(n_seq, groups),
        args=args, in_specs=in_specs, out_shape=out_shape, out_specs=out_specs,
        scratch_shapes=[pltpu.VMEM((hps // 2, dv, GLA_CHAINS * dk), F32),
                        pltpu.VMEM((GLA_CHAINS * GLA_CHUNK, GLA_CHAINS * GLA_CHUNK), BF16)])
    return outs


def _rope_tables(lay):
    n_freq = HEAD_DIM // 4
    pos = np.arange(lay.dec_seq)
    freqs = (np.float32(ROPE_THETA) ** (-np.arange(n_freq, dtype=np.float32) / np.float32(n_freq))).astype(np.float32)
    ang_r = (pos // GRID_W).astype(np.float32)[:, None] * freqs
    ang_c = (pos % GRID_W).astype(np.float32)[:, None] * freqs
    cos = np.concatenate([np.cos(ang_r)] * 2 + [np.cos(ang_c)] * 2, axis=-1)
    sin = np.concatenate([-np.sin(ang_r), np.sin(ang_r), -np.sin(ang_c), np.sin(ang_c)], axis=-1)
    cos = np.concatenate([np.ones((TM, HEAD_DIM), np.float32), cos], axis=0).astype(np.float32)
    sin = np.concatenate([np.zeros((TM, HEAD_DIM), np.float32), sin], axis=0).astype(np.float32)
    return jnp.asarray(cos), jnp.asarray(sin)


def kernel(x_prompt, x_sample, c, cache_k, cache_v, state_gla_fwd, state_gla_bwd, c_ctx, w_ada, b_ada, ln_g, ln_b, conv_w_in, conv_w, conv_w_out, attn_w_qkv, attn_q_norm, attn_k_norm, attn_w_o, gla_w_in, gla_w_gate1, gla_w_gate2, gla_b_gate, gla_norm, gla_w_o, ffn_w_in, ffn_w_out):
    batch, seq, d = x_prompt.shape
    dec_batch, dec_seq, _ = x_sample.shape
    depth = w_ada.shape[0]
    lay = _Layout(batch, seq, dec_batch, dec_seq, d, depth)
    n_steps = lay.n_steps
    kv = N_KV_HEADS * HEAD_DIM
    past = cache_k.shape[2]

    cond = jnp.concatenate([c_ctx[None, :], c, jnp.zeros((MOD_ROWS - 1 - dec_batch, d), F32)], axis=0)
    b_ada3 = b_ada.reshape(depth, 1, -1)
    lng = ln_g.reshape(depth * 2, d)
    lnb = ln_b.reshape(depth * 2, d)
    g1 = jnp.concatenate([gla_w_gate1[:, 0], gla_w_gate1[:, 1]], axis=-1).astype(BF16)
    zeros = jnp.zeros_like(gla_w_gate2[:, 0])
    g2 = jnp.concatenate([jnp.concatenate([gla_w_gate2[:, 0], zeros], axis=-1),
                          jnp.concatenate([zeros, gla_w_gate2[:, 1]], axis=-1)], axis=1).astype(BF16)
    gb = jnp.concatenate([gla_b_gate[:, 0], gla_b_gate[:, 1]], axis=-1)[:, None, :]
    cos_t, sin_t = _rope_tables(lay)

    def mixer_weights(i):
        kind, j = i % N_MIXERS, i // N_MIXERS
        return [[(conv_w_in, j), (conv_w_out, j)], [(attn_w_qkv, j), (attn_w_o, j)], [(gla_w_in, j), (gla_w_o, j)]][kind]

    x = (x_prompt.reshape(lay.n_prompt, d), x_sample.reshape(dec_batch * dec_seq, d))
    mod = _modulation(cond, w_ada, b_ada3, 0).reshape(MOD_ROWS, 6, d)
    mix_w = mixer_weights(0)
    new_k, new_v, new_sf, new_sb = [], [], [], []
    y_prompt = y_sample = None
    for i in range(depth):
        kind, j = i % N_MIXERS, i // N_MIXERS
        last = i == depth - 1
        ffn_jobs = [_cast_job(ffn_w_in, i, n_steps), _cast_job(ffn_w_out, i, n_steps)]
        gla_pre = None
        if kind == 0:
            x, jr = _conv_mixer(lay, i, j, x if i == 0 else (x,), mod, lng, lnb, mix_w[0], conv_w, mix_w[1], ffn_jobs)
        elif kind == 1:
            (q, k, v, kf, vf), jr = _attn_qkv(lay, i, j, x, mod, mix_w[0], attn_q_norm[:, None, :],
                                              attn_k_norm[:, None, :], cos_t, sin_t, ffn_jobs)
            new_k.append(kf[:lay.n_prompt].reshape(batch, seq, N_KV_HEADS, HEAD_DIM))
            new_v.append(vf[:lay.n_prompt].reshape(batch, seq, N_KV_HEADS, HEAD_DIM))
            kc = cache_k[:, j].reshape(dec_batch * past, kv).astype(BF16)
            vc = cache_v[:, j].reshape(dec_batch * past, kv).astype(BF16)
            x = _attention(lay, i, q, k, v, None, x, mod, lng, lnb, mix_w[1],
                           row0=0, n_seq=batch, seq=seq, tq=TM)
            x = _attention(lay, i, q, k, v, (kc, vc), x, mod, lng, lnb, mix_w[1],
                           row0=lay.n_prompt, n_seq=dec_batch, seq=dec_seq, tq=TM)
        else:
            (q, k, v, og, bf, bb), jr = _gla_proj(lay, i, j, x, mod, mix_w[0], g1, g2, gb, ffn_jobs)
            s0f = state_gla_fwd.reshape(dec_batch, -1, *state_gla_fwd.shape[3:])
            s0b = state_gla_bwd.reshape(dec_batch, -1, *state_gla_bwd.shape[3:])
            o_ctx, sf, sb = _gla_scan(lay, j, q, k, v, bf, bb, None, row0=0, n_seq=batch, seq=seq,
                                      hps=GLA_HEADS_PER_STEP_CONTEXT)
            o_lat, = _gla_scan(lay, j, q, k, v, bf, bb, (s0f, s0b), row0=lay.n_prompt, n_seq=dec_batch,
                               seq=dec_seq, hps=GLA_HEADS_PER_STEP_LATENT)
            new_sf.append(sf)
            new_sb.append(sb)
            w_o = mix_w[1]
            if isinstance(w_o, tuple):
                w_o = w_o[0][w_o[1]].astype(BF16)
            gla_pre = (o_ctx, o_lat, og, gla_norm[j][None, :], w_o)
        ffn_w = (jr[0][0], jr[1][0])
        next_jobs = []
        if not last:
            next_jobs = [_mod_job(cond, w_ada, b_ada3, i + 1, n_steps)]
            next_jobs += [_cast_job(arr, jj, n_steps) for arr, jj in mixer_weights(i + 1)]
        outs, jr = _ffn(lay, i, x, mod, lng, lnb, ffn_w[0], ffn_w[1], split_out=last, gla_pre=gla_pre,
                        jobs=next_jobs)
        if last:
            y_prompt, y_sample = outs
        else:
            x, = outs
            mod = jr[0][0].reshape(MOD_ROWS, 6, d)
            mix_w = [jr[1][0], jr[2][0]]

    def stack_layers(parts):
        return parts[0][:, None] if len(parts) == 1 else jnp.stack(parts, axis=1)

    y_prompt = y_prompt.reshape(batch, seq, d)
    y_sample = y_sample.reshape(dec_batch, dec_seq, d)
    return (y_prompt, y_sample, stack_layers(new_k), stack_layers(new_v),
            stack_layers(new_sf), stack_layers(new_sb))
```

```python
import functools

import jax
import jax.numpy as jnp
import numpy as np
from jax import lax
from jax.experimental import pallas as pl
from jax.experimental.pallas import tpu as pltpu

F32 = jnp.float32
BF16 = jnp.bfloat16

N_MIXERS = 3
CONV_WIDTH = 3
HEAD_DIM = 128
N_KV_HEADS = 2
GRID_W = 64
ROPE_THETA = 10000.0
GLA_HEADS = 4
GLA_TAU = 16.0
GLA_CHUNK = 64
LN_EPS = 1e-5
RMS_EPS = 1e-6

LANES = 128
BF16_SUBLANES = 16
VMEM_LIMIT = 58 * 1024 * 1024

MOD_GROUP = 8
MOD_NT = 1536
TM = 512
SUB_TILES = 2
FFN_SUB_ROWS = (256, 256)
HALO = BF16_SUBLANES
WEIGHT_CAST_ROWS = 128
GLA_SUB = 16
GLA_SAFE_RANGE = 80.0
GLA_CHAINS = 4
GLA_HEADS_PER_STEP_CONTEXT = 4
GLA_HEADS_PER_STEP_LATENT = 2
NEG_BIG = -1e30


def _dot(a, b):
    return jnp.dot(a, b, preferred_element_type=F32)


def _dot_nt(a, b):
    return lax.dot_general(a, b, (((1,), (1,)), ((), ())), preferred_element_type=F32)


def _dot_tn(a, b):
    return lax.dot_general(a, b, (((0,), (0,)), ((), ())), preferred_element_type=F32)


def _layer_norm(y, g, b):
    mu = jnp.mean(y, axis=-1, keepdims=True)
    yc = y - mu
    var = jnp.mean(yc * yc, axis=-1, keepdims=True)
    return yc * lax.rsqrt(var + LN_EPS) * g + b


def _silu(x):
    return x * jax.nn.sigmoid(x)


def _resident(block_shape, index_map):
    return pl.BlockSpec(block_shape, index_map, pipeline_mode=pl.Buffered(1))


class _Layout:
    def __init__(self, batch, seq, dec_batch, dec_seq, d_model, depth):
        self.batch, self.seq, self.dec_batch, self.dec_seq = batch, seq, dec_batch, dec_seq
        self.d, self.depth = d_model, depth
        self.n_prompt = batch * seq
        self.n_tok = self.n_prompt + dec_batch * dec_seq
        self.n_steps = self.n_tok // TM
        assert self.n_prompt % TM == 0 and dec_seq % TM == 0 and TM % seq == 0
        assert seq & (seq - 1) == 0 and dec_seq & (dec_seq - 1) == 0
        self.alpha = (2.0 * depth) ** 0.25

    def mod_index(self, i, rows):
        r0 = i * rows
        return jnp.where(r0 < self.n_prompt, 0, 1 + (r0 - self.n_prompt) // self.dec_seq)

    def mod_spec(self, tile_of):
        def index(*ids):
            tile, rows = tile_of(*ids)
            return (self.mod_index(tile, rows), 0)

        return pl.BlockSpec((MOD_GROUP, 6 * self.d), index)


def _mod_part(mod_ref, k):
    d = mod_ref.shape[1] // 6
    return mod_ref[0:1, k * d:(k + 1) * d]


class _Job:
    def __init__(self, args, in_specs, out_shapes, out_specs, fn):
        self.args, self.in_specs, self.out_shapes, self.out_specs, self.fn = args, in_specs, out_shapes, out_specs, fn


def _cast_job(w, j, n_steps):
    _, n_rows, n_cols = w.shape
    rows = n_rows // n_steps
    assert rows * n_steps == n_rows and rows % BF16_SUBLANES == 0

    def fn(in_refs, out_refs):
        out_refs[0][...] = in_refs[0][...].astype(BF16)

    return _Job([w], [pl.BlockSpec((None, rows, n_cols), lambda i: (j, i, 0))],
                [jax.ShapeDtypeStruct((n_rows, n_cols), BF16)], [pl.BlockSpec((rows, n_cols), lambda i: (i, 0))], fn)


def _mod_job(cond, w_ada, b_ada, layer, n_steps):
    depth, d, n_out = w_ada.shape
    n_rows = cond.shape[0]
    cols = n_out // n_steps
    assert cols * n_steps == n_out and cols % LANES == 0

    def fn(in_refs, out_refs):
        cond_ref, w_ref, b_ref = in_refs
        out_refs[0][...] = (_dot(_silu(cond_ref[...]).astype(BF16), w_ref[...].astype(BF16))
                            + b_ref[layer:layer + 1, :])

    return _Job([cond, w_ada, b_ada],
                [_resident((n_rows, d), lambda i: (0, 0)),
                 pl.BlockSpec((None, d, cols), lambda i: (layer, 0, i)),
                 pl.BlockSpec((depth, cols), lambda i: (0, i))],
                [jax.ShapeDtypeStruct((n_rows, n_out), F32)], [pl.BlockSpec((n_rows, cols), lambda i: (0, i))], fn)


def _run(kernel_fn, *, name, grid, args, in_specs, out_shape, out_specs, scratch_shapes=(), jobs=(), aliases=None):
    n_in, n_out = len(args), len(out_shape)

    def body(*refs):
        pos = n_in
        job_in = []
        for jb in jobs:
            job_in.append(refs[pos:pos + len(jb.args)])
            pos += len(jb.args)
        main_out = refs[pos:pos + n_out]
        pos += n_out
        job_out = []
        for jb in jobs:
            job_out.append(refs[pos:pos + len(jb.out_shapes)])
            pos += len(jb.out_shapes)
        for jb, ji, jo in zip(jobs, job_in, job_out):
            jb.fn(ji, jo)
        kernel_fn(*refs[:n_in], *main_out, *refs[pos:])

    outs = pl.pallas_call(
        body,
        out_shape=tuple(out_shape) + tuple(s for jb in jobs for s in jb.out_shapes),
        grid=grid,
        in_specs=list(in_specs) + [s for jb in jobs for s in jb.in_specs],
        out_specs=tuple(out_specs) + tuple(s for jb in jobs for s in jb.out_specs),
        scratch_shapes=list(scratch_shapes),
        input_output_aliases=aliases or {},
        compiler_params=pltpu.CompilerParams(dimension_semantics=("arbitrary",) * len(grid),
                                             vmem_limit_bytes=VMEM_LIMIT),
        name=name,
    )(*args, *[a for jb in jobs for a in jb.args])
    main, rest = outs[:n_out], list(outs[n_out:])
    job_results = []
    for jb in jobs:
        job_results.append(tuple(rest[:len(jb.out_shapes)]))
        rest = rest[len(jb.out_shapes):]
    return tuple(main), job_results


def _weight_in(w):
    if isinstance(w, tuple):
        arr, j = w
        _, r, c = arr.shape
        return arr, _resident((None, r, c), lambda *_: (j, 0, 0)), pltpu.VMEM((r, c), BF16)
    r, c = w.shape
    return w, _resident((r, c), lambda *_: (0, 0)), None


def _weight_shape(w):
    return w[0].shape[1:] if isinstance(w, tuple) else w.shape


def _round_weight_once(first_step, w_ref, w_scr):
    n_rows = w_ref.shape[0]
    rows = min(n_rows, WEIGHT_CAST_ROWS)
    assert n_rows % rows == 0

    @pl.when(first_step)
    def _():
        def body(c, carry):
            sl = pl.ds(pl.multiple_of(c * rows, rows), rows)
            w_scr[sl, :] = w_ref[sl, :].astype(BF16)
            return carry

        lax.fori_loop(0, n_rows // rows, body, 0)


def _bf16_weights(first_step, w_refs, scratch):
    scratch = list(scratch)
    out = []
    for w in w_refs:
        if w.dtype == BF16:
            out.append(w)
        else:
            scr = scratch.pop(0)
            _round_weight_once(first_step, w, scr)
            out.append(scr)
    return out, scratch


def _modulation(cond, w_ada, b_ada, layer):
    _, d, n_out = w_ada.shape
    job = _mod_job(cond, w_ada, b_ada, layer, n_out // MOD_NT)
    _, (result,) = _run(lambda: None, name=f"modulation_{layer}", grid=(n_out // MOD_NT,), args=[], in_specs=[],
                        out_shape=[], out_specs=[], jobs=[job])
    return result[0]


def _conv_kernel(lay, layer, split, *refs):
    d = lay.d
    i = pl.program_id(0)
    if split:
        (xca, xpa, xna, xcb, xpb, xnb, mod_ref, lng_ref, lnb_ref, win_ref, cw_ref, wout_ref,
         o_ref, h_scr, uu_scr, *scratch) = refs
        is_context = i * TM < lay.n_prompt
        x = jnp.where(is_context, xca[...], xcb[...])
        x_prev = jnp.where(is_context, xpa[...], xpb[...])
        x_next = jnp.where(is_context, xna[...], xnb[...])
    else:
        (xc_ref, xp_ref, xn_ref, mod_ref, lng_ref, lnb_ref, win_ref, cw_ref, wout_ref,
         o_ref, h_scr, uu_scr, *scratch) = refs
        x, x_prev, x_next = xc_ref[...], xp_ref[...], xn_ref[...]
    (win, wout), _ = _bf16_weights(i == 0, (win_ref, wout_ref), scratch)
    sh, sc, ga = _mod_part(mod_ref, 0), _mod_part(mod_ref, 1), _mod_part(mod_ref, 2)
    one_sc = 1.0 + sc
    h_scr[0:HALO, :] = (x_prev * one_sc + sh).astype(BF16)
    h_scr[HALO:HALO + TM, :] = (x * one_sc + sh).astype(BF16)
    h_scr[HALO + TM:HALO + TM + HALO, :] = (x_next * one_sc + sh).astype(BF16)
    cgu = _dot(h_scr[...], win[:, d:3 * d])
    uu_scr[...] = cgu[:, 0:d] * cgu[:, d:2 * d]
    sub = TM // SUB_TILES
    for s in range(SUB_TILES):
        r0 = s * sub
        bg = _dot(h_scr[HALO + r0:HALO + r0 + sub, :], win[:, 0:d])
        row = i * TM + r0 + lax.broadcasted_iota(jnp.int32, (sub, 1), 0)
        seq_len = jnp.where(row < lay.n_prompt, lay.seq, lay.dec_seq)
        pos = jnp.bitwise_and(row, seq_len - 1)
        u_prev = jnp.where(pos != 0, uu_scr[pl.ds(HALO + r0 - 1, sub), :], 0.0)
        u_next = jnp.where(pos != seq_len - 1, uu_scr[pl.ds(HALO + r0 + 1, sub), :], 0.0)
        y = (u_prev * cw_ref[0:1, :] + uu_scr[pl.ds(HALO + r0, sub), :] * cw_ref[1:2, :]
             + u_next * cw_ref[2:3, :])
        mix = _dot((bg * y).astype(BF16), wout[...])
        o_ref[r0:r0 + sub, :] = _layer_norm(lay.alpha * x[r0:r0 + sub] + ga * mix,
                                            lng_ref[layer, 0:1, :],
                                            lnb_ref[layer, 0:1, :])


def _conv_mixer(lay, layer, j, x_parts, mod, lng, lnb, w_in, cw, w_out, jobs):
    d = lay.d
    per = TM // HALO
    x_specs, x_args = [], []
    tile0 = 0
    for part in x_parts:
        n_tiles = part.shape[0] // TM

        def center(i, t0=tile0, n=n_tiles):
            return (jnp.clip(i - t0, 0, n - 1), 0)

        def prev_halo(i, t0=tile0, n=n_tiles):
            return (jnp.clip((i - t0) * per - 1, 0, n * per - 1), 0)

        def next_halo(i, t0=tile0, n=n_tiles):
            return (jnp.clip((i - t0 + 1) * per, 0, n * per - 1), 0)

        x_specs += [pl.BlockSpec((TM, d), center), pl.BlockSpec((HALO, d), prev_halo),
                    pl.BlockSpec((HALO, d), next_halo)]
        x_args += [part, part, part]
        tile0 += n_tiles
    win_arr, win_spec, win_scr = _weight_in(w_in)
    wout_arr, wout_spec, wout_scr = _weight_in(w_out)
    (x,), job_results = _run(
        functools.partial(_conv_kernel, lay, layer, len(x_parts) == 2),
        name=f"conv_mixer_{layer}", grid=(lay.n_steps,),
        args=x_args + [mod, lng, lnb, win_arr, cw, wout_arr],
        in_specs=x_specs + [
            lay.mod_spec(lambda i: (i, TM)),
            _resident(lng.shape, lambda i: (0, 0, 0)),
            _resident(lnb.shape, lambda i: (0, 0, 0)),
            win_spec,
            _resident((None, CONV_WIDTH, d), lambda i: (j, 0, 0)),
            wout_spec,
        ],
        out_shape=[jax.ShapeDtypeStruct((lay.n_tok, d), F32)],
        out_specs=[pl.BlockSpec((TM, d), lambda i: (i, 0))],
        scratch_shapes=[pltpu.VMEM((TM + 2 * HALO, d), BF16), pltpu.VMEM((TM + 2 * HALO, d), F32)]
        + [s for s in (win_scr, wout_scr) if s is not None],
        jobs=jobs)
    return x, job_results


def _ffn_kernel(lay, layer, split_out, gla_pre, *refs):
    x_ref, mod_ref, lng_ref, lnb_ref, win_ref, wout_ref = refs[:6]
    pos = 6
    if gla_pre:
        oc_ref, ol_ref, og_ref, ng_ref, wo_ref = refs[pos:pos + 5]
        pos += 5
    n_out = 2 if split_out else 1
    out_refs = refs[pos:pos + n_out]
    scratch = refs[pos + n_out:]
    d_ff = wout_ref.shape[0]
    is_context = pl.program_id(0) * TM < lay.n_prompt
    sh, sc, ga = _mod_part(mod_ref, 3), _mod_part(mod_ref, 4), _mod_part(mod_ref, 5)
    ln_g = lng_ref[layer, 1:2, :]
    ln_b = lnb_ref[layer, 1:2, :]
    results = []
    row_starts = np.cumsum((0,) + FFN_SUB_ROWS)
    assert row_starts[-1] == TM
    for r0, r1 in zip(row_starts[:-1], row_starts[1:]):
        rows = slice(int(r0), int(r1))
        x = x_ref[rows, :]
        if gla_pre:
            z_scr, = scratch
            dv = ng_ref.shape[1]
            for h in range(GLA_HEADS):
                sl = slice(h * dv, (h + 1) * dv)
                o = jnp.where(is_context, oc_ref[rows, sl], ol_ref[rows, sl])
                o = o * lax.rsqrt(jnp.mean(o * o, axis=-1, keepdims=True) + RMS_EPS) * ng_ref[...]
                z_scr[rows, sl] = (o * _silu(og_ref[rows, sl])).astype(BF16)
            mix = _dot(z_scr[rows, :], wo_ref[...])
            x = _layer_norm(lay.alpha * x + _mod_part(mod_ref, 2) * mix,
                            lng_ref[layer, 0:1, :], lnb_ref[layer, 0:1, :])
        h = (x * (1.0 + sc) + sh).astype(BF16)
        g = _dot(h, win_ref[:, 0:d_ff])
        u = _dot(h, win_ref[:, d_ff:2 * d_ff])
        a = (_silu(g) * u).astype(BF16)
        y = _dot(a, wout_ref[...])
        results.append((rows, _layer_norm(lay.alpha * x + ga * y, ln_g, ln_b)))
    if split_out:
        @pl.when(is_context)
        def _():
            for rows, res in results:
                out_refs[0][rows, :] = res

        @pl.when(jnp.logical_not(is_context))
        def _():
            for rows, res in results:
                out_refs[1][rows, :] = res
    else:
        for rows, res in results:
            out_refs[0][rows, :] = res


def _ffn(lay, layer, x, mod, lng, lnb, w_in, w_out, *, split_out, gla_pre, jobs):
    d = lay.d
    d_ff = w_out.shape[0]
    row = lambda i: (i, 0)
    n_ctx = lay.n_prompt // TM
    n_lat = lay.n_steps - n_ctx
    ctx_row = lambda i: (jnp.minimum(i, n_ctx - 1), 0)
    lat_row = lambda i: (jnp.clip(i - n_ctx, 0, n_lat - 1), 0)
    args = [x, mod, lng, lnb, w_in, w_out]
    in_specs = [
        pl.BlockSpec((TM, d), row),
        lay.mod_spec(lambda i: (i, TM)),
        _resident(lng.shape, lambda i: (0, 0, 0)),
        _resident(lnb.shape, lambda i: (0, 0, 0)),
        _resident((d, 2 * d_ff), lambda i: (0, 0)),
        _resident((d_ff, d), lambda i: (0, 0)),
    ]
    scratch = []
    if gla_pre is not None:
        o_context, o_latent, og, norm_g, w_o = gla_pre
        hv = og.shape[1]
        args += [o_context, o_latent, og, norm_g, w_o]
        in_specs += [pl.BlockSpec((TM, hv), ctx_row), pl.BlockSpec((TM, hv), lat_row), pl.BlockSpec((TM, hv), row),
                     _resident(norm_g.shape, lambda i: (0, 0)), _resident((hv, d), lambda i: (0, 0))]
        scratch = [pltpu.VMEM((TM, hv), BF16)]
    if split_out:
        out_shape = [jax.ShapeDtypeStruct((lay.n_prompt, d), F32),
                     jax.ShapeDtypeStruct((lay.n_tok - lay.n_prompt, d), F32)]
        out_specs = [pl.BlockSpec((TM, d), ctx_row), pl.BlockSpec((TM, d), lat_row)]
    else:
        out_shape = [jax.ShapeDtypeStruct((lay.n_tok, d), F32)]
        out_specs = [pl.BlockSpec((TM, d), row)]
    return _run(functools.partial(_ffn_kernel, lay, layer, split_out, gla_pre is not None),
                name=f"ffn_{layer}", grid=(lay.n_steps,), args=args, in_specs=in_specs,
                out_shape=out_shape, out_specs=out_specs, scratch_shapes=scratch, jobs=jobs)


def _qkv_kernel(lay, n_heads, x_ref, mod_ref, w_ref, qg_ref, kg_ref, cos_ref, sin_ref,
                q_ref, k_ref, v_ref, kf_ref, vf_ref, *scratch):
    (w,), _ = _bf16_weights(pl.program_id(0) == 0, (w_ref,), scratch)
    sh, sc = _mod_part(mod_ref, 0), _mod_part(mod_ref, 1)
    lane = lax.broadcasted_iota(jnp.int32, (1, HEAD_DIM), 1)
    first_half = jnp.bitwise_and(lane, HEAD_DIM // 4) == 0
    q_gain = qg_ref[...] * (HEAD_DIM ** -0.5)
    k0 = n_heads * HEAD_DIM
    v0 = k0 + N_KV_HEADS * HEAD_DIM
    for s in range(SUB_TILES):
        rows = slice(s * (TM // SUB_TILES), (s + 1) * (TM // SUB_TILES))
        h = (x_ref[rows, :] * (1.0 + sc) + sh).astype(BF16)
        qkv = _dot(h, w[...])
        cos, sin = cos_ref[rows, :], sin_ref[rows, :]

        def norm_rope(t, g):
            t = t * lax.rsqrt(jnp.mean(t * t, axis=-1, keepdims=True) + RMS_EPS) * g
            partner = jnp.where(first_half,
                                pltpu.roll(t, HEAD_DIM - HEAD_DIM // 4, axis=1),
                                pltpu.roll(t, HEAD_DIM // 4, axis=1))
            return t * cos + partner * sin

        for hq in range(n_heads):
            sl = slice(hq * HEAD_DIM, (hq + 1) * HEAD_DIM)
            q_ref[rows, sl] = norm_rope(qkv[:, sl], q_gain).astype(BF16)
        for hk in range(N_KV_HEADS):
            sl = slice(hk * HEAD_DIM, (hk + 1) * HEAD_DIM)
            kh = norm_rope(qkv[:, k0 + hk * HEAD_DIM:k0 + (hk + 1) * HEAD_DIM], kg_ref[...])
            kf_ref[rows, sl] = kh
            k_ref[rows, sl] = kh.astype(BF16)
        v = qkv[:, v0:v0 + N_KV_HEADS * HEAD_DIM]
        vf_ref[rows, :] = v
        v_ref[rows, :] = v.astype(BF16)


def _attn_qkv(lay, layer, j, x, mod, w_qkv, q_gain, k_gain, cos_t, sin_t, jobs):
    d = lay.d
    n_qkv = _weight_shape(w_qkv)[1]
    kv = N_KV_HEADS * HEAD_DIM
    n_heads = (n_qkv - 2 * kv) // HEAD_DIM
    row = lambda i: (i, 0)

    def rope_row(i):
        r0 = i * TM
        return (jnp.where(r0 < lay.n_prompt, 0, 1 + ((r0 - lay.n_prompt) % lay.dec_seq) // TM), 0)

    w_arr, w_spec, w_scr = _weight_in(w_qkv)
    return _run(
        functools.partial(_qkv_kernel, lay, n_heads),
        name=f"attn_qkv_{layer}", grid=(lay.n_steps,),
        args=[x, mod, w_arr, q_gain, k_gain, cos_t, sin_t],
        in_specs=[
            pl.BlockSpec((TM, d), row),
            lay.mod_spec(lambda i: (i, TM)),
            w_spec,
            _resident((None, 1, HEAD_DIM), lambda i: (j, 0, 0)),
            _resident((None, 1, HEAD_DIM), lambda i: (j, 0, 0)),
            pl.BlockSpec((TM, HEAD_DIM), rope_row),
            pl.BlockSpec((TM, HEAD_DIM), rope_row),
        ],
        out_shape=[jax.ShapeDtypeStruct((lay.n_tok, n_heads * HEAD_DIM), BF16),
                   jax.ShapeDtypeStruct((lay.n_tok, kv), BF16),
                   jax.ShapeDtypeStruct((lay.n_tok, kv), BF16),
                   jax.ShapeDtypeStruct((lay.n_tok, kv), F32),
                   jax.ShapeDtypeStruct((lay.n_tok, kv), F32)],
        out_specs=[pl.BlockSpec((TM, n_heads * HEAD_DIM), row), pl.BlockSpec((TM, kv), row),
                   pl.BlockSpec((TM, kv), row), pl.BlockSpec((TM, kv), row), pl.BlockSpec((TM, kv), row)],
        scratch_shapes=[s for s in (w_scr,) if s is not None],
        jobs=jobs)


def _attn_kernel(lay, layer, n_heads, has_cache, seqs, *refs):
    if has_cache:
        (q_ref, k_ref, v_ref, kc_ref, vc_ref, x_ref, mod_ref, lng_ref, lnb_ref, wo_ref, o_ref, att_scr,
         *scratch) = refs
    else:
        (q_ref, k_ref, v_ref, x_ref, mod_ref, lng_ref, lnb_ref, wo_ref, o_ref, att_scr, *scratch) = refs
    (wo,), _ = _bf16_weights((pl.program_id(0) == 0) & (pl.program_id(1) == 0), (wo_ref,), scratch)
    group = n_heads // N_KV_HEADS
    q_rows = q_ref.shape[0] // seqs
    k_rows = k_ref.shape[0] // seqs
    for sq in range(seqs):
        rq = slice(sq * q_rows, (sq + 1) * q_rows)
        rk = slice(sq * k_rows, (sq + 1) * k_rows)
        for hk in range(N_KV_HEADS):
            ksl = slice(hk * HEAD_DIM, (hk + 1) * HEAD_DIM)
            for g in range(group):
                hq = hk * group + g
                qsl = slice(hq * HEAD_DIM, (hq + 1) * HEAD_DIM)
                qh = q_ref[rq, qsl]
                s = _dot_nt(qh, k_ref[rk, ksl])
                m = jnp.max(s, axis=-1, keepdims=True)
                if has_cache:
                    s_c = _dot_nt(qh, kc_ref[:, ksl])
                    m = jnp.maximum(m, jnp.max(s_c, axis=-1, keepdims=True))
                p = jnp.exp(s - m)
                l = jnp.sum(p, axis=-1, keepdims=True)
                o = _dot(p.astype(BF16), v_ref[rk, ksl])
                if has_cache:
                    p_c = jnp.exp(s_c - m)
                    l = l + jnp.sum(p_c, axis=-1, keepdims=True)
                    o = o + _dot(p_c.astype(BF16), vc_ref[:, ksl])
                att_scr[rq, qsl] = (o / l).astype(BF16)
    ga = _mod_part(mod_ref, 2)
    mix = _dot(att_scr[...], wo[...])
    o_ref[...] = _layer_norm(lay.alpha * x_ref[...] + ga * mix,
                             lng_ref[layer, 0:1, :], lnb_ref[layer, 0:1, :])


def _attention(lay, layer, q, k, v, cache, x, mod, lng, lnb, w_o, *, row0, n_seq, seq, tq):
    d = lay.d
    n_heads = q.shape[1] // HEAD_DIM
    kv = N_KV_HEADS * HEAD_DIM
    has_cache = cache is not None
    seqs = max(tq // seq, 1)
    assert not (has_cache and seqs > 1) and n_seq % seqs == 0
    n_seq, seq = n_seq // seqs, seq * seqs
    q_blocks = seq // tq
    tile = lambda b, i: (row0 // tq + b * q_blocks + i, 0)
    seq_blk = lambda b, i: (row0 // seq + b, 0)
    in_specs = [pl.BlockSpec((tq, n_heads * HEAD_DIM), tile),
                pl.BlockSpec((seq, kv), seq_blk), pl.BlockSpec((seq, kv), seq_blk)]
    args = [q, k, v]
    if has_cache:
        kc, vc = cache
        past = kc.shape[0] // n_seq
        in_specs += [pl.BlockSpec((past, kv), lambda b, i: (b, 0))] * 2
        args += [kc, vc]
    x_index = len(args)
    wo_arr, wo_spec, wo_scr = _weight_in(w_o)
    in_specs += [
        pl.BlockSpec((tq, d), tile),
        lay.mod_spec(lambda b, i: (row0 // tq + b * q_blocks + i, tq)),
        _resident(lng.shape, lambda b, i: (0, 0, 0)),
        _resident(lnb.shape, lambda b, i: (0, 0, 0)),
        wo_spec,
    ]
    args += [x, mod, lng, lnb, wo_arr]
    (x,), _ = _run(
        functools.partial(_attn_kernel, lay, layer, n_heads, has_cache, seqs),
        name=f"attention_{layer}_{'latent' if has_cache else 'context'}", grid=(n_seq, q_blocks),
        args=args, in_specs=in_specs,
        out_shape=[jax.ShapeDtypeStruct((lay.n_tok, d), F32)], out_specs=[pl.BlockSpec((tq, d), tile)],
        scratch_shapes=[pltpu.VMEM((tq, n_heads * HEAD_DIM), BF16)] + [s for s in (wo_scr,) if s is not None],
        aliases={x_index: 0})
    return x


def _split3(x):
    hi = x.astype(BF16)
    r = x - hi.astype(F32)
    mid = r.astype(BF16)
    lo = (r - mid.astype(F32)).astype(BF16)
    return hi, mid, lo


def _gla_proj_kernel(lay, hk, hv, x_ref, mod_ref, win_ref, w1_ref, w2_ref, bg_ref,
                     q_ref, k_ref, v_ref, og_ref, bf_ref, bb_ref, *scratch):
    (win,), _ = _bf16_weights(pl.program_id(0) == 0, (win_ref,), scratch)
    sh, sc = _mod_part(mod_ref, 0), _mod_part(mod_ref, 1)
    dk = hk // GLA_HEADS
    r = lax.broadcasted_iota(jnp.int32, (GLA_CHUNK, GLA_CHUNK), 0)
    c = lax.broadcasted_iota(jnp.int32, (GLA_CHUNK, GLA_CHUNK), 1)
    lower = jnp.where(c <= r, 1.0, 0.0).astype(BF16)
    upper = jnp.where(c >= r, 1.0, 0.0).astype(BF16)
    h = (x_ref[...] * (1.0 + sc) + sh).astype(BF16)
    proj = _dot(h, win[...])
    q_ref[...] = proj[:, 0:hk] * (dk ** -0.5)
    k_ref[...] = proj[:, hk:2 * hk]
    v_ref[...] = proj[:, 2 * hk:2 * hk + hv].astype(BF16)
    og_ref[...] = proj[:, 2 * hk + hv:2 * hk + 2 * hv]
    z = _dot(_dot(h, w1_ref[...]).astype(BF16), w2_ref[...]) + bg_ref[...]
    log_gate = (jnp.minimum(z, 0.0) - jnp.log1p(jnp.exp(-jnp.abs(z)))) * (1.0 / GLA_TAU)
    for ch in range(TM // GLA_CHUNK):
        rows = slice(ch * GLA_CHUNK, (ch + 1) * GLA_CHUNK)
        f_hi, f_mid, f_lo = _split3(log_gate[rows, 0:hk])
        bf_ref[rows, :] = _dot(lower, f_hi) + _dot(lower, f_mid) + _dot(lower, f_lo)
        b_hi, b_mid, b_lo = _split3(log_gate[rows, hk:2 * hk])
        bb_ref[rows, :] = _dot(upper, b_hi) + _dot(upper, b_mid) + _dot(upper, b_lo)


def _gla_proj(lay, layer, j, x, mod, w_in, w1, w2, b_gate, jobs):
    d = lay.d
    n_in = _weight_shape(w_in)[1]
    hk = w2.shape[2] // 2
    hv = (n_in - 2 * hk) // 2
    rank2 = w1.shape[2]
    row = lambda i: (i, 0)
    win_arr, win_spec, win_scr = _weight_in(w_in)
    return _run(
        functools.partial(_gla_proj_kernel, lay, hk, hv),
        name=f"gla_proj_{layer}", grid=(lay.n_steps,),
        args=[x, mod, win_arr, w1, w2, b_gate],
        in_specs=[
            pl.BlockSpec((TM, d), row),
            lay.mod_spec(lambda i: (i, TM)),
            win_spec,
            _resident((None, d, rank2), lambda i: (j, 0, 0)),
            _resident((None, rank2, 2 * hk), lambda i: (j, 0, 0)),
            _resident((None, 1, 2 * hk), lambda i: (j, 0, 0)),
        ],
        out_shape=[jax.ShapeDtypeStruct((lay.n_tok, hk), F32),
                   jax.ShapeDtypeStruct((lay.n_tok, hk), F32),
                   jax.ShapeDtypeStruct((lay.n_tok, hv), BF16),
                   jax.ShapeDtypeStruct((lay.n_tok, hv), F32),
                   jax.ShapeDtypeStruct((lay.n_tok, hk), F32),
                   jax.ShapeDtypeStruct((lay.n_tok, hk), F32)],
        out_specs=[pl.BlockSpec((TM, hk), row), pl.BlockSpec((TM, hk), row), pl.BlockSpec((TM, hv), row),
                   pl.BlockSpec((TM, hv), row), pl.BlockSpec((TM, hk), row), pl.BlockSpec((TM, hk), row)],
        scratch_shapes=[s for s in (win_scr,) if s is not None],
        jobs=jobs)


def _block_diag(blocks):
    n = len(blocks)
    zero = jnp.zeros_like(blocks[0])
    return jnp.concatenate(
        [jnp.concatenate([blocks[i] if j == i else zero for j in range(n)], axis=1) for i in range(n)], axis=0)


def _gla_chain_mask(dtype):
    n = GLA_CHAINS * GLA_CHUNK
    r = lax.broadcasted_iota(jnp.int32, (n, n), 0)
    c = lax.broadcasted_iota(jnp.int32, (n, n), 1)
    shift = GLA_CHUNK.bit_length() - 1
    assert 1 << shift == GLA_CHUNK
    r_chain, r_t = jnp.right_shift(r, shift), jnp.bitwise_and(r, GLA_CHUNK - 1)
    c_chain, c_t = jnp.right_shift(c, shift), jnp.bitwise_and(c, GLA_CHUNK - 1)
    direction = 1 - 2 * jnp.bitwise_and(r_chain, 1)
    ordered = jnp.where((c_t - r_t) * direction <= 0, 1.0, 0.0)
    return jnp.where(r_chain == c_chain, ordered, 0.0).astype(dtype)


def _gla_fast_step(refs, st_scr, mask_scr, g, t, accumulate, n_chunks, dk, dv):
    q_ref, k_ref, v_ref, bf_ref, bb_ref, o_ref = refs
    mid = GLA_CHUNK // 2
    slots = []
    for u in range(2):
        qs_l, ks_l, v_l, qi_l, kh_l, dec_l, dst_l, old_l = [], [], [], [], [], [], [], []
        for j in range(2):
            h = 2 * g + j
            kcols = slice(h * dk, (h + 1) * dk)
            vcols = slice(h * dv, (h + 1) * dv)
            for forward, b_ref in ((True, bf_ref), (False, bb_ref)):
                c = (2 * t + u) if forward else (n_chunks - 1 - 2 * t - u)
                rows = pl.ds(pl.multiple_of(c * GLA_CHUNK, GLA_CHUNK), GLA_CHUNK)
                q, k, b, v = q_ref[rows, kcols], k_ref[rows, kcols], b_ref[rows, kcols], v_ref[rows, vcols]
                rho = b[mid:mid + 1]
                q_s = q * jnp.exp(b - rho)
                k_s = k * jnp.exp(rho - b)
                b_exit = b[GLA_CHUNK - 1:GLA_CHUNK] if forward else b[0:1]
                qs_l.append(q_s.astype(BF16))
                ks_l.append(k_s.astype(BF16))
                v_l.append(v)
                qi_l.append((q_s * jnp.exp(rho)).astype(BF16))
                kh_l.append((k_s * jnp.exp(b_exit - rho)).astype(BF16))
                dec_l.append(jnp.exp(b_exit))
                dst_l.append((rows, vcols))
                old_l.append(o_ref[rows, vcols] if accumulate else None)
        v_all = jnp.concatenate(v_l, axis=0)
        a = _dot_nt(jnp.concatenate(qs_l, axis=0), jnp.concatenate(ks_l, axis=0)).astype(BF16) * mask_scr[...]
        slots.append((_dot(a, v_all), _block_diag(qi_l), _block_diag(kh_l), v_all,
                      jnp.concatenate(dec_l, axis=1), dst_l, old_l))
    st = st_scr[g]
    stores = []
    for o_intra, q_bd, k_bd, v_all, decay, dst_l, old_l in slots:
        o_all = o_intra + _dot_nt(q_bd, st.astype(BF16))
        st = st * decay + _dot_tn(v_all, k_bd)
        for ci, ((rows, vcols), old) in enumerate(zip(dst_l, old_l)):
            o = o_all[ci * GLA_CHUNK:(ci + 1) * GLA_CHUNK]
            stores.append((rows, vcols, o if old is None else o + old))
    st_scr[g] = st
    for rows, vcols, o in stores:
        o_ref[rows, vcols] = o


def _gla_chunk_safe(q, k, v, b, st, forward):
    n_sub = GLA_CHUNK // GLA_SUB
    o_rows = []
    qi = (q * jnp.exp(b)).astype(BF16)
    o_inter = _dot_nt(qi, st.astype(BF16))
    t_idx = lax.broadcasted_iota(jnp.int32, (GLA_SUB, 1), 0)
    for blk in range(n_sub):
        rows = slice(blk * GLA_SUB, (blk + 1) * GLA_SUB)
        qb, kb, bb_, vb = q[rows], k[rows], b[rows], v[rows].astype(F32)
        o_blk = o_inter[rows]
        if forward and blk > 0:
            others = slice(0, blk * GLA_SUB)
            rho = b[blk * GLA_SUB - 1:blk * GLA_SUB]
        elif (not forward) and blk < n_sub - 1:
            others = slice((blk + 1) * GLA_SUB, GLA_CHUNK)
            rho = b[(blk + 1) * GLA_SUB:(blk + 1) * GLA_SUB + 1]
        else:
            others = None
        if others is not None:
            q_s = (qb * jnp.exp(bb_ - rho)).astype(BF16)
            k_s = (k[others] * jnp.exp(rho - b[others])).astype(BF16)
            a = _dot_nt(q_s, k_s)
            o_blk = o_blk + _dot(a.astype(BF16), v[others])
        for s in range(GLA_SUB):
            keep = (t_idx >= s) if forward else (t_idx <= s)
            decay = jnp.exp(jnp.where(keep, bb_ - bb_[s:s + 1], NEG_BIG))
            a_col = jnp.sum(qb * kb[s:s + 1] * decay, axis=-1, keepdims=True)
            o_blk = o_blk + a_col * vb[s:s + 1]
        o_rows.append(o_blk)
    b_exit = b[GLA_CHUNK - 1:GLA_CHUNK] if forward else b[0:1]
    k_hat = (k * jnp.exp(b_exit - b)).astype(BF16)
    st_new = st * jnp.exp(b_exit) + _dot_tn(v, k_hat)
    return jnp.concatenate(o_rows, axis=0), st_new


def _gla_scan_kernel(n_chunks, hps, dk, dv, has_init, *refs):
    n_groups = hps // 2
    if has_init:
        (q_ref, k_ref, v_ref, bf_ref, bb_ref, s0f_ref, s0b_ref, o_ref, st_scr, mask_scr) = refs
    else:
        (q_ref, k_ref, v_ref, bf_ref, bb_ref, o_ref, sf_ref, sb_ref, st_scr, mask_scr) = refs
    data_refs = (q_ref, k_ref, v_ref, bf_ref, bb_ref, o_ref)

    def chain_cols(h, forward):
        ci = 2 * (h % 2) + (0 if forward else 1)
        return h // 2, slice(ci * dk, (ci + 1) * dk)

    for h in range(hps):
        for forward in (True, False):
            g, cols = chain_cols(h, forward)
            if has_init:
                st_scr[g, :, cols] = (s0f_ref if forward else s0b_ref)[h].T
            else:
                st_scr[g, :, cols] = jnp.zeros((dv, dk), F32)
    mask_scr[...] = _gla_chain_mask(BF16)

    def safe_step(i, accumulate):
        pending = []
        for h in range(hps):
            kcols = slice(h * dk, (h + 1) * dk)
            vcols = slice(h * dv, (h + 1) * dv)
            for forward, b_ref in ((True, bf_ref), (False, bb_ref)):
                c = i if forward else n_chunks - 1 - i
                rows = pl.ds(pl.multiple_of(c * GLA_CHUNK, GLA_CHUNK), GLA_CHUNK)
                g, cols = chain_cols(h, forward)
                o, st_new = _gla_chunk_safe(q_ref[rows, kcols], k_ref[rows, kcols], v_ref[rows, vcols],
                                            b_ref[rows, kcols], st_scr[g, :, cols], forward)
                if accumulate:
                    o = o + o_ref[rows, vcols]
                pending.append((rows, vcols, g, cols, o, st_new))
        for rows, vcols, g, cols, o, st_new in pending:
            st_scr[g, :, cols] = st_new
            o_ref[rows, vcols] = o

    def fast_step(t, accumulate):
        for g in range(n_groups):
            _gla_fast_step(data_refs, st_scr, mask_scr, g, t, accumulate, n_chunks, dk, dv)

    def run(step, n_steps):
        def first(i, carry):
            step(i, False)
            return carry

        def second(i, carry):
            step(i, True)
            return carry

        lax.fori_loop(0, n_steps // 2, first, 0)
        lax.fori_loop(n_steps // 2, n_steps, second, 0)

    span = jnp.zeros((1, hps * dk), F32)
    for c in range(n_chunks):
        top = slice(c * GLA_CHUNK, c * GLA_CHUNK + 1)
        bottom = slice((c + 1) * GLA_CHUNK - 1, (c + 1) * GLA_CHUNK)
        span = jnp.maximum(span, jnp.maximum(bf_ref[top, :] - bf_ref[bottom, :],
                                             bb_ref[bottom, :] - bb_ref[top, :]))
    bounded = jnp.max(span) < GLA_SAFE_RANGE

    @pl.when(bounded)
    def _():
        run(fast_step, n_chunks // 2)

    @pl.when(jnp.logical_not(bounded))
    def _():
        run(safe_step, n_chunks)

    if not has_init:
        for h in range(hps):
            for forward in (True, False):
                g, cols = chain_cols(h, forward)
                (sf_ref if forward else sb_ref)[h] = st_scr[g, :, cols].T


def _gla_scan(lay, j, q, k, v, bf, bb, init, *, row0, n_seq, seq, hps):
    hk, hv = q.shape[1], v.shape[1]
    dk, dv = hk // GLA_HEADS, hv // GLA_HEADS
    n_chunks = seq // GLA_CHUNK
    groups = GLA_HEADS // hps
    assert n_chunks % 4 == 0 and GLA_HEADS % hps == 0 and hps % 2 == 0
    has_init = init is not None
    blk = lambda b, g: (row0 // seq + b, g)
    kspec = pl.BlockSpec((seq, hps * dk), blk)
    vspec = pl.BlockSpec((seq, hps * dv), blk)
    in_specs = [kspec, kspec, vspec, kspec, kspec]
    args = [q, k, v, bf, bb]
    out_shape = [jax.ShapeDtypeStruct((n_seq * seq, hv), F32)]
    out_specs = [pl.BlockSpec((seq, hps * dv), lambda b, g: (b, g))]
    if has_init:
        s0f, s0b = init
        st_spec = pl.BlockSpec((None, hps, dk, dv), lambda b, g: (b, j * groups + g, 0, 0))
        in_specs += [st_spec, st_spec]
        args += [s0f, s0b]
    else:
        st_shape = jax.ShapeDtypeStruct((n_seq, GLA_HEADS, dk, dv), F32)
        st_spec = pl.BlockSpec((None, hps, dk, dv), lambda b, g: (b, g, 0, 0))
        out_shape += [st_shape, st_shape]
        out_specs += [st_spec, st_spec]
    outs, _ = _run(
        functools.partial(_gla_scan_kernel, n_chunks, hps, dk, dv, has_init),
        name=f"gla_scan_{'latent' if has_init else 'context'}", grid=(n_seq, groups),
        args=args, in_specs=in_specs, out_shape=out_shape, out_specs=out_specs,
        scratch_shapes=[pltpu.VMEM((hps // 2, dv, GLA_CHAINS * dk), F32),
                        pltpu.VMEM((GLA_CHAINS * GLA_CHUNK, GLA_CHAINS * GLA_CHUNK), BF16)])
    return outs


def _rope_tables(lay):
    n_freq = HEAD_DIM // 4
    pos = np.arange(lay.dec_seq)
    freqs = (np.float32(ROPE_THETA) ** (-np.arange(n_freq, dtype=np.float32) / np.float32(n_freq))).astype(np.float32)
    ang_r = (pos // GRID_W).astype(np.float32)[:, None] * freqs
    ang_c = (pos % GRID_W).astype(np.float32)[:, None] * freqs
    cos = np.concatenate([np.cos(ang_r)] * 2 + [np.cos(ang_c)] * 2, axis=-1)
    sin = np.concatenate([-np.sin(ang_r), np.sin(ang_r), -np.sin(ang_c), np.sin(ang_c)], axis=-1)
    cos = np.concatenate([np.ones((TM, HEAD_DIM), np.float32), cos], axis=0).astype(np.float32)
    sin = np.concatenate([np.zeros((TM, HEAD_DIM), np.float32), sin], axis=0).astype(np.float32)
    return jnp.asarray(cos), jnp.asarray(sin)


def kernel(x_prompt, x_sample, c, cache_k, cache_v, state_gla_fwd, state_gla_bwd, c_ctx, w_ada, b_ada, ln_g, ln_b, conv_w_in, conv_w, conv_w_out, attn_w_qkv, attn_q_norm, attn_k_norm, attn_w_o, gla_w_in, gla_w_gate1, gla_w_gate2, gla_b_gate, gla_norm, gla_w_o, ffn_w_in, ffn_w_out):
    batch, seq, d = x_prompt.shape
    dec_batch, dec_seq, _ = x_sample.shape
    depth = w_ada.shape[0]
    lay = _Layout(batch, seq, dec_batch, dec_seq, d, depth)
    n_steps = lay.n_steps
    kv = N_KV_HEADS * HEAD_DIM
    past = cache_k.shape[2]

    cond = jnp.repeat(jnp.concatenate([c_ctx[None, :], c], axis=0), MOD_GROUP, axis=0)
    lng, lnb = ln_g, ln_b
    g1 = jnp.concatenate([gla_w_gate1[:, 0], gla_w_gate1[:, 1]], axis=-1).astype(BF16)
    zeros = jnp.zeros_like(gla_w_gate2[:, 0])
    g2 = jnp.concatenate([jnp.concatenate([gla_w_gate2[:, 0], zeros], axis=-1),
                          jnp.concatenate([zeros, gla_w_gate2[:, 1]], axis=-1)], axis=1).astype(BF16)
    gb = jnp.concatenate([gla_b_gate[:, 0], gla_b_gate[:, 1]], axis=-1)[:, None, :]
    cos_t, sin_t = _rope_tables(lay)

    def mixer_weights(i):
        kind, j = i % N_MIXERS, i // N_MIXERS
        return [[(conv_w_in, j), (conv_w_out, j)], [(attn_w_qkv, j), (attn_w_o, j)], [(gla_w_in, j), (gla_w_o, j)]][kind]

    x = (x_prompt.reshape(lay.n_prompt, d), x_sample.reshape(dec_batch * dec_seq, d))
    mod = _modulation(cond, w_ada, b_ada, 0)
    mix_w = mixer_weights(0)
    new_k, new_v, new_sf, new_sb = [], [], [], []
    y_prompt = y_sample = None
    for i in range(depth):
        kind, j = i % N_MIXERS, i // N_MIXERS
        last = i == depth - 1
        ffn_jobs = [_cast_job(ffn_w_in, i, n_steps), _cast_job(ffn_w_out, i, n_steps)]
        gla_pre = None
        if kind == 0:
            x, jr = _conv_mixer(lay, i, j, x if i == 0 else (x,), mod, lng, lnb, mix_w[0], conv_w, mix_w[1], ffn_jobs)
        elif kind == 1:
            (q, k, v, kf, vf), jr = _attn_qkv(lay, i, j, x, mod, mix_w[0], attn_q_norm[:, None, :],
                                              attn_k_norm[:, None, :], cos_t, sin_t, ffn_jobs)
            new_k.append(kf[:lay.n_prompt].reshape(batch, seq, N_KV_HEADS, HEAD_DIM))
            new_v.append(vf[:lay.n_prompt].reshape(batch, seq, N_KV_HEADS, HEAD_DIM))
            kc = cache_k[:, j].reshape(dec_batch * past, kv).astype(BF16)
            vc = cache_v[:, j].reshape(dec_batch * past, kv).astype(BF16)
            x = _attention(lay, i, q, k, v, None, x, mod, lng, lnb, mix_w[1],
                           row0=0, n_seq=batch, seq=seq, tq=TM)
            x = _attention(lay, i, q, k, v, (kc, vc), x, mod, lng, lnb, mix_w[1],
                           row0=lay.n_prompt, n_seq=dec_batch, seq=dec_seq, tq=TM)
        else:
            (q, k, v, og, bf, bb), jr = _gla_proj(lay, i, j, x, mod, mix_w[0], g1, g2, gb, ffn_jobs)
            s0f = state_gla_fwd.reshape(dec_batch, -1, *state_gla_fwd.shape[3:])
            s0b = state_gla_bwd.reshape(dec_batch, -1, *state_gla_bwd.shape[3:])
            o_ctx, sf, sb = _gla_scan(lay, j, q, k, v, bf, bb, None, row0=0, n_seq=batch, seq=seq,
                                      hps=GLA_HEADS_PER_STEP_CONTEXT)
            o_lat, = _gla_scan(lay, j, q, k, v, bf, bb, (s0f, s0b), row0=lay.n_prompt, n_seq=dec_batch,
                               seq=dec_seq, hps=GLA_HEADS_PER_STEP_LATENT)
            new_sf.append(sf)
            new_sb.append(sb)
            w_o = mix_w[1]
            if isinstance(w_o, tuple):
                w_o = w_o[0][w_o[1]].astype(BF16)
            gla_pre = (o_ctx, o_lat, og, gla_norm[j][None, :], w_o)
        ffn_w = (jr[0][0], jr[1][0])
        next_jobs = []
        if not last:
            next_jobs = [_mod_job(cond, w_ada, b_ada, i + 1, n_steps)]
            next_jobs += [_cast_job(arr, jj, n_steps) for arr, jj in mixer_weights(i + 1)]
        outs, jr = _ffn(lay, i, x, mod, lng, lnb, ffn_w[0], ffn_w[1], split_out=last, gla_pre=gla_pre,
                        jobs=next_jobs)
        if last:
            y_prompt, y_sample = outs
        else:
            x, = outs
            mod = jr[0][0]
            mix_w = [jr[1][0], jr[2][0]]

    def stack_layers(parts):
        return parts[0][:, None] if len(parts) == 1 else jnp.stack(parts, axis=1)

    y_prompt = y_prompt.reshape(batch, seq, d)
    y_sample = y_sample.reshape(dec_batch, dec_seq, d)
    return (y_prompt, y_sample, stack_layers(new_k), stack_layers(new_v),
            stack_layers(new_sf), stack_layers(new_sb))
```

```python
import functools

import jax
import jax.numpy as jnp
import numpy as np
from jax import lax
from jax.experimental import pallas as pl
from jax.experimental.pallas import tpu as pltpu

F32 = jnp.float32
BF16 = jnp.bfloat16

N_MIXERS = 3
CONV_WIDTH = 3
HEAD_DIM = 128
N_KV_HEADS = 2
GRID_W = 64
ROPE_THETA = 10000.0
GLA_HEADS = 4
GLA_TAU = 16.0
GLA_CHUNK = 64
LN_EPS = 1e-5
RMS_EPS = 1e-6

LANES = 128
BF16_SUBLANES = 16
VMEM_LIMIT = 58 * 1024 * 1024

MOD_GROUP = 8
MOD_NT = 1536
TM = 512
SUB_TILES = 2
FFN_SUB_ROWS = (256, 256)
HALO = BF16_SUBLANES
WEIGHT_CAST_ROWS = 128
GLA_SUB = 16
GLA_SAFE_RANGE = 80.0
GLA_CHAINS = 4
GLA_HEADS_PER_STEP_CONTEXT = 4
GLA_HEADS_PER_STEP_LATENT = 2
NEG_BIG = -1e30


def _dot(a, b):
    return jnp.dot(a, b, preferred_element_type=F32)


def _dot_nt(a, b):
    return lax.dot_general(a, b, (((1,), (1,)), ((), ())), preferred_element_type=F32)


def _dot_tn(a, b):
    return lax.dot_general(a, b, (((0,), (0,)), ((), ())), preferred_element_type=F32)


def _layer_norm(y, g, b):
    mu = jnp.mean(y, axis=-1, keepdims=True)
    yc = y - mu
    var = jnp.mean(yc * yc, axis=-1, keepdims=True)
    return yc * lax.rsqrt(var + LN_EPS) * g + b


def _silu(x):
    return x * jax.nn.sigmoid(x)


def _resident(block_shape, index_map):
    return pl.BlockSpec(block_shape, index_map, pipeline_mode=pl.Buffered(1))


class _Layout:
    def __init__(self, batch, seq, dec_batch, dec_seq, d_model, depth):
        self.batch, self.seq, self.dec_batch, self.dec_seq = batch, seq, dec_batch, dec_seq
        self.d, self.depth = d_model, depth
        self.n_prompt = batch * seq
        self.n_tok = self.n_prompt + dec_batch * dec_seq
        self.n_steps = self.n_tok // TM
        assert self.n_prompt % TM == 0 and dec_seq % TM == 0 and TM % seq == 0
        assert seq & (seq - 1) == 0 and dec_seq & (dec_seq - 1) == 0
        self.alpha = (2.0 * depth) ** 0.25

    def mod_index(self, i, rows):
        r0 = i * rows
        return jnp.where(r0 < self.n_prompt, 0, 1 + (r0 - self.n_prompt) // self.dec_seq)

    def mod_spec(self, tile_of):
        def index(*ids):
            tile, rows = tile_of(*ids)
            return (self.mod_index(tile, rows), 0)

        return pl.BlockSpec((MOD_GROUP, 6 * self.d), index)


def _mod_part(mod_ref, k):
    d = mod_ref.shape[1] // 6
    return mod_ref[0:1, k * d:(k + 1) * d]


class _Job:
    def __init__(self, args, in_specs, out_shapes, out_specs, fn):
        self.args, self.in_specs, self.out_shapes, self.out_specs, self.fn = args, in_specs, out_shapes, out_specs, fn


def _cast_job(w, j, n_steps):
    _, n_rows, n_cols = w.shape
    rows = n_rows // n_steps
    assert rows * n_steps == n_rows and rows % BF16_SUBLANES == 0

    def fn(in_refs, out_refs):
        out_refs[0][...] = in_refs[0][...].astype(BF16)

    return _Job([w], [pl.BlockSpec((None, rows, n_cols), lambda i: (j, i, 0))],
                [jax.ShapeDtypeStruct((n_rows, n_cols), BF16)], [pl.BlockSpec((rows, n_cols), lambda i: (i, 0))], fn)


def _mod_job(cond, w_ada, b_ada, layer, n_steps):
    depth, d, n_out = w_ada.shape
    n_rows = cond.shape[0]
    cols = n_out // n_steps
    assert cols * n_steps == n_out and cols % LANES == 0

    def fn(in_refs, out_refs):
        cond_ref, w_ref, b_ref = in_refs
        out_refs[0][...] = (_dot(_silu(cond_ref[...]).astype(BF16), w_ref[...].astype(BF16))
                            + b_ref[layer:layer + 1, :])

    return _Job([cond, w_ada, b_ada],
                [_resident((n_rows, d), lambda i: (0, 0)),
                 pl.BlockSpec((None, d, cols), lambda i: (layer, 0, i)),
                 pl.BlockSpec((depth, cols), lambda i: (0, i))],
                [jax.ShapeDtypeStruct((n_rows, n_out), F32)], [pl.BlockSpec((n_rows, cols), lambda i: (0, i))], fn)


def _run(kernel_fn, *, name, grid, args, in_specs, out_shape, out_specs, scratch_shapes=(), jobs=(), aliases=None):
    n_in, n_out = len(args), len(out_shape)

    def body(*refs):
        pos = n_in
        job_in = []
        for jb in jobs:
            job_in.append(refs[pos:pos + len(jb.args)])
            pos += len(jb.args)
        main_out = refs[pos:pos + n_out]
        pos += n_out
        job_out = []
        for jb in jobs:
            job_out.append(refs[pos:pos + len(jb.out_shapes)])
            pos += len(jb.out_shapes)
        for jb, ji, jo in zip(jobs, job_in, job_out):
            jb.fn(ji, jo)
        kernel_fn(*refs[:n_in], *main_out, *refs[pos:])

    outs = pl.pallas_call(
        body,
        out_shape=tuple(out_shape) + tuple(s for jb in jobs for s in jb.out_shapes),
        grid=grid,
        in_specs=list(in_specs) + [s for jb in jobs for s in jb.in_specs],
        out_specs=tuple(out_specs) + tuple(s for jb in jobs for s in jb.out_specs),
        scratch_shapes=list(scratch_shapes),
        input_output_aliases=aliases or {},
        compiler_params=pltpu.CompilerParams(dimension_semantics=("arbitrary",) * len(grid),
                                             vmem_limit_bytes=VMEM_LIMIT),
        name=name,
    )(*args, *[a for jb in jobs for a in jb.args])
    main, rest = outs[:n_out], list(outs[n_out:])
    job_results = []
    for jb in jobs:
        job_results.append(tuple(rest[:len(jb.out_shapes)]))
        rest = rest[len(jb.out_shapes):]
    return tuple(main), job_results


def _weight_in(w):
    if isinstance(w, tuple):
        arr, j = w
        _, r, c = arr.shape
        return arr, _resident((None, r, c), lambda *_: (j, 0, 0)), pltpu.VMEM((r, c), BF16)
    r, c = w.shape
    return w, _resident((r, c), lambda *_: (0, 0)), None


def _weight_shape(w):
    return w[0].shape[1:] if isinstance(w, tuple) else w.shape


def _round_weight_once(first_step, w_ref, w_scr):
    n_rows = w_ref.shape[0]
    rows = min(n_rows, WEIGHT_CAST_ROWS)
    assert n_rows % rows == 0

    @pl.when(first_step)
    def _():
        def body(c, carry):
            sl = pl.ds(pl.multiple_of(c * rows, rows), rows)
            w_scr[sl, :] = w_ref[sl, :].astype(BF16)
            return carry

        lax.fori_loop(0, n_rows // rows, body, 0)


def _bf16_weights(first_step, w_refs, scratch):
    scratch = list(scratch)
    out = []
    for w in w_refs:
        if w.dtype == BF16:
            out.append(w)
        else:
            scr = scratch.pop(0)
            _round_weight_once(first_step, w, scr)
            out.append(scr)
    return out, scratch


def _modulation(cond, w_ada, b_ada, layer):
    _, d, n_out = w_ada.shape
    job = _mod_job(cond, w_ada, b_ada, layer, n_out // MOD_NT)
    _, (result,) = _run(lambda: None, name=f"modulation_{layer}", grid=(n_out // MOD_NT,), args=[], in_specs=[],
                        out_shape=[], out_specs=[], jobs=[job])
    return result[0]


def _conv_kernel(lay, layer, split, *refs):
    d = lay.d
    i = pl.program_id(0)
    if split:
        (xca, xpa, xna, xcb, xpb, xnb, mod_ref, lng_ref, lnb_ref, win_ref, cw_ref, wout_ref,
         o_ref, h_scr, uu_scr, *scratch) = refs
        is_context = i * TM < lay.n_prompt
        x = jnp.where(is_context, xca[...], xcb[...])
        x_prev = jnp.where(is_context, xpa[...], xpb[...])
        x_next = jnp.where(is_context, xna[...], xnb[...])
    else:
        (xc_ref, xp_ref, xn_ref, mod_ref, lng_ref, lnb_ref, win_ref, cw_ref, wout_ref,
         o_ref, h_scr, uu_scr, *scratch) = refs
        x, x_prev, x_next = xc_ref[...], xp_ref[...], xn_ref[...]
    (win, wout), _ = _bf16_weights(i == 0, (win_ref, wout_ref), scratch)
    sh, sc, ga = _mod_part(mod_ref, 0), _mod_part(mod_ref, 1), _mod_part(mod_ref, 2)
    one_sc = 1.0 + sc
    h_scr[0:HALO, :] = (x_prev * one_sc + sh).astype(BF16)
    h_scr[HALO:HALO + TM, :] = (x * one_sc + sh).astype(BF16)
    h_scr[HALO + TM:HALO + TM + HALO, :] = (x_next * one_sc + sh).astype(BF16)
    cgu = _dot(h_scr[...], win[:, d:3 * d])
    uu_scr[...] = cgu[:, 0:d] * cgu[:, d:2 * d]
    sub = TM // SUB_TILES
    for s in range(SUB_TILES):
        r0 = s * sub
        bg = _dot(h_scr[HALO + r0:HALO + r0 + sub, :], win[:, 0:d])
        row = i * TM + r0 + lax.broadcasted_iota(jnp.int32, (sub, 1), 0)
        seq_len = jnp.where(row < lay.n_prompt, lay.seq, lay.dec_seq)
        pos = jnp.bitwise_and(row, seq_len - 1)
        u_prev = jnp.where(pos != 0, uu_scr[pl.ds(HALO + r0 - 1, sub), :], 0.0)
        u_next = jnp.where(pos != seq_len - 1, uu_scr[pl.ds(HALO + r0 + 1, sub), :], 0.0)
        y = (u_prev * cw_ref[0:1, :] + uu_scr[pl.ds(HALO + r0, sub), :] * cw_ref[1:2, :]
             + u_next * cw_ref[2:3, :])
        mix = _dot((bg * y).astype(BF16), wout[...])
        o_ref[r0:r0 + sub, :] = _layer_norm(lay.alpha * x[r0:r0 + sub] + ga * mix,
                                            lng_ref[layer, 0:1, :],
                                            lnb_ref[layer, 0:1, :])


def _conv_mixer(lay, layer, j, x_parts, mod, lng, lnb, w_in, cw, w_out, jobs):
    d = lay.d
    per = TM // HALO
    x_specs, x_args = [], []
    tile0 = 0
    for part in x_parts:
        n_tiles = part.shape[0] // TM

        def center(i, t0=tile0, n=n_tiles):
            return (jnp.clip(i - t0, 0, n - 1), 0)

        def prev_halo(i, t0=tile0, n=n_tiles):
            return (jnp.clip((i - t0) * per - 1, 0, n * per - 1), 0)

        def next_halo(i, t0=tile0, n=n_tiles):
            return (jnp.clip((i - t0 + 1) * per, 0, n * per - 1), 0)

        x_specs += [pl.BlockSpec((TM, d), center), pl.BlockSpec((HALO, d), prev_halo),
                    pl.BlockSpec((HALO, d), next_halo)]
        x_args += [part, part, part]
        tile0 += n_tiles
    win_arr, win_spec, win_scr = _weight_in(w_in)
    wout_arr, wout_spec, wout_scr = _weight_in(w_out)
    (x,), job_results = _run(
        functools.partial(_conv_kernel, lay, layer, len(x_parts) == 2),
        name=f"conv_mixer_{layer}", grid=(lay.n_steps,),
        args=x_args + [mod, lng, lnb, win_arr, cw, wout_arr],
        in_specs=x_specs + [
            lay.mod_spec(lambda i: (i, TM)),
            _resident(lng.shape, lambda i: (0, 0, 0)),
            _resident(lnb.shape, lambda i: (0, 0, 0)),
            win_spec,
            _resident((None, CONV_WIDTH, d), lambda i: (j, 0, 0)),
            wout_spec,
        ],
        out_shape=[jax.ShapeDtypeStruct((lay.n_tok, d), F32)],
        out_specs=[pl.BlockSpec((TM, d), lambda i: (i, 0))],
        scratch_shapes=[pltpu.VMEM((TM + 2 * HALO, d), BF16), pltpu.VMEM((TM + 2 * HALO, d), F32)]
        + [s for s in (win_scr, wout_scr) if s is not None],
        jobs=jobs)
    return x, job_results


def _ffn_kernel(lay, layer, split_out, gla_pre, *refs):
    x_ref, mod_ref, lng_ref, lnb_ref, win_ref, wout_ref = refs[:6]
    pos = 6
    if gla_pre:
        oc_ref, ol_ref, og_ref, ng_ref, wo_ref = refs[pos:pos + 5]
        pos += 5
    n_out = 2 if split_out else 1
    out_refs = refs[pos:pos + n_out]
    scratch = refs[pos + n_out:]
    d_ff = wout_ref.shape[0]
    is_context = pl.program_id(0) * TM < lay.n_prompt
    sh, sc, ga = _mod_part(mod_ref, 3), _mod_part(mod_ref, 4), _mod_part(mod_ref, 5)
    ln_g = lng_ref[layer, 1:2, :]
    ln_b = lnb_ref[layer, 1:2, :]
    results = []
    row_starts = np.cumsum((0,) + FFN_SUB_ROWS)
    assert row_starts[-1] == TM
    for r0, r1 in zip(row_starts[:-1], row_starts[1:]):
        rows = slice(int(r0), int(r1))
        x = x_ref[rows, :]
        if gla_pre:
            z_scr, = scratch
            dv = ng_ref.shape[1]
            for h in range(GLA_HEADS):
                sl = slice(h * dv, (h + 1) * dv)
                o = jnp.where(is_context, oc_ref[rows, sl], ol_ref[rows, sl])
                o = o * lax.rsqrt(jnp.mean(o * o, axis=-1, keepdims=True) + RMS_EPS) * ng_ref[...]
                z_scr[rows, sl] = (o * _silu(og_ref[rows, sl])).astype(BF16)
            mix = _dot(z_scr[rows, :], wo_ref[...])
            x = _layer_norm(lay.alpha * x + _mod_part(mod_ref, 2) * mix,
                            lng_ref[layer, 0:1, :], lnb_ref[layer, 0:1, :])
        h = (x * (1.0 + sc) + sh).astype(BF16)
        g = _dot(h, win_ref[:, 0:d_ff])
        u = _dot(h, win_ref[:, d_ff:2 * d_ff])
        a = (_silu(g) * u).astype(BF16)
        y = _dot(a, wout_ref[...])
        results.append((rows, _layer_norm(lay.alpha * x + ga * y, ln_g, ln_b)))
    if split_out:
        @pl.when(is_context)
        def _():
            for rows, res in results:
                out_refs[0][rows, :] = res

        @pl.when(jnp.logical_not(is_context))
        def _():
            for rows, res in results:
                out_refs[1][rows, :] = res
    else:
        for rows, res in results:
            out_refs[0][rows, :] = res


def _ffn(lay, layer, x, mod, lng, lnb, w_in, w_out, *, split_out, gla_pre, jobs):
    d = lay.d
    d_ff = w_out.shape[0]
    row = lambda i: (i, 0)
    n_ctx = lay.n_prompt // TM
    n_lat = lay.n_steps - n_ctx
    ctx_row = lambda i: (jnp.minimum(i, n_ctx - 1), 0)
    lat_row = lambda i: (jnp.clip(i - n_ctx, 0, n_lat - 1), 0)
    args = [x, mod, lng, lnb, w_in, w_out]
    in_specs = [
        pl.BlockSpec((TM, d), row),
        lay.mod_spec(lambda i: (i, TM)),
        _resident(lng.shape, lambda i: (0, 0, 0)),
        _resident(lnb.shape, lambda i: (0, 0, 0)),
        _resident((d, 2 * d_ff), lambda i: (0, 0)),
        _resident((d_ff, d), lambda i: (0, 0)),
    ]
    scratch = []
    if gla_pre is not None:
        o_context, o_latent, og, norm_g, w_o = gla_pre
        hv = og.shape[1]
        args += [o_context, o_latent, og, norm_g, w_o]
        in_specs += [pl.BlockSpec((TM, hv), ctx_row), pl.BlockSpec((TM, hv), lat_row), pl.BlockSpec((TM, hv), row),
                     _resident(norm_g.shape, lambda i: (0, 0)), _resident((hv, d), lambda i: (0, 0))]
        scratch = [pltpu.VMEM((TM, hv), BF16)]
    if split_out:
        out_shape = [jax.ShapeDtypeStruct((lay.n_prompt, d), F32),
                     jax.ShapeDtypeStruct((lay.n_tok - lay.n_prompt, d), F32)]
        out_specs = [pl.BlockSpec((TM, d), ctx_row), pl.BlockSpec((TM, d), lat_row)]
    else:
        out_shape = [jax.ShapeDtypeStruct((lay.n_tok, d), F32)]
        out_specs = [pl.BlockSpec((TM, d), row)]
    return _run(functools.partial(_ffn_kernel, lay, layer, split_out, gla_pre is not None),
                name=f"ffn_{layer}", grid=(lay.n_steps,), args=args, in_specs=in_specs,
                out_shape=out_shape, out_specs=out_specs, scratch_shapes=scratch, jobs=jobs)


def _qkv_kernel(lay, n_heads, x_ref, mod_ref, w_ref, qg_ref, kg_ref, cos_ref, sin_ref,
                q_ref, k_ref, v_ref, kf_ref, vf_ref, *scratch):
    (w,), _ = _bf16_weights(pl.program_id(0) == 0, (w_ref,), scratch)
    sh, sc = _mod_part(mod_ref, 0), _mod_part(mod_ref, 1)
    lane = lax.broadcasted_iota(jnp.int32, (1, HEAD_DIM), 1)
    first_half = jnp.bitwise_and(lane, HEAD_DIM // 4) == 0
    q_gain = qg_ref[...] * (HEAD_DIM ** -0.5)
    ones = jnp.ones((HEAD_DIM, HEAD_DIM), BF16)
    k0 = n_heads * HEAD_DIM
    v0 = k0 + N_KV_HEADS * HEAD_DIM
    for s in range(SUB_TILES):
        rows = slice(s * (TM // SUB_TILES), (s + 1) * (TM // SUB_TILES))
        h = (x_ref[rows, :] * (1.0 + sc) + sh).astype(BF16)
        qkv = _dot(h, w[...])
        cos, sin = cos_ref[rows, :], sin_ref[rows, :]

        def norm_rope(t, g):
            sq = t * t
            sq_hi = sq.astype(BF16)
            sq_lo = (sq - sq_hi.astype(F32)).astype(BF16)
            mean_sq = (_dot(sq_hi, ones) + _dot(sq_lo, ones)) * (1.0 / HEAD_DIM)
            t = t * lax.rsqrt(mean_sq + RMS_EPS) * g
            partner = jnp.where(first_half,
                                pltpu.roll(t, HEAD_DIM - HEAD_DIM // 4, axis=1),
                                pltpu.roll(t, HEAD_DIM // 4, axis=1))
            return t * cos + partner * sin

        for hq in range(n_heads):
            sl = slice(hq * HEAD_DIM, (hq + 1) * HEAD_DIM)
            q_ref[rows, sl] = norm_rope(qkv[:, sl], q_gain).astype(BF16)
        for hk in range(N_KV_HEADS):
            sl = slice(hk * HEAD_DIM, (hk + 1) * HEAD_DIM)
            kh = norm_rope(qkv[:, k0 + hk * HEAD_DIM:k0 + (hk + 1) * HEAD_DIM], kg_ref[...])
            kf_ref[rows, sl] = kh
            k_ref[rows, sl] = kh.astype(BF16)
        v = qkv[:, v0:v0 + N_KV_HEADS * HEAD_DIM]
        vf_ref[rows, :] = v
        v_ref[rows, :] = v.astype(BF16)


def _attn_qkv(lay, layer, j, x, mod, w_qkv, q_gain, k_gain, cos_t, sin_t, jobs):
    d = lay.d
    n_qkv = _weight_shape(w_qkv)[1]
    kv = N_KV_HEADS * HEAD_DIM
    n_heads = (n_qkv - 2 * kv) // HEAD_DIM
    row = lambda i: (i, 0)

    def rope_row(i):
        r0 = i * TM
        return (jnp.where(r0 < lay.n_prompt, 0, 1 + ((r0 - lay.n_prompt) % lay.dec_seq) // TM), 0)

    w_arr, w_spec, w_scr = _weight_in(w_qkv)
    return _run(
        functools.partial(_qkv_kernel, lay, n_heads),
        name=f"attn_qkv_{layer}", grid=(lay.n_steps,),
        args=[x, mod, w_arr, q_gain, k_gain, cos_t, sin_t],
        in_specs=[
            pl.BlockSpec((TM, d), row),
            lay.mod_spec(lambda i: (i, TM)),
            w_spec,
            _resident((None, 1, HEAD_DIM), lambda i: (j, 0, 0)),
            _resident((None, 1, HEAD_DIM), lambda i: (j, 0, 0)),
            pl.BlockSpec((TM, HEAD_DIM), rope_row),
            pl.BlockSpec((TM, HEAD_DIM), rope_row),
        ],
        out_shape=[jax.ShapeDtypeStruct((lay.n_tok, n_heads * HEAD_DIM), BF16),
                   jax.ShapeDtypeStruct((lay.n_tok, kv), BF16),
                   jax.ShapeDtypeStruct((lay.n_tok, kv), BF16),
                   jax.ShapeDtypeStruct((lay.n_tok, kv), F32),
                   jax.ShapeDtypeStruct((lay.n_tok, kv), F32)],
        out_specs=[pl.BlockSpec((TM, n_heads * HEAD_DIM), row), pl.BlockSpec((TM, kv), row),
                   pl.BlockSpec((TM, kv), row), pl.BlockSpec((TM, kv), row), pl.BlockSpec((TM, kv), row)],
        scratch_shapes=[s for s in (w_scr,) if s is not None],
        jobs=jobs)


def _attn_kernel(lay, layer, n_heads, has_cache, seqs, *refs):
    if has_cache:
        (q_ref, k_ref, v_ref, kc_ref, vc_ref, x_ref, mod_ref, lng_ref, lnb_ref, wo_ref, o_ref, att_scr,
         *scratch) = refs
    else:
        (q_ref, k_ref, v_ref, x_ref, mod_ref, lng_ref, lnb_ref, wo_ref, o_ref, att_scr, *scratch) = refs
    (wo,), _ = _bf16_weights((pl.program_id(0) == 0) & (pl.program_id(1) == 0), (wo_ref,), scratch)
    group = n_heads // N_KV_HEADS
    q_rows = q_ref.shape[0] // seqs
    k_rows = k_ref.shape[0] // seqs
    for sq in range(seqs):
        rq = slice(sq * q_rows, (sq + 1) * q_rows)
        rk = slice(sq * k_rows, (sq + 1) * k_rows)
        for hk in range(N_KV_HEADS):
            ksl = slice(hk * HEAD_DIM, (hk + 1) * HEAD_DIM)
            v_ext = jnp.concatenate([v_ref[rk, ksl], jnp.ones((k_rows, HEAD_DIM), BF16)], axis=1)
            if has_cache:
                vc_ext = jnp.concatenate([vc_ref[:, ksl], jnp.ones((vc_ref.shape[0], HEAD_DIM), BF16)], axis=1)
            for g in range(group):
                hq = hk * group + g
                qsl = slice(hq * HEAD_DIM, (hq + 1) * HEAD_DIM)
                qh = q_ref[rq, qsl]
                s = _dot_nt(qh, k_ref[rk, ksl])
                m = jnp.max(s, axis=-1, keepdims=True)
                if has_cache:
                    s_c = _dot_nt(qh, kc_ref[:, ksl])
                    m = jnp.maximum(m, jnp.max(s_c, axis=-1, keepdims=True))
                o = _dot(jnp.exp(s - m).astype(BF16), v_ext)
                if has_cache:
                    o = o + _dot(jnp.exp(s_c - m).astype(BF16), vc_ext)
                att_scr[rq, qsl] = (o[:, 0:HEAD_DIM] / o[:, HEAD_DIM:2 * HEAD_DIM]).astype(BF16)
    ga = _mod_part(mod_ref, 2)
    mix = _dot(att_scr[...], wo[...])
    o_ref[...] = _layer_norm(lay.alpha * x_ref[...] + ga * mix,
                             lng_ref[layer, 0:1, :], lnb_ref[layer, 0:1, :])


def _attention(lay, layer, q, k, v, cache, x, mod, lng, lnb, w_o, *, row0, n_seq, seq, tq):
    d = lay.d
    n_heads = q.shape[1] // HEAD_DIM
    kv = N_KV_HEADS * HEAD_DIM
    has_cache = cache is not None
    seqs = max(tq // seq, 1)
    assert not (has_cache and seqs > 1) and n_seq % seqs == 0
    n_seq, seq = n_seq // seqs, seq * seqs
    q_blocks = seq // tq
    tile = lambda b, i: (row0 // tq + b * q_blocks + i, 0)
    seq_blk = lambda b, i: (row0 // seq + b, 0)
    in_specs = [pl.BlockSpec((tq, n_heads * HEAD_DIM), tile),
                pl.BlockSpec((seq, kv), seq_blk), pl.BlockSpec((seq, kv), seq_blk)]
    args = [q, k, v]
    if has_cache:
        kc, vc = cache
        past = kc.shape[0] // n_seq
        in_specs += [pl.BlockSpec((past, kv), lambda b, i: (b, 0))] * 2
        args += [kc, vc]
    x_index = len(args)
    wo_arr, wo_spec, wo_scr = _weight_in(w_o)
    in_specs += [
        pl.BlockSpec((tq, d), tile),
        lay.mod_spec(lambda b, i: (row0 // tq + b * q_blocks + i, tq)),
        _resident(lng.shape, lambda b, i: (0, 0, 0)),
        _resident(lnb.shape, lambda b, i: (0, 0, 0)),
        wo_spec,
    ]
    args += [x, mod, lng, lnb, wo_arr]
    (x,), _ = _run(
        functools.partial(_attn_kernel, lay, layer, n_heads, has_cache, seqs),
        name=f"attention_{layer}_{'latent' if has_cache else 'context'}", grid=(n_seq, q_blocks),
        args=args, in_specs=in_specs,
        out_shape=[jax.ShapeDtypeStruct((lay.n_tok, d), F32)], out_specs=[pl.BlockSpec((tq, d), tile)],
        scratch_shapes=[pltpu.VMEM((tq, n_heads * HEAD_DIM), BF16)] + [s for s in (wo_scr,) if s is not None],
        aliases={x_index: 0})
    return x


def _split3(x):
    hi = x.astype(BF16)
    r = x - hi.astype(F32)
    mid = r.astype(BF16)
    lo = (r - mid.astype(F32)).astype(BF16)
    return hi, mid, lo


def _gla_proj_kernel(lay, hk, hv, x_ref, mod_ref, win_ref, w1_ref, w2_ref, bg_ref,
                     q_ref, k_ref, v_ref, og_ref, bf_ref, bb_ref, *scratch):
    (win,), _ = _bf16_weights(pl.program_id(0) == 0, (win_ref,), scratch)
    sh, sc = _mod_part(mod_ref, 0), _mod_part(mod_ref, 1)
    dk = hk // GLA_HEADS
    r = lax.broadcasted_iota(jnp.int32, (GLA_CHUNK, GLA_CHUNK), 0)
    c = lax.broadcasted_iota(jnp.int32, (GLA_CHUNK, GLA_CHUNK), 1)
    lower = jnp.concatenate([jnp.where(c <= r, 1.0, 0.0).astype(BF16)] * 3, axis=1)
    upper = jnp.concatenate([jnp.where(c >= r, 1.0, 0.0).astype(BF16)] * 3, axis=1)
    h = (x_ref[...] * (1.0 + sc) + sh).astype(BF16)
    proj = _dot(h, win[...])
    q_ref[...] = proj[:, 0:hk] * (dk ** -0.5)
    k_ref[...] = proj[:, hk:2 * hk]
    v_ref[...] = proj[:, 2 * hk:2 * hk + hv].astype(BF16)
    og_ref[...] = proj[:, 2 * hk + hv:2 * hk + 2 * hv]
    z = _dot(_dot(h, w1_ref[...]).astype(BF16), w2_ref[...]) + bg_ref[...]
    log_gate = (jnp.minimum(z, 0.0) - jnp.log1p(jnp.exp(-jnp.abs(z)))) * (1.0 / GLA_TAU)
    for ch in range(TM // GLA_CHUNK):
        rows = slice(ch * GLA_CHUNK, (ch + 1) * GLA_CHUNK)
        bf_ref[rows, :] = _dot(lower, jnp.concatenate(_split3(log_gate[rows, 0:hk]), axis=0))
        bb_ref[rows, :] = _dot(upper, jnp.concatenate(_split3(log_gate[rows, hk:2 * hk]), axis=0))


def _gla_proj(lay, layer, j, x, mod, w_in, w1, w2, b_gate, jobs):
    d = lay.d
    n_in = _weight_shape(w_in)[1]
    hk = w2.shape[2] // 2
    hv = (n_in - 2 * hk) // 2
    rank2 = w1.shape[2]
    row = lambda i: (i, 0)
    win_arr, win_spec, win_scr = _weight_in(w_in)
    return _run(
        functools.partial(_gla_proj_kernel, lay, hk, hv),
        name=f"gla_proj_{layer}", grid=(lay.n_steps,),
        args=[x, mod, win_arr, w1, w2, b_gate],
        in_specs=[
            pl.BlockSpec((TM, d), row),
            lay.mod_spec(lambda i: (i, TM)),
            win_spec,
            _resident((None, d, rank2), lambda i: (j, 0, 0)),
            _resident((None, rank2, 2 * hk), lambda i: (j, 0, 0)),
            _resident((None, 1, 2 * hk), lambda i: (j, 0, 0)),
        ],
        out_shape=[jax.ShapeDtypeStruct((lay.n_tok, hk), F32),
                   jax.ShapeDtypeStruct((lay.n_tok, hk), F32),
                   jax.ShapeDtypeStruct((lay.n_tok, hv), BF16),
                   jax.ShapeDtypeStruct((lay.n_tok, hv), F32),
                   jax.ShapeDtypeStruct((lay.n_tok, hk), F32),
                   jax.ShapeDtypeStruct((lay.n_tok, hk), F32)],
        out_specs=[pl.BlockSpec((TM, hk), row), pl.BlockSpec((TM, hk), row), pl.BlockSpec((TM, hv), row),
                   pl.BlockSpec((TM, hv), row), pl.BlockSpec((TM, hk), row), pl.BlockSpec((TM, hk), row)],
        scratch_shapes=[s for s in (win_scr,) if s is not None],
        jobs=jobs)


def _block_diag(blocks):
    n = len(blocks)
    zero = jnp.zeros_like(blocks[0])
    return jnp.concatenate(
        [jnp.concatenate([blocks[i] if j == i else zero for j in range(n)], axis=1) for i in range(n)], axis=0)


def _gla_chain_mask(dtype):
    n = GLA_CHAINS * GLA_CHUNK
    r = lax.broadcasted_iota(jnp.int32, (n, n), 0)
    c = lax.broadcasted_iota(jnp.int32, (n, n), 1)
    shift = GLA_CHUNK.bit_length() - 1
    assert 1 << shift == GLA_CHUNK
    r_chain, r_t = jnp.right_shift(r, shift), jnp.bitwise_and(r, GLA_CHUNK - 1)
    c_chain, c_t = jnp.right_shift(c, shift), jnp.bitwise_and(c, GLA_CHUNK - 1)
    direction = 1 - 2 * jnp.bitwise_and(r_chain, 1)
    ordered = jnp.where((c_t - r_t) * direction <= 0, 1.0, 0.0)
    return jnp.where(r_chain == c_chain, ordered, 0.0).astype(dtype)


def _gla_fast_step(refs, st_scr, mask_scr, g, t, accumulate, n_chunks, dk, dv):
    q_ref, k_ref, v_ref, bf_ref, bb_ref, o_ref = refs
    mid = GLA_CHUNK // 2
    slots = []
    for u in range(2):
        qs_l, ks_l, v_l, qi_l, kh_l, dec_l, dst_l, old_l = [], [], [], [], [], [], [], []
        for j in range(2):
            h = 2 * g + j
            kcols = slice(h * dk, (h + 1) * dk)
            vcols = slice(h * dv, (h + 1) * dv)
            for forward, b_ref in ((True, bf_ref), (False, bb_ref)):
                c = (2 * t + u) if forward else (n_chunks - 1 - 2 * t - u)
                rows = pl.ds(pl.multiple_of(c * GLA_CHUNK, GLA_CHUNK), GLA_CHUNK)
                q, k, b, v = q_ref[rows, kcols], k_ref[rows, kcols], b_ref[rows, kcols], v_ref[rows, vcols]
                rho = b[mid:mid + 1]
                q_s = q * jnp.exp(b - rho)
                k_s = k * jnp.exp(rho - b)
                b_exit = b[GLA_CHUNK - 1:GLA_CHUNK] if forward else b[0:1]
                qs_l.append(q_s.astype(BF16))
                ks_l.append(k_s.astype(BF16))
                v_l.append(v)
                qi_l.append((q_s * jnp.exp(rho)).astype(BF16))
                kh_l.append((k_s * jnp.exp(b_exit - rho)).astype(BF16))
                dec_l.append(jnp.exp(b_exit))
                dst_l.append((rows, vcols))
                old_l.append(o_ref[rows, vcols] if accumulate else None)
        v_all = jnp.concatenate(v_l, axis=0)
        a = _dot_nt(jnp.concatenate(qs_l, axis=0), jnp.concatenate(ks_l, axis=0)).astype(BF16) * mask_scr[...]
        slots.append((_dot(a, v_all), _block_diag(qi_l), _block_diag(kh_l), v_all,
                      jnp.concatenate(dec_l, axis=1), dst_l, old_l))
    st = st_scr[g]
    stores = []
    for o_intra, q_bd, k_bd, v_all, decay, dst_l, old_l in slots:
        o_all = o_intra + _dot_nt(q_bd, st.astype(BF16))
        st = st * decay + _dot_tn(v_all, k_bd)
        for ci, ((rows, vcols), old) in enumerate(zip(dst_l, old_l)):
            o = o_all[ci * GLA_CHUNK:(ci + 1) * GLA_CHUNK]
            stores.append((rows, vcols, o if old is None else o + old))
    st_scr[g] = st
    for rows, vcols, o in stores:
        o_ref[rows, vcols] = o


def _gla_chunk_safe(q, k, v, b, st, forward):
    n_sub = GLA_CHUNK // GLA_SUB
    o_rows = []
    qi = (q * jnp.exp(b)).astype(BF16)
    o_inter = _dot_nt(qi, st.astype(BF16))
    t_idx = lax.broadcasted_iota(jnp.int32, (GLA_SUB, 1), 0)
    for blk in range(n_sub):
        rows = slice(blk * GLA_SUB, (blk + 1) * GLA_SUB)
        qb, kb, bb_, vb = q[rows], k[rows], b[rows], v[rows].astype(F32)
        o_blk = o_inter[rows]
        if forward and blk > 0:
            others = slice(0, blk * GLA_SUB)
            rho = b[blk * GLA_SUB - 1:blk * GLA_SUB]
        elif (not forward) and blk < n_sub - 1:
            others = slice((blk + 1) * GLA_SUB, GLA_CHUNK)
            rho = b[(blk + 1) * GLA_SUB:(blk + 1) * GLA_SUB + 1]
        else:
            others = None
        if others is not None:
            q_s = (qb * jnp.exp(bb_ - rho)).astype(BF16)
            k_s = (k[others] * jnp.exp(rho - b[others])).astype(BF16)
            a = _dot_nt(q_s, k_s)
            o_blk = o_blk + _dot(a.astype(BF16), v[others])
        for s in range(GLA_SUB):
            keep = (t_idx >= s) if forward else (t_idx <= s)
            decay = jnp.exp(jnp.where(keep, bb_ - bb_[s:s + 1], NEG_BIG))
            a_col = jnp.sum(qb * kb[s:s + 1] * decay, axis=-1, keepdims=True)
            o_blk = o_blk + a_col * vb[s:s + 1]
        o_rows.append(o_blk)
    b_exit = b[GLA_CHUNK - 1:GLA_CHUNK] if forward else b[0:1]
    k_hat = (k * jnp.exp(b_exit - b)).astype(BF16)
    st_new = st * jnp.exp(b_exit) + _dot_tn(v, k_hat)
    return jnp.concatenate(o_rows, axis=0), st_new


def _gla_scan_kernel(n_chunks, hps, dk, dv, has_init, *refs):
    n_groups = hps // 2
    if has_init:
        (q_ref, k_ref, v_ref, bf_ref, bb_ref, s0f_ref, s0b_ref, o_ref, st_scr, mask_scr) = refs
    else:
        (q_ref, k_ref, v_ref, bf_ref, bb_ref, o_ref, sf_ref, sb_ref, st_scr, mask_scr) = refs
    data_refs = (q_ref, k_ref, v_ref, bf_ref, bb_ref, o_ref)

    def chain_cols(h, forward):
        ci = 2 * (h % 2) + (0 if forward else 1)
        return h // 2, slice(ci * dk, (ci + 1) * dk)

    for h in range(hps):
        for forward in (True, False):
            g, cols = chain_cols(h, forward)
            if has_init:
                st_scr[g, :, cols] = (s0f_ref if forward else s0b_ref)[h].T
            else:
                st_scr[g, :, cols] = jnp.zeros((dv, dk), F32)
    mask_scr[...] = _gla_chain_mask(BF16)

    def safe_step(i, accumulate):
        pending = []
        for h in range(hps):
            kcols = slice(h * dk, (h + 1) * dk)
            vcols = slice(h * dv, (h + 1) * dv)
            for forward, b_ref in ((True, bf_ref), (False, bb_ref)):
                c = i if forward else n_chunks - 1 - i
                rows = pl.ds(pl.multiple_of(c * GLA_CHUNK, GLA_CHUNK), GLA_CHUNK)
                g, cols = chain_cols(h, forward)
                o, st_new = _gla_chunk_safe(q_ref[rows, kcols], k_ref[rows, kcols], v_ref[rows, vcols],
                                            b_ref[rows, kcols], st_scr[g, :, cols], forward)
                if accumulate:
                    o = o + o_ref[rows, vcols]
                pending.append((rows, vcols, g, cols, o, st_new))
        for rows, vcols, g, cols, o, st_new in pending:
            st_scr[g, :, cols] = st_new
            o_ref[rows, vcols] = o

    def fast_step(t, accumulate):
        for g in range(n_groups):
            _gla_fast_step(data_refs, st_scr, mask_scr, g, t, accumulate, n_chunks, dk, dv)

    def run(step, n_steps):
        def first(i, carry):
            step(i, False)
            return carry

        def second(i, carry):
            step(i, True)
            return carry

        lax.fori_loop(0, n_steps // 2, first, 0)
        lax.fori_loop(n_steps // 2, n_steps, second, 0)

    span = jnp.zeros((1, hps * dk), F32)
    for c in range(n_chunks):
        top = slice(c * GLA_CHUNK, c * GLA_CHUNK + 1)
        bottom = slice((c + 1) * GLA_CHUNK - 1, (c + 1) * GLA_CHUNK)
        span = jnp.maximum(span, jnp.maximum(bf_ref[top, :] - bf_ref[bottom, :],
                                             bb_ref[bottom, :] - bb_ref[top, :]))
    bounded = jnp.max(span) < GLA_SAFE_RANGE

    @pl.when(bounded)
    def _():
        run(fast_step, n_chunks // 2)

    @pl.when(jnp.logical_not(bounded))
    def _():
        run(safe_step, n_chunks)

    if not has_init:
        for h in range(hps):
            for forward in (True, False):
                g, cols = chain_cols(h, forward)
                (sf_ref if forward else sb_ref)[h] = st_scr[g, :, cols].T


def _gla_scan(lay, j, q, k, v, bf, bb, init, *, row0, n_seq, seq, hps):
    hk, hv = q.shape[1], v.shape[1]
    dk, dv = hk // GLA_HEADS, hv // GLA_HEADS
    n_chunks = seq // GLA_CHUNK
    groups = GLA_HEADS // hps
    assert n_chunks % 4 == 0 and GLA_HEADS % hps == 0 and hps % 2 == 0
    has_init = init is not None
    blk = lambda b, g: (row0 // seq + b, g)
    kspec = pl.BlockSpec((seq, hps * dk), blk)
    vspec = pl.BlockSpec((seq, hps * dv), blk)
    in_specs = [kspec, kspec, vspec, kspec, kspec]
    args = [q, k, v, bf, bb]
    out_shape = [jax.ShapeDtypeStruct((n_seq * seq, hv), F32)]
    out_specs = [pl.BlockSpec((seq, hps * dv), lambda b, g: (b, g))]
    if has_init:
        s0f, s0b = init
        st_spec = pl.BlockSpec((None, hps, dk, dv), lambda b, g: (b, j * groups + g, 0, 0))
        in_specs += [st_spec, st_spec]
        args += [s0f, s0b]
    else:
        st_shape = jax.ShapeDtypeStruct((n_seq, GLA_HEADS, dk, dv), F32)
        st_spec = pl.BlockSpec((None, hps, dk, dv), lambda b, g: (b, g, 0, 0))
        out_shape += [st_shape, st_shape]
        out_specs += [st_spec, st_spec]
    outs, _ = _run(
        functools.partial(_gla_scan_kernel, n_chunks, hps, dk, dv, has_init),
        name=f"gla_scan_{'latent' if has_init else 'context'}", grid=(n_seq, groups),
        args=args, in_specs=in_specs, out_shape=out_shape, out_specs=out_specs,
        scratch_shapes=[pltpu.VMEM((hps // 2, dv, GLA_CHAINS * dk), F32),
                        pltpu.VMEM((GLA_CHAINS * GLA_CHUNK, GLA_CHAINS * GLA_CHUNK), BF16)])
    return outs


def _rope_tables(lay):
    n_freq = HEAD_DIM // 4
    pos = np.arange(lay.dec_seq)
    freqs = (np.float32(ROPE_THETA) ** (-np.arange(n_freq, dtype=np.float32) / np.float32(n_freq))).astype(np.float32)
    ang_r = (pos // GRID_W).astype(np.float32)[:, None] * freqs
    ang_c = (pos % GRID_W).astype(np.float32)[:, None] * freqs
    cos = np.concatenate([np.cos(ang_r)] * 2 + [np.cos(ang_c)] * 2, axis=-1)
    sin = np.concatenate([-np.sin(ang_r), np.sin(ang_r), -np.sin(ang_c), np.sin(ang_c)], axis=-1)
    cos = np.concatenate([np.ones((TM, HEAD_DIM), np.float32), cos], axis=0).astype(np.float32)
    sin = np.concatenate([np.zeros((TM, HEAD_DIM), np.float32), sin], axis=0).astype(np.float32)
    return jnp.asarray(cos), jnp.asarray(sin)


def kernel(x_prompt, x_sample, c, cache_k, cache_v, state_gla_fwd, state_gla_bwd, c_ctx, w_ada, b_ada, ln_g, ln_b, conv_w_in, conv_w, conv_w_out, attn_w_qkv, attn_q_norm, attn_k_norm, attn_w_o, gla_w_in, gla_w_gate1, gla_w_gate2, gla_b_gate, gla_norm, gla_w_o, ffn_w_in, ffn_w_out):
    batch, seq, d = x_prompt.shape
    dec_batch, dec_seq, _ = x_sample.shape
    depth = w_ada.shape[0]
    lay = _Layout(batch, seq, dec_batch, dec_seq, d, depth)
    n_steps = lay.n_steps
    kv = N_KV_HEADS * HEAD_DIM
    past = cache_k.shape[2]

    cond = jnp.repeat(jnp.concatenate([c_ctx[None, :], c], axis=0), MOD_GROUP, axis=0)
    lng, lnb = ln_g, ln_b
    g1 = jnp.concatenate([gla_w_gate1[:, 0], gla_w_gate1[:, 1]], axis=-1).astype(BF16)
    zeros = jnp.zeros_like(gla_w_gate2[:, 0])
    g2 = jnp.concatenate([jnp.concatenate([gla_w_gate2[:, 0], zeros], axis=-1),
                          jnp.concatenate([zeros, gla_w_gate2[:, 1]], axis=-1)], axis=1).astype(BF16)
    gb = jnp.concatenate([gla_b_gate[:, 0], gla_b_gate[:, 1]], axis=-1)[:, None, :]
    cos_t, sin_t = _rope_tables(lay)

    def mixer_weights(i):
        kind, j = i % N_MIXERS, i // N_MIXERS
        return [[(conv_w_in, j), (conv_w_out, j)], [(attn_w_qkv, j), (attn_w_o, j)], [(gla_w_in, j), (gla_w_o, j)]][kind]

    x = (x_prompt.reshape(lay.n_prompt, d), x_sample.reshape(dec_batch * dec_seq, d))
    mod = _modulation(cond, w_ada, b_ada, 0)
    mix_w = mixer_weights(0)
    new_k, new_v, new_sf, new_sb = [], [], [], []
    y_prompt = y_sample = None
    for i in range(depth):
        kind, j = i % N_MIXERS, i // N_MIXERS
        last = i == depth - 1
        ffn_jobs = [_cast_job(ffn_w_in, i, n_steps), _cast_job(ffn_w_out, i, n_steps)]
        gla_pre = None
        if kind == 0:
            x, jr = _conv_mixer(lay, i, j, x if i == 0 else (x,), mod, lng, lnb, mix_w[0], conv_w, mix_w[1], ffn_jobs)
        elif kind == 1:
            (q, k, v, kf, vf), jr = _attn_qkv(lay, i, j, x, mod, mix_w[0], attn_q_norm[:, None, :],
                                              attn_k_norm[:, None, :], cos_t, sin_t, ffn_jobs)
            new_k.append(kf[:lay.n_prompt].reshape(batch, seq, N_KV_HEADS, HEAD_DIM))
            new_v.append(vf[:lay.n_prompt].reshape(batch, seq, N_KV_HEADS, HEAD_DIM))
            kc = cache_k[:, j].reshape(dec_batch * past, kv).astype(BF16)
            vc = cache_v[:, j].reshape(dec_batch * past, kv).astype(BF16)
            x = _attention(lay, i, q, k, v, None, x, mod, lng, lnb, mix_w[1],
                           row0=0, n_seq=batch, seq=seq, tq=TM)
            x = _attention(lay, i, q, k, v, (kc, vc), x, mod, lng, lnb, mix_w[1],
                           row0=lay.n_prompt, n_seq=dec_batch, seq=dec_seq, tq=TM)
        else:
            (q, k, v, og, bf, bb), jr = _gla_proj(lay, i, j, x, mod, mix_w[0], g1, g2, gb, ffn_jobs)
            s0f = state_gla_fwd.reshape(dec_batch, -1, *state_gla_fwd.shape[3:])
            s0b = state_gla_bwd.reshape(dec_batch, -1, *state_gla_bwd.shape[3:])
            o_ctx, sf, sb = _gla_scan(lay, j, q, k, v, bf, bb, None, row0=0, n_seq=batch, seq=seq,
                                      hps=GLA_HEADS_PER_STEP_CONTEXT)
            o_lat, = _gla_scan(lay, j, q, k, v, bf, bb, (s0f, s0b), row0=lay.n_prompt, n_seq=dec_batch,
                               seq=dec_seq, hps=GLA_HEADS_PER_STEP_LATENT)
            new_sf.append(sf)
            new_sb.append(sb)
            w_o = mix_w[1]
            if isinstance(w_o, tuple):
                w_o = w_o[0][w_o[1]].astype(BF16)
            gla_pre = (o_ctx, o_lat, og, gla_norm[j][None, :], w_o)
        ffn_w = (jr[0][0], jr[1][0])
        next_jobs = []
        if not last:
            next_jobs = [_mod_job(cond, w_ada, b_ada, i + 1, n_steps)]
            next_jobs += [_cast_job(arr, jj, n_steps) for arr, jj in mixer_weights(i + 1)]
        outs, jr = _ffn(lay, i, x, mod, lng, lnb, ffn_w[0], ffn_w[1], split_out=last, gla_pre=gla_pre,
                        jobs=next_jobs)
        if last:
            y_prompt, y_sample = outs
        else:
            x, = outs
            mod = jr[0][0]
            mix_w = [jr[1][0], jr[2][0]]

    def stack_layers(parts):
        return parts[0][:, None] if len(parts) == 1 else jnp.stack(parts, axis=1)

    y_prompt = y_prompt.reshape(batch, seq, d)
    y_sample = y_sample.reshape(dec_batch, dec_seq, d)
    return (y_prompt, y_sample, stack_layers(new_k), stack_layers(new_v),
            stack_layers(new_sf), stack_layers(new_sb))
```

```python
import functools

import jax
import jax.numpy as jnp
import numpy as np
from jax import lax
from jax.experimental import pallas as pl
from jax.experimental.pallas import tpu as pltpu

F32 = jnp.float32
BF16 = jnp.bfloat16

N_MIXERS = 3
CONV_WIDTH = 3
HEAD_DIM = 128
N_KV_HEADS = 2
GRID_W = 64
ROPE_THETA = 10000.0
GLA_HEADS = 4
GLA_TAU = 16.0
GLA_CHUNK = 64
LN_EPS = 1e-5
RMS_EPS = 1e-6

LANES = 128
BF16_SUBLANES = 16
VMEM_LIMIT = 58 * 1024 * 1024

MOD_GROUP = 8
MOD_NT = 1536
TM = 512
SUB_TILES = 2
ATTN_LATENT_TQ = 1024
HALO = BF16_SUBLANES
WEIGHT_CAST_ROWS = 128
GLA_SUB = 16
GLA_SAFE_RANGE = 80.0
GLA_CHAINS = 4
GLA_HEADS_PER_STEP_CONTEXT = 4
GLA_HEADS_PER_STEP_LATENT = 2
NEG_BIG = -1e30


def _dot(a, b):
    return jnp.dot(a, b, preferred_element_type=F32)


def _dot_nt(a, b):
    return lax.dot_general(a, b, (((1,), (1,)), ((), ())), preferred_element_type=F32)


def _dot_tn(a, b):
    return lax.dot_general(a, b, (((0,), (0,)), ((), ())), preferred_element_type=F32)


def _layer_norm(y, g, b):
    mu = jnp.mean(y, axis=-1, keepdims=True)
    yc = y - mu
    var = jnp.mean(yc * yc, axis=-1, keepdims=True)
    return yc * lax.rsqrt(var + LN_EPS) * g + b


def _silu(x):
    return x * jax.nn.sigmoid(x)


def _resident(block_shape, index_map):
    return pl.BlockSpec(block_shape, index_map, pipeline_mode=pl.Buffered(1))


class _Layout:
    def __init__(self, batch, seq, dec_batch, dec_seq, d_model, depth):
        self.batch, self.seq, self.dec_batch, self.dec_seq = batch, seq, dec_batch, dec_seq
        self.d, self.depth = d_model, depth
        self.n_prompt = batch * seq
        self.n_tok = self.n_prompt + dec_batch * dec_seq
        self.n_steps = self.n_tok // TM
        assert self.n_prompt % TM == 0 and dec_seq % TM == 0 and TM % seq == 0
        assert seq & (seq - 1) == 0 and dec_seq & (dec_seq - 1) == 0
        self.alpha = (2.0 * depth) ** 0.25

    def mod_index(self, i, rows):
        r0 = i * rows
        return jnp.where(r0 < self.n_prompt, 0, 1 + (r0 - self.n_prompt) // self.dec_seq)

    def mod_spec(self, tile_of):
        def index(*ids):
            tile, rows = tile_of(*ids)
            return (self.mod_index(tile, rows), 0)

        return pl.BlockSpec((MOD_GROUP, 6 * self.d), index)


def _mod_part(mod_ref, k):
    d = mod_ref.shape[1] // 6
    return mod_ref[0:1, k * d:(k + 1) * d]


class _Job:
    def __init__(self, args, in_specs, out_shapes, out_specs, fn):
        self.args, self.in_specs, self.out_shapes, self.out_specs, self.fn = args, in_specs, out_shapes, out_specs, fn


def _cast_job(w, j, n_steps):
    _, n_rows, n_cols = w.shape
    rows = n_rows // n_steps
    assert rows * n_steps == n_rows and rows % BF16_SUBLANES == 0

    def fn(in_refs, out_refs):
        out_refs[0][...] = in_refs[0][...].astype(BF16)

    return _Job([w], [pl.BlockSpec((None, rows, n_cols), lambda i: (j, i, 0))],
                [jax.ShapeDtypeStruct((n_rows, n_cols), BF16)], [pl.BlockSpec((rows, n_cols), lambda i: (i, 0))], fn)


def _mod_job(cond, w_ada, b_ada, layer, n_steps):
    depth, d, n_out = w_ada.shape
    n_rows = cond.shape[0]
    cols = n_out // n_steps
    assert cols * n_steps == n_out and cols % LANES == 0

    def fn(in_refs, out_refs):
        cond_ref, w_ref, b_ref = in_refs
        out_refs[0][...] = (_dot(_silu(cond_ref[...]).astype(BF16), w_ref[...].astype(BF16))
                            + b_ref[layer:layer + 1, :])

    return _Job([cond, w_ada, b_ada],
                [_resident((n_rows, d), lambda i: (0, 0)),
                 pl.BlockSpec((None, d, cols), lambda i: (layer, 0, i)),
                 pl.BlockSpec((depth, cols), lambda i: (0, i))],
                [jax.ShapeDtypeStruct((n_rows, n_out), F32)], [pl.BlockSpec((n_rows, cols), lambda i: (0, i))], fn)


def _run(kernel_fn, *, name, grid, args, in_specs, out_shape, out_specs, scratch_shapes=(), jobs=(), aliases=None):
    n_in, n_out = len(args), len(out_shape)

    def body(*refs):
        pos = n_in
        job_in = []
        for jb in jobs:
            job_in.append(refs[pos:pos + len(jb.args)])
            pos += len(jb.args)
        main_out = refs[pos:pos + n_out]
        pos += n_out
        job_out = []
        for jb in jobs:
            job_out.append(refs[pos:pos + len(jb.out_shapes)])
            pos += len(jb.out_shapes)
        for jb, ji, jo in zip(jobs, job_in, job_out):
            jb.fn(ji, jo)
        kernel_fn(*refs[:n_in], *main_out, *refs[pos:])

    outs = pl.pallas_call(
        body,
        out_shape=tuple(out_shape) + tuple(s for jb in jobs for s in jb.out_shapes),
        grid=grid,
        in_specs=list(in_specs) + [s for jb in jobs for s in jb.in_specs],
        out_specs=tuple(out_specs) + tuple(s for jb in jobs for s in jb.out_specs),
        scratch_shapes=list(scratch_shapes),
        input_output_aliases=aliases or {},
        compiler_params=pltpu.CompilerParams(dimension_semantics=("arbitrary",) * len(grid),
                                             vmem_limit_bytes=VMEM_LIMIT),
        name=name,
    )(*args, *[a for jb in jobs for a in jb.args])
    main, rest = outs[:n_out], list(outs[n_out:])
    job_results = []
    for jb in jobs:
        job_results.append(tuple(rest[:len(jb.out_shapes)]))
        rest = rest[len(jb.out_shapes):]
    return tuple(main), job_results


def _weight_in(w):
    if isinstance(w, tuple):
        arr, j = w
        _, r, c = arr.shape
        return arr, _resident((None, r, c), lambda *_: (j, 0, 0)), pltpu.VMEM((r, c), BF16)
    r, c = w.shape
    return w, _resident((r, c), lambda *_: (0, 0)), None


def _weight_shape(w):
    return w[0].shape[1:] if isinstance(w, tuple) else w.shape


def _round_weight_once(first_step, w_ref, w_scr):
    n_rows = w_ref.shape[0]
    rows = min(n_rows, WEIGHT_CAST_ROWS)
    assert n_rows % rows == 0

    @pl.when(first_step)
    def _():
        def body(c, carry):
            sl = pl.ds(pl.multiple_of(c * rows, rows), rows)
            w_scr[sl, :] = w_ref[sl, :].astype(BF16)
            return carry

        lax.fori_loop(0, n_rows // rows, body, 0)


def _bf16_weights(first_step, w_refs, scratch):
    scratch = list(scratch)
    out = []
    for w in w_refs:
        if w.dtype == BF16:
            out.append(w)
        else:
            scr = scratch.pop(0)
            _round_weight_once(first_step, w, scr)
            out.append(scr)
    return out, scratch


def _modulation(cond, w_ada, b_ada, layer):
    _, d, n_out = w_ada.shape
    job = _mod_job(cond, w_ada, b_ada, layer, n_out // MOD_NT)
    _, (result,) = _run(lambda: None, name=f"modulation_{layer}", grid=(n_out // MOD_NT,), args=[], in_specs=[],
                        out_shape=[], out_specs=[], jobs=[job])
    return result[0]


def _conv_kernel(lay, layer, split, *refs):
    d = lay.d
    i = pl.program_id(0)
    if split:
        (xca, xpa, xna, xcb, xpb, xnb, mod_ref, lng_ref, lnb_ref, win_ref, cw_ref, wout_ref,
         o_ref, h_scr, uu_scr, *scratch) = refs
        is_context = i * TM < lay.n_prompt
        x = jnp.where(is_context, xca[...], xcb[...])
        x_prev = jnp.where(is_context, xpa[...], xpb[...])
        x_next = jnp.where(is_context, xna[...], xnb[...])
    else:
        (xc_ref, xp_ref, xn_ref, mod_ref, lng_ref, lnb_ref, win_ref, cw_ref, wout_ref,
         o_ref, h_scr, uu_scr, *scratch) = refs
        x, x_prev, x_next = xc_ref[...], xp_ref[...], xn_ref[...]
    (win, wout), _ = _bf16_weights(i == 0, (win_ref, wout_ref), scratch)
    sh, sc, ga = _mod_part(mod_ref, 0), _mod_part(mod_ref, 1), _mod_part(mod_ref, 2)
    one_sc = 1.0 + sc
    h_scr[0:HALO, :] = (x_prev * one_sc + sh).astype(BF16)
    h_scr[HALO:HALO + TM, :] = (x * one_sc + sh).astype(BF16)
    h_scr[HALO + TM:HALO + TM + HALO, :] = (x_next * one_sc + sh).astype(BF16)
    cgu = _dot(h_scr[...], win[:, d:3 * d])
    uu_scr[...] = cgu[:, 0:d] * cgu[:, d:2 * d]
    sub = TM // SUB_TILES
    for s in range(SUB_TILES):
        r0 = s * sub
        bg = _dot(h_scr[HALO + r0:HALO + r0 + sub, :], win[:, 0:d])
        row = i * TM + r0 + lax.broadcasted_iota(jnp.int32, (sub, 1), 0)
        seq_len = jnp.where(row < lay.n_prompt, lay.seq, lay.dec_seq)
        pos = jnp.bitwise_and(row, seq_len - 1)
        u_prev = jnp.where(pos != 0, uu_scr[pl.ds(HALO + r0 - 1, sub), :], 0.0)
        u_next = jnp.where(pos != seq_len - 1, uu_scr[pl.ds(HALO + r0 + 1, sub), :], 0.0)
        y = (u_prev * cw_ref[0:1, :] + uu_scr[pl.ds(HALO + r0, sub), :] * cw_ref[1:2, :]
             + u_next * cw_ref[2:3, :])
        mix = _dot((bg * y).astype(BF16), wout[...])
        o_ref[r0:r0 + sub, :] = _layer_norm(lay.alpha * x[r0:r0 + sub] + ga * mix,
                                            lng_ref[layer, 0:1, :],
                                            lnb_ref[layer, 0:1, :])


def _conv_mixer(lay, layer, j, x_parts, mod, lng, lnb, w_in, cw, w_out, jobs):
    d = lay.d
    per = TM // HALO
    x_specs, x_args = [], []
    tile0 = 0
    for part in x_parts:
        n_tiles = part.shape[0] // TM

        def center(i, t0=tile0, n=n_tiles):
            return (jnp.clip(i - t0, 0, n - 1), 0)

        def prev_halo(i, t0=tile0, n=n_tiles):
            return (jnp.clip((i - t0) * per - 1, 0, n * per - 1), 0)

        def next_halo(i, t0=tile0, n=n_tiles):
            return (jnp.clip((i - t0 + 1) * per, 0, n * per - 1), 0)

        x_specs += [pl.BlockSpec((TM, d), center), pl.BlockSpec((HALO, d), prev_halo),
                    pl.BlockSpec((HALO, d), next_halo)]
        x_args += [part, part, part]
        tile0 += n_tiles
    win_arr, win_spec, win_scr = _weight_in(w_in)
    wout_arr, wout_spec, wout_scr = _weight_in(w_out)
    (x,), job_results = _run(
        functools.partial(_conv_kernel, lay, layer, len(x_parts) == 2),
        name=f"conv_mixer_{layer}", grid=(lay.n_steps,),
        args=x_args + [mod, lng, lnb, win_arr, cw, wout_arr],
        in_specs=x_specs + [
            lay.mod_spec(lambda i: (i, TM)),
            _resident(lng.shape, lambda i: (0, 0, 0)),
            _resident(lnb.shape, lambda i: (0, 0, 0)),
            win_spec,
            _resident((None, CONV_WIDTH, d), lambda i: (j, 0, 0)),
            wout_spec,
        ],
        out_shape=[jax.ShapeDtypeStruct((lay.n_tok, d), F32)],
        out_specs=[pl.BlockSpec((TM, d), lambda i: (i, 0))],
        scratch_shapes=[pltpu.VMEM((TM + 2 * HALO, d), BF16), pltpu.VMEM((TM + 2 * HALO, d), F32)]
        + [s for s in (win_scr, wout_scr) if s is not None],
        jobs=jobs)
    return x, job_results


def _ffn_kernel(lay, layer, split_out, gla_pre, *refs):
    x_ref, mod_ref, lng_ref, lnb_ref, win_ref, wout_ref = refs[:6]
    pos = 6
    if gla_pre:
        oc_ref, ol_ref, og_ref, ng_ref, wo_ref = refs[pos:pos + 5]
        pos += 5
    n_out = 2 if split_out else 1
    out_refs = refs[pos:pos + n_out]
    scratch = refs[pos + n_out:]
    d_ff = wout_ref.shape[0]
    is_context = pl.program_id(0) * TM < lay.n_prompt
    sh, sc, ga = _mod_part(mod_ref, 3), _mod_part(mod_ref, 4), _mod_part(mod_ref, 5)
    ln_g = lng_ref[layer, 1:2, :]
    ln_b = lnb_ref[layer, 1:2, :]
    results = []
    sub = TM // SUB_TILES
    for s in range(SUB_TILES):
        rows = slice(s * sub, (s + 1) * sub)
        x = x_ref[rows, :]
        if gla_pre:
            z_scr, = scratch
            dv = ng_ref.shape[1]
            for h in range(GLA_HEADS):
                sl = slice(h * dv, (h + 1) * dv)
                o = jnp.where(is_context, oc_ref[rows, sl], ol_ref[rows, sl])
                o = o * lax.rsqrt(jnp.mean(o * o, axis=-1, keepdims=True) + RMS_EPS) * ng_ref[...]
                z_scr[rows, sl] = (o * _silu(og_ref[rows, sl])).astype(BF16)
            mix = _dot(z_scr[rows, :], wo_ref[...])
            x = _layer_norm(lay.alpha * x + _mod_part(mod_ref, 2) * mix,
                            lng_ref[layer, 0:1, :], lnb_ref[layer, 0:1, :])
        h = (x * (1.0 + sc) + sh).astype(BF16)
        g = _dot(h, win_ref[:, 0:d_ff])
        u = _dot(h, win_ref[:, d_ff:2 * d_ff])
        a = (_silu(g) * u).astype(BF16)
        y = _dot(a, wout_ref[...])
        results.append((rows, _layer_norm(lay.alpha * x + ga * y, ln_g, ln_b)))
    if split_out:
        @pl.when(is_context)
        def _():
            for rows, res in results:
                out_refs[0][rows, :] = res

        @pl.when(jnp.logical_not(is_context))
        def _():
            for rows, res in results:
                out_refs[1][rows, :] = res
    else:
        for rows, res in results:
            out_refs[0][rows, :] = res


def _ffn(lay, layer, x, mod, lng, lnb, w_in, w_out, *, split_out, gla_pre, jobs):
    d = lay.d
    d_ff = w_out.shape[0]
    row = lambda i: (i, 0)
    n_ctx = lay.n_prompt // TM
    n_lat = lay.n_steps - n_ctx
    ctx_row = lambda i: (jnp.minimum(i, n_ctx - 1), 0)
    lat_row = lambda i: (jnp.clip(i - n_ctx, 0, n_lat - 1), 0)
    args = [x, mod, lng, lnb, w_in, w_out]
    in_specs = [
        pl.BlockSpec((TM, d), row),
        lay.mod_spec(lambda i: (i, TM)),
        _resident(lng.shape, lambda i: (0, 0, 0)),
        _resident(lnb.shape, lambda i: (0, 0, 0)),
        _resident((d, 2 * d_ff), lambda i: (0, 0)),
        _resident((d_ff, d), lambda i: (0, 0)),
    ]
    scratch = []
    if gla_pre is not None:
        o_context, o_latent, og, norm_g, w_o = gla_pre
        hv = og.shape[1]
        args += [o_context, o_latent, og, norm_g, w_o]
        in_specs += [pl.BlockSpec((TM, hv), ctx_row), pl.BlockSpec((TM, hv), lat_row), pl.BlockSpec((TM, hv), row),
                     _resident(norm_g.shape, lambda i: (0, 0)), _resident((hv, d), lambda i: (0, 0))]
        scratch = [pltpu.VMEM((TM, hv), BF16)]
    if split_out:
        out_shape = [jax.ShapeDtypeStruct((lay.n_prompt, d), F32),
                     jax.ShapeDtypeStruct((lay.n_tok - lay.n_prompt, d), F32)]
        out_specs = [pl.BlockSpec((TM, d), ctx_row), pl.BlockSpec((TM, d), lat_row)]
    else:
        out_shape = [jax.ShapeDtypeStruct((lay.n_tok, d), F32)]
        out_specs = [pl.BlockSpec((TM, d), row)]
    return _run(functools.partial(_ffn_kernel, lay, layer, split_out, gla_pre is not None),
                name=f"ffn_{layer}", grid=(lay.n_steps,), args=args, in_specs=in_specs,
                out_shape=out_shape, out_specs=out_specs, scratch_shapes=scratch, jobs=jobs)


def _qkv_kernel(lay, n_heads, x_ref, mod_ref, w_ref, qg_ref, kg_ref, cos_ref, sin_ref,
                q_ref, k_ref, v_ref, kf_ref, vf_ref, *scratch):
    (w,), _ = _bf16_weights(pl.program_id(0) == 0, (w_ref,), scratch)
    sh, sc = _mod_part(mod_ref, 0), _mod_part(mod_ref, 1)
    lane = lax.broadcasted_iota(jnp.int32, (1, HEAD_DIM), 1)
    first_half = jnp.bitwise_and(lane, HEAD_DIM // 4) == 0
    q_gain = qg_ref[...] * (HEAD_DIM ** -0.5)
    ones = jnp.ones((HEAD_DIM, HEAD_DIM), BF16)
    k0 = n_heads * HEAD_DIM
    v0 = k0 + N_KV_HEADS * HEAD_DIM
    for s in range(SUB_TILES):
        rows = slice(s * (TM // SUB_TILES), (s + 1) * (TM // SUB_TILES))
        h = (x_ref[rows, :] * (1.0 + sc) + sh).astype(BF16)
        qkv = _dot(h, w[...])
        cos, sin = cos_ref[rows, :], sin_ref[rows, :]

        def norm_rope(t, g):
            sq = t * t
            sq_hi = sq.astype(BF16)
            sq_lo = (sq - sq_hi.astype(F32)).astype(BF16)
            mean_sq = (_dot(sq_hi, ones) + _dot(sq_lo, ones)) * (1.0 / HEAD_DIM)
            t = t * lax.rsqrt(mean_sq + RMS_EPS) * g
            partner = jnp.where(first_half,
                                pltpu.roll(t, HEAD_DIM - HEAD_DIM // 4, axis=1),
                                pltpu.roll(t, HEAD_DIM // 4, axis=1))
            return t * cos + partner * sin

        for hq in range(n_heads):
            sl = slice(hq * HEAD_DIM, (hq + 1) * HEAD_DIM)
            q_ref[rows, sl] = norm_rope(qkv[:, sl], q_gain).astype(BF16)
        for hk in range(N_KV_HEADS):
            sl = slice(hk * HEAD_DIM, (hk + 1) * HEAD_DIM)
            kh = norm_rope(qkv[:, k0 + hk * HEAD_DIM:k0 + (hk + 1) * HEAD_DIM], kg_ref[...])
            kf_ref[rows, sl] = kh
            k_ref[rows, sl] = kh.astype(BF16)
        v = qkv[:, v0:v0 + N_KV_HEADS * HEAD_DIM]
        vf_ref[rows, :] = v
        v_ref[rows, :] = v.astype(BF16)


def _attn_qkv(lay, layer, j, x, mod, w_qkv, q_gain, k_gain, cos_t, sin_t, jobs):
    d = lay.d
    n_qkv = _weight_shape(w_qkv)[1]
    kv = N_KV_HEADS * HEAD_DIM
    n_heads = (n_qkv - 2 * kv) // HEAD_DIM
    row = lambda i: (i, 0)

    def rope_row(i):
        r0 = i * TM
        return (jnp.where(r0 < lay.n_prompt, 0, 1 + ((r0 - lay.n_prompt) % lay.dec_seq) // TM), 0)

    w_arr, w_spec, w_scr = _weight_in(w_qkv)
    return _run(
        functools.partial(_qkv_kernel, lay, n_heads),
        name=f"attn_qkv_{layer}", grid=(lay.n_steps,),
        args=[x, mod, w_arr, q_gain, k_gain, cos_t, sin_t],
        in_specs=[
            pl.BlockSpec((TM, d), row),
            lay.mod_spec(lambda i: (i, TM)),
            w_spec,
            _resident((None, 1, HEAD_DIM), lambda i: (j, 0, 0)),
            _resident((None, 1, HEAD_DIM), lambda i: (j, 0, 0)),
            pl.BlockSpec((TM, HEAD_DIM), rope_row),
            pl.BlockSpec((TM, HEAD_DIM), rope_row),
        ],
        out_shape=[jax.ShapeDtypeStruct((lay.n_tok, n_heads * HEAD_DIM), BF16),
                   jax.ShapeDtypeStruct((lay.n_tok, kv), BF16),
                   jax.ShapeDtypeStruct((lay.n_tok, kv), BF16),
                   jax.ShapeDtypeStruct((lay.n_tok, kv), F32),
                   jax.ShapeDtypeStruct((lay.n_tok, kv), F32)],
        out_specs=[pl.BlockSpec((TM, n_heads * HEAD_DIM), row), pl.BlockSpec((TM, kv), row),
                   pl.BlockSpec((TM, kv), row), pl.BlockSpec((TM, kv), row), pl.BlockSpec((TM, kv), row)],
        scratch_shapes=[s for s in (w_scr,) if s is not None],
        jobs=jobs)


def _attn_kernel(lay, layer, n_heads, has_cache, seqs, *refs):
    if has_cache:
        (q_ref, k_ref, v_ref, kc_ref, vc_ref, x_ref, mod_ref, lng_ref, lnb_ref, wo_ref, o_ref, att_scr,
         *scratch) = refs
    else:
        (q_ref, k_ref, v_ref, x_ref, mod_ref, lng_ref, lnb_ref, wo_ref, o_ref, att_scr, *scratch) = refs
    (wo,), _ = _bf16_weights((pl.program_id(0) == 0) & (pl.program_id(1) == 0), (wo_ref,), scratch)
    group = n_heads // N_KV_HEADS
    q_rows = q_ref.shape[0] // seqs
    k_rows = k_ref.shape[0] // seqs
    for sq in range(seqs):
        rq = slice(sq * q_rows, (sq + 1) * q_rows)
        rk = slice(sq * k_rows, (sq + 1) * k_rows)
        for hk in range(N_KV_HEADS):
            ksl = slice(hk * HEAD_DIM, (hk + 1) * HEAD_DIM)
            v_ext = jnp.concatenate([v_ref[rk, ksl], jnp.ones((k_rows, HEAD_DIM), BF16)], axis=1)
            if has_cache:
                vc_ext = jnp.concatenate([vc_ref[:, ksl], jnp.ones((vc_ref.shape[0], HEAD_DIM), BF16)], axis=1)
            for g in range(group):
                hq = hk * group + g
                qsl = slice(hq * HEAD_DIM, (hq + 1) * HEAD_DIM)
                qh = q_ref[rq, qsl]
                s = _dot_nt(qh, k_ref[rk, ksl])
                m = jnp.max(s, axis=-1, keepdims=True)
                if has_cache:
                    s_c = _dot_nt(qh, kc_ref[:, ksl])
                    m = jnp.maximum(m, jnp.max(s_c, axis=-1, keepdims=True))
                o = _dot(jnp.exp(s - m).astype(BF16), v_ext)
                if has_cache:
                    o = o + _dot(jnp.exp(s_c - m).astype(BF16), vc_ext)
                att_scr[rq, qsl] = (o[:, 0:HEAD_DIM] / o[:, HEAD_DIM:2 * HEAD_DIM]).astype(BF16)
    ga = _mod_part(mod_ref, 2)
    mix = _dot(att_scr[...], wo[...])
    o_ref[...] = _layer_norm(lay.alpha * x_ref[...] + ga * mix,
                             lng_ref[layer, 0:1, :], lnb_ref[layer, 0:1, :])


def _attention(lay, layer, q, k, v, cache, x, mod, lng, lnb, w_o, *, row0, n_seq, seq, tq):
    d = lay.d
    n_heads = q.shape[1] // HEAD_DIM
    kv = N_KV_HEADS * HEAD_DIM
    has_cache = cache is not None
    seqs = max(tq // seq, 1)
    assert not (has_cache and seqs > 1) and n_seq % seqs == 0
    n_seq, seq = n_seq // seqs, seq * seqs
    q_blocks = seq // tq
    tile = lambda b, i: (row0 // tq + b * q_blocks + i, 0)
    seq_blk = lambda b, i: (row0 // seq + b, 0)
    in_specs = [pl.BlockSpec((tq, n_heads * HEAD_DIM), tile),
                pl.BlockSpec((seq, kv), seq_blk), pl.BlockSpec((seq, kv), seq_blk)]
    args = [q, k, v]
    if has_cache:
        kc, vc = cache
        past = kc.shape[0] // n_seq
        in_specs += [pl.BlockSpec((past, kv), lambda b, i: (b, 0))] * 2
        args += [kc, vc]
    x_index = len(args)
    wo_arr, wo_spec, wo_scr = _weight_in(w_o)
    in_specs += [
        pl.BlockSpec((tq, d), tile),
        lay.mod_spec(lambda b, i: (row0 // tq + b * q_blocks + i, tq)),
        _resident(lng.shape, lambda b, i: (0, 0, 0)),
        _resident(lnb.shape, lambda b, i: (0, 0, 0)),
        wo_spec,
    ]
    args += [x, mod, lng, lnb, wo_arr]
    (x,), _ = _run(
        functools.partial(_attn_kernel, lay, layer, n_heads, has_cache, seqs),
        name=f"attention_{layer}_{'latent' if has_cache else 'context'}", grid=(n_seq, q_blocks),
        args=args, in_specs=in_specs,
        out_shape=[jax.ShapeDtypeStruct((lay.n_tok, d), F32)], out_specs=[pl.BlockSpec((tq, d), tile)],
        scratch_shapes=[pltpu.VMEM((tq, n_heads * HEAD_DIM), BF16)] + [s for s in (wo_scr,) if s is not None],
        aliases={x_index: 0})
    return x


def _split3(x):
    hi = x.astype(BF16)
    r = x - hi.astype(F32)
    mid = r.astype(BF16)
    lo = (r - mid.astype(F32)).astype(BF16)
    return hi, mid, lo


def _gla_proj_kernel(lay, hk, hv, x_ref, mod_ref, win_ref, w1_ref, w2_ref, bg_ref,
                     q_ref, k_ref, v_ref, og_ref, bf_ref, bb_ref, *scratch):
    (win,), _ = _bf16_weights(pl.program_id(0) == 0, (win_ref,), scratch)
    sh, sc = _mod_part(mod_ref, 0), _mod_part(mod_ref, 1)
    dk = hk // GLA_HEADS
    r = lax.broadcasted_iota(jnp.int32, (GLA_CHUNK, GLA_CHUNK), 0)
    c = lax.broadcasted_iota(jnp.int32, (GLA_CHUNK, GLA_CHUNK), 1)
    lower = jnp.concatenate([jnp.where(c <= r, 1.0, 0.0).astype(BF16)] * 3, axis=1)
    upper = jnp.concatenate([jnp.where(c >= r, 1.0, 0.0).astype(BF16)] * 3, axis=1)
    h = (x_ref[...] * (1.0 + sc) + sh).astype(BF16)
    proj = _dot(h, win[...])
    q_ref[...] = proj[:, 0:hk] * (dk ** -0.5)
    k_ref[...] = proj[:, hk:2 * hk]
    v_ref[...] = proj[:, 2 * hk:2 * hk + hv].astype(BF16)
    og_ref[...] = proj[:, 2 * hk + hv:2 * hk + 2 * hv]
    z = _dot(_dot(h, w1_ref[...]).astype(BF16), w2_ref[...]) + bg_ref[...]
    log_gate = (jnp.minimum(z, 0.0) - jnp.log1p(jnp.exp(-jnp.abs(z)))) * (1.0 / GLA_TAU)
    for ch in range(TM // GLA_CHUNK):
        rows = slice(ch * GLA_CHUNK, (ch + 1) * GLA_CHUNK)
        bf_ref[rows, :] = _dot(lower, jnp.concatenate(_split3(log_gate[rows, 0:hk]), axis=0))
        bb_ref[rows, :] = _dot(upper, jnp.concatenate(_split3(log_gate[rows, hk:2 * hk]), axis=0))


def _gla_proj(lay, layer, j, x, mod, w_in, w1, w2, b_gate, jobs):
    d = lay.d
    n_in = _weight_shape(w_in)[1]
    hk = w2.shape[2] // 2
    hv = (n_in - 2 * hk) // 2
    rank2 = w1.shape[2]
    row = lambda i: (i, 0)
    win_arr, win_spec, win_scr = _weight_in(w_in)
    return _run(
        functools.partial(_gla_proj_kernel, lay, hk, hv),
        name=f"gla_proj_{layer}", grid=(lay.n_steps,),
        args=[x, mod, win_arr, w1, w2, b_gate],
        in_specs=[
            pl.BlockSpec((TM, d), row),
            lay.mod_spec(lambda i: (i, TM)),
            win_spec,
            _resident((None, d, rank2), lambda i: (j, 0, 0)),
            _resident((None, rank2, 2 * hk), lambda i: (j, 0, 0)),
            _resident((None, 1, 2 * hk), lambda i: (j, 0, 0)),
        ],
        out_shape=[jax.ShapeDtypeStruct((lay.n_tok, hk), F32),
                   jax.ShapeDtypeStruct((lay.n_tok, hk), F32),
                   jax.ShapeDtypeStruct((lay.n_tok, hv), BF16),
                   jax.ShapeDtypeStruct((lay.n_tok, hv), F32),
                   jax.ShapeDtypeStruct((lay.n_tok, hk), F32),
                   jax.ShapeDtypeStruct((lay.n_tok, hk), F32)],
        out_specs=[pl.BlockSpec((TM, hk), row), pl.BlockSpec((TM, hk), row), pl.BlockSpec((TM, hv), row),
                   pl.BlockSpec((TM, hv), row), pl.BlockSpec((TM, hk), row), pl.BlockSpec((TM, hk), row)],
        scratch_shapes=[s for s in (win_scr,) if s is not None],
        jobs=jobs)


def _block_diag(blocks):
    n = len(blocks)
    zero = jnp.zeros_like(blocks[0])
    return jnp.concatenate(
        [jnp.concatenate([blocks[i] if j == i else zero for j in range(n)], axis=1) for i in range(n)], axis=0)


def _gla_chain_mask(dtype):
    n = GLA_CHAINS * GLA_CHUNK
    r = lax.broadcasted_iota(jnp.int32, (n, n), 0)
    c = lax.broadcasted_iota(jnp.int32, (n, n), 1)
    shift = GLA_CHUNK.bit_length() - 1
    assert 1 << shift == GLA_CHUNK
    r_chain, r_t = jnp.right_shift(r, shift), jnp.bitwise_and(r, GLA_CHUNK - 1)
    c_chain, c_t = jnp.right_shift(c, shift), jnp.bitwise_and(c, GLA_CHUNK - 1)
    direction = 1 - 2 * jnp.bitwise_and(r_chain, 1)
    ordered = jnp.where((c_t - r_t) * direction <= 0, 1.0, 0.0)
    return jnp.where(r_chain == c_chain, ordered, 0.0).astype(dtype)


def _gla_fast_step(refs, st_scr, mask_scr, g, t, accumulate, n_chunks, dk, dv):
    q_ref, k_ref, v_ref, bf_ref, bb_ref, o_ref = refs
    mid = GLA_CHUNK // 2
    slots = []
    for u in range(2):
        qs_l, ks_l, v_l, qi_l, kh_l, dec_l, dst_l, old_l = [], [], [], [], [], [], [], []
        for j in range(2):
            h = 2 * g + j
            kcols = slice(h * dk, (h + 1) * dk)
            vcols = slice(h * dv, (h + 1) * dv)
            for forward, b_ref in ((True, bf_ref), (False, bb_ref)):
                c = (2 * t + u) if forward else (n_chunks - 1 - 2 * t - u)
                rows = pl.ds(pl.multiple_of(c * GLA_CHUNK, GLA_CHUNK), GLA_CHUNK)
                q, k, b, v = q_ref[rows, kcols], k_ref[rows, kcols], b_ref[rows, kcols], v_ref[rows, vcols]
                rho = b[mid:mid + 1]
                q_s = q * jnp.exp(b - rho)
                k_s = k * jnp.exp(rho - b)
                b_exit = b[GLA_CHUNK - 1:GLA_CHUNK] if forward else b[0:1]
                qs_l.append(q_s.astype(BF16))
                ks_l.append(k_s.astype(BF16))
                v_l.append(v)
                qi_l.append((q_s * jnp.exp(rho)).astype(BF16))
                kh_l.append((k_s * jnp.exp(b_exit - rho)).astype(BF16))
                dec_l.append(jnp.exp(b_exit))
                dst_l.append((rows, vcols))
                old_l.append(o_ref[rows, vcols] if accumulate else None)
        v_all = jnp.concatenate(v_l, axis=0)
        a = _dot_nt(jnp.concatenate(qs_l, axis=0), jnp.concatenate(ks_l, axis=0)).astype(BF16) * mask_scr[...]
        slots.append((_dot(a, v_all), _block_diag(qi_l), _block_diag(kh_l), v_all,
                      jnp.concatenate(dec_l, axis=1), dst_l, old_l))
    st = st_scr[g]
    stores = []
    for o_intra, q_bd, k_bd, v_all, decay, dst_l, old_l in slots:
        o_all = o_intra + _dot_nt(q_bd, st.astype(BF16))
        st = st * decay + _dot_tn(v_all, k_bd)
        for ci, ((rows, vcols), old) in enumerate(zip(dst_l, old_l)):
            o = o_all[ci * GLA_CHUNK:(ci + 1) * GLA_CHUNK]
            stores.append((rows, vcols, o if old is None else o + old))
    st_scr[g] = st
    for rows, vcols, o in stores:
        o_ref[rows, vcols] = o


def _gla_chunk_safe(q, k, v, b, st, forward):
    n_sub = GLA_CHUNK // GLA_SUB
    o_rows = []
    qi = (q * jnp.exp(b)).astype(BF16)
    o_inter = _dot_nt(qi, st.astype(BF16))
    t_idx = lax.broadcasted_iota(jnp.int32, (GLA_SUB, 1), 0)
    for blk in range(n_sub):
        rows = slice(blk * GLA_SUB, (blk + 1) * GLA_SUB)
        qb, kb, bb_, vb = q[rows], k[rows], b[rows], v[rows].astype(F32)
        o_blk = o_inter[rows]
        if forward and blk > 0:
            others = slice(0, blk * GLA_SUB)
            rho = b[blk * GLA_SUB - 1:blk * GLA_SUB]
        elif (not forward) and blk < n_sub - 1:
            others = slice((blk + 1) * GLA_SUB, GLA_CHUNK)
            rho = b[(blk + 1) * GLA_SUB:(blk + 1) * GLA_SUB + 1]
        else:
            others = None
        if others is not None:
            q_s = (qb * jnp.exp(bb_ - rho)).astype(BF16)
            k_s = (k[others] * jnp.exp(rho - b[others])).astype(BF16)
            a = _dot_nt(q_s, k_s)
            o_blk = o_blk + _dot(a.astype(BF16), v[others])
        for s in range(GLA_SUB):
            keep = (t_idx >= s) if forward else (t_idx <= s)
            decay = jnp.exp(jnp.where(keep, bb_ - bb_[s:s + 1], NEG_BIG))
            a_col = jnp.sum(qb * kb[s:s + 1] * decay, axis=-1, keepdims=True)
            o_blk = o_blk + a_col * vb[s:s + 1]
        o_rows.append(o_blk)
    b_exit = b[GLA_CHUNK - 1:GLA_CHUNK] if forward else b[0:1]
    k_hat = (k * jnp.exp(b_exit - b)).astype(BF16)
    st_new = st * jnp.exp(b_exit) + _dot_tn(v, k_hat)
    return jnp.concatenate(o_rows, axis=0), st_new


def _gla_scan_kernel(n_chunks, hps, dk, dv, has_init, *refs):
    n_groups = hps // 2
    if has_init:
        (q_ref, k_ref, v_ref, bf_ref, bb_ref, s0f_ref, s0b_ref, o_ref, st_scr, mask_scr) = refs
    else:
        (q_ref, k_ref, v_ref, bf_ref, bb_ref, o_ref, sf_ref, sb_ref, st_scr, mask_scr) = refs
    data_refs = (q_ref, k_ref, v_ref, bf_ref, bb_ref, o_ref)

    def chain_cols(h, forward):
        ci = 2 * (h % 2) + (0 if forward else 1)
        return h // 2, slice(ci * dk, (ci + 1) * dk)

    for h in range(hps):
        for forward in (True, False):
            g, cols = chain_cols(h, forward)
            if has_init:
                st_scr[g, :, cols] = (s0f_ref if forward else s0b_ref)[h].T
            else:
                st_scr[g, :, cols] = jnp.zeros((dv, dk), F32)
    mask_scr[...] = _gla_chain_mask(BF16)

    def safe_step(i, accumulate):
        pending = []
        for h in range(hps):
            kcols = slice(h * dk, (h + 1) * dk)
            vcols = slice(h * dv, (h + 1) * dv)
            for forward, b_ref in ((True, bf_ref), (False, bb_ref)):
                c = i if forward else n_chunks - 1 - i
                rows = pl.ds(pl.multiple_of(c * GLA_CHUNK, GLA_CHUNK), GLA_CHUNK)
                g, cols = chain_cols(h, forward)
                o, st_new = _gla_chunk_safe(q_ref[rows, kcols], k_ref[rows, kcols], v_ref[rows, vcols],
                                            b_ref[rows, kcols], st_scr[g, :, cols], forward)
                if accumulate:
                    o = o + o_ref[rows, vcols]
                pending.append((rows, vcols, g, cols, o, st_new))
        for rows, vcols, g, cols, o, st_new in pending:
            st_scr[g, :, cols] = st_new
            o_ref[rows, vcols] = o

    def fast_step(t, accumulate):
        for g in range(n_groups):
            _gla_fast_step(data_refs, st_scr, mask_scr, g, t, accumulate, n_chunks, dk, dv)

    def run(step, n_steps):
        def first(i, carry):
            step(i, False)
            return carry

        def second(i, carry):
            step(i, True)
            return carry

        lax.fori_loop(0, n_steps // 2, first, 0)
        lax.fori_loop(n_steps // 2, n_steps, second, 0)

    span = jnp.zeros((1, hps * dk), F32)
    for c in range(n_chunks):
        top = slice(c * GLA_CHUNK, c * GLA_CHUNK + 1)
        bottom = slice((c + 1) * GLA_CHUNK - 1, (c + 1) * GLA_CHUNK)
        span = jnp.maximum(span, jnp.maximum(bf_ref[top, :] - bf_ref[bottom, :],
                                             bb_ref[bottom, :] - bb_ref[top, :]))
    bounded = jnp.max(span) < GLA_SAFE_RANGE

    @pl.when(bounded)
    def _():
        run(fast_step, n_chunks // 2)

    @pl.when(jnp.logical_not(bounded))
    def _():
        run(safe_step, n_chunks)

    if not has_init:
        for h in range(hps):
            for forward in (True, False):
                g, cols = chain_cols(h, forward)
                (sf_ref if forward else sb_ref)[h] = st_scr[g, :, cols].T


def _gla_scan(lay, j, q, k, v, bf, bb, init, *, row0, n_seq, seq, hps):
    hk, hv = q.shape[1], v.shape[1]
    dk, dv = hk // GLA_HEADS, hv // GLA_HEADS
    n_chunks = seq // GLA_CHUNK
    groups = GLA_HEADS // hps
    assert n_chunks % 4 == 0 and GLA_HEADS % hps == 0 and hps % 2 == 0
    has_init = init is not None
    blk = lambda b, g: (row0 // seq + b, g)
    kspec = pl.BlockSpec((seq, hps * dk), blk)
    vspec = pl.BlockSpec((seq, hps * dv), blk)
    in_specs = [kspec, kspec, vspec, kspec, kspec]
    args = [q, k, v, bf, bb]
    out_shape = [jax.ShapeDtypeStruct((n_seq * seq, hv), F32)]
    out_specs = [pl.BlockSpec((seq, hps * dv), lambda b, g: (b, g))]
    if has_init:
        s0f, s0b = init
        st_spec = pl.BlockSpec((None, hps, dk, dv), lambda b, g: (b, j * groups + g, 0, 0))
        in_specs += [st_spec, st_spec]
        args += [s0f, s0b]
    else:
        st_shape = jax.ShapeDtypeStruct((n_seq, GLA_HEADS, dk, dv), F32)
        st_spec = pl.BlockSpec((None, hps, dk, dv), lambda b, g: (b, g, 0, 0))
        out_shape += [st_shape, st_shape]
        out_specs += [st_spec, st_spec]
    outs, _ = _run(
        functools.partial(_gla_scan_kernel, n_chunks, hps, dk, dv, has_init),
        name=f"gla_scan_{'latent' if has_init else 'context'}", grid=(n_seq, groups),
        args=args, in_specs=in_specs, out_shape=out_shape, out_specs=out_specs,
        scratch_shapes=[pltpu.VMEM((hps // 2, dv, GLA_CHAINS * dk), F32),
                        pltpu.VMEM((GLA_CHAINS * GLA_CHUNK, GLA_CHAINS * GLA_CHUNK), BF16)])
    return outs


def _rope_tables(lay):
    n_freq = HEAD_DIM // 4
    pos = np.arange(lay.dec_seq)
    freqs = (np.float32(ROPE_THETA) ** (-np.arange(n_freq, dtype=np.float32) / np.float32(n_freq))).astype(np.float32)
    ang_r = (pos // GRID_W).astype(np.float32)[:, None] * freqs
    ang_c = (pos % GRID_W).astype(np.float32)[:, None] * freqs
    cos = np.concatenate([np.cos(ang_r)] * 2 + [np.cos(ang_c)] * 2, axis=-1)
    sin = np.concatenate([-np.sin(ang_r), np.sin(ang_r), -np.sin(ang_c), np.sin(ang_c)], axis=-1)
    cos = np.concatenate([np.ones((TM, HEAD_DIM), np.float32), cos], axis=0).astype(np.float32)
    sin = np.concatenate([np.zeros((TM, HEAD_DIM), np.float32), sin], axis=0).astype(np.float32)
    return jnp.asarray(cos), jnp.asarray(sin)


def kernel(x_prompt, x_sample, c, cache_k, cache_v, state_gla_fwd, state_gla_bwd, c_ctx, w_ada, b_ada, ln_g, ln_b, conv_w_in, conv_w, conv_w_out, attn_w_qkv, attn_q_norm, attn_k_norm, attn_w_o, gla_w_in, gla_w_gate1, gla_w_gate2, gla_b_gate, gla_norm, gla_w_o, ffn_w_in, ffn_w_out):
    batch, seq, d = x_prompt.shape
    dec_batch, dec_seq, _ = x_sample.shape
    depth = w_ada.shape[0]
    lay = _Layout(batch, seq, dec_batch, dec_seq, d, depth)
    n_steps = lay.n_steps
    kv = N_KV_HEADS * HEAD_DIM
    past = cache_k.shape[2]

    cond = jnp.repeat(jnp.concatenate([c_ctx[None, :], c], axis=0), MOD_GROUP, axis=0)
    lng, lnb = ln_g, ln_b
    g1 = jnp.concatenate([gla_w_gate1[:, 0], gla_w_gate1[:, 1]], axis=-1).astype(BF16)
    zeros = jnp.zeros_like(gla_w_gate2[:, 0])
    g2 = jnp.concatenate([jnp.concatenate([gla_w_gate2[:, 0], zeros], axis=-1),
                          jnp.concatenate([zeros, gla_w_gate2[:, 1]], axis=-1)], axis=1).astype(BF16)
    gb = jnp.concatenate([gla_b_gate[:, 0], gla_b_gate[:, 1]], axis=-1)[:, None, :]
    cos_t, sin_t = _rope_tables(lay)

    def mixer_weights(i):
        kind, j = i % N_MIXERS, i // N_MIXERS
        return [[(conv_w_in, j), (conv_w_out, j)], [(attn_w_qkv, j), (attn_w_o, j)], [(gla_w_in, j), (gla_w_o, j)]][kind]

    x = (x_prompt.reshape(lay.n_prompt, d), x_sample.reshape(dec_batch * dec_seq, d))
    mod = _modulation(cond, w_ada, b_ada, 0)
    mix_w = mixer_weights(0)
    new_k, new_v, new_sf, new_sb = [], [], [], []
    y_prompt = y_sample = None
    for i in range(depth):
        kind, j = i % N_MIXERS, i // N_MIXERS
        last = i == depth - 1
        ffn_jobs = [_cast_job(ffn_w_in, i, n_steps), _cast_job(ffn_w_out, i, n_steps)]
        gla_pre = None
        if kind == 0:
            x, jr = _conv_mixer(lay, i, j, x if i == 0 else (x,), mod, lng, lnb, mix_w[0], conv_w, mix_w[1], ffn_jobs)
        elif kind == 1:
            (q, k, v, kf, vf), jr = _attn_qkv(lay, i, j, x, mod, mix_w[0], attn_q_norm[:, None, :],
                                              attn_k_norm[:, None, :], cos_t, sin_t, ffn_jobs)
            new_k.append(kf[:lay.n_prompt].reshape(batch, seq, N_KV_HEADS, HEAD_DIM))
            new_v.append(vf[:lay.n_prompt].reshape(batch, seq, N_KV_HEADS, HEAD_DIM))
            kc = cache_k[:, j].reshape(dec_batch * past, kv).astype(BF16)
            vc = cache_v[:, j].reshape(dec_batch * past, kv).astype(BF16)
            x = _attention(lay, i, q, k, v, None, x, mod, lng, lnb, mix_w[1],
                           row0=0, n_seq=batch, seq=seq, tq=TM)
            x = _attention(lay, i, q, k, v, (kc, vc), x, mod, lng, lnb, mix_w[1],
                           row0=lay.n_prompt, n_seq=dec_batch, seq=dec_seq, tq=ATTN_LATENT_TQ)
        else:
            (q, k, v, og, bf, bb), jr = _gla_proj(lay, i, j, x, mod, mix_w[0], g1, g2, gb, ffn_jobs)
            s0f = state_gla_fwd.reshape(dec_batch, -1, *state_gla_fwd.shape[3:])
            s0b = state_gla_bwd.reshape(dec_batch, -1, *state_gla_bwd.shape[3:])
            o_ctx, sf, sb = _gla_scan(lay, j, q, k, v, bf, bb, None, row0=0, n_seq=batch, seq=seq,
                                      hps=GLA_HEADS_PER_STEP_CONTEXT)
            o_lat, = _gla_scan(lay, j, q, k, v, bf, bb, (s0f, s0b), row0=lay.n_prompt, n_seq=dec_batch,
                               seq=dec_seq, hps=GLA_HEADS_PER_STEP_LATENT)
            new_sf.append(sf)
            new_sb.append(sb)
            w_o = mix_w[1]
            if isinstance(w_o, tuple):
                w_o = w_o[0][w_o[1]].astype(BF16)
            gla_pre = (o_ctx, o_lat, og, gla_norm[j][None, :], w_o)
        ffn_w = (jr[0][0], jr[1][0])
        next_jobs = []
        if not last:
            next_jobs = [_mod_job(cond, w_ada, b_ada, i + 1, n_steps)]
            next_jobs += [_cast_job(arr, jj, n_steps) for arr, jj in mixer_weights(i + 1)]
        outs, jr = _ffn(lay, i, x, mod, lng, lnb, ffn_w[0], ffn_w[1], split_out=last, gla_pre=gla_pre,
                        jobs=next_jobs)
        if last:
            y_prompt, y_sample = outs
        else:
            x, = outs
            mod = jr[0][0]
            mix_w = [jr[1][0], jr[2][0]]

    def stack_layers(parts):
        return parts[0][:, None] if len(parts) == 1 else jnp.stack(parts, axis=1)

    y_prompt = y_prompt.reshape(batch, seq, d)
    y_sample = y_sample.reshape(dec_batch, dec_seq, d)
    return (y_prompt, y_sample, stack_layers(new_k), stack_layers(new_v),
            stack_layers(new_sf), stack_layers(new_sb))
```

```python
import functools

import jax
import jax.numpy as jnp
import numpy as np
from jax import lax
from jax.experimental import pallas as pl
from jax.experimental.pallas import tpu as pltpu

F32 = jnp.float32
BF16 = jnp.bfloat16

N_MIXERS = 3
CONV_WIDTH = 3
HEAD_DIM = 128
N_KV_HEADS = 2
GRID_W = 64
ROPE_THETA = 10000.0
GLA_HEADS = 4
GLA_TAU = 16.0
GLA_CHUNK = 64
LN_EPS = 1e-5
RMS_EPS = 1e-6

LANES = 128
BF16_SUBLANES = 16
VMEM_LIMIT = 58 * 1024 * 1024

MOD_GROUP = 8
MOD_NT = 1536
TM = 512
SUB_TILES = 2
ATTN_TQ = 1024
HALO = BF16_SUBLANES
WEIGHT_CAST_ROWS = 128
GLA_SUB = 16
GLA_SAFE_RANGE = 80.0
GLA_CHAINS = 4
GLA_HEADS_PER_STEP_CONTEXT = 4
GLA_HEADS_PER_STEP_LATENT = 2
NEG_BIG = -1e30


def _dot(a, b):
    return jnp.dot(a, b, preferred_element_type=F32)


def _dot_nt(a, b):
    return lax.dot_general(a, b, (((1,), (1,)), ((), ())), preferred_element_type=F32)


def _dot_tn(a, b):
    return lax.dot_general(a, b, (((0,), (0,)), ((), ())), preferred_element_type=F32)


def _layer_norm(y, g, b):
    mu = jnp.mean(y, axis=-1, keepdims=True)
    yc = y - mu
    var = jnp.mean(yc * yc, axis=-1, keepdims=True)
    return yc * lax.rsqrt(var + LN_EPS) * g + b


def _silu(x):
    return x * jax.nn.sigmoid(x)


def _resident(block_shape, index_map):
    return pl.BlockSpec(block_shape, index_map, pipeline_mode=pl.Buffered(1))


class _Layout:
    def __init__(self, batch, seq, dec_batch, dec_seq, d_model, depth):
        self.batch, self.seq, self.dec_batch, self.dec_seq = batch, seq, dec_batch, dec_seq
        self.d, self.depth = d_model, depth
        self.n_prompt = batch * seq
        self.n_tok = self.n_prompt + dec_batch * dec_seq
        self.n_steps = self.n_tok // TM
        assert self.n_prompt % TM == 0 and dec_seq % TM == 0 and TM % seq == 0
        assert seq & (seq - 1) == 0 and dec_seq & (dec_seq - 1) == 0
        self.alpha = (2.0 * depth) ** 0.25

    def mod_index(self, i, rows):
        r0 = i * rows
        return jnp.where(r0 < self.n_prompt, 0, 1 + (r0 - self.n_prompt) // self.dec_seq)

    def mod_spec(self, tile_of):
        def index(*ids):
            tile, rows = tile_of(*ids)
            return (self.mod_index(tile, rows), 0)

        return pl.BlockSpec((MOD_GROUP, 6 * self.d), index)


def _mod_part(mod_ref, k):
    d = mod_ref.shape[1] // 6
    return mod_ref[0:1, k * d:(k + 1) * d]


class _Job:
    def __init__(self, args, in_specs, out_shapes, out_specs, fn):
        self.args, self.in_specs, self.out_shapes, self.out_specs, self.fn = args, in_specs, out_shapes, out_specs, fn


def _cast_job(w, j, n_steps):
    _, n_rows, n_cols = w.shape
    rows = n_rows // n_steps
    assert rows * n_steps == n_rows and rows % BF16_SUBLANES == 0

    def fn(in_refs, out_refs):
        out_refs[0][...] = in_refs[0][...].astype(BF16)

    return _Job([w], [pl.BlockSpec((None, rows, n_cols), lambda i: (j, i, 0))],
                [jax.ShapeDtypeStruct((n_rows, n_cols), BF16)], [pl.BlockSpec((rows, n_cols), lambda i: (i, 0))], fn)


def _mod_job(cond, w_ada, b_ada, layer, n_steps):
    depth, d, n_out = w_ada.shape
    n_rows = cond.shape[0]
    cols = n_out // n_steps
    assert cols * n_steps == n_out and cols % LANES == 0

    def fn(in_refs, out_refs):
        cond_ref, w_ref, b_ref = in_refs
        out_refs[0][...] = (_dot(_silu(cond_ref[...]).astype(BF16), w_ref[...].astype(BF16))
                            + b_ref[layer:layer + 1, :])

    return _Job([cond, w_ada, b_ada],
                [_resident((n_rows, d), lambda i: (0, 0)),
                 pl.BlockSpec((None, d, cols), lambda i: (layer, 0, i)),
                 pl.BlockSpec((depth, cols), lambda i: (0, i))],
                [jax.ShapeDtypeStruct((n_rows, n_out), F32)], [pl.BlockSpec((n_rows, cols), lambda i: (0, i))], fn)


def _run(kernel_fn, *, name, grid, args, in_specs, out_shape, out_specs, scratch_shapes=(), jobs=(), aliases=None):
    n_in, n_out = len(args), len(out_shape)

    def body(*refs):
        pos = n_in
        job_in = []
        for jb in jobs:
            job_in.append(refs[pos:pos + len(jb.args)])
            pos += len(jb.args)
        main_out = refs[pos:pos + n_out]
        pos += n_out
        job_out = []
        for jb in jobs:
            job_out.append(refs[pos:pos + len(jb.out_shapes)])
            pos += len(jb.out_shapes)
        for jb, ji, jo in zip(jobs, job_in, job_out):
            jb.fn(ji, jo)
        kernel_fn(*refs[:n_in], *main_out, *refs[pos:])

    outs = pl.pallas_call(
        body,
        out_shape=tuple(out_shape) + tuple(s for jb in jobs for s in jb.out_shapes),
        grid=grid,
        in_specs=list(in_specs) + [s for jb in jobs for s in jb.in_specs],
        out_specs=tuple(out_specs) + tuple(s for jb in jobs for s in jb.out_specs),
        scratch_shapes=list(scratch_shapes),
        input_output_aliases=aliases or {},
        compiler_params=pltpu.CompilerParams(dimension_semantics=("arbitrary",) * len(grid),
                                             vmem_limit_bytes=VMEM_LIMIT),
        name=name,
    )(*args, *[a for jb in jobs for a in jb.args])
    main, rest = outs[:n_out], list(outs[n_out:])
    job_results = []
    for jb in jobs:
        job_results.append(tuple(rest[:len(jb.out_shapes)]))
        rest = rest[len(jb.out_shapes):]
    return tuple(main), job_results


def _weight_in(w):
    if isinstance(w, tuple):
        arr, j = w
        _, r, c = arr.shape
        return arr, _resident((None, r, c), lambda *_: (j, 0, 0)), pltpu.VMEM((r, c), BF16)
    r, c = w.shape
    return w, _resident((r, c), lambda *_: (0, 0)), None


def _weight_shape(w):
    return w[0].shape[1:] if isinstance(w, tuple) else w.shape


def _round_weight_once(first_step, w_ref, w_scr):
    n_rows = w_ref.shape[0]
    rows = min(n_rows, WEIGHT_CAST_ROWS)
    assert n_rows % rows == 0

    @pl.when(first_step)
    def _():
        def body(c, carry):
            sl = pl.ds(pl.multiple_of(c * rows, rows), rows)
            w_scr[sl, :] = w_ref[sl, :].astype(BF16)
            return carry

        lax.fori_loop(0, n_rows // rows, body, 0)


def _bf16_weights(first_step, w_refs, scratch):
    scratch = list(scratch)
    out = []
    for w in w_refs:
        if w.dtype == BF16:
            out.append(w)
        else:
            scr = scratch.pop(0)
            _round_weight_once(first_step, w, scr)
            out.append(scr)
    return out, scratch


def _modulation(cond, w_ada, b_ada, layer):
    _, d, n_out = w_ada.shape
    job = _mod_job(cond, w_ada, b_ada, layer, n_out // MOD_NT)
    _, (result,) = _run(lambda: None, name=f"modulation_{layer}", grid=(n_out // MOD_NT,), args=[], in_specs=[],
                        out_shape=[], out_specs=[], jobs=[job])
    return result[0]


def _conv_kernel(lay, layer, split, *refs):
    d = lay.d
    i = pl.program_id(0)
    if split:
        (xca, xpa, xna, xcb, xpb, xnb, mod_ref, lng_ref, lnb_ref, win_ref, cw_ref, wout_ref,
         o_ref, h_scr, uu_scr, *scratch) = refs
        is_context = i * TM < lay.n_prompt
        x = jnp.where(is_context, xca[...], xcb[...])
        x_prev = jnp.where(is_context, xpa[...], xpb[...])
        x_next = jnp.where(is_context, xna[...], xnb[...])
    else:
        (xc_ref, xp_ref, xn_ref, mod_ref, lng_ref, lnb_ref, win_ref, cw_ref, wout_ref,
         o_ref, h_scr, uu_scr, *scratch) = refs
        x, x_prev, x_next = xc_ref[...], xp_ref[...], xn_ref[...]
    (win, wout), _ = _bf16_weights(i == 0, (win_ref, wout_ref), scratch)
    sh, sc, ga = _mod_part(mod_ref, 0), _mod_part(mod_ref, 1), _mod_part(mod_ref, 2)
    one_sc = 1.0 + sc
    h_scr[0:HALO, :] = (x_prev * one_sc + sh).astype(BF16)
    h_scr[HALO:HALO + TM, :] = (x * one_sc + sh).astype(BF16)
    h_scr[HALO + TM:HALO + TM + HALO, :] = (x_next * one_sc + sh).astype(BF16)
    cgu = _dot(h_scr[...], win[:, d:3 * d])
    uu_scr[...] = cgu[:, 0:d] * cgu[:, d:2 * d]
    sub = TM // SUB_TILES
    for s in range(SUB_TILES):
        r0 = s * sub
        bg = _dot(h_scr[HALO + r0:HALO + r0 + sub, :], win[:, 0:d])
        row = i * TM + r0 + lax.broadcasted_iota(jnp.int32, (sub, 1), 0)
        seq_len = jnp.where(row < lay.n_prompt, lay.seq, lay.dec_seq)
        pos = jnp.bitwise_and(row, seq_len - 1)
        u_prev = jnp.where(pos != 0, uu_scr[pl.ds(HALO + r0 - 1, sub), :], 0.0)
        u_next = jnp.where(pos != seq_len - 1, uu_scr[pl.ds(HALO + r0 + 1, sub), :], 0.0)
        y = (u_prev * cw_ref[0:1, :] + uu_scr[pl.ds(HALO + r0, sub), :] * cw_ref[1:2, :]
             + u_next * cw_ref[2:3, :])
        mix = _dot((bg * y).astype(BF16), wout[...])
        o_ref[r0:r0 + sub, :] = _layer_norm(lay.alpha * x[r0:r0 + sub] + ga * mix,
                                            lng_ref[layer, 0:1, :],
                                            lnb_ref[layer, 0:1, :])


def _conv_mixer(lay, layer, j, x_parts, mod, lng, lnb, w_in, cw, w_out, jobs):
    d = lay.d
    per = TM // HALO
    x_specs, x_args = [], []
    tile0 = 0
    for part in x_parts:
        n_tiles = part.shape[0] // TM

        def center(i, t0=tile0, n=n_tiles):
            return (jnp.clip(i - t0, 0, n - 1), 0)

        def prev_halo(i, t0=tile0, n=n_tiles):
            return (jnp.clip((i - t0) * per - 1, 0, n * per - 1), 0)

        def next_halo(i, t0=tile0, n=n_tiles):
            return (jnp.clip((i - t0 + 1) * per, 0, n * per - 1), 0)

        x_specs += [pl.BlockSpec((TM, d), center), pl.BlockSpec((HALO, d), prev_halo),
                    pl.BlockSpec((HALO, d), next_halo)]
        x_args += [part, part, part]
        tile0 += n_tiles
    win_arr, win_spec, win_scr = _weight_in(w_in)
    wout_arr, wout_spec, wout_scr = _weight_in(w_out)
    (x,), job_results = _run(
        functools.partial(_conv_kernel, lay, layer, len(x_parts) == 2),
        name=f"conv_mixer_{layer}", grid=(lay.n_steps,),
        args=x_args + [mod, lng, lnb, win_arr, cw, wout_arr],
        in_specs=x_specs + [
            lay.mod_spec(lambda i: (i, TM)),
            _resident(lng.shape, lambda i: (0, 0, 0)),
            _resident(lnb.shape, lambda i: (0, 0, 0)),
            win_spec,
            _resident((None, CONV_WIDTH, d), lambda i: (j, 0, 0)),
            wout_spec,
        ],
        out_shape=[jax.ShapeDtypeStruct((lay.n_tok, d), F32)],
        out_specs=[pl.BlockSpec((TM, d), lambda i: (i, 0))],
        scratch_shapes=[pltpu.VMEM((TM + 2 * HALO, d), BF16), pltpu.VMEM((TM + 2 * HALO, d), F32)]
        + [s for s in (win_scr, wout_scr) if s is not None],
        jobs=jobs)
    return x, job_results


def _ffn_kernel(lay, layer, split_out, gla_pre, *refs):
    x_ref, mod_ref, lng_ref, lnb_ref, win_ref, wout_ref = refs[:6]
    pos = 6
    if gla_pre:
        oc_ref, ol_ref, og_ref, ng_ref, wo_ref = refs[pos:pos + 5]
        pos += 5
    n_out = 2 if split_out else 1
    out_refs = refs[pos:pos + n_out]
    scratch = refs[pos + n_out:]
    d_ff = wout_ref.shape[0]
    is_context = pl.program_id(0) * TM < lay.n_prompt
    sh, sc, ga = _mod_part(mod_ref, 3), _mod_part(mod_ref, 4), _mod_part(mod_ref, 5)
    ln_g = lng_ref[layer, 1:2, :]
    ln_b = lnb_ref[layer, 1:2, :]
    results = []
    sub = TM // SUB_TILES
    for s in range(SUB_TILES):
        rows = slice(s * sub, (s + 1) * sub)
        x = x_ref[rows, :]
        if gla_pre:
            z_scr, = scratch
            dv = ng_ref.shape[1]
            for h in range(GLA_HEADS):
                sl = slice(h * dv, (h + 1) * dv)
                o = jnp.where(is_context, oc_ref[rows, sl], ol_ref[rows, sl])
                o = o * lax.rsqrt(jnp.mean(o * o, axis=-1, keepdims=True) + RMS_EPS) * ng_ref[...]
                z_scr[rows, sl] = (o * _silu(og_ref[rows, sl])).astype(BF16)
            mix = _dot(z_scr[rows, :], wo_ref[...])
            x = _layer_norm(lay.alpha * x + _mod_part(mod_ref, 2) * mix,
                            lng_ref[layer, 0:1, :], lnb_ref[layer, 0:1, :])
        h = (x * (1.0 + sc) + sh).astype(BF16)
        g = _dot(h, win_ref[:, 0:d_ff])
        u = _dot(h, win_ref[:, d_ff:2 * d_ff])
        a = (_silu(g) * u).astype(BF16)
        y = _dot(a, wout_ref[...])
        results.append((rows, _layer_norm(lay.alpha * x + ga * y, ln_g, ln_b)))
    if split_out:
        @pl.when(is_context)
        def _():
            for rows, res in results:
                out_refs[0][rows, :] = res

        @pl.when(jnp.logical_not(is_context))
        def _():
            for rows, res in results:
                out_refs[1][rows, :] = res
    else:
        for rows, res in results:
            out_refs[0][rows, :] = res


def _ffn(lay, layer, x, mod, lng, lnb, w_in, w_out, *, split_out, gla_pre, jobs):
    d = lay.d
    d_ff = w_out.shape[0]
    row = lambda i: (i, 0)
    n_ctx = lay.n_prompt // TM
    n_lat = lay.n_steps - n_ctx
    ctx_row = lambda i: (jnp.minimum(i, n_ctx - 1), 0)
    lat_row = lambda i: (jnp.clip(i - n_ctx, 0, n_lat - 1), 0)
    args = [x, mod, lng, lnb, w_in, w_out]
    in_specs = [
        pl.BlockSpec((TM, d), row),
        lay.mod_spec(lambda i: (i, TM)),
        _resident(lng.shape, lambda i: (0, 0, 0)),
        _resident(lnb.shape, lambda i: (0, 0, 0)),
        _resident((d, 2 * d_ff), lambda i: (0, 0)),
        _resident((d_ff, d), lambda i: (0, 0)),
    ]
    scratch = []
    if gla_pre is not None:
        o_context, o_latent, og, norm_g, w_o = gla_pre
        hv = og.shape[1]
        args += [o_context, o_latent, og, norm_g, w_o]
        in_specs += [pl.BlockSpec((TM, hv), ctx_row), pl.BlockSpec((TM, hv), lat_row), pl.BlockSpec((TM, hv), row),
                     _resident(norm_g.shape, lambda i: (0, 0)), _resident((hv, d), lambda i: (0, 0))]
        scratch = [pltpu.VMEM((TM, hv), BF16)]
    if split_out:
        out_shape = [jax.ShapeDtypeStruct((lay.n_prompt, d), F32),
                     jax.ShapeDtypeStruct((lay.n_tok - lay.n_prompt, d), F32)]
        out_specs = [pl.BlockSpec((TM, d), ctx_row), pl.BlockSpec((TM, d), lat_row)]
    else:
        out_shape = [jax.ShapeDtypeStruct((lay.n_tok, d), F32)]
        out_specs = [pl.BlockSpec((TM, d), row)]
    return _run(functools.partial(_ffn_kernel, lay, layer, split_out, gla_pre is not None),
                name=f"ffn_{layer}", grid=(lay.n_steps,), args=args, in_specs=in_specs,
                out_shape=out_shape, out_specs=out_specs, scratch_shapes=scratch, jobs=jobs)


def _qkv_kernel(lay, n_heads, x_ref, mod_ref, w_ref, qg_ref, kg_ref, cos_ref, sin_ref,
                q_ref, k_ref, v_ref, kf_ref, vf_ref, *scratch):
    (w,), _ = _bf16_weights(pl.program_id(0) == 0, (w_ref,), scratch)
    sh, sc = _mod_part(mod_ref, 0), _mod_part(mod_ref, 1)
    lane = lax.broadcasted_iota(jnp.int32, (1, HEAD_DIM), 1)
    first_half = jnp.bitwise_and(lane, HEAD_DIM // 4) == 0
    q_gain = qg_ref[...] * (HEAD_DIM ** -0.5)
    ones = jnp.ones((HEAD_DIM, HEAD_DIM), BF16)
    k0 = n_heads * HEAD_DIM
    v0 = k0 + N_KV_HEADS * HEAD_DIM
    for s in range(SUB_TILES):
        rows = slice(s * (TM // SUB_TILES), (s + 1) * (TM // SUB_TILES))
        h = (x_ref[rows, :] * (1.0 + sc) + sh).astype(BF16)
        qkv = _dot(h, w[...])
        cos, sin = cos_ref[rows, :], sin_ref[rows, :]

        def norm_rope(t, g):
            sq = t * t
            sq_hi = sq.astype(BF16)
            sq_lo = (sq - sq_hi.astype(F32)).astype(BF16)
            mean_sq = (_dot(sq_hi, ones) + _dot(sq_lo, ones)) * (1.0 / HEAD_DIM)
            t = t * lax.rsqrt(mean_sq + RMS_EPS) * g
            partner = jnp.where(first_half,
                                pltpu.roll(t, HEAD_DIM - HEAD_DIM // 4, axis=1),
                                pltpu.roll(t, HEAD_DIM // 4, axis=1))
            return t * cos + partner * sin

        for hq in range(n_heads):
            sl = slice(hq * HEAD_DIM, (hq + 1) * HEAD_DIM)
            q_ref[rows, sl] = norm_rope(qkv[:, sl], q_gain).astype(BF16)
        for hk in range(N_KV_HEADS):
            sl = slice(hk * HEAD_DIM, (hk + 1) * HEAD_DIM)
            kh = norm_rope(qkv[:, k0 + hk * HEAD_DIM:k0 + (hk + 1) * HEAD_DIM], kg_ref[...])
            kf_ref[rows, sl] = kh
            k_ref[rows, sl] = kh.astype(BF16)
        v = qkv[:, v0:v0 + N_KV_HEADS * HEAD_DIM]
        vf_ref[rows, :] = v
        v_ref[rows, :] = v.astype(BF16)


def _attn_qkv(lay, layer, j, x, mod, w_qkv, q_gain, k_gain, cos_t, sin_t, jobs):
    d = lay.d
    n_qkv = _weight_shape(w_qkv)[1]
    kv = N_KV_HEADS * HEAD_DIM
    n_heads = (n_qkv - 2 * kv) // HEAD_DIM
    row = lambda i: (i, 0)

    def rope_row(i):
        r0 = i * TM
        return (jnp.where(r0 < lay.n_prompt, 0, 1 + ((r0 - lay.n_prompt) % lay.dec_seq) // TM), 0)

    w_arr, w_spec, w_scr = _weight_in(w_qkv)
    return _run(
        functools.partial(_qkv_kernel, lay, n_heads),
        name=f"attn_qkv_{layer}", grid=(lay.n_steps,),
        args=[x, mod, w_arr, q_gain, k_gain, cos_t, sin_t],
        in_specs=[
            pl.BlockSpec((TM, d), row),
            lay.mod_spec(lambda i: (i, TM)),
            w_spec,
            _resident((None, 1, HEAD_DIM), lambda i: (j, 0, 0)),
            _resident((None, 1, HEAD_DIM), lambda i: (j, 0, 0)),
            pl.BlockSpec((TM, HEAD_DIM), rope_row),
            pl.BlockSpec((TM, HEAD_DIM), rope_row),
        ],
        out_shape=[jax.ShapeDtypeStruct((lay.n_tok, n_heads * HEAD_DIM), BF16),
                   jax.ShapeDtypeStruct((lay.n_tok, kv), BF16),
                   jax.ShapeDtypeStruct((lay.n_tok, kv), BF16),
                   jax.ShapeDtypeStruct((lay.n_tok, kv), F32),
                   jax.ShapeDtypeStruct((lay.n_tok, kv), F32)],
        out_specs=[pl.BlockSpec((TM, n_heads * HEAD_DIM), row), pl.BlockSpec((TM, kv), row),
                   pl.BlockSpec((TM, kv), row), pl.BlockSpec((TM, kv), row), pl.BlockSpec((TM, kv), row)],
        scratch_shapes=[s for s in (w_scr,) if s is not None],
        jobs=jobs)


def _attn_kernel(lay, layer, n_heads, has_cache, seqs, *refs):
    if has_cache:
        (q_ref, k_ref, v_ref, kc_ref, vc_ref, x_ref, mod_ref, lng_ref, lnb_ref, wo_ref, o_ref, att_scr,
         *scratch) = refs
    else:
        (q_ref, k_ref, v_ref, x_ref, mod_ref, lng_ref, lnb_ref, wo_ref, o_ref, att_scr, *scratch) = refs
    (wo,), _ = _bf16_weights((pl.program_id(0) == 0) & (pl.program_id(1) == 0), (wo_ref,), scratch)
    group = n_heads // N_KV_HEADS
    q_rows = q_ref.shape[0] // seqs
    k_rows = k_ref.shape[0] // seqs
    for sq in range(seqs):
        rq = slice(sq * q_rows, (sq + 1) * q_rows)
        rk = slice(sq * k_rows, (sq + 1) * k_rows)
        for hk in range(N_KV_HEADS):
            ksl = slice(hk * HEAD_DIM, (hk + 1) * HEAD_DIM)
            v_ext = jnp.concatenate([v_ref[rk, ksl], jnp.ones((k_rows, HEAD_DIM), BF16)], axis=1)
            if has_cache:
                vc_ext = jnp.concatenate([vc_ref[:, ksl], jnp.ones((vc_ref.shape[0], HEAD_DIM), BF16)], axis=1)
            for g in range(group):
                hq = hk * group + g
                qsl = slice(hq * HEAD_DIM, (hq + 1) * HEAD_DIM)
                qh = q_ref[rq, qsl]
                s = _dot_nt(qh, k_ref[rk, ksl])
                m = jnp.max(s, axis=-1, keepdims=True)
                if has_cache:
                    s_c = _dot_nt(qh, kc_ref[:, ksl])
                    m = jnp.maximum(m, jnp.max(s_c, axis=-1, keepdims=True))
                o = _dot(jnp.exp(s - m).astype(BF16), v_ext)
                if has_cache:
                    o = o + _dot(jnp.exp(s_c - m).astype(BF16), vc_ext)
                att_scr[rq, qsl] = (o[:, 0:HEAD_DIM] / o[:, HEAD_DIM:2 * HEAD_DIM]).astype(BF16)
    ga = _mod_part(mod_ref, 2)
    mix = _dot(att_scr[...], wo[...])
    o_ref[...] = _layer_norm(lay.alpha * x_ref[...] + ga * mix,
                             lng_ref[layer, 0:1, :], lnb_ref[layer, 0:1, :])


def _attention(lay, layer, q, k, v, cache, x, mod, lng, lnb, w_o, *, row0, n_seq, seq, tq):
    d = lay.d
    n_heads = q.shape[1] // HEAD_DIM
    kv = N_KV_HEADS * HEAD_DIM
    has_cache = cache is not None
    seqs = max(tq // seq, 1)
    assert not (has_cache and seqs > 1) and n_seq % seqs == 0
    n_seq, seq = n_seq // seqs, seq * seqs
    q_blocks = seq // tq
    tile = lambda b, i: (row0 // tq + b * q_blocks + i, 0)
    seq_blk = lambda b, i: (row0 // seq + b, 0)
    in_specs = [pl.BlockSpec((tq, n_heads * HEAD_DIM), tile),
                pl.BlockSpec((seq, kv), seq_blk), pl.BlockSpec((seq, kv), seq_blk)]
    args = [q, k, v]
    if has_cache:
        kc, vc = cache
        past = kc.shape[0] // n_seq
        in_specs += [pl.BlockSpec((past, kv), lambda b, i: (b, 0))] * 2
        args += [kc, vc]
    x_index = len(args)
    wo_arr, wo_spec, wo_scr = _weight_in(w_o)
    in_specs += [
        pl.BlockSpec((tq, d), tile),
        lay.mod_spec(lambda b, i: (row0 // tq + b * q_blocks + i, tq)),
        _resident(lng.shape, lambda b, i: (0, 0, 0)),
        _resident(lnb.shape, lambda b, i: (0, 0, 0)),
        wo_spec,
    ]
    args += [x, mod, lng, lnb, wo_arr]
    (x,), _ = _run(
        functools.partial(_attn_kernel, lay, layer, n_heads, has_cache, seqs),
        name=f"attention_{layer}_{'latent' if has_cache else 'context'}", grid=(n_seq, q_blocks),
        args=args, in_specs=in_specs,
        out_shape=[jax.ShapeDtypeStruct((lay.n_tok, d), F32)], out_specs=[pl.BlockSpec((tq, d), tile)],
        scratch_shapes=[pltpu.VMEM((tq, n_heads * HEAD_DIM), BF16)] + [s for s in (wo_scr,) if s is not None],
        aliases={x_index: 0})
    return x


def _split3(x):
    hi = x.astype(BF16)
    r = x - hi.astype(F32)
    mid = r.astype(BF16)
    lo = (r - mid.astype(F32)).astype(BF16)
    return hi, mid, lo


def _gla_proj_kernel(lay, hk, hv, x_ref, mod_ref, win_ref, w1_ref, w2_ref, bg_ref,
                     q_ref, k_ref, v_ref, og_ref, bf_ref, bb_ref, *scratch):
    (win,), _ = _bf16_weights(pl.program_id(0) == 0, (win_ref,), scratch)
    sh, sc = _mod_part(mod_ref, 0), _mod_part(mod_ref, 1)
    dk = hk // GLA_HEADS
    r = lax.broadcasted_iota(jnp.int32, (GLA_CHUNK, GLA_CHUNK), 0)
    c = lax.broadcasted_iota(jnp.int32, (GLA_CHUNK, GLA_CHUNK), 1)
    lower = jnp.concatenate([jnp.where(c <= r, 1.0, 0.0).astype(BF16)] * 3, axis=1)
    upper = jnp.concatenate([jnp.where(c >= r, 1.0, 0.0).astype(BF16)] * 3, axis=1)
    h = (x_ref[...] * (1.0 + sc) + sh).astype(BF16)
    proj = _dot(h, win[...])
    q_ref[...] = proj[:, 0:hk] * (dk ** -0.5)
    k_ref[...] = proj[:, hk:2 * hk]
    v_ref[...] = proj[:, 2 * hk:2 * hk + hv].astype(BF16)
    og_ref[...] = proj[:, 2 * hk + hv:2 * hk + 2 * hv]
    z = _dot(_dot(h, w1_ref[...]).astype(BF16), w2_ref[...]) + bg_ref[...]
    log_gate = (jnp.minimum(z, 0.0) - jnp.log1p(jnp.exp(-jnp.abs(z)))) * (1.0 / GLA_TAU)
    for ch in range(TM // GLA_CHUNK):
        rows = slice(ch * GLA_CHUNK, (ch + 1) * GLA_CHUNK)
        bf_ref[rows, :] = _dot(lower, jnp.concatenate(_split3(log_gate[rows, 0:hk]), axis=0))
        bb_ref[rows, :] = _dot(upper, jnp.concatenate(_split3(log_gate[rows, hk:2 * hk]), axis=0))


def _gla_proj(lay, layer, j, x, mod, w_in, w1, w2, b_gate, jobs):
    d = lay.d
    n_in = _weight_shape(w_in)[1]
    hk = w2.shape[2] // 2
    hv = (n_in - 2 * hk) // 2
    rank2 = w1.shape[2]
    row = lambda i: (i, 0)
    win_arr, win_spec, win_scr = _weight_in(w_in)
    return _run(
        functools.partial(_gla_proj_kernel, lay, hk, hv),
        name=f"gla_proj_{layer}", grid=(lay.n_steps,),
        args=[x, mod, win_arr, w1, w2, b_gate],
        in_specs=[
            pl.BlockSpec((TM, d), row),
            lay.mod_spec(lambda i: (i, TM)),
            win_spec,
            _resident((None, d, rank2), lambda i: (j, 0, 0)),
            _resident((None, rank2, 2 * hk), lambda i: (j, 0, 0)),
            _resident((None, 1, 2 * hk), lambda i: (j, 0, 0)),
        ],
        out_shape=[jax.ShapeDtypeStruct((lay.n_tok, hk), F32),
                   jax.ShapeDtypeStruct((lay.n_tok, hk), F32),
                   jax.ShapeDtypeStruct((lay.n_tok, hv), BF16),
                   jax.ShapeDtypeStruct((lay.n_tok, hv), F32),
                   jax.ShapeDtypeStruct((lay.n_tok, hk), F32),
                   jax.ShapeDtypeStruct((lay.n_tok, hk), F32)],
        out_specs=[pl.BlockSpec((TM, hk), row), pl.BlockSpec((TM, hk), row), pl.BlockSpec((TM, hv), row),
                   pl.BlockSpec((TM, hv), row), pl.BlockSpec((TM, hk), row), pl.BlockSpec((TM, hk), row)],
        scratch_shapes=[s for s in (win_scr,) if s is not None],
        jobs=jobs)


def _block_diag(blocks):
    n = len(blocks)
    zero = jnp.zeros_like(blocks[0])
    return jnp.concatenate(
        [jnp.concatenate([blocks[i] if j == i else zero for j in range(n)], axis=1) for i in range(n)], axis=0)


def _gla_chain_mask(dtype):
    n = GLA_CHAINS * GLA_CHUNK
    r = lax.broadcasted_iota(jnp.int32, (n, n), 0)
    c = lax.broadcasted_iota(jnp.int32, (n, n), 1)
    shift = GLA_CHUNK.bit_length() - 1
    assert 1 << shift == GLA_CHUNK
    r_chain, r_t = jnp.right_shift(r, shift), jnp.bitwise_and(r, GLA_CHUNK - 1)
    c_chain, c_t = jnp.right_shift(c, shift), jnp.bitwise_and(c, GLA_CHUNK - 1)
    direction = 1 - 2 * jnp.bitwise_and(r_chain, 1)
    ordered = jnp.where((c_t - r_t) * direction <= 0, 1.0, 0.0)
    return jnp.where(r_chain == c_chain, ordered, 0.0).astype(dtype)


def _gla_fast_step(refs, st_scr, mask_scr, g, t, accumulate, n_chunks, dk, dv):
    q_ref, k_ref, v_ref, bf_ref, bb_ref, o_ref = refs
    mid = GLA_CHUNK // 2
    slots = []
    for u in range(2):
        qs_l, ks_l, v_l, qi_l, kh_l, dec_l, dst_l, old_l = [], [], [], [], [], [], [], []
        for j in range(2):
            h = 2 * g + j
            kcols = slice(h * dk, (h + 1) * dk)
            vcols = slice(h * dv, (h + 1) * dv)
            for forward, b_ref in ((True, bf_ref), (False, bb_ref)):
                c = (2 * t + u) if forward else (n_chunks - 1 - 2 * t - u)
                rows = pl.ds(pl.multiple_of(c * GLA_CHUNK, GLA_CHUNK), GLA_CHUNK)
                q, k, b, v = q_ref[rows, kcols], k_ref[rows, kcols], b_ref[rows, kcols], v_ref[rows, vcols]
                rho = b[mid:mid + 1]
                q_s = q * jnp.exp(b - rho)
                k_s = k * jnp.exp(rho - b)
                b_exit = b[GLA_CHUNK - 1:GLA_CHUNK] if forward else b[0:1]
                qs_l.append(q_s.astype(BF16))
                ks_l.append(k_s.astype(BF16))
                v_l.append(v)
                qi_l.append((q_s * jnp.exp(rho)).astype(BF16))
                kh_l.append((k_s * jnp.exp(b_exit - rho)).astype(BF16))
                dec_l.append(jnp.exp(b_exit))
                dst_l.append((rows, vcols))
                old_l.append(o_ref[rows, vcols] if accumulate else None)
        v_all = jnp.concatenate(v_l, axis=0)
        a = _dot_nt(jnp.concatenate(qs_l, axis=0), jnp.concatenate(ks_l, axis=0)).astype(BF16) * mask_scr[...]
        slots.append((_dot(a, v_all), _block_diag(qi_l), _block_diag(kh_l), v_all,
                      jnp.concatenate(dec_l, axis=1), dst_l, old_l))
    st = st_scr[g]
    stores = []
    for o_intra, q_bd, k_bd, v_all, decay, dst_l, old_l in slots:
        o_all = o_intra + _dot_nt(q_bd, st.astype(BF16))
        st = st * decay + _dot_tn(v_all, k_bd)
        for ci, ((rows, vcols), old) in enumerate(zip(dst_l, old_l)):
            o = o_all[ci * GLA_CHUNK:(ci + 1) * GLA_CHUNK]
            stores.append((rows, vcols, o if old is None else o + old))
    st_scr[g] = st
    for rows, vcols, o in stores:
        o_ref[rows, vcols] = o


def _gla_chunk_safe(q, k, v, b, st, forward):
    n_sub = GLA_CHUNK // GLA_SUB
    o_rows = []
    qi = (q * jnp.exp(b)).astype(BF16)
    o_inter = _dot_nt(qi, st.astype(BF16))
    t_idx = lax.broadcasted_iota(jnp.int32, (GLA_SUB, 1), 0)
    for blk in range(n_sub):
        rows = slice(blk * GLA_SUB, (blk + 1) * GLA_SUB)
        qb, kb, bb_, vb = q[rows], k[rows], b[rows], v[rows].astype(F32)
        o_blk = o_inter[rows]
        if forward and blk > 0:
            others = slice(0, blk * GLA_SUB)
            rho = b[blk * GLA_SUB - 1:blk * GLA_SUB]
        elif (not forward) and blk < n_sub - 1:
            others = slice((blk + 1) * GLA_SUB, GLA_CHUNK)
            rho = b[(blk + 1) * GLA_SUB:(blk + 1) * GLA_SUB + 1]
        else:
            others = None
        if others is not None:
            q_s = (qb * jnp.exp(bb_ - rho)).astype(BF16)
            k_s = (k[others] * jnp.exp(rho - b[others])).astype(BF16)
            a = _dot_nt(q_s, k_s)
            o_blk = o_blk + _dot(a.astype(BF16), v[others])
        for s in range(GLA_SUB):
            keep = (t_idx >= s) if forward else (t_idx <= s)
            decay = jnp.exp(jnp.where(keep, bb_ - bb_[s:s + 1], NEG_BIG))
            a_col = jnp.sum(qb * kb[s:s + 1] * decay, axis=-1, keepdims=True)
            o_blk = o_blk + a_col * vb[s:s + 1]
        o_rows.append(o_blk)
    b_exit = b[GLA_CHUNK - 1:GLA_CHUNK] if forward else b[0:1]
    k_hat = (k * jnp.exp(b_exit - b)).astype(BF16)
    st_new = st * jnp.exp(b_exit) + _dot_tn(v, k_hat)
    return jnp.concatenate(o_rows, axis=0), st_new


def _gla_scan_kernel(n_chunks, hps, dk, dv, has_init, *refs):
    n_groups = hps // 2
    if has_init:
        (q_ref, k_ref, v_ref, bf_ref, bb_ref, s0f_ref, s0b_ref, o_ref, st_scr, mask_scr) = refs
    else:
        (q_ref, k_ref, v_ref, bf_ref, bb_ref, o_ref, sf_ref, sb_ref, st_scr, mask_scr) = refs
    data_refs = (q_ref, k_ref, v_ref, bf_ref, bb_ref, o_ref)

    def chain_cols(h, forward):
        ci = 2 * (h % 2) + (0 if forward else 1)
        return h // 2, slice(ci * dk, (ci + 1) * dk)

    for h in range(hps):
        for forward in (True, False):
            g, cols = chain_cols(h, forward)
            if has_init:
                st_scr[g, :, cols] = (s0f_ref if forward else s0b_ref)[h].T
            else:
                st_scr[g, :, cols] = jnp.zeros((dv, dk), F32)
    mask_scr[...] = _gla_chain_mask(BF16)

    def safe_step(i, accumulate):
        pending = []
        for h in range(hps):
            kcols = slice(h * dk, (h + 1) * dk)
            vcols = slice(h * dv, (h + 1) * dv)
            for forward, b_ref in ((True, bf_ref), (False, bb_ref)):
                c = i if forward else n_chunks - 1 - i
                rows = pl.ds(pl.multiple_of(c * GLA_CHUNK, GLA_CHUNK), GLA_CHUNK)
                g, cols = chain_cols(h, forward)
                o, st_new = _gla_chunk_safe(q_ref[rows, kcols], k_ref[rows, kcols], v_ref[rows, vcols],
                                            b_ref[rows, kcols], st_scr[g, :, cols], forward)
                if accumulate:
                    o = o + o_ref[rows, vcols]
                pending.append((rows, vcols, g, cols, o, st_new))
        for rows, vcols, g, cols, o, st_new in pending:
            st_scr[g, :, cols] = st_new
            o_ref[rows, vcols] = o

    def fast_step(t, accumulate):
        for g in range(n_groups):
            _gla_fast_step(data_refs, st_scr, mask_scr, g, t, accumulate, n_chunks, dk, dv)

    def run(step, n_steps):
        def first(i, carry):
            step(i, False)
            return carry

        def second(i, carry):
            step(i, True)
            return carry

        lax.fori_loop(0, n_steps // 2, first, 0)
        lax.fori_loop(n_steps // 2, n_steps, second, 0)

    span = jnp.zeros((1, hps * dk), F32)
    for c in range(n_chunks):
        top = slice(c * GLA_CHUNK, c * GLA_CHUNK + 1)
        bottom = slice((c + 1) * GLA_CHUNK - 1, (c + 1) * GLA_CHUNK)
        span = jnp.maximum(span, jnp.maximum(bf_ref[top, :] - bf_ref[bottom, :],
                                             bb_ref[bottom, :] - bb_ref[top, :]))
    bounded = jnp.max(span) < GLA_SAFE_RANGE

    @pl.when(bounded)
    def _():
        run(fast_step, n_chunks // 2)

    @pl.when(jnp.logical_not(bounded))
    def _():
        run(safe_step, n_chunks)

    if not has_init:
        for h in range(hps):
            for forward in (True, False):
                g, cols = chain_cols(h, forward)
                (sf_ref if forward else sb_ref)[h] = st_scr[g, :, cols].T


def _gla_scan(lay, j, q, k, v, bf, bb, init, *, row0, n_seq, seq, hps):
    hk, hv = q.shape[1], v.shape[1]
    dk, dv = hk // GLA_HEADS, hv // GLA_HEADS
    n_chunks = seq // GLA_CHUNK
    groups = GLA_HEADS // hps
    assert n_chunks % 4 == 0 and GLA_HEADS % hps == 0 and hps % 2 == 0
    has_init = init is not None
    blk = lambda b, g: (row0 // seq + b, g)
    kspec = pl.BlockSpec((seq, hps * dk), blk)
    vspec = pl.BlockSpec((seq, hps * dv), blk)
    in_specs = [kspec, kspec, vspec, kspec, kspec]
    args = [q, k, v, bf, bb]
    out_shape = [jax.ShapeDtypeStruct((n_seq * seq, hv), F32)]
    out_specs = [pl.BlockSpec((seq, hps * dv), lambda b, g: (b, g))]
    if has_init:
        s0f, s0b = init
        st_spec = pl.BlockSpec((None, hps, dk, dv), lambda b, g: (b, j * groups + g, 0, 0))
        in_specs += [st_spec, st_spec]
        args += [s0f, s0b]
    else:
        st_shape = jax.ShapeDtypeStruct((n_seq, GLA_HEADS, dk, dv), F32)
        st_spec = pl.BlockSpec((None, hps, dk, dv), lambda b, g: (b, g, 0, 0))
        out_shape += [st_shape, st_shape]
        out_specs += [st_spec, st_spec]
    outs, _ = _run(
        functools.partial(_gla_scan_kernel, n_chunks, hps, dk, dv, has_init),
        name=f"gla_scan_{'latent' if has_init else 'context'}", grid=(n_seq, groups),
        args=args, in_specs=in_specs, out_shape=out_shape, out_specs=out_specs,
        scratch_shapes=[pltpu.VMEM((hps // 2, dv, GLA_CHAINS * dk), F32),
                        pltpu.VMEM((GLA_CHAINS * GLA_CHUNK, GLA_CHAINS * GLA_CHUNK), BF16)])
    return outs


def _rope_tables(lay):
    n_freq = HEAD_DIM // 4
    pos = np.arange(lay.dec_seq)
    freqs = (np.float32(ROPE_THETA) ** (-np.arange(n_freq, dtype=np.float32) / np.float32(n_freq))).astype(np.float32)
    ang_r = (pos // GRID_W).astype(np.float32)[:, None] * freqs
    ang_c = (pos % GRID_W).astype(np.float32)[:, None] * freqs
    cos = np.concatenate([np.cos(ang_r)] * 2 + [np.cos(ang_c)] * 2, axis=-1)
    sin = np.concatenate([-np.sin(ang_r), np.sin(ang_r), -np.sin(ang_c), np.sin(ang_c)], axis=-1)
    cos = np.concatenate([np.ones((TM, HEAD_DIM), np.float32), cos], axis=0).astype(np.float32)
    sin = np.concatenate([np.zeros((TM, HEAD_DIM), np.float32), sin], axis=0).astype(np.float32)
    return jnp.asarray(cos), jnp.asarray(sin)


def kernel(x_prompt, x_sample, c, cache_k, cache_v, state_gla_fwd, state_gla_bwd, c_ctx, w_ada, b_ada, ln_g, ln_b, conv_w_in, conv_w, conv_w_out, attn_w_qkv, attn_q_norm, attn_k_norm, attn_w_o, gla_w_in, gla_w_gate1, gla_w_gate2, gla_b_gate, gla_norm, gla_w_o, ffn_w_in, ffn_w_out):
    batch, seq, d = x_prompt.shape
    dec_batch, dec_seq, _ = x_sample.shape
    depth = w_ada.shape[0]
    lay = _Layout(batch, seq, dec_batch, dec_seq, d, depth)
    n_steps = lay.n_steps
    kv = N_KV_HEADS * HEAD_DIM
    past = cache_k.shape[2]

    cond = jnp.repeat(jnp.concatenate([c_ctx[None, :], c], axis=0), MOD_GROUP, axis=0)
    lng, lnb = ln_g, ln_b
    g1 = jnp.concatenate([gla_w_gate1[:, 0], gla_w_gate1[:, 1]], axis=-1).astype(BF16)
    zeros = jnp.zeros_like(gla_w_gate2[:, 0])
    g2 = jnp.concatenate([jnp.concatenate([gla_w_gate2[:, 0], zeros], axis=-1),
                          jnp.concatenate([zeros, gla_w_gate2[:, 1]], axis=-1)], axis=1).astype(BF16)
    gb = jnp.concatenate([gla_b_gate[:, 0], gla_b_gate[:, 1]], axis=-1)[:, None, :]
    cos_t, sin_t = _rope_tables(lay)

    def mixer_weights(i):
        kind, j = i % N_MIXERS, i // N_MIXERS
        return [[(conv_w_in, j), (conv_w_out, j)], [(attn_w_qkv, j), (attn_w_o, j)], [(gla_w_in, j), (gla_w_o, j)]][kind]

    x = (x_prompt.reshape(lay.n_prompt, d), x_sample.reshape(dec_batch * dec_seq, d))
    mod = _modulation(cond, w_ada, b_ada, 0)
    mix_w = mixer_weights(0)
    new_k, new_v, new_sf, new_sb = [], [], [], []
    y_prompt = y_sample = None
    for i in range(depth):
        kind, j = i % N_MIXERS, i // N_MIXERS
        last = i == depth - 1
        ffn_jobs = [_cast_job(ffn_w_in, i, n_steps), _cast_job(ffn_w_out, i, n_steps)]
        gla_pre = None
        if kind == 0:
            x, jr = _conv_mixer(lay, i, j, x if i == 0 else (x,), mod, lng, lnb, mix_w[0], conv_w, mix_w[1], ffn_jobs)
        elif kind == 1:
            (q, k, v, kf, vf), jr = _attn_qkv(lay, i, j, x, mod, mix_w[0], attn_q_norm[:, None, :],
                                              attn_k_norm[:, None, :], cos_t, sin_t, ffn_jobs)
            new_k.append(kf[:lay.n_prompt].reshape(batch, seq, N_KV_HEADS, HEAD_DIM))
            new_v.append(vf[:lay.n_prompt].reshape(batch, seq, N_KV_HEADS, HEAD_DIM))
            kc = cache_k[:, j].reshape(dec_batch * past, kv).astype(BF16)
            vc = cache_v[:, j].reshape(dec_batch * past, kv).astype(BF16)
            x = _attention(lay, i, q, k, v, None, x, mod, lng, lnb, mix_w[1],
                           row0=0, n_seq=batch, seq=seq, tq=ATTN_TQ)
            x = _attention(lay, i, q, k, v, (kc, vc), x, mod, lng, lnb, mix_w[1],
                           row0=lay.n_prompt, n_seq=dec_batch, seq=dec_seq, tq=ATTN_TQ)
        else:
            (q, k, v, og, bf, bb), jr = _gla_proj(lay, i, j, x, mod, mix_w[0], g1, g2, gb, ffn_jobs)
            s0f = state_gla_fwd.reshape(dec_batch, -1, *state_gla_fwd.shape[3:])
            s0b = state_gla_bwd.reshape(dec_batch, -1, *state_gla_bwd.shape[3:])
            o_ctx, sf, sb = _gla_scan(lay, j, q, k, v, bf, bb, None, row0=0, n_seq=batch, seq=seq,
                                      hps=GLA_HEADS_PER_STEP_CONTEXT)
            o_lat, = _gla_scan(lay, j, q, k, v, bf, bb, (s0f, s0b), row0=lay.n_prompt, n_seq=dec_batch,
                               seq=dec_seq, hps=GLA_HEADS_PER_STEP_LATENT)
            new_sf.append(sf)
            new_sb.append(sb)
            w_o = mix_w[1]
            if isinstance(w_o, tuple):
                w_o = w_o[0][w_o[1]].astype(BF16)
            gla_pre = (o_ctx, o_lat, og, gla_norm[j][None, :], w_o)
        ffn_w = (jr[0][0], jr[1][0])
        next_jobs = []
        if not last:
            next_jobs = [_mod_job(cond, w_ada, b_ada, i + 1, n_steps)]
            next_jobs += [_cast_job(arr, jj, n_steps) for arr, jj in mixer_weights(i + 1)]
        outs, jr = _ffn(lay, i, x, mod, lng, lnb, ffn_w[0], ffn_w[1], split_out=last, gla_pre=gla_pre,
                        jobs=next_jobs)
        if last:
            y_prompt, y_sample = outs
        else:
            x, = outs
            mod = jr[0][0]
            mix_w = [jr[1][0], jr[2][0]]

    def stack_layers(parts):
        return parts[0][:, None] if len(parts) == 1 else jnp.stack(parts, axis=1)

    y_prompt = y_prompt.reshape(batch, seq, d)
    y_sample = y_sample.reshape(dec_batch, dec_seq, d)
    return (y_prompt, y_sample, stack_layers(new_k), stack_layers(new_v),
            stack_layers(new_sf), stack_layers(new_sb))
```

```python
import functools

import jax
import jax.numpy as jnp
import numpy as np
from jax import lax
from jax.experimental import pallas as pl
from jax.experimental.pallas import tpu as pltpu

F32 = jnp.float32
BF16 = jnp.bfloat16

N_MIXERS = 3
CONV_WIDTH = 3
HEAD_DIM = 128
N_KV_HEADS = 2
GRID_W = 64
ROPE_THETA = 10000.0
GLA_HEADS = 4
GLA_TAU = 16.0
GLA_CHUNK = 64
LN_EPS = 1e-5
RMS_EPS = 1e-6

LANES = 128
BF16_SUBLANES = 16
VMEM_LIMIT = 58 * 1024 * 1024

MOD_GROUP = 8
MOD_NT = 1536
TM = 512
SUB_TILES = 2
ATTN_LATENT_TQ = 1024
HALO = BF16_SUBLANES
WEIGHT_CAST_ROWS = 128
GLA_SUB = 16
GLA_SAFE_RANGE = 80.0
GLA_CHAINS = 4
GLA_HEADS_PER_STEP_CONTEXT = 4
GLA_HEADS_PER_STEP_LATENT = 2
NEG_BIG = -1e30


def _dot(a, b):
    return jnp.dot(a, b, preferred_element_type=F32)


def _dot_nt(a, b):
    return lax.dot_general(a, b, (((1,), (1,)), ((), ())), preferred_element_type=F32)


def _dot_tn(a, b):
    return lax.dot_general(a, b, (((0,), (0,)), ((), ())), preferred_element_type=F32)


def _layer_norm(y, g, b):
    mu = jnp.mean(y, axis=-1, keepdims=True)
    yc = y - mu
    var = jnp.mean(yc * yc, axis=-1, keepdims=True)
    return yc * lax.rsqrt(var + LN_EPS) * g + b


def _silu(x):
    return x * jax.nn.sigmoid(x)


def _resident(block_shape, index_map):
    return pl.BlockSpec(block_shape, index_map, pipeline_mode=pl.Buffered(1))


class _Layout:
    def __init__(self, batch, seq, dec_batch, dec_seq, d_model, depth):
        self.batch, self.seq, self.dec_batch, self.dec_seq = batch, seq, dec_batch, dec_seq
        self.d, self.depth = d_model, depth
        self.n_prompt = batch * seq
        self.n_tok = self.n_prompt + dec_batch * dec_seq
        self.n_steps = self.n_tok // TM
        assert self.n_prompt % TM == 0 and dec_seq % TM == 0 and TM % seq == 0
        assert seq & (seq - 1) == 0 and dec_seq & (dec_seq - 1) == 0
        self.alpha = (2.0 * depth) ** 0.25

    def mod_index(self, i, rows):
        r0 = i * rows
        return jnp.where(r0 < self.n_prompt, 0, 1 + (r0 - self.n_prompt) // self.dec_seq)

    def mod_spec(self, tile_of):
        def index(*ids):
            tile, rows = tile_of(*ids)
            return (self.mod_index(tile, rows), 0)

        return pl.BlockSpec((MOD_GROUP, 6 * self.d), index)


def _mod_part(mod_ref, k):
    d = mod_ref.shape[1] // 6
    return mod_ref[0:1, k * d:(k + 1) * d]


class _Job:
    def __init__(self, args, in_specs, out_shapes, out_specs, fn):
        self.args, self.in_specs, self.out_shapes, self.out_specs, self.fn = args, in_specs, out_shapes, out_specs, fn


def _cast_job(w, j, n_steps):
    _, n_rows, n_cols = w.shape
    rows = n_rows // n_steps
    assert rows * n_steps == n_rows and rows % BF16_SUBLANES == 0

    def fn(in_refs, out_refs):
        out_refs[0][...] = in_refs[0][...].astype(BF16)

    return _Job([w], [pl.BlockSpec((None, rows, n_cols), lambda i: (j, i, 0))],
                [jax.ShapeDtypeStruct((n_rows, n_cols), BF16)], [pl.BlockSpec((rows, n_cols), lambda i: (i, 0))], fn)


def _mod_job(cond, w_ada, b_ada, layer, n_steps):
    depth, d, n_out = w_ada.shape
    n_rows = cond.shape[0]
    cols = n_out // n_steps
    assert cols * n_steps == n_out and cols % LANES == 0

    def fn(in_refs, out_refs):
        cond_ref, w_ref, b_ref = in_refs
        out_refs[0][...] = (_dot(_silu(cond_ref[...]).astype(BF16), w_ref[...].astype(BF16))
                            + b_ref[layer:layer + 1, :])

    return _Job([cond, w_ada, b_ada],
                [_resident((n_rows, d), lambda i: (0, 0)),
                 pl.BlockSpec((None, d, cols), lambda i: (layer, 0, i)),
                 pl.BlockSpec((depth, cols), lambda i: (0, i))],
                [jax.ShapeDtypeStruct((n_rows, n_out), F32)], [pl.BlockSpec((n_rows, cols), lambda i: (0, i))], fn)


def _run(kernel_fn, *, name, grid, args, in_specs, out_shape, out_specs, scratch_shapes=(), jobs=(), aliases=None):
    n_in, n_out = len(args), len(out_shape)

    def body(*refs):
        pos = n_in
        job_in = []
        for jb in jobs:
            job_in.append(refs[pos:pos + len(jb.args)])
            pos += len(jb.args)
        main_out = refs[pos:pos + n_out]
        pos += n_out
        job_out = []
        for jb in jobs:
            job_out.append(refs[pos:pos + len(jb.out_shapes)])
            pos += len(jb.out_shapes)
        for jb, ji, jo in zip(jobs, job_in, job_out):
            jb.fn(ji, jo)
        kernel_fn(*refs[:n_in], *main_out, *refs[pos:])

    outs = pl.pallas_call(
        body,
        out_shape=tuple(out_shape) + tuple(s for jb in jobs for s in jb.out_shapes),
        grid=grid,
        in_specs=list(in_specs) + [s for jb in jobs for s in jb.in_specs],
        out_specs=tuple(out_specs) + tuple(s for jb in jobs for s in jb.out_specs),
        scratch_shapes=list(scratch_shapes),
        input_output_aliases=aliases or {},
        compiler_params=pltpu.CompilerParams(dimension_semantics=("arbitrary",) * len(grid),
                                             vmem_limit_bytes=VMEM_LIMIT),
        name=name,
    )(*args, *[a for jb in jobs for a in jb.args])
    main, rest = outs[:n_out], list(outs[n_out:])
    job_results = []
    for jb in jobs:
        job_results.append(tuple(rest[:len(jb.out_shapes)]))
        rest = rest[len(jb.out_shapes):]
    return tuple(main), job_results


def _weight_in(w):
    if isinstance(w, tuple):
        arr, j = w
        _, r, c = arr.shape
        return arr, _resident((None, r, c), lambda *_: (j, 0, 0)), pltpu.VMEM((r, c), BF16)
    r, c = w.shape
    return w, _resident((r, c), lambda *_: (0, 0)), None


def _weight_shape(w):
    return w[0].shape[1:] if isinstance(w, tuple) else w.shape


def _round_weight_once(first_step, w_ref, w_scr):
    n_rows = w_ref.shape[0]
    rows = min(n_rows, WEIGHT_CAST_ROWS)
    assert n_rows % rows == 0

    @pl.when(first_step)
    def _():
        def body(c, carry):
            sl = pl.ds(pl.multiple_of(c * rows, rows), rows)
            w_scr[sl, :] = w_ref[sl, :].astype(BF16)
            return carry

        lax.fori_loop(0, n_rows // rows, body, 0)


def _bf16_weights(first_step, w_refs, scratch):
    scratch = list(scratch)
    out = []
    for w in w_refs:
        if w.dtype == BF16:
            out.append(w)
        else:
            scr = scratch.pop(0)
            _round_weight_once(first_step, w, scr)
            out.append(scr)
    return out, scratch


def _modulation(cond, w_ada, b_ada, layer):
    _, d, n_out = w_ada.shape
    job = _mod_job(cond, w_ada, b_ada, layer, n_out // MOD_NT)
    _, (result,) = _run(lambda: None, name=f"modulation_{layer}", grid=(n_out // MOD_NT,), args=[], in_specs=[],
                        out_shape=[], out_specs=[], jobs=[job])
    return result[0]


def _conv_kernel(lay, layer, split, *refs):
    d = lay.d
    i = pl.program_id(0)
    if split:
        (xca, xpa, xna, xcb, xpb, xnb, mod_ref, lng_ref, lnb_ref, win_ref, cw_ref, wout_ref,
         o_ref, h_scr, uu_scr, *scratch) = refs
        is_context = i * TM < lay.n_prompt
        x = jnp.where(is_context, xca[...], xcb[...])
        x_prev = jnp.where(is_context, xpa[...], xpb[...])
        x_next = jnp.where(is_context, xna[...], xnb[...])
    else:
        (xc_ref, xp_ref, xn_ref, mod_ref, lng_ref, lnb_ref, win_ref, cw_ref, wout_ref,
         o_ref, h_scr, uu_scr, *scratch) = refs
        x, x_prev, x_next = xc_ref[...], xp_ref[...], xn_ref[...]
    (win, wout), _ = _bf16_weights(i == 0, (win_ref, wout_ref), scratch)
    sh, sc, ga = _mod_part(mod_ref, 0), _mod_part(mod_ref, 1), _mod_part(mod_ref, 2)
    one_sc = 1.0 + sc
    h_scr[0:HALO, :] = (x_prev * one_sc + sh).astype(BF16)
    h_scr[HALO:HALO + TM, :] = (x * one_sc + sh).astype(BF16)
    h_scr[HALO + TM:HALO + TM + HALO, :] = (x_next * one_sc + sh).astype(BF16)
    cgu = _dot(h_scr[...], win[:, d:3 * d])
    uu_scr[...] = cgu[:, 0:d] * cgu[:, d:2 * d]
    sub = TM // SUB_TILES
    for s in range(SUB_TILES):
        r0 = s * sub
        bg = _dot(h_scr[HALO + r0:HALO + r0 + sub, :], win[:, 0:d])
        row = i * TM + r0 + lax.broadcasted_iota(jnp.int32, (sub, 1), 0)
        seq_len = jnp.where(row < lay.n_prompt, lay.seq, lay.dec_seq)
        pos = jnp.bitwise_and(row, seq_len - 1)
        u_prev = jnp.where(pos != 0, uu_scr[pl.ds(HALO + r0 - 1, sub), :], 0.0)
        u_next = jnp.where(pos != seq_len - 1, uu_scr[pl.ds(HALO + r0 + 1, sub), :], 0.0)
        y = (u_prev * cw_ref[0:1, :] + uu_scr[pl.ds(HALO + r0, sub), :] * cw_ref[1:2, :]
             + u_next * cw_ref[2:3, :])
        mix = _dot((bg * y).astype(BF16), wout[...])
        o_ref[r0:r0 + sub, :] = _layer_norm(lay.alpha * x[r0:r0 + sub] + ga * mix,
                                            lng_ref[layer, 0:1, :],
                                            lnb_ref[layer, 0:1, :])


def _conv_mixer(lay, layer, j, x_parts, mod, lng, lnb, w_in, cw, w_out, jobs):
    d = lay.d
    per = TM // HALO
    x_specs, x_args = [], []
    tile0 = 0
    for part in x_parts:
        n_tiles = part.shape[0] // TM

        def center(i, t0=tile0, n=n_tiles):
            return (jnp.clip(i - t0, 0, n - 1), 0)

        def prev_halo(i, t0=tile0, n=n_tiles):
            return (jnp.clip((i - t0) * per - 1, 0, n * per - 1), 0)

        def next_halo(i, t0=tile0, n=n_tiles):
            return (jnp.clip((i - t0 + 1) * per, 0, n * per - 1), 0)

        x_specs += [pl.BlockSpec((TM, d), center), pl.BlockSpec((HALO, d), prev_halo),
                    pl.BlockSpec((HALO, d), next_halo)]
        x_args += [part, part, part]
        tile0 += n_tiles
    win_arr, win_spec, win_scr = _weight_in(w_in)
    wout_arr, wout_spec, wout_scr = _weight_in(w_out)
    (x,), job_results = _run(
        functools.partial(_conv_kernel, lay, layer, len(x_parts) == 2),
        name=f"conv_mixer_{layer}", grid=(lay.n_steps,),
        args=x_args + [mod, lng, lnb, win_arr, cw, wout_arr],
        in_specs=x_specs + [
            lay.mod_spec(lambda i: (i, TM)),
            _resident(lng.shape, lambda i: (0, 0, 0)),
            _resident(lnb.shape, lambda i: (0, 0, 0)),
            win_spec,
            _resident((None, CONV_WIDTH, d), lambda i: (j, 0, 0)),
            wout_spec,
        ],
        out_shape=[jax.ShapeDtypeStruct((lay.n_tok, d), F32)],
        out_specs=[pl.BlockSpec((TM, d), lambda i: (i, 0))],
        scratch_shapes=[pltpu.VMEM((TM + 2 * HALO, d), BF16), pltpu.VMEM((TM + 2 * HALO, d), F32)]
        + [s for s in (win_scr, wout_scr) if s is not None],
        jobs=jobs)
    return x, job_results


def _ffn_kernel(lay, layer, split_out, gla_pre, *refs):
    x_ref, mod_ref, lng_ref, lnb_ref, win_ref, wout_ref = refs[:6]
    pos = 6
    if gla_pre:
        oc_ref, ol_ref, og_ref, ng_ref, wo_ref = refs[pos:pos + 5]
        pos += 5
    n_out = 2 if split_out else 1
    out_refs = refs[pos:pos + n_out]
    scratch = refs[pos + n_out:]
    d_ff = wout_ref.shape[0]
    is_context = pl.program_id(0) * TM < lay.n_prompt
    sh, sc, ga = _mod_part(mod_ref, 3), _mod_part(mod_ref, 4), _mod_part(mod_ref, 5)
    ln_g = lng_ref[layer, 1:2, :]
    ln_b = lnb_ref[layer, 1:2, :]
    results = []
    sub = TM // SUB_TILES
    for s in range(SUB_TILES):
        rows = slice(s * sub, (s + 1) * sub)
        x = x_ref[rows, :]
        if gla_pre:
            z_scr, = scratch
            dv = ng_ref.shape[1]
            for h in range(GLA_HEADS):
                sl = slice(h * dv, (h + 1) * dv)
                o = jnp.where(is_context, oc_ref[rows, sl], ol_ref[rows, sl])
                o = o * lax.rsqrt(jnp.mean(o * o, axis=-1, keepdims=True) + RMS_EPS) * ng_ref[...]
                z_scr[rows, sl] = (o * _silu(og_ref[rows, sl])).astype(BF16)
            mix = _dot(z_scr[rows, :], wo_ref[...])
            x = _layer_norm(lay.alpha * x + _mod_part(mod_ref, 2) * mix,
                            lng_ref[layer, 0:1, :], lnb_ref[layer, 0:1, :])
        h = (x * (1.0 + sc) + sh).astype(BF16)
        g = _dot(h, win_ref[:, 0:d_ff])
        u = _dot(h, win_ref[:, d_ff:2 * d_ff])
        a = (_silu(g) * u).astype(BF16)
        y = _dot(a, wout_ref[...])
        results.append((rows, _layer_norm(lay.alpha * x + ga * y, ln_g, ln_b)))
    if split_out:
        @pl.when(is_context)
        def _():
            for rows, res in results:
                out_refs[0][rows, :] = res

        @pl.when(jnp.logical_not(is_context))
        def _():
            for rows, res in results:
                out_refs[1][rows, :] = res
    else:
        for rows, res in results:
            out_refs[0][rows, :] = res


def _ffn(lay, layer, x, mod, lng, lnb, w_in, w_out, *, split_out, gla_pre, jobs):
    d = lay.d
    d_ff = w_out.shape[0]
    row = lambda i: (i, 0)
    n_ctx = lay.n_prompt // TM
    n_lat = lay.n_steps - n_ctx
    ctx_row = lambda i: (jnp.minimum(i, n_ctx - 1), 0)
    lat_row = lambda i: (jnp.clip(i - n_ctx, 0, n_lat - 1), 0)
    args = [x, mod, lng, lnb, w_in, w_out]
    in_specs = [
        pl.BlockSpec((TM, d), row),
        lay.mod_spec(lambda i: (i, TM)),
        _resident(lng.shape, lambda i: (0, 0, 0)),
        _resident(lnb.shape, lambda i: (0, 0, 0)),
        _resident((d, 2 * d_ff), lambda i: (0, 0)),
        _resident((d_ff, d), lambda i: (0, 0)),
    ]
    scratch = []
    if gla_pre is not None:
        o_context, o_latent, og, norm_g, w_o = gla_pre
        hv = og.shape[1]
        args += [o_context, o_latent, og, norm_g, w_o]
        in_specs += [pl.BlockSpec((TM, hv), ctx_row), pl.BlockSpec((TM, hv), lat_row), pl.BlockSpec((TM, hv), row),
                     _resident(norm_g.shape, lambda i: (0, 0)), _resident((hv, d), lambda i: (0, 0))]
        scratch = [pltpu.VMEM((TM, hv), BF16)]
    if split_out:
        out_shape = [jax.ShapeDtypeStruct((lay.n_prompt, d), F32),
                     jax.ShapeDtypeStruct((lay.n_tok - lay.n_prompt, d), F32)]
        out_specs = [pl.BlockSpec((TM, d), ctx_row), pl.BlockSpec((TM, d), lat_row)]
    else:
        out_shape = [jax.ShapeDtypeStruct((lay.n_tok, d), F32)]
        out_specs = [pl.BlockSpec((TM, d), row)]
    return _run(functools.partial(_ffn_kernel, lay, layer, split_out, gla_pre is not None),
                name=f"ffn_{layer}", grid=(lay.n_steps,), args=args, in_specs=in_specs,
                out_shape=out_shape, out_specs=out_specs, scratch_shapes=scratch, jobs=jobs)


def _qkv_kernel(lay, n_heads, x_ref, mod_ref, w_ref, qg_ref, kg_ref, cos_ref, sin_ref,
                q_ref, k_ref, v_ref, kf_ref, vf_ref, *scratch):
    (w,), _ = _bf16_weights(pl.program_id(0) == 0, (w_ref,), scratch)
    sh, sc = _mod_part(mod_ref, 0), _mod_part(mod_ref, 1)
    lane = lax.broadcasted_iota(jnp.int32, (1, HEAD_DIM), 1)
    first_half = jnp.bitwise_and(lane, HEAD_DIM // 4) == 0
    q_gain = qg_ref[...] * (HEAD_DIM ** -0.5)
    ones = jnp.ones((HEAD_DIM, HEAD_DIM), BF16)
    k0 = n_heads * HEAD_DIM
    v0 = k0 + N_KV_HEADS * HEAD_DIM
    for s in range(SUB_TILES):
        rows = slice(s * (TM // SUB_TILES), (s + 1) * (TM // SUB_TILES))
        h = (x_ref[rows, :] * (1.0 + sc) + sh).astype(BF16)
        qkv = _dot(h, w[...])
        cos, sin = cos_ref[rows, :], sin_ref[rows, :]

        def norm_rope(t, g):
            sq = t * t
            sq_hi = sq.astype(BF16)
            sq_lo = (sq - sq_hi.astype(F32)).astype(BF16)
            mean_sq = (_dot(sq_hi, ones) + _dot(sq_lo, ones)) * (1.0 / HEAD_DIM)
            t = t * lax.rsqrt(mean_sq + RMS_EPS) * g
            partner = jnp.where(first_half,
                                pltpu.roll(t, HEAD_DIM - HEAD_DIM // 4, axis=1),
                                pltpu.roll(t, HEAD_DIM // 4, axis=1))
            return t * cos + partner * sin

        for hq in range(n_heads):
            sl = slice(hq * HEAD_DIM, (hq + 1) * HEAD_DIM)
            q_ref[rows, sl] = norm_rope(qkv[:, sl], q_gain).astype(BF16)
        for hk in range(N_KV_HEADS):
            sl = slice(hk * HEAD_DIM, (hk + 1) * HEAD_DIM)
            kh = norm_rope(qkv[:, k0 + hk * HEAD_DIM:k0 + (hk + 1) * HEAD_DIM], kg_ref[...])
            kf_ref[rows, sl] = kh
            k_ref[rows, sl] = kh.astype(BF16)
        v = qkv[:, v0:v0 + N_KV_HEADS * HEAD_DIM]
        vf_ref[rows, :] = v
        v_ref[rows, :] = v.astype(BF16)


def _attn_qkv(lay, layer, j, x, mod, w_qkv, q_gain, k_gain, cos_t, sin_t, jobs):
    d = lay.d
    n_qkv = _weight_shape(w_qkv)[1]
    kv = N_KV_HEADS * HEAD_DIM
    n_heads = (n_qkv - 2 * kv) // HEAD_DIM
    row = lambda i: (i, 0)

    def rope_row(i):
        r0 = i * TM
        return (jnp.where(r0 < lay.n_prompt, 0, 1 + ((r0 - lay.n_prompt) % lay.dec_seq) // TM), 0)

    w_arr, w_spec, w_scr = _weight_in(w_qkv)
    return _run(
        functools.partial(_qkv_kernel, lay, n_heads),
        name=f"attn_qkv_{layer}", grid=(lay.n_steps,),
        args=[x, mod, w_arr, q_gain, k_gain, cos_t, sin_t],
        in_specs=[
            pl.BlockSpec((TM, d), row),
            lay.mod_spec(lambda i: (i, TM)),
            w_spec,
            _resident((None, 1, HEAD_DIM), lambda i: (j, 0, 0)),
            _resident((None, 1, HEAD_DIM), lambda i: (j, 0, 0)),
            pl.BlockSpec((TM, HEAD_DIM), rope_row),
            pl.BlockSpec((TM, HEAD_DIM), rope_row),
        ],
        out_shape=[jax.ShapeDtypeStruct((lay.n_tok, n_heads * HEAD_DIM), BF16),
                   jax.ShapeDtypeStruct((lay.n_tok, kv), BF16),
                   jax.ShapeDtypeStruct((lay.n_tok, kv), BF16),
                   jax.ShapeDtypeStruct((lay.n_tok, kv), F32),
                   jax.ShapeDtypeStruct((lay.n_tok, kv), F32)],
        out_specs=[pl.BlockSpec((TM, n_heads * HEAD_DIM), row), pl.BlockSpec((TM, kv), row),
                   pl.BlockSpec((TM, kv), row), pl.BlockSpec((TM, kv), row), pl.BlockSpec((TM, kv), row)],
        scratch_shapes=[s for s in (w_scr,) if s is not None],
        jobs=jobs)


def _attn_kernel(lay, layer, n_heads, has_cache, seqs, *refs):
    if has_cache:
        (q_ref, k_ref, v_ref, kc_ref, vc_ref, x_ref, mod_ref, lng_ref, lnb_ref, wo_ref, o_ref, att_scr,
         *scratch) = refs
    else:
        (q_ref, k_ref, v_ref, x_ref, mod_ref, lng_ref, lnb_ref, wo_ref, o_ref, att_scr, *scratch) = refs
    (wo,), _ = _bf16_weights((pl.program_id(0) == 0) & (pl.program_id(1) == 0), (wo_ref,), scratch)
    group = n_heads // N_KV_HEADS
    q_rows = q_ref.shape[0] // seqs
    k_rows = k_ref.shape[0] // seqs
    for sq in range(seqs):
        rq = slice(sq * q_rows, (sq + 1) * q_rows)
        rk = slice(sq * k_rows, (sq + 1) * k_rows)
        for hk in range(N_KV_HEADS):
            ksl = slice(hk * HEAD_DIM, (hk + 1) * HEAD_DIM)
            v_ext = jnp.concatenate([v_ref[rk, ksl], jnp.ones((k_rows, HEAD_DIM), BF16)], axis=1)
            if has_cache:
                vc_ext = jnp.concatenate([vc_ref[:, ksl], jnp.ones((vc_ref.shape[0], HEAD_DIM), BF16)], axis=1)
            for g in range(group):
                hq = hk * group + g
                qsl = slice(hq * HEAD_DIM, (hq + 1) * HEAD_DIM)
                qh = q_ref[rq, qsl]
                s = _dot_nt(qh, k_ref[rk, ksl])
                m = jnp.max(s, axis=-1, keepdims=True)
                if has_cache:
                    s_c = _dot_nt(qh, kc_ref[:, ksl])
                    m = jnp.maximum(m, jnp.max(s_c, axis=-1, keepdims=True))
                o = _dot(jnp.exp(s - m).astype(BF16), v_ext)
                if has_cache:
                    o = o + _dot(jnp.exp(s_c - m).astype(BF16), vc_ext)
                att_scr[rq, qsl] = (o[:, 0:HEAD_DIM] / o[:, HEAD_DIM:2 * HEAD_DIM]).astype(BF16)
    ga = _mod_part(mod_ref, 2)
    mix = _dot(att_scr[...], wo[...])
    o_ref[...] = _layer_norm(lay.alpha * x_ref[...] + ga * mix,
                             lng_ref[layer, 0:1, :], lnb_ref[layer, 0:1, :])


def _attention(lay, layer, q, k, v, cache, x, mod, lng, lnb, w_o, *, row0, n_seq, seq, tq):
    d = lay.d
    n_heads = q.shape[1] // HEAD_DIM
    kv = N_KV_HEADS * HEAD_DIM
    has_cache = cache is not None
    seqs = max(tq // seq, 1)
    assert not (has_cache and seqs > 1) and n_seq % seqs == 0
    n_seq, seq = n_seq // seqs, seq * seqs
    q_blocks = seq // tq
    tile = lambda b, i: (row0 // tq + b * q_blocks + i, 0)
    seq_blk = lambda b, i: (row0 // seq + b, 0)
    in_specs = [pl.BlockSpec((tq, n_heads * HEAD_DIM), tile),
                pl.BlockSpec((seq, kv), seq_blk), pl.BlockSpec((seq, kv), seq_blk)]
    args = [q, k, v]
    if has_cache:
        kc, vc = cache
        past = kc.shape[0] // n_seq
        in_specs += [pl.BlockSpec((past, kv), lambda b, i: (b, 0))] * 2
        args += [kc, vc]
    x_index = len(args)
    wo_arr, wo_spec, wo_scr = _weight_in(w_o)
    in_specs += [
        pl.BlockSpec((tq, d), tile),
        lay.mod_spec(lambda b, i: (row0 // tq + b * q_blocks + i, tq)),
        _resident(lng.shape, lambda b, i: (0, 0, 0)),
        _resident(lnb.shape, lambda b, i: (0, 0, 0)),
        wo_spec,
    ]
    args += [x, mod, lng, lnb, wo_arr]
    (x,), _ = _run(
        functools.partial(_attn_kernel, lay, layer, n_heads, has_cache, seqs),
        name=f"attention_{layer}_{'latent' if has_cache else 'context'}", grid=(n_seq, q_blocks),
        args=args, in_specs=in_specs,
        out_shape=[jax.ShapeDtypeStruct((lay.n_tok, d), F32)], out_specs=[pl.BlockSpec((tq, d), tile)],
        scratch_shapes=[pltpu.VMEM((tq, n_heads * HEAD_DIM), BF16)] + [s for s in (wo_scr,) if s is not None],
        aliases={x_index: 0})
    return x


def _split3(x):
    hi = x.astype(BF16)
    r = x - hi.astype(F32)
    mid = r.astype(BF16)
    lo = (r - mid.astype(F32)).astype(BF16)
    return hi, mid, lo


def _gla_proj_kernel(lay, hk, hv, x_ref, mod_ref, win_ref, w1_ref, w2_ref, bg_ref,
                     q_ref, k_ref, v_ref, og_ref, bf_ref, bb_ref, *scratch):
    (win,), _ = _bf16_weights(pl.program_id(0) == 0, (win_ref,), scratch)
    sh, sc = _mod_part(mod_ref, 0), _mod_part(mod_ref, 1)
    dk = hk // GLA_HEADS
    r = lax.broadcasted_iota(jnp.int32, (GLA_CHUNK, GLA_CHUNK), 0)
    c = lax.broadcasted_iota(jnp.int32, (GLA_CHUNK, GLA_CHUNK), 1)
    lower = jnp.concatenate([jnp.where(c <= r, 1.0, 0.0).astype(BF16)] * 3, axis=1)
    upper = jnp.concatenate([jnp.where(c >= r, 1.0, 0.0).astype(BF16)] * 3, axis=1)
    h = (x_ref[...] * (1.0 + sc) + sh).astype(BF16)
    proj = _dot(h, win[...])
    q_ref[...] = proj[:, 0:hk] * (dk ** -0.5)
    k_ref[...] = proj[:, hk:2 * hk]
    v_ref[...] = proj[:, 2 * hk:2 * hk + hv].astype(BF16)
    og_ref[...] = proj[:, 2 * hk + hv:2 * hk + 2 * hv]
    z = _dot(_dot(h, w1_ref[...]).astype(BF16), w2_ref[...]) + bg_ref[...]
    log_gate = (jnp.minimum(z, 0.0) - jnp.log1p(jnp.exp(-jnp.abs(z)))) * (1.0 / GLA_TAU)
    for ch in range(TM // GLA_CHUNK):
        rows = slice(ch * GLA_CHUNK, (ch + 1) * GLA_CHUNK)
        bf_ref[rows, :] = _dot(lower, jnp.concatenate(_split3(log_gate[rows, 0:hk]), axis=0))
        bb_ref[rows, :] = _dot(upper, jnp.concatenate(_split3(log_gate[rows, hk:2 * hk]), axis=0))


def _gla_proj(lay, layer, j, x, mod, w_in, w1, w2, b_gate, jobs):
    d = lay.d
    n_in = _weight_shape(w_in)[1]
    hk = w2.shape[2] // 2
    hv = (n_in - 2 * hk) // 2
    rank2 = w1.shape[2]
    row = lambda i: (i, 0)
    win_arr, win_spec, win_scr = _weight_in(w_in)
    return _run(
        functools.partial(_gla_proj_kernel, lay, hk, hv),
        name=f"gla_proj_{layer}", grid=(lay.n_steps,),
        args=[x, mod, win_arr, w1, w2, b_gate],
        in_specs=[
            pl.BlockSpec((TM, d), row),
            lay.mod_spec(lambda i: (i, TM)),
            win_spec,
            _resident((None, d, rank2), lambda i: (j, 0, 0)),
            _resident((None, rank2, 2 * hk), lambda i: (j, 0, 0)),
            _resident((None, 1, 2 * hk), lambda i: (j, 0, 0)),
        ],
        out_shape=[jax.ShapeDtypeStruct((lay.n_tok, hk), F32),
                   jax.ShapeDtypeStruct((lay.n_tok, hk), F32),
                   jax.ShapeDtypeStruct((lay.n_tok, hv), BF16),
                   jax.ShapeDtypeStruct((lay.n_tok, hv), F32),
                   jax.ShapeDtypeStruct((lay.n_tok, hk), F32),
                   jax.ShapeDtypeStruct((lay.n_tok, hk), F32)],
        out_specs=[pl.BlockSpec((TM, hk), row), pl.BlockSpec((TM, hk), row), pl.BlockSpec((TM, hv), row),
                   pl.BlockSpec((TM, hv), row), pl.BlockSpec((TM, hk), row), pl.BlockSpec((TM, hk), row)],
        scratch_shapes=[s for s in (win_scr,) if s is not None],
        jobs=jobs)


def _block_diag(blocks):
    n = len(blocks)
    zero = jnp.zeros_like(blocks[0])
    return jnp.concatenate(
        [jnp.concatenate([blocks[i] if j == i else zero for j in range(n)], axis=1) for i in range(n)], axis=0)


def _gla_chain_mask(dtype):
    n = GLA_CHAINS * GLA_CHUNK
    r = lax.broadcasted_iota(jnp.int32, (n, n), 0)
    c = lax.broadcasted_iota(jnp.int32, (n, n), 1)
    shift = GLA_CHUNK.bit_length() - 1
    assert 1 << shift == GLA_CHUNK
    r_chain, r_t = jnp.right_shift(r, shift), jnp.bitwise_and(r, GLA_CHUNK - 1)
    c_chain, c_t = jnp.right_shift(c, shift), jnp.bitwise_and(c, GLA_CHUNK - 1)
    direction = 1 - 2 * jnp.bitwise_and(r_chain, 1)
    ordered = jnp.where((c_t - r_t) * direction <= 0, 1.0, 0.0)
    return jnp.where(r_chain == c_chain, ordered, 0.0).astype(dtype)


def _gla_fast_step(refs, st_scr, mask_scr, g, t, accumulate, n_chunks, dk, dv):
    q_ref, k_ref, v_ref, bf_ref, bb_ref, o_ref = refs
    mid = GLA_CHUNK // 2
    slots = []
    for u in range(2):
        qs_l, ks_l, v_l, qi_l, kh_l, dec_l, dst_l, old_l = [], [], [], [], [], [], [], []
        for j in range(2):
            h = 2 * g + j
            kcols = slice(h * dk, (h + 1) * dk)
            vcols = slice(h * dv, (h + 1) * dv)
            for forward, b_ref in ((True, bf_ref), (False, bb_ref)):
                c = (2 * t + u) if forward else (n_chunks - 1 - 2 * t - u)
                rows = pl.ds(pl.multiple_of(c * GLA_CHUNK, GLA_CHUNK), GLA_CHUNK)
                q, k, b, v = q_ref[rows, kcols], k_ref[rows, kcols], b_ref[rows, kcols], v_ref[rows, vcols]
                rho = b[mid:mid + 1]
                q_s = q * jnp.exp(b - rho)
                k_s = k * jnp.exp(rho - b)
                b_exit = b[GLA_CHUNK - 1:GLA_CHUNK] if forward else b[0:1]
                qs_l.append(q_s.astype(BF16))
                ks_l.append(k_s.astype(BF16))
                v_l.append(v)
                qi_l.append((q_s * jnp.exp(rho)).astype(BF16))
                kh_l.append((k_s * jnp.exp(b_exit - rho)).astype(BF16))
                dec_l.append(jnp.exp(b_exit))
                dst_l.append((rows, vcols))
                old_l.append(o_ref[rows, vcols] if accumulate else None)
        v_all = jnp.concatenate(v_l, axis=0)
        a = _dot_nt(jnp.concatenate(qs_l, axis=0), jnp.concatenate(ks_l, axis=0)).astype(BF16) * mask_scr[...]
        slots.append((_dot(a, v_all), _block_diag(qi_l), _block_diag(kh_l), v_all,
                      jnp.concatenate(dec_l, axis=1), dst_l, old_l))
    st = st_scr[g]
    stores = []
    for o_intra, q_bd, k_bd, v_all, decay, dst_l, old_l in slots:
        o_all = o_intra + _dot_nt(q_bd, st.astype(BF16))
        st = st * decay + _dot_tn(v_all, k_bd)
        for ci, ((rows, vcols), old) in enumerate(zip(dst_l, old_l)):
            o = o_all[ci * GLA_CHUNK:(ci + 1) * GLA_CHUNK]
            stores.append((rows, vcols, o if old is None else o + old))
    st_scr[g] = st
    for rows, vcols, o in stores:
        o_ref[rows, vcols] = o


def _gla_chunk_safe(q, k, v, b, st, forward):
    n_sub = GLA_CHUNK // GLA_SUB
    o_rows = []
    qi = (q * jnp.exp(b)).astype(BF16)
    o_inter = _dot_nt(qi, st.astype(BF16))
    t_idx = lax.broadcasted_iota(jnp.int32, (GLA_SUB, 1), 0)
    for blk in range(n_sub):
        rows = slice(blk * GLA_SUB, (blk + 1) * GLA_SUB)
        qb, kb, bb_, vb = q[rows], k[rows], b[rows], v[rows].astype(F32)
        o_blk = o_inter[rows]
        if forward and blk > 0:
            others = slice(0, blk * GLA_SUB)
            rho = b[blk * GLA_SUB - 1:blk * GLA_SUB]
        elif (not forward) and blk < n_sub - 1:
            others = slice((blk + 1) * GLA_SUB, GLA_CHUNK)
            rho = b[(blk + 1) * GLA_SUB:(blk + 1) * GLA_SUB + 1]
        else:
            others = None
        if others is not None:
            q_s = (qb * jnp.exp(bb_ - rho)).astype(BF16)
            k_s = (k[others] * jnp.exp(rho - b[others])).astype(BF16)
            a = _dot_nt(q_s, k_s)
            o_blk = o_blk + _dot(a.astype(BF16), v[others])
        for s in range(GLA_SUB):
            keep = (t_idx >= s) if forward else (t_idx <= s)
            decay = jnp.exp(jnp.where(keep, bb_ - bb_[s:s + 1], NEG_BIG))
            a_col = jnp.sum(qb * kb[s:s + 1] * decay, axis=-1, keepdims=True)
            o_blk = o_blk + a_col * vb[s:s + 1]
        o_rows.append(o_blk)
    b_exit = b[GLA_CHUNK - 1:GLA_CHUNK] if forward else b[0:1]
    k_hat = (k * jnp.exp(b_exit - b)).astype(BF16)
    st_new = st * jnp.exp(b_exit) + _dot_tn(v, k_hat)
    return jnp.concatenate(o_rows, axis=0), st_new


def _gla_scan_kernel(n_chunks, hps, dk, dv, has_init, *refs):
    n_groups = hps // 2
    if has_init:
        (q_ref, k_ref, v_ref, bf_ref, bb_ref, s0f_ref, s0b_ref, o_ref, st_scr, mask_scr) = refs
    else:
        (q_ref, k_ref, v_ref, bf_ref, bb_ref, o_ref, sf_ref, sb_ref, st_scr, mask_scr) = refs
    data_refs = (q_ref, k_ref, v_ref, bf_ref, bb_ref, o_ref)

    def chain_cols(h, forward):
        ci = 2 * (h % 2) + (0 if forward else 1)
        return h // 2, slice(ci * dk, (ci + 1) * dk)

    for h in range(hps):
        for forward in (True, False):
            g, cols = chain_cols(h, forward)
            if has_init:
                st_scr[g, :, cols] = (s0f_ref if forward else s0b_ref)[h].T
            else:
                st_scr[g, :, cols] = jnp.zeros((dv, dk), F32)
    mask_scr[...] = _gla_chain_mask(BF16)

    def safe_step(i, accumulate):
        pending = []
        for h in range(hps):
            kcols = slice(h * dk, (h + 1) * dk)
            vcols = slice(h * dv, (h + 1) * dv)
            for forward, b_ref in ((True, bf_ref), (False, bb_ref)):
                c = i if forward else n_chunks - 1 - i
                rows = pl.ds(pl.multiple_of(c * GLA_CHUNK, GLA_CHUNK), GLA_CHUNK)
                g, cols = chain_cols(h, forward)
                o, st_new = _gla_chunk_safe(q_ref[rows, kcols], k_ref[rows, kcols], v_ref[rows, vcols],
                                            b_ref[rows, kcols], st_scr[g, :, cols], forward)
                if accumulate:
                    o = o + o_ref[rows, vcols]
                pending.append((rows, vcols, g, cols, o, st_new))
        for rows, vcols, g, cols, o, st_new in pending:
            st_scr[g, :, cols] = st_new
            o_ref[rows, vcols] = o

    def fast_step(t, accumulate):
        for g in range(n_groups):
            _gla_fast_step(data_refs, st_scr, mask_scr, g, t, accumulate, n_chunks, dk, dv)

    def run(step, n_steps):
        def first(i, carry):
            step(i, False)
            return carry

        def second(i, carry):
            step(i, True)
            return carry

        unroll = 2 if (n_steps // 2) % 2 == 0 else 1
        lax.fori_loop(0, n_steps // 2, first, 0, unroll=unroll)
        lax.fori_loop(n_steps // 2, n_steps, second, 0, unroll=unroll)

    span = jnp.zeros((1, hps * dk), F32)
    for c in range(n_chunks):
        top = slice(c * GLA_CHUNK, c * GLA_CHUNK + 1)
        bottom = slice((c + 1) * GLA_CHUNK - 1, (c + 1) * GLA_CHUNK)
        span = jnp.maximum(span, jnp.maximum(bf_ref[top, :] - bf_ref[bottom, :],
                                             bb_ref[bottom, :] - bb_ref[top, :]))
    bounded = jnp.max(span) < GLA_SAFE_RANGE

    @pl.when(bounded)
    def _():
        run(fast_step, n_chunks // 2)

    @pl.when(jnp.logical_not(bounded))
    def _():
        run(safe_step, n_chunks)

    if not has_init:
        for h in range(hps):
            for forward in (True, False):
                g, cols = chain_cols(h, forward)
                (sf_ref if forward else sb_ref)[h] = st_scr[g, :, cols].T


def _gla_scan(lay, j, q, k, v, bf, bb, init, *, row0, n_seq, seq, hps):
    hk, hv = q.shape[1], v.shape[1]
    dk, dv = hk // GLA_HEADS, hv // GLA_HEADS
    n_chunks = seq // GLA_CHUNK
    groups = GLA_HEADS // hps
    assert n_chunks % 4 == 0 and GLA_HEADS % hps == 0 and hps % 2 == 0
    has_init = init is not None
    blk = lambda b, g: (row0 // seq + b, g)
    kspec = pl.BlockSpec((seq, hps * dk), blk)
    vspec = pl.BlockSpec((seq, hps * dv), blk)
    in_specs = [kspec, kspec, vspec, kspec, kspec]
    args = [q, k, v, bf, bb]
    out_shape = [jax.ShapeDtypeStruct((n_seq * seq, hv), F32)]
    out_specs = [pl.BlockSpec((seq, hps * dv), lambda b, g: (b, g))]
    if has_init:
        s0f, s0b = init
        st_spec = pl.BlockSpec((None, hps, dk, dv), lambda b, g: (b, j * groups + g, 0, 0))
        in_specs += [st_spec, st_spec]
        args += [s0f, s0b]
    else:
        st_shape = jax.ShapeDtypeStruct((n_seq, GLA_HEADS, dk, dv), F32)
        st_spec = pl.BlockSpec((None, hps, dk, dv), lambda b, g: (b, g, 0, 0))
        out_shape += [st_shape, st_shape]
        out_specs += [st_spec, st_spec]
    outs, _ = _run(
        functools.partial(_gla_scan_kernel, n_chunks, hps, dk, dv, has_init),
        name=f"gla_scan_{'latent' if has_init else 'context'}", grid=(n_seq, groups),
        args=args, in_specs=in_specs, out_shape=out_shape, out_specs=out_specs,
        scratch_shapes=[pltpu.VMEM((hps // 2, dv, GLA_CHAINS * dk), F32),
                        pltpu.VMEM((GLA_CHAINS * GLA_CHUNK, GLA_CHAINS * GLA_CHUNK), BF16)])
    return outs


def _rope_tables(lay):
    n_freq = HEAD_DIM // 4
    pos = np.arange(lay.dec_seq)
    freqs = (np.float32(ROPE_THETA) ** (-np.arange(n_freq, dtype=np.float32) / np.float32(n_freq))).astype(np.float32)
    ang_r = (pos // GRID_W).astype(np.float32)[:, None] * freqs
    ang_c = (pos % GRID_W).astype(np.float32)[:, None] * freqs
    cos = np.concatenate([np.cos(ang_r)] * 2 + [np.cos(ang_c)] * 2, axis=-1)
    sin = np.concatenate([-np.sin(ang_r), np.sin(ang_r), -np.sin(ang_c), np.sin(ang_c)], axis=-1)
    cos = np.concatenate([np.ones((TM, HEAD_DIM), np.float32), cos], axis=0).astype(np.float32)
    sin = np.concatenate([np.zeros((TM, HEAD_DIM), np.float32), sin], axis=0).astype(np.float32)
    return jnp.asarray(cos), jnp.asarray(sin)


def kernel(x_prompt, x_sample, c, cache_k, cache_v, state_gla_fwd, state_gla_bwd, c_ctx, w_ada, b_ada, ln_g, ln_b, conv_w_in, conv_w, conv_w_out, attn_w_qkv, attn_q_norm, attn_k_norm, attn_w_o, gla_w_in, gla_w_gate1, gla_w_gate2, gla_b_gate, gla_norm, gla_w_o, ffn_w_in, ffn_w_out):
    batch, seq, d = x_prompt.shape
    dec_batch, dec_seq, _ = x_sample.shape
    depth = w_ada.shape[0]
    lay = _Layout(batch, seq, dec_batch, dec_seq, d, depth)
    n_steps = lay.n_steps
    kv = N_KV_HEADS * HEAD_DIM
    past = cache_k.shape[2]

    cond = jnp.repeat(jnp.concatenate([c_ctx[None, :], c], axis=0), MOD_GROUP, axis=0)
    lng, lnb = ln_g, ln_b
    g1 = jnp.concatenate([gla_w_gate1[:, 0], gla_w_gate1[:, 1]], axis=-1).astype(BF16)
    zeros = jnp.zeros_like(gla_w_gate2[:, 0])
    g2 = jnp.concatenate([jnp.concatenate([gla_w_gate2[:, 0], zeros], axis=-1),
                          jnp.concatenate([zeros, gla_w_gate2[:, 1]], axis=-1)], axis=1).astype(BF16)
    gb = jnp.concatenate([gla_b_gate[:, 0], gla_b_gate[:, 1]], axis=-1)[:, None, :]
    cos_t, sin_t = _rope_tables(lay)

    def mixer_weights(i):
        kind, j = i % N_MIXERS, i // N_MIXERS
        return [[(conv_w_in, j), (conv_w_out, j)], [(attn_w_qkv, j), (attn_w_o, j)], [(gla_w_in, j), (gla_w_o, j)]][kind]

    x = (x_prompt.reshape(lay.n_prompt, d), x_sample.reshape(dec_batch * dec_seq, d))
    mod = _modulation(cond, w_ada, b_ada, 0)
    mix_w = mixer_weights(0)
    new_k, new_v, new_sf, new_sb = [], [], [], []
    y_prompt = y_sample = None
    for i in range(depth):
        kind, j = i % N_MIXERS, i // N_MIXERS
        last = i == depth - 1
        ffn_jobs = [_cast_job(ffn_w_in, i, n_steps), _cast_job(ffn_w_out, i, n_steps)]
        gla_pre = None
        if kind == 0:
            x, jr = _conv_mixer(lay, i, j, x if i == 0 else (x,), mod, lng, lnb, mix_w[0], conv_w, mix_w[1], ffn_jobs)
        elif kind == 1:
            (q, k, v, kf, vf), jr = _attn_qkv(lay, i, j, x, mod, mix_w[0], attn_q_norm[:, None, :],
                                              attn_k_norm[:, None, :], cos_t, sin_t, ffn_jobs)
            new_k.append(kf[:lay.n_prompt].reshape(batch, seq, N_KV_HEADS, HEAD_DIM))
            new_v.append(vf[:lay.n_prompt].reshape(batch, seq, N_KV_HEADS, HEAD_DIM))
            kc = cache_k[:, j].reshape(dec_batch * past, kv).astype(BF16)
            vc = cache_v[:, j].reshape(dec_batch * past, kv).astype(BF16)
            x = _attention(lay, i, q, k, v, None, x, mod, lng, lnb, mix_w[1],
                           row0=0, n_seq=batch, seq=seq, tq=TM)
            x = _attention(lay, i, q, k, v, (kc, vc), x, mod, lng, lnb, mix_w[1],
                           row0=lay.n_prompt, n_seq=dec_batch, seq=dec_seq, tq=ATTN_LATENT_TQ)
        else:
            (q, k, v, og, bf, bb), jr = _gla_proj(lay, i, j, x, mod, mix_w[0], g1, g2, gb, ffn_jobs)
            s0f = state_gla_fwd.reshape(dec_batch, -1, *state_gla_fwd.shape[3:])
            s0b = state_gla_bwd.reshape(dec_batch, -1, *state_gla_bwd.shape[3:])
            o_ctx, sf, sb = _gla_scan(lay, j, q, k, v, bf, bb, None, row0=0, n_seq=batch, seq=seq,
                                      hps=GLA_HEADS_PER_STEP_CONTEXT)
            o_lat, = _gla_scan(lay, j, q, k, v, bf, bb, (s0f, s0b), row0=lay.n_prompt, n_seq=dec_batch,
                               seq=dec_seq, hps=GLA_HEADS_PER_STEP_LATENT)
            new_sf.append(sf)
            new_sb.append(sb)
            w_o = mix_w[1]
            if isinstance(w_o, tuple):
                w_o = w_o[0][w_o[1]].astype(BF16)
            gla_pre = (o_ctx, o_lat, og, gla_norm[j][None, :], w_o)
        ffn_w = (jr[0][0], jr[1][0])
        next_jobs = []
        if not last:
            next_jobs = [_mod_job(cond, w_ada, b_ada, i + 1, n_steps)]
            next_jobs += [_cast_job(arr, jj, n_steps) for arr, jj in mixer_weights(i + 1)]
        outs, jr = _ffn(lay, i, x, mod, lng, lnb, ffn_w[0], ffn_w[1], split_out=last, gla_pre=gla_pre,
                        jobs=next_jobs)
        if last:
            y_prompt, y_sample = outs
        else:
            x, = outs
            mod = jr[0][0]
            mix_w = [jr[1][0], jr[2][0]]

    def stack_layers(parts):
        return parts[0][:, None] if len(parts) == 1 else jnp.stack(parts, axis=1)

    y_prompt = y_prompt.reshape(batch, seq, d)
    y_sample = y_sample.reshape(dec_batch, dec_seq, d)
    return (y_prompt, y_sample, stack_layers(new_k), stack_layers(new_v),
            stack_layers(new_sf), stack_layers(new_sb))
```

```python
import functools

import jax
import jax.numpy as jnp
import numpy as np
from jax import lax
from jax.experimental import pallas as pl
from jax.experimental.pallas import tpu as pltpu

F32 = jnp.float32
BF16 = jnp.bfloat16

N_MIXERS = 3
CONV_WIDTH = 3
HEAD_DIM = 128
N_KV_HEADS = 2
GRID_W = 64
ROPE_THETA = 10000.0
GLA_HEADS = 4
GLA_TAU = 16.0
GLA_CHUNK = 64
LN_EPS = 1e-5
RMS_EPS = 1e-6

LANES = 128
BF16_SUBLANES = 16
VMEM_LIMIT = 58 * 1024 * 1024

MOD_GROUP = 8
MOD_NT = 1536
TM = 512
SUB_TILES = 2
ATTN_LATENT_TQ = 1024
HALO = BF16_SUBLANES
WEIGHT_CAST_ROWS = 128
GLA_SUB = 16
GLA_SAFE_RANGE = 80.0
GLA_CHAINS = 4
GLA_HEADS_PER_STEP_CONTEXT = 4
GLA_HEADS_PER_STEP_LATENT = 2
NEG_BIG = -1e30


def _dot(a, b):
    return jnp.dot(a, b, preferred_element_type=F32)


def _dot_nt(a, b):
    return lax.dot_general(a, b, (((1,), (1,)), ((), ())), preferred_element_type=F32)


def _dot_tn(a, b):
    return lax.dot_general(a, b, (((0,), (0,)), ((), ())), preferred_element_type=F32)


def _layer_norm(y, g, b):
    mu = jnp.mean(y, axis=-1, keepdims=True)
    yc = y - mu
    var = jnp.mean(yc * yc, axis=-1, keepdims=True)
    return yc * lax.rsqrt(var + LN_EPS) * g + b


def _silu(x):
    return x * jax.nn.sigmoid(x)


def _resident(block_shape, index_map):
    return pl.BlockSpec(block_shape, index_map, pipeline_mode=pl.Buffered(1))


class _Layout:
    def __init__(self, batch, seq, dec_batch, dec_seq, d_model, depth):
        self.batch, self.seq, self.dec_batch, self.dec_seq = batch, seq, dec_batch, dec_seq
        self.d, self.depth = d_model, depth
        self.n_prompt = batch * seq
        self.n_tok = self.n_prompt + dec_batch * dec_seq
        self.n_steps = self.n_tok // TM
        assert self.n_prompt % TM == 0 and dec_seq % TM == 0 and TM % seq == 0
        assert seq & (seq - 1) == 0 and dec_seq & (dec_seq - 1) == 0
        self.alpha = (2.0 * depth) ** 0.25

    def mod_index(self, i, rows):
        r0 = i * rows
        return jnp.where(r0 < self.n_prompt, 0, 1 + (r0 - self.n_prompt) // self.dec_seq)

    def mod_spec(self, tile_of):
        def index(*ids):
            tile, rows = tile_of(*ids)
            return (self.mod_index(tile, rows), 0)

        return pl.BlockSpec((MOD_GROUP, 6 * self.d), index)


def _mod_part(mod_ref, k):
    d = mod_ref.shape[1] // 6
    return mod_ref[0:1, k * d:(k + 1) * d]


class _Job:
    def __init__(self, args, in_specs, out_shapes, out_specs, fn):
        self.args, self.in_specs, self.out_shapes, self.out_specs, self.fn = args, in_specs, out_shapes, out_specs, fn


def _cast_job(w, j, n_steps):
    _, n_rows, n_cols = w.shape
    rows = n_rows // n_steps
    assert rows * n_steps == n_rows and rows % BF16_SUBLANES == 0

    def fn(in_refs, out_refs):
        out_refs[0][...] = in_refs[0][...].astype(BF16)

    return _Job([w], [pl.BlockSpec((None, rows, n_cols), lambda i: (j, i, 0))],
                [jax.ShapeDtypeStruct((n_rows, n_cols), BF16)], [pl.BlockSpec((rows, n_cols), lambda i: (i, 0))], fn)


def _mod_job(cond, w_ada, b_ada, layer, n_steps):
    depth, d, n_out = w_ada.shape
    n_rows = cond.shape[0]
    cols = n_out // n_steps
    assert cols * n_steps == n_out and cols % LANES == 0

    def fn(in_refs, out_refs):
        cond_ref, w_ref, b_ref = in_refs
        out_refs[0][...] = (_dot(_silu(cond_ref[...]).astype(BF16), w_ref[...].astype(BF16))
                            + b_ref[layer:layer + 1, :])

    return _Job([cond, w_ada, b_ada],
                [_resident((n_rows, d), lambda i: (0, 0)),
                 pl.BlockSpec((None, d, cols), lambda i: (layer, 0, i)),
                 pl.BlockSpec((depth, cols), lambda i: (0, i))],
                [jax.ShapeDtypeStruct((n_rows, n_out), F32)], [pl.BlockSpec((n_rows, cols), lambda i: (0, i))], fn)


def _run(kernel_fn, *, name, grid, args, in_specs, out_shape, out_specs, scratch_shapes=(), jobs=(), aliases=None):
    n_in, n_out = len(args), len(out_shape)

    def body(*refs):
        pos = n_in
        job_in = []
        for jb in jobs:
            job_in.append(refs[pos:pos + len(jb.args)])
            pos += len(jb.args)
        main_out = refs[pos:pos + n_out]
        pos += n_out
        job_out = []
        for jb in jobs:
            job_out.append(refs[pos:pos + len(jb.out_shapes)])
            pos += len(jb.out_shapes)
        for jb, ji, jo in zip(jobs, job_in, job_out):
            jb.fn(ji, jo)
        kernel_fn(*refs[:n_in], *main_out, *refs[pos:])

    outs = pl.pallas_call(
        body,
        out_shape=tuple(out_shape) + tuple(s for jb in jobs for s in jb.out_shapes),
        grid=grid,
        in_specs=list(in_specs) + [s for jb in jobs for s in jb.in_specs],
        out_specs=tuple(out_specs) + tuple(s for jb in jobs for s in jb.out_specs),
        scratch_shapes=list(scratch_shapes),
        input_output_aliases=aliases or {},
        compiler_params=pltpu.CompilerParams(dimension_semantics=("arbitrary",) * len(grid),
                                             vmem_limit_bytes=VMEM_LIMIT),
        name=name,
    )(*args, *[a for jb in jobs for a in jb.args])
    main, rest = outs[:n_out], list(outs[n_out:])
    job_results = []
    for jb in jobs:
        job_results.append(tuple(rest[:len(jb.out_shapes)]))
        rest = rest[len(jb.out_shapes):]
    return tuple(main), job_results


def _weight_in(w):
    if isinstance(w, tuple):
        arr, j = w
        _, r, c = arr.shape
        return arr, _resident((None, r, c), lambda *_: (j, 0, 0)), pltpu.VMEM((r, c), BF16)
    r, c = w.shape
    return w, _resident((r, c), lambda *_: (0, 0)), None


def _weight_shape(w):
    return w[0].shape[1:] if isinstance(w, tuple) else w.shape


def _round_weight_once(first_step, w_ref, w_scr):
    n_rows = w_ref.shape[0]
    rows = min(n_rows, WEIGHT_CAST_ROWS)
    assert n_rows % rows == 0

    @pl.when(first_step)
    def _():
        def body(c, carry):
            sl = pl.ds(pl.multiple_of(c * rows, rows), rows)
            w_scr[sl, :] = w_ref[sl, :].astype(BF16)
            return carry

        lax.fori_loop(0, n_rows // rows, body, 0)


def _bf16_weights(first_step, w_refs, scratch):
    scratch = list(scratch)
    out = []
    for w in w_refs:
        if w.dtype == BF16:
            out.append(w)
        else:
            scr = scratch.pop(0)
            _round_weight_once(first_step, w, scr)
            out.append(scr)
    return out, scratch


def _modulation(cond, w_ada, b_ada, layer):
    _, d, n_out = w_ada.shape
    job = _mod_job(cond, w_ada, b_ada, layer, n_out // MOD_NT)
    _, (result,) = _run(lambda: None, name=f"modulation_{layer}", grid=(n_out // MOD_NT,), args=[], in_specs=[],
                        out_shape=[], out_specs=[], jobs=[job])
    return result[0]


def _conv_kernel(lay, layer, split, *refs):
    d = lay.d
    i = pl.program_id(0)
    if split:
        (xca, xpa, xna, xcb, xpb, xnb, mod_ref, lng_ref, lnb_ref, win_ref, cw_ref, wout_ref,
         o_ref, h_scr, uu_scr, *scratch) = refs
        is_context = i * TM < lay.n_prompt
        x = jnp.where(is_context, xca[...], xcb[...])
        x_prev = jnp.where(is_context, xpa[...], xpb[...])
        x_next = jnp.where(is_context, xna[...], xnb[...])
    else:
        (xc_ref, xp_ref, xn_ref, mod_ref, lng_ref, lnb_ref, win_ref, cw_ref, wout_ref,
         o_ref, h_scr, uu_scr, *scratch) = refs
        x, x_prev, x_next = xc_ref[...], xp_ref[...], xn_ref[...]
    (win, wout), _ = _bf16_weights(i == 0, (win_ref, wout_ref), scratch)
    sh, sc, ga = _mod_part(mod_ref, 0), _mod_part(mod_ref, 1), _mod_part(mod_ref, 2)
    one_sc = 1.0 + sc
    h_scr[0:HALO, :] = (x_prev * one_sc + sh).astype(BF16)
    h_scr[HALO:HALO + TM, :] = (x * one_sc + sh).astype(BF16)
    h_scr[HALO + TM:HALO + TM + HALO, :] = (x_next * one_sc + sh).astype(BF16)
    cgu = _dot(h_scr[...], win[:, d:3 * d])
    uu_scr[...] = cgu[:, 0:d] * cgu[:, d:2 * d]
    sub = TM // SUB_TILES
    for s in range(SUB_TILES):
        r0 = s * sub
        bg = _dot(h_scr[HALO + r0:HALO + r0 + sub, :], win[:, 0:d])
        row = i * TM + r0 + lax.broadcasted_iota(jnp.int32, (sub, 1), 0)
        seq_len = jnp.where(row < lay.n_prompt, lay.seq, lay.dec_seq)
        pos = jnp.bitwise_and(row, seq_len - 1)
        u_prev = jnp.where(pos != 0, uu_scr[pl.ds(HALO + r0 - 1, sub), :], 0.0)
        u_next = jnp.where(pos != seq_len - 1, uu_scr[pl.ds(HALO + r0 + 1, sub), :], 0.0)
        y = (u_prev * cw_ref[0:1, :] + uu_scr[pl.ds(HALO + r0, sub), :] * cw_ref[1:2, :]
             + u_next * cw_ref[2:3, :])
        mix = _dot((bg * y).astype(BF16), wout[...])
        o_ref[r0:r0 + sub, :] = _layer_norm(lay.alpha * x[r0:r0 + sub] + ga * mix,
                                            lng_ref[layer, 0:1, :],
                                            lnb_ref[layer, 0:1, :])


def _conv_mixer(lay, layer, j, x_parts, mod, lng, lnb, w_in, cw, w_out, jobs):
    d = lay.d
    per = TM // HALO
    x_specs, x_args = [], []
    tile0 = 0
    for part in x_parts:
        n_tiles = part.shape[0] // TM

        def center(i, t0=tile0, n=n_tiles):
            return (jnp.clip(i - t0, 0, n - 1), 0)

        def prev_halo(i, t0=tile0, n=n_tiles):
            return (jnp.clip((i - t0) * per - 1, 0, n * per - 1), 0)

        def next_halo(i, t0=tile0, n=n_tiles):
            return (jnp.clip((i - t0 + 1) * per, 0, n * per - 1), 0)

        x_specs += [pl.BlockSpec((TM, d), center), pl.BlockSpec((HALO, d), prev_halo),
                    pl.BlockSpec((HALO, d), next_halo)]
        x_args += [part, part, part]
        tile0 += n_tiles
    win_arr, win_spec, win_scr = _weight_in(w_in)
    wout_arr, wout_spec, wout_scr = _weight_in(w_out)
    (x,), job_results = _run(
        functools.partial(_conv_kernel, lay, layer, len(x_parts) == 2),
        name=f"conv_mixer_{layer}", grid=(lay.n_steps,),
        args=x_args + [mod, lng, lnb, win_arr, cw, wout_arr],
        in_specs=x_specs + [
            lay.mod_spec(lambda i: (i, TM)),
            _resident(lng.shape, lambda i: (0, 0, 0)),
            _resident(lnb.shape, lambda i: (0, 0, 0)),
            win_spec,
            _resident((None, CONV_WIDTH, d), lambda i: (j, 0, 0)),
            wout_spec,
        ],
        out_shape=[jax.ShapeDtypeStruct((lay.n_tok, d), F32)],
        out_specs=[pl.BlockSpec((TM, d), lambda i: (i, 0))],
        scratch_shapes=[pltpu.VMEM((TM + 2 * HALO, d), BF16), pltpu.VMEM((TM + 2 * HALO, d), F32)]
        + [s for s in (win_scr, wout_scr) if s is not None],
        jobs=jobs)
    return x, job_results


def _ffn_kernel(lay, layer, split_out, gla_pre, *refs):
    x_ref, mod_ref, lng_ref, lnb_ref, win_ref, wout_ref = refs[:6]
    pos = 6
    if gla_pre:
        oc_ref, ol_ref, og_ref, ng_ref, wo_ref = refs[pos:pos + 5]
        pos += 5
    n_out = 2 if split_out else 1
    out_refs = refs[pos:pos + n_out]
    scratch = refs[pos + n_out:]
    d_ff = wout_ref.shape[0]
    is_context = pl.program_id(0) * TM < lay.n_prompt
    sh, sc, ga = _mod_part(mod_ref, 3), _mod_part(mod_ref, 4), _mod_part(mod_ref, 5)
    ln_g = lng_ref[layer, 1:2, :]
    ln_b = lnb_ref[layer, 1:2, :]
    results = []
    sub = TM // SUB_TILES
    for s in range(SUB_TILES):
        rows = slice(s * sub, (s + 1) * sub)
        x = x_ref[rows, :]
        if gla_pre:
            z_scr, = scratch
            dv = ng_ref.shape[1]
            for h in range(GLA_HEADS):
                sl = slice(h * dv, (h + 1) * dv)
                o = jnp.where(is_context, oc_ref[rows, sl], ol_ref[rows, sl])
                o = o * lax.rsqrt(jnp.mean(o * o, axis=-1, keepdims=True) + RMS_EPS) * ng_ref[...]
                z_scr[rows, sl] = (o * _silu(og_ref[rows, sl])).astype(BF16)
            mix = _dot(z_scr[rows, :], wo_ref[...])
            x = _layer_norm(lay.alpha * x + _mod_part(mod_ref, 2) * mix,
                            lng_ref[layer, 0:1, :], lnb_ref[layer, 0:1, :])
        h = (x * (1.0 + sc) + sh).astype(BF16)
        g = _dot(h, win_ref[:, 0:d_ff])
        u = _dot(h, win_ref[:, d_ff:2 * d_ff])
        a = (_silu(g) * u).astype(BF16)
        y = _dot(a, wout_ref[...])
        results.append((rows, _layer_norm(lay.alpha * x + ga * y, ln_g, ln_b)))
    if split_out:
        @pl.when(is_context)
        def _():
            for rows, res in results:
                out_refs[0][rows, :] = res

        @pl.when(jnp.logical_not(is_context))
        def _():
            for rows, res in results:
                out_refs[1][rows, :] = res
    else:
        for rows, res in results:
            out_refs[0][rows, :] = res


def _ffn(lay, layer, x, mod, lng, lnb, w_in, w_out, *, split_out, gla_pre, jobs):
    d = lay.d
    d_ff = w_out.shape[0]
    row = lambda i: (i, 0)
    n_ctx = lay.n_prompt // TM
    n_lat = lay.n_steps - n_ctx
    ctx_row = lambda i: (jnp.minimum(i, n_ctx - 1), 0)
    lat_row = lambda i: (jnp.clip(i - n_ctx, 0, n_lat - 1), 0)
    args = [x, mod, lng, lnb, w_in, w_out]
    in_specs = [
        pl.BlockSpec((TM, d), row),
        lay.mod_spec(lambda i: (i, TM)),
        _resident(lng.shape, lambda i: (0, 0, 0)),
        _resident(lnb.shape, lambda i: (0, 0, 0)),
        _resident((d, 2 * d_ff), lambda i: (0, 0)),
        _resident((d_ff, d), lambda i: (0, 0)),
    ]
    scratch = []
    if gla_pre is not None:
        o_context, o_latent, og, norm_g, w_o = gla_pre
        hv = og.shape[1]
        args += [o_context, o_latent, og, norm_g, w_o]
        in_specs += [pl.BlockSpec((TM, hv), ctx_row), pl.BlockSpec((TM, hv), lat_row), pl.BlockSpec((TM, hv), row),
                     _resident(norm_g.shape, lambda i: (0, 0)), _resident((hv, d), lambda i: (0, 0))]
        scratch = [pltpu.VMEM((TM, hv), BF16)]
    if split_out:
        out_shape = [jax.ShapeDtypeStruct((lay.n_prompt, d), F32),
                     jax.ShapeDtypeStruct((lay.n_tok - lay.n_prompt, d), F32)]
        out_specs = [pl.BlockSpec((TM, d), ctx_row), pl.BlockSpec((TM, d), lat_row)]
    else:
        out_shape = [jax.ShapeDtypeStruct((lay.n_tok, d), F32)]
        out_specs = [pl.BlockSpec((TM, d), row)]
    return _run(functools.partial(_ffn_kernel, lay, layer, split_out, gla_pre is not None),
                name=f"ffn_{layer}", grid=(lay.n_steps,), args=args, in_specs=in_specs,
                out_shape=out_shape, out_specs=out_specs, scratch_shapes=scratch, jobs=jobs)


def _qkv_kernel(lay, n_heads, x_ref, mod_ref, w_ref, qg_ref, kg_ref, cos_ref, sin_ref,
                q_ref, k_ref, v_ref, kf_ref, vf_ref, *scratch):
    (w,), _ = _bf16_weights(pl.program_id(0) == 0, (w_ref,), scratch)
    sh, sc = _mod_part(mod_ref, 0), _mod_part(mod_ref, 1)
    lane = lax.broadcasted_iota(jnp.int32, (1, HEAD_DIM), 1)
    first_half = jnp.bitwise_and(lane, HEAD_DIM // 4) == 0
    q_gain = qg_ref[...] * (HEAD_DIM ** -0.5)
    ones = jnp.ones((HEAD_DIM, HEAD_DIM), BF16)
    k0 = n_heads * HEAD_DIM
    v0 = k0 + N_KV_HEADS * HEAD_DIM
    for s in range(SUB_TILES):
        rows = slice(s * (TM // SUB_TILES), (s + 1) * (TM // SUB_TILES))
        h = (x_ref[rows, :] * (1.0 + sc) + sh).astype(BF16)
        qkv = _dot(h, w[...])
        cos, sin = cos_ref[rows, :], sin_ref[rows, :]

        def norm_rope(t, g):
            sq = t * t
            sq_hi = sq.astype(BF16)
            sq_lo = (sq - sq_hi.astype(F32)).astype(BF16)
            mean_sq = (_dot(sq_hi, ones) + _dot(sq_lo, ones)) * (1.0 / HEAD_DIM)
            t = t * lax.rsqrt(mean_sq + RMS_EPS) * g
            partner = jnp.where(first_half,
                                pltpu.roll(t, HEAD_DIM - HEAD_DIM // 4, axis=1),
                                pltpu.roll(t, HEAD_DIM // 4, axis=1))
            return t * cos + partner * sin

        for hq in range(n_heads):
            sl = slice(hq * HEAD_DIM, (hq + 1) * HEAD_DIM)
            q_ref[rows, sl] = norm_rope(qkv[:, sl], q_gain).astype(BF16)
        for hk in range(N_KV_HEADS):
            sl = slice(hk * HEAD_DIM, (hk + 1) * HEAD_DIM)
            kh = norm_rope(qkv[:, k0 + hk * HEAD_DIM:k0 + (hk + 1) * HEAD_DIM], kg_ref[...])
            kf_ref[rows, sl] = kh
            k_ref[rows, sl] = kh.astype(BF16)
        v = qkv[:, v0:v0 + N_KV_HEADS * HEAD_DIM]
        vf_ref[rows, :] = v
        v_ref[rows, :] = v.astype(BF16)


def _attn_qkv(lay, layer, j, x, mod, w_qkv, q_gain, k_gain, cos_t, sin_t, jobs):
    d = lay.d
    n_qkv = _weight_shape(w_qkv)[1]
    kv = N_KV_HEADS * HEAD_DIM
    n_heads = (n_qkv - 2 * kv) // HEAD_DIM
    row = lambda i: (i, 0)

    def rope_row(i):
        r0 = i * TM
        return (jnp.where(r0 < lay.n_prompt, 0, 1 + ((r0 - lay.n_prompt) % lay.dec_seq) // TM), 0)

    w_arr, w_spec, w_scr = _weight_in(w_qkv)
    return _run(
        functools.partial(_qkv_kernel, lay, n_heads),
        name=f"attn_qkv_{layer}", grid=(lay.n_steps,),
        args=[x, mod, w_arr, q_gain, k_gain, cos_t, sin_t],
        in_specs=[
            pl.BlockSpec((TM, d), row),
            lay.mod_spec(lambda i: (i, TM)),
            w_spec,
            _resident((None, 1, HEAD_DIM), lambda i: (j, 0, 0)),
            _resident((None, 1, HEAD_DIM), lambda i: (j, 0, 0)),
            pl.BlockSpec((TM, HEAD_DIM), rope_row),
            pl.BlockSpec((TM, HEAD_DIM), rope_row),
        ],
        out_shape=[jax.ShapeDtypeStruct((lay.n_tok, n_heads * HEAD_DIM), BF16),
                   jax.ShapeDtypeStruct((lay.n_tok, kv), BF16),
                   jax.ShapeDtypeStruct((lay.n_tok, kv), BF16),
                   jax.ShapeDtypeStruct((lay.n_tok, kv), F32),
                   jax.ShapeDtypeStruct((lay.n_tok, kv), F32)],
        out_specs=[pl.BlockSpec((TM, n_heads * HEAD_DIM), row), pl.BlockSpec((TM, kv), row),
                   pl.BlockSpec((TM, kv), row), pl.BlockSpec((TM, kv), row), pl.BlockSpec((TM, kv), row)],
        scratch_shapes=[s for s in (w_scr,) if s is not None],
        jobs=jobs)


def _attn_kernel(lay, layer, n_heads, has_cache, seqs, *refs):
    if has_cache:
        (q_ref, k_ref, v_ref, kc_ref, vc_ref, x_ref, mod_ref, lng_ref, lnb_ref, wo_ref, o_ref, att_scr,
         *scratch) = refs
    else:
        (q_ref, k_ref, v_ref, x_ref, mod_ref, lng_ref, lnb_ref, wo_ref, o_ref, att_scr, *scratch) = refs
    (wo,), _ = _bf16_weights((pl.program_id(0) == 0) & (pl.program_id(1) == 0), (wo_ref,), scratch)
    group = n_heads // N_KV_HEADS
    q_rows = q_ref.shape[0] // seqs
    k_rows = k_ref.shape[0] // seqs
    for sq in range(seqs):
        rq = slice(sq * q_rows, (sq + 1) * q_rows)
        rk = slice(sq * k_rows, (sq + 1) * k_rows)
        for hk in range(N_KV_HEADS):
            ksl = slice(hk * HEAD_DIM, (hk + 1) * HEAD_DIM)
            v_ext = jnp.concatenate([v_ref[rk, ksl], jnp.ones((k_rows, HEAD_DIM), BF16)], axis=1)
            if has_cache:
                vc_ext = jnp.concatenate([vc_ref[:, ksl], jnp.ones((vc_ref.shape[0], HEAD_DIM), BF16)], axis=1)
            for g in range(group):
                hq = hk * group + g
                qsl = slice(hq * HEAD_DIM, (hq + 1) * HEAD_DIM)
                qh = q_ref[rq, qsl]
                s = _dot_nt(qh, k_ref[rk, ksl])
                m = jnp.max(s, axis=-1, keepdims=True)
                if has_cache:
                    s_c = _dot_nt(qh, kc_ref[:, ksl])
                    m = jnp.maximum(m, jnp.max(s_c, axis=-1, keepdims=True))
                o = _dot(jnp.exp(s - m).astype(BF16), v_ext)
                if has_cache:
                    o = o + _dot(jnp.exp(s_c - m).astype(BF16), vc_ext)
                att_scr[rq, qsl] = (o[:, 0:HEAD_DIM] / o[:, HEAD_DIM:2 * HEAD_DIM]).astype(BF16)
    ga = _mod_part(mod_ref, 2)
    mix = _dot(att_scr[...], wo[...])
    o_ref[...] = _layer_norm(lay.alpha * x_ref[...] + ga * mix,
                             lng_ref[layer, 0:1, :], lnb_ref[layer, 0:1, :])


def _attention(lay, layer, q, k, v, cache, x, mod, lng, lnb, w_o, *, row0, n_seq, seq, tq):
    d = lay.d
    n_heads = q.shape[1] // HEAD_DIM
    kv = N_KV_HEADS * HEAD_DIM
    has_cache = cache is not None
    seqs = max(tq // seq, 1)
    assert not (has_cache and seqs > 1) and n_seq % seqs == 0
    n_seq, seq = n_seq // seqs, seq * seqs
    q_blocks = seq // tq
    tile = lambda b, i: (row0 // tq + b * q_blocks + i, 0)
    seq_blk = lambda b, i: (row0 // seq + b, 0)
    in_specs = [pl.BlockSpec((tq, n_heads * HEAD_DIM), tile),
                pl.BlockSpec((seq, kv), seq_blk), pl.BlockSpec((seq, kv), seq_blk)]
    args = [q, k, v]
    if has_cache:
        kc, vc = cache
        past = kc.shape[0] // n_seq
        in_specs += [pl.BlockSpec((past, kv), lambda b, i: (b, 0))] * 2
        args += [kc, vc]
    x_index = len(args)
    wo_arr, wo_spec, wo_scr = _weight_in(w_o)
    in_specs += [
        pl.BlockSpec((tq, d), tile),
        lay.mod_spec(lambda b, i: (row0 // tq + b * q_blocks + i, tq)),
        _resident(lng.shape, lambda b, i: (0, 0, 0)),
        _resident(lnb.shape, lambda b, i: (0, 0, 0)),
        wo_spec,
    ]
    args += [x, mod, lng, lnb, wo_arr]
    (x,), _ = _run(
        functools.partial(_attn_kernel, lay, layer, n_heads, has_cache, seqs),
        name=f"attention_{layer}_{'latent' if has_cache else 'context'}", grid=(n_seq, q_blocks),
        args=args, in_specs=in_specs,
        out_shape=[jax.ShapeDtypeStruct((lay.n_tok, d), F32)], out_specs=[pl.BlockSpec((tq, d), tile)],
        scratch_shapes=[pltpu.VMEM((tq, n_heads * HEAD_DIM), BF16)] + [s for s in (wo_scr,) if s is not None],
        aliases={x_index: 0})
    return x


def _split3(x):
    hi = x.astype(BF16)
    r = x - hi.astype(F32)
    mid = r.astype(BF16)
    lo = (r - mid.astype(F32)).astype(BF16)
    return hi, mid, lo


def _gla_proj_kernel(lay, hk, hv, x_ref, mod_ref, win_ref, w1_ref, w2_ref, bg_ref,
                     q_ref, k_ref, v_ref, og_ref, bf_ref, bb_ref, *scratch):
    (win,), _ = _bf16_weights(pl.program_id(0) == 0, (win_ref,), scratch)
    sh, sc = _mod_part(mod_ref, 0), _mod_part(mod_ref, 1)
    dk = hk // GLA_HEADS
    r = lax.broadcasted_iota(jnp.int32, (GLA_CHUNK, GLA_CHUNK), 0)
    c = lax.broadcasted_iota(jnp.int32, (GLA_CHUNK, GLA_CHUNK), 1)
    lower = jnp.concatenate([jnp.where(c <= r, 1.0, 0.0).astype(BF16)] * 3, axis=1)
    upper = jnp.concatenate([jnp.where(c >= r, 1.0, 0.0).astype(BF16)] * 3, axis=1)
    h = (x_ref[...] * (1.0 + sc) + sh).astype(BF16)
    proj = _dot(h, win[...])
    q_ref[...] = proj[:, 0:hk] * (dk ** -0.5)
    k_ref[...] = proj[:, hk:2 * hk]
    v_ref[...] = proj[:, 2 * hk:2 * hk + hv].astype(BF16)
    og_ref[...] = proj[:, 2 * hk + hv:2 * hk + 2 * hv]
    z = _dot(_dot(h, w1_ref[...]).astype(BF16), w2_ref[...]) + bg_ref[...]
    log_gate = (jnp.minimum(z, 0.0) - jnp.log1p(jnp.exp(-jnp.abs(z)))) * (1.0 / GLA_TAU)
    for ch in range(TM // GLA_CHUNK):
        rows = slice(ch * GLA_CHUNK, (ch + 1) * GLA_CHUNK)
        bf_ref[rows, :] = _dot(lower, jnp.concatenate(_split3(log_gate[rows, 0:hk]), axis=0))
        bb_ref[rows, :] = _dot(upper, jnp.concatenate(_split3(log_gate[rows, hk:2 * hk]), axis=0))


def _gla_proj(lay, layer, j, x, mod, w_in, w1, w2, b_gate, jobs):
    d = lay.d
    n_in = _weight_shape(w_in)[1]
    hk = w2.shape[2] // 2
    hv = (n_in - 2 * hk) // 2
    rank2 = w1.shape[2]
    row = lambda i: (i, 0)
    win_arr, win_spec, win_scr = _weight_in(w_in)
    return _run(
        functools.partial(_gla_proj_kernel, lay, hk, hv),
        name=f"gla_proj_{layer}", grid=(lay.n_steps,),
        args=[x, mod, win_arr, w1, w2, b_gate],
        in_specs=[
            pl.BlockSpec((TM, d), row),
            lay.mod_spec(lambda i: (i, TM)),
            win_spec,
            _resident((None, d, rank2), lambda i: (j, 0, 0)),
            _resident((None, rank2, 2 * hk), lambda i: (j, 0, 0)),
            _resident((None, 1, 2 * hk), lambda i: (j, 0, 0)),
        ],
        out_shape=[jax.ShapeDtypeStruct((lay.n_tok, hk), F32),
                   jax.ShapeDtypeStruct((lay.n_tok, hk), F32),
                   jax.ShapeDtypeStruct((lay.n_tok, hv), BF16),
                   jax.ShapeDtypeStruct((lay.n_tok, hv), F32),
                   jax.ShapeDtypeStruct((lay.n_tok, hk), F32),
                   jax.ShapeDtypeStruct((lay.n_tok, hk), F32)],
        out_specs=[pl.BlockSpec((TM, hk), row), pl.BlockSpec((TM, hk), row), pl.BlockSpec((TM, hv), row),
                   pl.BlockSpec((TM, hv), row), pl.BlockSpec((TM, hk), row), pl.BlockSpec((TM, hk), row)],
        scratch_shapes=[s for s in (win_scr,) if s is not None],
        jobs=jobs)


def _block_diag(blocks):
    n = len(blocks)
    zero = jnp.zeros_like(blocks[0])
    return jnp.concatenate(
        [jnp.concatenate([blocks[i] if j == i else zero for j in range(n)], axis=1) for i in range(n)], axis=0)


def _gla_chain_mask(dtype):
    n = GLA_CHAINS * GLA_CHUNK
    r = lax.broadcasted_iota(jnp.int32, (n, n), 0)
    c = lax.broadcasted_iota(jnp.int32, (n, n), 1)
    shift = GLA_CHUNK.bit_length() - 1
    assert 1 << shift == GLA_CHUNK
    r_chain, r_t = jnp.right_shift(r, shift), jnp.bitwise_and(r, GLA_CHUNK - 1)
    c_chain, c_t = jnp.right_shift(c, shift), jnp.bitwise_and(c, GLA_CHUNK - 1)
    direction = 1 - 2 * jnp.bitwise_and(r_chain, 1)
    ordered = jnp.where((c_t - r_t) * direction <= 0, 1.0, 0.0)
    return jnp.where(r_chain == c_chain, ordered, 0.0).astype(dtype)


def _gla_fast_step(refs, st_scr, mask_scr, g, t, accumulate, n_chunks, dk, dv):
    q_ref, k_ref, v_ref, bf_ref, bb_ref, o_ref = refs
    mid = GLA_CHUNK // 2
    slots = []
    for u in range(2):
        qs_l, ks_l, v_l, qi_l, kh_l, dec_l, dst_l, old_l = [], [], [], [], [], [], [], []
        for j in range(2):
            h = 2 * g + j
            kcols = slice(h * dk, (h + 1) * dk)
            vcols = slice(h * dv, (h + 1) * dv)
            for forward, b_ref in ((True, bf_ref), (False, bb_ref)):
                c = (2 * t + u) if forward else (n_chunks - 1 - 2 * t - u)
                rows = pl.ds(pl.multiple_of(c * GLA_CHUNK, GLA_CHUNK), GLA_CHUNK)
                q, k, b, v = q_ref[rows, kcols], k_ref[rows, kcols], b_ref[rows, kcols], v_ref[rows, vcols]
                rho = b[mid:mid + 1]
                q_s = q * jnp.exp(b - rho)
                k_s = k * jnp.exp(rho - b)
                b_exit = b[GLA_CHUNK - 1:GLA_CHUNK] if forward else b[0:1]
                qs_l.append(q_s.astype(BF16))
                ks_l.append(k_s.astype(BF16))
                v_l.append(v)
                qi_l.append((q_s * jnp.exp(rho)).astype(BF16))
                kh_l.append((k_s * jnp.exp(b_exit - rho)).astype(BF16))
                dec_l.append(jnp.exp(b_exit))
                dst_l.append((rows, vcols))
                old_l.append(o_ref[rows, vcols] if accumulate else None)
        v_all = jnp.concatenate(v_l, axis=0)
        a = _dot_nt(jnp.concatenate(qs_l, axis=0), jnp.concatenate(ks_l, axis=0)).astype(BF16) * mask_scr[...]
        slots.append((_dot(a, v_all), _block_diag(qi_l), _block_diag(kh_l), v_all,
                      jnp.concatenate(dec_l, axis=1), dst_l, old_l))
    st = st_scr[g]
    stores = []
    for o_intra, q_bd, k_bd, v_all, decay, dst_l, old_l in slots:
        o_all = o_intra + _dot_nt(q_bd, st.astype(BF16))
        st = st * decay + _dot_tn(v_all, k_bd)
        for ci, ((rows, vcols), old) in enumerate(zip(dst_l, old_l)):
            o = o_all[ci * GLA_CHUNK:(ci + 1) * GLA_CHUNK]
            stores.append((rows, vcols, o if old is None else o + old))
    st_scr[g] = st
    for rows, vcols, o in stores:
        o_ref[rows, vcols] = o


def _gla_chunk_safe(q, k, v, b, st, forward):
    n_sub = GLA_CHUNK // GLA_SUB
    o_rows = []
    qi = (q * jnp.exp(b)).astype(BF16)
    o_inter = _dot_nt(qi, st.astype(BF16))
    t_idx = lax.broadcasted_iota(jnp.int32, (GLA_SUB, 1), 0)
    for blk in range(n_sub):
        rows = slice(blk * GLA_SUB, (blk + 1) * GLA_SUB)
        qb, kb, bb_, vb = q[rows], k[rows], b[rows], v[rows].astype(F32)
        o_blk = o_inter[rows]
        if forward and blk > 0:
            others = slice(0, blk * GLA_SUB)
            rho = b[blk * GLA_SUB - 1:blk * GLA_SUB]
        elif (not forward) and blk < n_sub - 1:
            others = slice((blk + 1) * GLA_SUB, GLA_CHUNK)
            rho = b[(blk + 1) * GLA_SUB:(blk + 1) * GLA_SUB + 1]
        else:
            others = None
        if others is not None:
            q_s = (qb * jnp.exp(bb_ - rho)).astype(BF16)
            k_s = (k[others] * jnp.exp(rho - b[others])).astype(BF16)
            a = _dot_nt(q_s, k_s)
            o_blk = o_blk + _dot(a.astype(BF16), v[others])
        for s in range(GLA_SUB):
            keep = (t_idx >= s) if forward else (t_idx <= s)
            decay = jnp.exp(jnp.where(keep, bb_ - bb_[s:s + 1], NEG_BIG))
            a_col = jnp.sum(qb * kb[s:s + 1] * decay, axis=-1, keepdims=True)
            o_blk = o_blk + a_col * vb[s:s + 1]
        o_rows.append(o_blk)
    b_exit = b[GLA_CHUNK - 1:GLA_CHUNK] if forward else b[0:1]
    k_hat = (k * jnp.exp(b_exit - b)).astype(BF16)
    st_new = st * jnp.exp(b_exit) + _dot_tn(v, k_hat)
    return jnp.concatenate(o_rows, axis=0), st_new


def _gla_scan_kernel(n_chunks, hps, dk, dv, has_init, *refs):
    n_groups = hps // 2
    if has_init:
        (q_ref, k_ref, v_ref, bf_ref, bb_ref, s0f_ref, s0b_ref, o_ref, st_scr, mask_scr) = refs
    else:
        (q_ref, k_ref, v_ref, bf_ref, bb_ref, o_ref, sf_ref, sb_ref, st_scr, mask_scr) = refs
    data_refs = (q_ref, k_ref, v_ref, bf_ref, bb_ref, o_ref)

    def chain_cols(h, forward):
        ci = 2 * (h % 2) + (0 if forward else 1)
        return h // 2, slice(ci * dk, (ci + 1) * dk)

    for h in range(hps):
        for forward in (True, False):
            g, cols = chain_cols(h, forward)
            if has_init:
                st_scr[g, :, cols] = (s0f_ref if forward else s0b_ref)[h].T
            else:
                st_scr[g, :, cols] = jnp.zeros((dv, dk), F32)
    mask_scr[...] = _gla_chain_mask(BF16)

    def safe_step(i, accumulate):
        pending = []
        for h in range(hps):
            kcols = slice(h * dk, (h + 1) * dk)
            vcols = slice(h * dv, (h + 1) * dv)
            for forward, b_ref in ((True, bf_ref), (False, bb_ref)):
                c = i if forward else n_chunks - 1 - i
                rows = pl.ds(pl.multiple_of(c * GLA_CHUNK, GLA_CHUNK), GLA_CHUNK)
                g, cols = chain_cols(h, forward)
                o, st_new = _gla_chunk_safe(q_ref[rows, kcols], k_ref[rows, kcols], v_ref[rows, vcols],
                                            b_ref[rows, kcols], st_scr[g, :, cols], forward)
                if accumulate:
                    o = o + o_ref[rows, vcols]
                pending.append((rows, vcols, g, cols, o, st_new))
        for rows, vcols, g, cols, o, st_new in pending:
            st_scr[g, :, cols] = st_new
            o_ref[rows, vcols] = o

    def fast_step(t, accumulate):
        for g in range(n_groups):
            _gla_fast_step(data_refs, st_scr, mask_scr, g, t, accumulate, n_chunks, dk, dv)

    def run(step, n_steps):
        def first(i, carry):
            step(i, False)
            return carry

        def second(i, carry):
            step(i, True)
            return carry

        unroll = next(u for u in (4, 2, 1) if (n_steps // 2) % u == 0)
        lax.fori_loop(0, n_steps // 2, first, 0, unroll=unroll)
        lax.fori_loop(n_steps // 2, n_steps, second, 0, unroll=unroll)

    span = jnp.zeros((1, hps * dk), F32)
    for c in range(n_chunks):
        top = slice(c * GLA_CHUNK, c * GLA_CHUNK + 1)
        bottom = slice((c + 1) * GLA_CHUNK - 1, (c + 1) * GLA_CHUNK)
        span = jnp.maximum(span, jnp.maximum(bf_ref[top, :] - bf_ref[bottom, :],
                                             bb_ref[bottom, :] - bb_ref[top, :]))
    bounded = jnp.max(span) < GLA_SAFE_RANGE

    @pl.when(bounded)
    def _():
        run(fast_step, n_chunks // 2)

    @pl.when(jnp.logical_not(bounded))
    def _():
        run(safe_step, n_chunks)

    if not has_init:
        for h in range(hps):
            for forward in (True, False):
                g, cols = chain_cols(h, forward)
                (sf_ref if forward else sb_ref)[h] = st_scr[g, :, cols].T


def _gla_scan(lay, j, q, k, v, bf, bb, init, *, row0, n_seq, seq, hps):
    hk, hv = q.shape[1], v.shape[1]
    dk, dv = hk // GLA_HEADS, hv // GLA_HEADS
    n_chunks = seq // GLA_CHUNK
    groups = GLA_HEADS // hps
    assert n_chunks % 4 == 0 and GLA_HEADS % hps == 0 and hps % 2 == 0
    has_init = init is not None
    blk = lambda b, g: (row0 // seq + b, g)
    kspec = pl.BlockSpec((seq, hps * dk), blk)
    vspec = pl.BlockSpec((seq, hps * dv), blk)
    in_specs = [kspec, kspec, vspec, kspec, kspec]
    args = [q, k, v, bf, bb]
    out_shape = [jax.ShapeDtypeStruct((n_seq * seq, hv), F32)]
    out_specs = [pl.BlockSpec((seq, hps * dv), lambda b, g: (b, g))]
    if has_init:
        s0f, s0b = init
        st_spec = pl.BlockSpec((None, hps, dk, dv), lambda b, g: (b, j * groups + g, 0, 0))
        in_specs += [st_spec, st_spec]
        args += [s0f, s0b]
    else:
        st_shape = jax.ShapeDtypeStruct((n_seq, GLA_HEADS, dk, dv), F32)
        st_spec = pl.BlockSpec((None, hps, dk, dv), lambda b, g: (b, g, 0, 0))
        out_shape += [st_shape, st_shape]
        out_specs += [st_spec, st_spec]
    outs, _ = _run(
        functools.partial(_gla_scan_kernel, n_chunks, hps, dk, dv, has_init),
        name=f"gla_scan_{'latent' if has_init else 'context'}", grid=(n_seq, groups),
        args=args, in_specs=in_specs, out_shape=out_shape, out_specs=out_specs,
        scratch_shapes=[pltpu.VMEM((hps // 2, dv, GLA_CHAINS * dk), F32),
                        pltpu.VMEM((GLA_CHAINS * GLA_CHUNK, GLA_CHAINS * GLA_CHUNK), BF16)])
    return outs


def _rope_tables(lay):
    n_freq = HEAD_DIM // 4
    pos = np.arange(lay.dec_seq)
    freqs = (np.float32(ROPE_THETA) ** (-np.arange(n_freq, dtype=np.float32) / np.float32(n_freq))).astype(np.float32)
    ang_r = (pos // GRID_W).astype(np.float32)[:, None] * freqs
    ang_c = (pos % GRID_W).astype(np.float32)[:, None] * freqs
    cos = np.concatenate([np.cos(ang_r)] * 2 + [np.cos(ang_c)] * 2, axis=-1)
    sin = np.concatenate([-np.sin(ang_r), np.sin(ang_r), -np.sin(ang_c), np.sin(ang_c)], axis=-1)
    cos = np.concatenate([np.ones((TM, HEAD_DIM), np.float32), cos], axis=0).astype(np.float32)
    sin = np.concatenate([np.zeros((TM, HEAD_DIM), np.float32), sin], axis=0).astype(np.float32)
    return jnp.asarray(cos), jnp.asarray(sin)


def kernel(x_prompt, x_sample, c, cache_k, cache_v, state_gla_fwd, state_gla_bwd, c_ctx, w_ada, b_ada, ln_g, ln_b, conv_w_in, conv_w, conv_w_out, attn_w_qkv, attn_q_norm, attn_k_norm, attn_w_o, gla_w_in, gla_w_gate1, gla_w_gate2, gla_b_gate, gla_norm, gla_w_o, ffn_w_in, ffn_w_out):
    batch, seq, d = x_prompt.shape
    dec_batch, dec_seq, _ = x_sample.shape
    depth = w_ada.shape[0]
    lay = _Layout(batch, seq, dec_batch, dec_seq, d, depth)
    n_steps = lay.n_steps
    kv = N_KV_HEADS * HEAD_DIM
    past = cache_k.shape[2]

    cond = jnp.repeat(jnp.concatenate([c_ctx[None, :], c], axis=0), MOD_GROUP, axis=0)
    lng, lnb = ln_g, ln_b
    g1 = jnp.concatenate([gla_w_gate1[:, 0], gla_w_gate1[:, 1]], axis=-1).astype(BF16)
    zeros = jnp.zeros_like(gla_w_gate2[:, 0])
    g2 = jnp.concatenate([jnp.concatenate([gla_w_gate2[:, 0], zeros], axis=-1),
                          jnp.concatenate([zeros, gla_w_gate2[:, 1]], axis=-1)], axis=1).astype(BF16)
    gb = jnp.concatenate([gla_b_gate[:, 0], gla_b_gate[:, 1]], axis=-1)[:, None, :]
    cos_t, sin_t = _rope_tables(lay)

    def mixer_weights(i):
        kind, j = i % N_MIXERS, i // N_MIXERS
        return [[(conv_w_in, j), (conv_w_out, j)], [(attn_w_qkv, j), (attn_w_o, j)], [(gla_w_in, j), (gla_w_o, j)]][kind]

    x = (x_prompt.reshape(lay.n_prompt, d), x_sample.reshape(dec_batch * dec_seq, d))
    mod = _modulation(cond, w_ada, b_ada, 0)
    mix_w = mixer_weights(0)
    new_k, new_v, new_sf, new_sb = [], [], [], []
    y_prompt = y_sample = None
    for i in range(depth):
        kind, j = i % N_MIXERS, i // N_MIXERS
        last = i == depth - 1
        ffn_jobs = [_cast_job(ffn_w_in, i, n_steps), _cast_job(ffn_w_out, i, n_steps)]
        gla_pre = None
        if kind == 0:
            x, jr = _conv_mixer(lay, i, j, x if i == 0 else (x,), mod, lng, lnb, mix_w[0], conv_w, mix_w[1], ffn_jobs)
        elif kind == 1:
            (q, k, v, kf, vf), jr = _attn_qkv(lay, i, j, x, mod, mix_w[0], attn_q_norm[:, None, :],
                                              attn_k_norm[:, None, :], cos_t, sin_t, ffn_jobs)
            new_k.append(kf[:lay.n_prompt].reshape(batch, seq, N_KV_HEADS, HEAD_DIM))
            new_v.append(vf[:lay.n_prompt].reshape(batch, seq, N_KV_HEADS, HEAD_DIM))
            kc = cache_k[:, j].reshape(dec_batch * past, kv).astype(BF16)
            vc = cache_v[:, j].reshape(dec_batch * past, kv).astype(BF16)
            x = _attention(lay, i, q, k, v, None, x, mod, lng, lnb, mix_w[1],
                           row0=0, n_seq=batch, seq=seq, tq=TM)
            x = _attention(lay, i, q, k, v, (kc, vc), x, mod, lng, lnb, mix_w[1],
                           row0=lay.n_prompt, n_seq=dec_batch, seq=dec_seq, tq=ATTN_LATENT_TQ)
        else:
            (q, k, v, og, bf, bb), jr = _gla_proj(lay, i, j, x, mod, mix_w[0], g1, g2, gb, ffn_jobs)
            s0f = state_gla_fwd.reshape(dec_batch, -1, *state_gla_fwd.shape[3:])
            s0b = state_gla_bwd.reshape(dec_batch, -1, *state_gla_bwd.shape[3:])
            o_ctx, sf, sb = _gla_scan(lay, j, q, k, v, bf, bb, None, row0=0, n_seq=batch, seq=seq,
                                      hps=GLA_HEADS_PER_STEP_CONTEXT)
            o_lat, = _gla_scan(lay, j, q, k, v, bf, bb, (s0f, s0b), row0=lay.n_prompt, n_seq=dec_batch,
                               seq=dec_seq, hps=GLA_HEADS_PER_STEP_LATENT)
            new_sf.append(sf)
            new_sb.append(sb)
            w_o = mix_w[1]
            if isinstance(w_o, tuple):
                w_o = w_o[0][w_o[1]].astype(BF16)
            gla_pre = (o_ctx, o_lat, og, gla_norm[j][None, :], w_o)
        ffn_w = (jr[0][0], jr[1][0])
        next_jobs = []
        if not last:
            next_jobs = [_mod_job(cond, w_ada, b_ada, i + 1, n_steps)]
            next_jobs += [_cast_job(arr, jj, n_steps) for arr, jj in mixer_weights(i + 1)]
        outs, jr = _ffn(lay, i, x, mod, lng, lnb, ffn_w[0], ffn_w[1], split_out=last, gla_pre=gla_pre,
                        jobs=next_jobs)
        if last:
            y_prompt, y_sample = outs
        else:
            x, = outs
            mod = jr[0][0]
            mix_w = [jr[1][0], jr[2][0]]

    def stack_layers(parts):
        return parts[0][:, None] if len(parts) == 1 else jnp.stack(parts, axis=1)

    y_prompt = y_prompt.reshape(batch, seq, d)
    y_sample = y_sample.reshape(dec_batch, dec_seq, d)
    return (y_prompt, y_sample, stack_layers(new_k), stack_layers(new_v),
            stack_layers(new_sf), stack_layers(new_sb))
```
